```python
import math
import jax, jax.numpy as jnp
from jax import lax
import numpy as np

D_MODEL = 1024
BATCH = 2
SEQ = 8192
DEPTH = 2

N_MIXERS = 2
N_HEADS = 16
HEAD_DIM = D_MODEL // N_HEADS
MOBA_BLOCK = 256
MOBA_TOPK = 3
MOBA_QCHUNK = 64
DIL_GROUPS = ((128, 1), (512, 4), (2048, 16))
N_DIL_GROUPS = len(DIL_GROUPS)
DIL_BLOCK = 128
DIL_PAD = math.lcm(*[d * DIL_BLOCK for _, d in DIL_GROUPS])
N_EXPERTS = 256
TOP_K = 8
N_GROUPS = 8
TOPK_GROUPS = 4
D_EXPERT = 256
ROUTED_SCALE = 2.5
DISPATCH_BLOCK = 128
DN_ALPHA = (2 * DEPTH) ** 0.25
DN_BETA = (8 * DEPTH) ** -0.25
LN_EPS = 1e-5
NEG = -1e30

kernel_name = "hybrid_moba_dilated_moe_deepnorm"


def alibi_slopes():
    return 2.0 ** (-8.0 * jnp.arange(1, N_HEADS + 1, dtype=jnp.float32) / N_HEADS)


def layer_norm(x, g, b):
    xf = x.astype(jnp.float32)
    mu = xf.mean(-1, keepdims=True)
    var = jnp.square(xf - mu).mean(-1, keepdims=True)
    y = (xf - mu) * lax.rsqrt(var + LN_EPS) * g.astype(jnp.float32) + b.astype(jnp.float32)
    return y.astype(x.dtype)


def moba_attention(q, k, v, slopes):
    bsz, nh, s, dh = q.shape
    s_pad = -(-s // MOBA_BLOCK) * MOBA_BLOCK
    padw = ((0, 0), (0, 0), (0, s_pad - s), (0, 0))
    q, k, v = jnp.pad(q, padw), jnp.pad(k, padw), jnp.pad(v, padw)
    nb = s_pad // MOBA_BLOCK
    kb = k.reshape(bsz, nh, nb, MOBA_BLOCK, dh)
    vb = v.reshape(bsz, nh, nb, MOBA_BLOCK, dh)
    scale = dh ** -0.5
    k_mean = kb.mean(axis=3)
    gate = jnp.einsum('bhsd,bhnd->bhsn', q, k_mean).astype(jnp.float32)
    q_blk = jnp.arange(s_pad) // MOBA_BLOCK
    past = jnp.arange(nb)[None, :] < q_blk[:, None]
    gate = jnp.where(past, gate, NEG)
    n_sel = min(MOBA_TOPK, nb)
    _, sel = lax.top_k(gate, n_sel)
    sel_valid = jnp.arange(n_sel)[None, :] < q_blk[:, None]

    c = MOBA_QCHUNK
    nc = s_pad // c
    qc = q.reshape(bsz, nh, nc, c, dh).transpose(2, 0, 1, 3, 4)
    selc = sel.reshape(bsz, nh, nc, c, n_sel).transpose(2, 0, 1, 3, 4)
    validc = sel_valid.reshape(nc, c, n_sel)
    gather_blocks = jax.vmap(jax.vmap(lambda blocks, ix: blocks[ix]))

    def chunk_fn(args):
        ci, q_c, sel_c, valid_c = args
        t = ci * c + jnp.arange(c)
        k_sel = gather_blocks(kb, sel_c)
        v_sel = gather_blocks(vb, sel_c)
        s_pos = sel_c[..., None] * MOBA_BLOCK + jnp.arange(MOBA_BLOCK)
        l_sel = (jnp.einsum('bhcd,bhcjnd->bhcjn', q_c, k_sel).astype(jnp.float32) * scale
                 - slopes[:, None, None, None] * (t[:, None, None] - s_pos))
        l_sel = jnp.where(valid_c[None, None, :, :, None], l_sel, NEG)
        own = (ci * c) // MOBA_BLOCK
        k_own = lax.dynamic_index_in_dim(kb, own, axis=2, keepdims=False)
        v_own = lax.dynamic_index_in_dim(vb, own, axis=2, keepdims=False)
        dist = t[:, None] - (own * MOBA_BLOCK + jnp.arange(MOBA_BLOCK))[None, :]
        l_own = (jnp.einsum('bhcd,bhnd->bhcn', q_c, k_own).astype(jnp.float32) * scale
                 - slopes[:, None, None] * dist)
        l_own = jnp.where(dist >= 0, l_own, NEG)
        n_sel_keys = n_sel * MOBA_BLOCK
        logits = jnp.concatenate([l_sel.reshape(bsz, nh, c, n_sel_keys), l_own], axis=-1)
        p = jax.nn.softmax(logits, axis=-1).astype(v.dtype)
        p_sel = p[..., :n_sel_keys].reshape(bsz, nh, c, n_sel, MOBA_BLOCK)
        return (jnp.einsum('bhcjn,bhcjnd->bhcd', p_sel, v_sel)
                + jnp.einsum('bhcn,bhnd->bhcd', p[..., n_sel_keys:], v_own))

    out = lax.map(chunk_fn, (jnp.arange(nc), qc, selc, validc))
    out = out.transpose(1, 2, 0, 3, 4).reshape(bsz, nh, s_pad, dh)
    return out[:, :, :s]


def dilated_group_attention(q, k, v, slopes, window, dil):
    bsz, nh, s, dh = q.shape
    steps_max = window // dil
    L = s // dil
    nbk = L // DIL_BLOCK

    def to_blocks(tns):
        return tns.reshape(bsz, nh, L, dil, dh).transpose(0, 1, 3, 2, 4).reshape(
            bsz, nh, dil, nbk, DIL_BLOCK, dh)

    qb, kb, vb = to_blocks(q), to_blocks(k), to_blocks(v)
    padw = ((0, 0), (0, 0), (0, 0), (1, 0), (0, 0), (0, 0))
    k_band = jnp.concatenate([jnp.pad(kb, padw)[:, :, :, :-1], kb], axis=4)
    v_band = jnp.concatenate([jnp.pad(vb, padw)[:, :, :, :-1], vb], axis=4)
    j_k = jnp.arange(2 * DIL_BLOCK) - DIL_BLOCK
    steps = jnp.arange(DIL_BLOCK)[:, None] - j_k[None, :]
    key_sub = jnp.arange(nbk)[:, None] * DIL_BLOCK + j_k[None, :]
    mask = (steps >= 0) & (steps <= steps_max) & (key_sub[:, None, :] >= 0)
    logits = (jnp.einsum('bhrnqd,bhrnkd->bhrnqk', qb, k_band).astype(jnp.float32) * dh ** -0.5
              - slopes[:, None, None, None, None] * (steps * dil))
    logits = jnp.where(mask, logits, NEG)
    m = logits.max(-1, keepdims=True)
    e = jnp.exp(logits - m)
    den = e.sum(-1, keepdims=True)
    o = jnp.einsum('bhrnqk,bhrnkd->bhrnqd', (e / den).astype(v.dtype), v_band)
    lse = (m + jnp.log(den))[..., 0]
    o = o.reshape(bsz, nh, dil, L, dh).transpose(0, 1, 3, 2, 4).reshape(bsz, nh, s, dh)
    lse = lse.reshape(bsz, nh, dil, L).transpose(0, 1, 3, 2).reshape(bsz, nh, s)
    return o, lse


def moba_mixer(x, w_qkv, slopes):
    bsz, s, _ = x.shape
    qkv = (x @ w_qkv).reshape(bsz, s, 3, N_HEADS, HEAD_DIM)
    q, k, v = (qkv[:, :, j].transpose(0, 2, 1, 3) for j in range(3))
    o = moba_attention(q, k, v, slopes)
    return o.transpose(0, 2, 1, 3).reshape(bsz, s, N_HEADS * HEAD_DIM)


def dilated_mixer(x, w_qkv, slopes):
    bsz, s, _ = x.shape
    s_pad = -(-s // DIL_PAD) * DIL_PAD
    qkv = (x @ w_qkv).reshape(bsz, s, N_DIL_GROUPS, 3, N_HEADS, HEAD_DIM)
    qkv = jnp.pad(qkv, ((0, 0), (0, s_pad - s), (0, 0), (0, 0), (0, 0), (0, 0)))
    outs, lses = [], []
    for g, (window, dil) in enumerate(DIL_GROUPS):
        q, k, v = (qkv[:, :, g, j].transpose(0, 2, 1, 3) for j in range(3))
        o, lse = dilated_group_attention(q, k, v, slopes, window, dil)
        outs.append(o)
        lses.append(lse)
    wts = jax.nn.softmax(jnp.stack(lses, axis=0), axis=0)
    o = jnp.einsum('gbhs,gbhsd->bhsd', wts.astype(x.dtype), jnp.stack(outs, axis=0))
    return o[:, :, :s].transpose(0, 2, 1, 3).reshape(bsz, s, N_HEADS * HEAD_DIM)


def moe(x, w_router, router_bias, we_gate, we_up, we_down, ws_gate, ws_up, ws_down):
    bsz, s, d = x.shape
    T = bsz * s
    xt = x.reshape(T, d)
    scores = jax.nn.sigmoid((xt @ w_router).astype(jnp.float32))
    choice = scores + router_bias.astype(jnp.float32)
    grp = choice.reshape(T, N_GROUPS, N_EXPERTS // N_GROUPS)
    grp_score = lax.top_k(grp, 2)[0].sum(-1)
    _, top_grp = lax.top_k(grp_score, TOPK_GROUPS)
    grp_mask = jnp.any(top_grp[..., None] == jnp.arange(N_GROUPS), axis=1)
    choice = jnp.where(jnp.repeat(grp_mask, N_EXPERTS // N_GROUPS, axis=1), choice, NEG)
    _, top_e = lax.top_k(choice, TOP_K)
    w = jnp.take_along_axis(scores, top_e, axis=1)
    w = w / w.sum(-1, keepdims=True) * ROUTED_SCALE
    A = T * TOP_K
    bt = DISPATCH_BLOCK
    flat_e = top_e.reshape(A)
    flat_t = jnp.arange(A) // TOP_K
    flat_w = w.reshape(A)
    order = jnp.argsort(flat_e)
    e_sorted, t_sorted, w_sorted = flat_e[order], flat_t[order], flat_w[order]
    counts = jnp.bincount(flat_e, length=N_EXPERTS)
    starts = jnp.cumsum(counts) - counts
    pcounts = (counts + bt - 1) // bt * bt
    pends = jnp.cumsum(pcounts)
    pstarts = pends - pcounts
    dest = pstarts[e_sorted] + jnp.arange(A) - starts[e_sorted]
    n_blocks = (A + N_EXPERTS * (bt - 1) + bt - 1) // bt
    P = n_blocks * bt
    slot_t = jnp.full((P,), T, jnp.int32).at[dest].set(t_sorted)
    slot_w = jnp.zeros((P,), w.dtype).at[dest].set(w_sorted)
    block_e = jnp.minimum(jnp.searchsorted(pends, jnp.arange(n_blocks) * bt, side='right'),
                          N_EXPERTS - 1)
    x_pad = jnp.concatenate([xt, jnp.zeros((1, d), xt.dtype)], axis=0)

    def expert_block(args):
        e, tok, wt = args
        xb = x_pad[tok]
        h = jax.nn.silu(xb @ we_gate[e]) * (xb @ we_up[e])
        return (h @ we_down[e]) * wt[:, None].astype(xb.dtype)

    y = lax.map(expert_block, (block_e, slot_t.reshape(n_blocks, bt), slot_w.reshape(n_blocks, bt)))
    routed = jax.ops.segment_sum(y.reshape(P, d), slot_t, num_segments=T + 1)[:T]
    shared = (jax.nn.silu(xt @ ws_gate) * (xt @ ws_up)) @ ws_down
    return (routed + shared).reshape(bsz, s, d)


def setup_inputs(seed: int = 0) -> dict:
    key = jax.random.key(seed)
    ks = jax.random.split(key, 16)
    D, E, F = D_MODEL, N_EXPERTS, D_EXPERT
    n_a = (DEPTH + N_MIXERS - 1) // N_MIXERS
    n_b = DEPTH // N_MIXERS
    std = D ** -0.5
    qkv_scale = jnp.array([1.0, 1.0, DN_BETA], jnp.float32)[:, None]
    x = jax.random.normal(ks[0], (BATCH, SEQ, D), jnp.float32)
    w_qkv_a = (jax.random.normal(ks[1], (n_a, D, 3, D)) * std * qkv_scale).reshape(n_a, D, 3 * D)
    w_qkv_b = (jax.random.normal(ks[2], (n_b, D, N_DIL_GROUPS, 3, D)) * std * qkv_scale).reshape(
        n_b, D, N_DIL_GROUPS * 3 * D)
    w_o = jax.random.normal(ks[3], (DEPTH, D, D)) * std * DN_BETA
    ln_mix_g = 1.0 + 0.02 * jax.random.normal(ks[4], (DEPTH, D))
    ln_mix_b = 0.02 * jax.random.normal(ks[5], (DEPTH, D))
    w_router = jax.random.normal(ks[6], (DEPTH, D, E)) * std
    router_bias = 0.01 * jax.random.normal(ks[7], (DEPTH, E))
    we_gate = jax.random.normal(ks[8], (DEPTH, E, D, F)) * std
    we_up = jax.random.normal(ks[9], (DEPTH, E, D, F)) * std
    we_down = jax.random.normal(ks[10], (DEPTH, E, F, D)) * (F ** -0.5) * DN_BETA
    ws_gate = jax.random.normal(ks[11], (DEPTH, D, F)) * std
    ws_up = jax.random.normal(ks[12], (DEPTH, D, F)) * std
    ws_down = jax.random.normal(ks[13], (DEPTH, F, D)) * (F ** -0.5) * DN_BETA
    ln_ffn_g = 1.0 + 0.02 * jax.random.normal(ks[14], (DEPTH, D))
    ln_ffn_b = 0.02 * jax.random.normal(ks[15], (DEPTH, D))
    return {"x": x, "w_qkv_a": w_qkv_a, "w_qkv_b": w_qkv_b, "w_o": w_o,
            "ln_mix_g": ln_mix_g, "ln_mix_b": ln_mix_b, "w_router": w_router,
            "router_bias": router_bias, "we_gate": we_gate, "we_up": we_up,
            "we_down": we_down, "ws_gate": ws_gate, "ws_up": ws_up, "ws_down": ws_down,
            "ln_ffn_g": ln_ffn_g, "ln_ffn_b": ln_ffn_b}


def reference(x, w_qkv_a, w_qkv_b, w_o, ln_mix_g, ln_mix_b, w_router, router_bias,
              we_gate, we_up, we_down, ws_gate, ws_up, ws_down, ln_ffn_g, ln_ffn_b):
    slopes = alibi_slopes()
    for i in range(DEPTH):
        if i % N_MIXERS == 0:
            mix = moba_mixer(x, w_qkv_a[i // N_MIXERS], slopes)
        else:
            mix = dilated_mixer(x, w_qkv_b[i // N_MIXERS], slopes)
        x = layer_norm(DN_ALPHA * x + mix @ w_o[i], ln_mix_g[i], ln_mix_b[i])
        f = moe(x, w_router[i], router_bias[i], we_gate[i], we_up[i], we_down[i],
                ws_gate[i], ws_up[i], ws_down[i])
        x = layer_norm(DN_ALPHA * x + f, ln_ffn_g[i], ln_ffn_b[i])
    return x
```

```python
import functools
import math

import jax
import jax.numpy as jnp
import numpy as np
from jax import lax
from jax.experimental import pallas as pl
from jax.experimental.pallas import tpu as pltpu

N_HEADS = 16
HEAD_DIM = 64
DEPTH = 2
N_MIXERS = 2
MOBA_BLOCK = 256
MOBA_TOPK = 3
DIL_GROUPS = ((128, 1), (512, 4), (2048, 16))
DIL_BLOCK = 128
N_EXPERTS = 256
TOP_K = 8
N_GROUPS = 8
TOPK_GROUPS = 4
ROUTED_SCALE = 2.5
DN_ALPHA = (2 * DEPTH) ** 0.25
LN_EPS = 1e-5
NEG = -1e30

LANES = 128
HEADS_PER_SLAB = LANES // HEAD_DIM
N_SLABS = N_HEADS // HEADS_PER_SLAB

EXPERT_ROW_BLOCK = 256

BF16 = jnp.bfloat16
F32 = jnp.float32
_HI = lax.Precision.HIGHEST

_ARB = "arbitrary"


def _cparams(n_axes, vmem_mb=48):
    return pltpu.CompilerParams(
        dimension_semantics=(_ARB,) * n_axes, vmem_limit_bytes=vmem_mb * 1024 * 1024
    )


def _alibi_slopes():
    return 2.0 ** (-8.0 * jnp.arange(1, N_HEADS + 1, dtype=F32) / N_HEADS)


def _proj_kernel(x_ref, w_ref, o_ref):
    o_ref[...] = jnp.dot(
        x_ref[...].astype(BF16), w_ref[...], preferred_element_type=F32
    ).astype(o_ref.dtype)


def _proj(x, w, tm=512, tn=1024):
    t, k = x.shape
    n = w.shape[1]
    return pl.pallas_call(
        _proj_kernel,
        grid=(t // tm, n // tn),
        in_specs=[
            pl.BlockSpec((tm, k), lambda i, j: (i, 0)),
            pl.BlockSpec((k, tn), lambda i, j: (0, j)),
        ],
        out_specs=pl.BlockSpec((tm, tn), lambda i, j: (i, j)),
        out_shape=jax.ShapeDtypeStruct((t, n), BF16),
        compiler_params=_cparams(2),
        name="proj",
    )(x, w)


def _moba_kernel(slopes_ref, q_ref, k_ref, v_ref, o_ref, vt_ref, km_ref, sel_ref, *, n_blocks):
    blk = MOBA_BLOCK
    h2 = pl.program_id(1)
    i = pl.program_id(2)

    @pl.when(i == 0)
    def _prepare_keys():
        def body(n, c):
            r0 = pl.multiple_of(n * blk, blk)
            vb = v_ref[0, pl.ds(r0, blk), :].astype(F32)
            vt_ref[n] = vb.T.astype(BF16)
            kb = k_ref[0, pl.ds(r0, blk), :].astype(F32)
            km_ref[pl.ds(n, 1), :] = jnp.mean(kb, axis=0, keepdims=True)
            return c

        lax.fori_loop(0, n_blocks, body, 0)

    qt = q_ref[0].astype(F32).T
    feat = lax.broadcasted_iota(jnp.int32, qt.shape, 0)
    kmean = km_ref[...]
    km_lane = lax.broadcasted_iota(jnp.int32, kmean.shape, 1)
    blk_row = lax.broadcasted_iota(jnp.int32, (n_blocks, blk), 0)
    key_row = lax.broadcasted_iota(jnp.int32, (blk, blk), 0)
    qry_col = lax.broadcasted_iota(jnp.int32, (blk, blk), 1)
    key_row_f = key_row.astype(F32)

    outs = []
    for hh in range(HEADS_PER_SLAB):
        slope = slopes_ref[h2 * HEADS_PER_SLAB + hh]
        in_head = (feat >= hh * HEAD_DIM) & (feat < (hh + 1) * HEAD_DIM)
        qt_h = jnp.where(in_head, qt, 0.0)
        qt_s = (qt_h * (HEAD_DIM ** -0.5)).astype(BF16)

        km_h = jnp.where((km_lane >= hh * HEAD_DIM) & (km_lane < (hh + 1) * HEAD_DIM), kmean, 0.0)
        gate = jnp.dot(km_h, qt_h, preferred_element_type=F32, precision=_HI)
        g = jnp.where(blk_row < i, gate, -jnp.inf)
        sel = jnp.zeros(g.shape, F32)
        for _ in range(MOBA_TOPK):
            gmax = jnp.max(g, axis=0, keepdims=True)
            first = jnp.min(jnp.where(g == gmax, blk_row, n_blocks), axis=0, keepdims=True)
            hit = blk_row == first
            sel = jnp.where(hit & (gmax > -jnp.inf), 1.0, sel)
            g = jnp.where(hit, -jnp.inf, g)
        sel_ref[hh] = sel

        bias = slope * key_row_f

        def body(j, carry, hh=hh, slope=slope, qt_s=qt_s, bias=bias):
            m, l, acc = carry
            r0 = pl.multiple_of(j * blk, blk)
            s = jnp.dot(k_ref[0, pl.ds(r0, blk), :], qt_s, preferred_element_type=F32) + bias
            c_j = slope * (blk * (j - i)).astype(F32)
            chosen = sel_ref[hh, pl.ds(j, 1), :] > 0.5
            m_new = jnp.where(chosen, jnp.maximum(m, jnp.max(s, axis=0, keepdims=True) + c_j), m)
            alpha = jnp.exp(m - m_new)
            p = jnp.exp(s - jnp.where(chosen, m_new - c_j, -NEG))
            l = alpha * l + jnp.sum(p, axis=0, keepdims=True)
            vt = vt_ref[j][hh * HEAD_DIM:(hh + 1) * HEAD_DIM]
            acc = alpha * acc + jnp.dot(vt, p.astype(BF16), preferred_element_type=F32)
            return m_new, l, acc

        init = (jnp.full((1, blk), NEG, F32), jnp.zeros((1, blk), F32), jnp.zeros((HEAD_DIM, blk), F32))
        m, l, acc = lax.fori_loop(0, i, body, init)

        r0 = pl.multiple_of(i * blk, blk)
        s = jnp.dot(k_ref[0, pl.ds(r0, blk), :], qt_s, preferred_element_type=F32) + bias
        s = jnp.where(key_row <= qry_col, s, NEG)
        m_new = jnp.maximum(m, jnp.max(s, axis=0, keepdims=True))
        alpha = jnp.exp(m - m_new)
        p = jnp.exp(s - m_new)
        l = alpha * l + jnp.sum(p, axis=0, keepdims=True)
        vt = vt_ref[i][hh * HEAD_DIM:(hh + 1) * HEAD_DIM]
        acc = alpha * acc + jnp.dot(vt, p.astype(BF16), preferred_element_type=F32)
        outs.append(acc / l)

    o_ref[0] = jnp.concatenate(outs, axis=0).T.astype(o_ref.dtype)


def _moba_attention(qkv, slopes):
    b, s, _ = qkv.shape
    blk = MOBA_BLOCK
    n_blocks = s // blk
    grid_spec = pltpu.PrefetchScalarGridSpec(
        num_scalar_prefetch=1,
        grid=(b, N_SLABS, n_blocks),
        in_specs=[
            pl.BlockSpec((1, blk, LANES), lambda bi, h, i, sl: (bi, i, h)),
            pl.BlockSpec((1, s, LANES), lambda bi, h, i, sl: (bi, 0, N_SLABS + h)),
            pl.BlockSpec((1, s, LANES), lambda bi, h, i, sl: (bi, 0, 2 * N_SLABS + h)),
        ],
        out_specs=pl.BlockSpec((1, blk, LANES), lambda bi, h, i, sl: (bi, i, h)),
        scratch_shapes=[
            pltpu.VMEM((n_blocks, LANES, blk), BF16),
            pltpu.VMEM((n_blocks, LANES), F32),
            pltpu.VMEM((HEADS_PER_SLAB, n_blocks, blk), F32),
        ],
    )
    return pl.pallas_call(
        functools.partial(_moba_kernel, n_blocks=n_blocks),
        grid_spec=grid_spec,
        out_shape=jax.ShapeDtypeStruct((b, s, N_HEADS * HEAD_DIM), BF16),
        compiler_params=_cparams(3),
        name="moba",
    )(slopes, qkv, qkv, qkv)


def _dilated_kernel(bias_ref, q_ref, kc_ref, kp_ref, vc_ref, vp_ref, o_ref, lse_ref, lse_s):
    blk = DIL_BLOCK
    n = pl.program_id(2)
    band_row = lax.broadcasted_iota(jnp.int32, (2 * blk, blk), 0)
    prev_invalid = (band_row < blk) & (n == 0)
    lse_s[...] = jnp.zeros(lse_s.shape, F32)
    for h2 in range(N_SLABS):
        cols = slice(h2 * LANES, (h2 + 1) * LANES)
        qt = q_ref[0, :, cols].astype(F32).T * (HEAD_DIM ** -0.5)
        feat = lax.broadcasted_iota(jnp.int32, qt.shape, 0)
        kband = jnp.concatenate([kp_ref[0, :, cols], kc_ref[0, :, cols]], axis=0)
        vband = jnp.concatenate([vp_ref[0, :, cols], vc_ref[0, :, cols]], axis=0)
        vt = vband.astype(F32).T.astype(BF16)
        outs = []
        for hh in range(HEADS_PER_SLAB):
            h = h2 * HEADS_PER_SLAB + hh
            in_head = (feat >= hh * HEAD_DIM) & (feat < (hh + 1) * HEAD_DIM)
            qt_h = jnp.where(in_head, qt, 0.0).astype(BF16)
            s = jnp.dot(kband, qt_h, preferred_element_type=F32) + bias_ref[h]
            s = jnp.where(prev_invalid, NEG, s)
            m = jnp.max(s, axis=0, keepdims=True)
            p = jnp.exp(s - m)
            l = jnp.sum(p, axis=0, keepdims=True)
            pv = jnp.dot(vt[hh * HEAD_DIM:(hh + 1) * HEAD_DIM], p.astype(BF16), preferred_element_type=F32)
            outs.append(pv / l)
            lse_s[pl.ds(h, 1), :] = m + jnp.log(l)
        o_ref[0, :, cols] = jnp.concatenate(outs, axis=0).T.astype(o_ref.dtype)
    lse_ref[0] = lse_s[...].T


def _dilated_bias(slopes, dil):
    blk = DIL_BLOCK
    steps = jnp.arange(blk)[None, :] - (jnp.arange(2 * blk)[:, None] - blk)
    steps_max = blk
    valid = (steps >= 0) & (steps <= steps_max)
    bias = -slopes[:, None, None] * (steps * dil).astype(F32)[None]
    return jnp.where(valid[None], bias, NEG).astype(F32)


def _dilated_group(qkv, slopes, g, dil):
    b, s, c = qkv.shape
    d = N_HEADS * HEAD_DIM
    blk = DIL_BLOCK
    l_sub = s // dil
    nb = l_sub // blk
    per_row = c // d
    view = qkv.reshape(b, l_sub, dil * c)
    bias = _dilated_bias(slopes, dil)

    def col(j):
        return lambda bi, r, n: (bi, n, r * per_row + g * 3 + j)

    def col_prev(j):
        return lambda bi, r, n: (bi, jnp.maximum(n - 1, 0), r * per_row + g * 3 + j)

    o, lse = pl.pallas_call(
        _dilated_kernel,
        grid=(b, dil, nb),
        in_specs=[
            pl.BlockSpec((N_HEADS, 2 * blk, blk), lambda bi, r, n: (0, 0, 0)),
            pl.BlockSpec((1, blk, d), col(0)),
            pl.BlockSpec((1, blk, d), col(1)),
            pl.BlockSpec((1, blk, d), col_prev(1)),
            pl.BlockSpec((1, blk, d), col(2)),
            pl.BlockSpec((1, blk, d), col_prev(2)),
        ],
        out_specs=[
            pl.BlockSpec((1, blk, d), lambda bi, r, n: (bi, n, r)),
            pl.BlockSpec((1, blk, LANES), lambda bi, r, n: (bi, n, r)),
        ],
        out_shape=[
            jax.ShapeDtypeStruct((b, l_sub, dil * d), BF16),
            jax.ShapeDtypeStruct((b, l_sub, dil * LANES), F32),
        ],
        scratch_shapes=[pltpu.VMEM((LANES, blk), F32)],
        compiler_params=_cparams(3),
        name=f"dilated_g{g}",
    )(bias, view, view, view, view, view)
    return o.reshape(b, s, d), lse.reshape(b, s, LANES)


def _merge_kernel(e_ref, o1, o2, o3, l1, l2, l3, out_ref):
    ls = [l1[...], l2[...], l3[...]]
    mx = jnp.maximum(jnp.maximum(ls[0], ls[1]), ls[2])
    es = [jnp.exp(x - mx) for x in ls]
    den = es[0] + es[1] + es[2]
    acc = None
    for e, o in zip(es, (o1, o2, o3)):
        w = jnp.dot(e / den, e_ref[...], preferred_element_type=F32, precision=_HI)
        term = w * o[...].astype(F32)
        acc = term if acc is None else acc + term
    out_ref[...] = acc.astype(out_ref.dtype)


def _merge_groups(os, lses, tm=512):
    t, d = os[0].shape
    expand = (jnp.arange(LANES)[:, None] == (jnp.arange(d)[None, :] // HEAD_DIM)).astype(F32)
    row = lambda w: pl.BlockSpec((tm, w), lambda i: (i, 0))
    return pl.pallas_call(
        _merge_kernel,
        grid=(t // tm,),
        in_specs=[pl.BlockSpec((LANES, d), lambda i: (0, 0))] + [row(d)] * 3 + [row(LANES)] * 3,
        out_specs=row(d),
        out_shape=jax.ShapeDtypeStruct((t, d), BF16),
        compiler_params=_cparams(1),
        name="merge_groups",
    )(expand, *os, *lses)


def _layer_norm(y, g, b):
    mu = jnp.mean(y, axis=-1, keepdims=True)
    yc = y - mu
    var = jnp.mean(yc * yc, axis=-1, keepdims=True)
    return yc * lax.rsqrt(var + LN_EPS) * g + b


def _wo_ln_kernel(mix_ref, wo_ref, x_ref, g_ref, b_ref, o_ref):
    f = jnp.dot(mix_ref[...], wo_ref[...], preferred_element_type=F32)
    o_ref[...] = _layer_norm(DN_ALPHA * x_ref[...] + f, g_ref[...], b_ref[...])


def _wo_ln(mix, wo, x, g, b, tm=256):
    t, d = x.shape
    row = pl.BlockSpec((tm, d), lambda i: (i, 0))
    vec = pl.BlockSpec((1, d), lambda i: (0, 0))
    return pl.pallas_call(
        _wo_ln_kernel,
        grid=(t // tm,),
        in_specs=[row, pl.BlockSpec((d, d), lambda i: (0, 0)), row, vec, vec],
        out_specs=row,
        out_shape=jax.ShapeDtypeStruct((t, d), F32),
        compiler_params=_cparams(1),
        name="wo_ln",
    )(mix, wo, x, g.reshape(1, d), b.reshape(1, d))


def _router_kernel(wr_ref, rb_ref, x_ref, e_ref, w_ref, pos_ref, cnt_ref, run_ref, *, tm):
    step = pl.program_id(0)
    ne = N_EXPERTS
    gs = ne // N_GROUPS

    @pl.when(step == 0)
    def _():
        run_ref[...] = jnp.zeros(run_ref.shape, F32)

    logits = lax.dot_general(
        wr_ref[...], x_ref[...], (((1,), (1,)), ((), ())), preferred_element_type=F32, precision=_HI
    )
    scores = jax.nn.sigmoid(logits)
    choice = scores + rb_ref[...][:, :1]

    grp_rows = lax.broadcasted_iota(jnp.int32, (gs, tm), 0)
    blocks, gscore = [], []
    for gi in range(N_GROUPS):
        cb = choice[gi * gs:(gi + 1) * gs]
        m1 = jnp.max(cb, axis=0, keepdims=True)
        i1 = jnp.min(jnp.where(cb == m1, grp_rows, gs), axis=0, keepdims=True)
        m2 = jnp.max(jnp.where(grp_rows == i1, -jnp.inf, cb), axis=0, keepdims=True)
        blocks.append(cb)
        gscore.append(m1 + m2)
    masked = []
    for gi in range(N_GROUPS):
        beaten = jnp.zeros((1, tm), jnp.int32)
        for gj in range(N_GROUPS):
            if gj == gi:
                continue
            wins = (gscore[gj] > gscore[gi]) | ((gscore[gj] == gscore[gi]) & (gj < gi))
            beaten = beaten + wins.astype(jnp.int32)
        masked.append(jnp.where(beaten < TOPK_GROUPS, blocks[gi], NEG))
    cand = jnp.concatenate(masked, axis=0)

    rows = lax.broadcasted_iota(jnp.int32, (ne, tm), 0)
    member = jnp.zeros((ne, tm), F32)
    picks, raw_w = [], []
    for _ in range(TOP_K):
        cmax = jnp.max(cand, axis=0, keepdims=True)
        first = jnp.min(jnp.where(cand == cmax, rows, ne), axis=0, keepdims=True)
        hit = rows == first
        raw_w.append(jnp.sum(jnp.where(hit, scores, 0.0), axis=0, keepdims=True))
        member = jnp.where(hit, 1.0, member)
        cand = jnp.where(hit, -jnp.inf, cand)
        picks.append(first)
    wsum = raw_w[0]
    for r in raw_w[1:]:
        wsum = wsum + r

    tri = (lax.broadcasted_iota(jnp.int32, (tm, tm), 0) < lax.broadcasted_iota(jnp.int32, (tm, tm), 1))
    member_b = member.astype(BF16)
    before = jnp.dot(member_b, tri.astype(BF16), preferred_element_type=F32)
    run = run_ref[...]
    rank = before + jnp.concatenate([run] * (tm // LANES), axis=1)
    for k in range(TOP_K):
        hit = rows == picks[k]
        e_ref[pl.ds(k, 1), :] = picks[k]
        w_ref[pl.ds(k, 1), :] = raw_w[k] / wsum * ROUTED_SCALE
        pos_ref[pl.ds(k, 1), :] = jnp.sum(jnp.where(hit, rank, 0.0), axis=0, keepdims=True).astype(jnp.int32)
    run = run + jnp.dot(member_b, jnp.ones((tm, LANES), BF16), preferred_element_type=F32)
    run_ref[...] = run
    cnt_ref[...] = run


def _router(x, w_router, router_bias, tm=256):
    t, d = x.shape
    ne = N_EXPERTS
    wr_t = w_router.T
    rb = jnp.broadcast_to(router_bias.astype(F32)[:, None], (ne, LANES))
    tok = pl.BlockSpec((TOP_K, tm), lambda i: (0, i))
    return pl.pallas_call(
        functools.partial(_router_kernel, tm=tm),
        grid=(t // tm,),
        in_specs=[
            pl.BlockSpec((ne, d), lambda i: (0, 0)),
            pl.BlockSpec((ne, LANES), lambda i: (0, 0)),
            pl.BlockSpec((tm, d), lambda i: (i, 0)),
        ],
        out_specs=[tok, tok, tok, pl.BlockSpec((ne, LANES), lambda i: (0, 0))],
        out_shape=[
            jax.ShapeDtypeStruct((TOP_K, t), jnp.int32),
            jax.ShapeDtypeStruct((TOP_K, t), F32),
            jax.ShapeDtypeStruct((TOP_K, t), jnp.int32),
            jax.ShapeDtypeStruct((ne, LANES), F32),
        ],
        scratch_shapes=[pltpu.VMEM((ne, LANES), F32)],
        compiler_params=_cparams(1),
        name="router",
    )(wr_t, rb, x)


def _dispatch_kernel(dest_hbm, x_ref, xs_hbm, dest_smem, sem_idx, sem_rows, *, tm):
    step = pl.program_id(0)
    idx_copy = pltpu.make_async_copy(dest_hbm.at[step], dest_smem, sem_idx)
    idx_copy.start()
    idx_copy.wait()

    def row_copy(t, d):
        return pltpu.make_async_copy(x_ref.at[pl.ds(t, 1)], xs_hbm.at[pl.ds(d, 1)], sem_rows)

    def issue(t, c):
        for k in range(TOP_K):
            row_copy(t, dest_smem[k * tm + t]).start()
        return c

    lax.fori_loop(0, tm, issue, 0)

    def drain(t, c):
        for _ in range(TOP_K):
            row_copy(0, 0).wait()
        return c

    lax.fori_loop(0, tm, drain, 0)


def _dispatch(x, dest_tiles, n_rows, tm=256):
    t, d = x.shape
    return pl.pallas_call(
        functools.partial(_dispatch_kernel, tm=tm),
        grid=(t // tm,),
        in_specs=[pl.BlockSpec(memory_space=pl.ANY), pl.BlockSpec((tm, d), lambda i: (i, 0))],
        out_specs=pl.BlockSpec(memory_space=pl.ANY),
        out_shape=jax.ShapeDtypeStruct((n_rows, d), x.dtype),
        scratch_shapes=[
            pltpu.SMEM((TOP_K * tm,), jnp.int32),
            pltpu.SemaphoreType.DMA,
            pltpu.SemaphoreType.DMA,
        ],
        compiler_params=_cparams(1),
        name="dispatch",
    )(dest_tiles, x)


def _experts_kernel(blk_e_ref, blk_row_ref, n_real_ref, xs_ref, wg_ref, wu_ref, wd_ref, ys_ref):
    @pl.when(pl.program_id(0) < n_real_ref[0])
    def _():
        xb = xs_ref[...].astype(BF16)
        gate = jnp.dot(xb, wg_ref[0].astype(BF16), preferred_element_type=F32)
        up = jnp.dot(xb, wu_ref[0].astype(BF16), preferred_element_type=F32)
        h = (gate * jax.nn.sigmoid(gate) * up).astype(BF16)
        ys_ref[...] = jnp.dot(h, wd_ref[0].astype(BF16), preferred_element_type=F32)


def _experts(xs, we_gate, we_up, we_down, blk_e, blk_row, n_real):
    p, d = xs.shape
    f = we_gate.shape[-1]
    rb = EXPERT_ROW_BLOCK
    grid_spec = pltpu.PrefetchScalarGridSpec(
        num_scalar_prefetch=3,
        grid=(p // rb,),
        in_specs=[
            pl.BlockSpec((rb, d), lambda i, be, br, nr: (br[i], 0)),
            pl.BlockSpec((1, d, f), lambda i, be, br, nr: (be[i], 0, 0)),
            pl.BlockSpec((1, d, f), lambda i, be, br, nr: (be[i], 0, 0)),
            pl.BlockSpec((1, f, d), lambda i, be, br, nr: (be[i], 0, 0)),
        ],
        out_specs=pl.BlockSpec((rb, d), lambda i, be, br, nr: (br[i], 0)),
    )
    return pl.pallas_call(
        _experts_kernel,
        grid_spec=grid_spec,
        out_shape=jax.ShapeDtypeStruct((p, d), F32),
        compiler_params=_cparams(1),
        name="experts",
    )(blk_e, blk_row, n_real, xs, we_gate, we_up, we_down)


def _combine_kernel(dest_hbm, ys_hbm, x_ref, w_ref, sg_ref, su_ref, sd_ref, g_ref, b_ref, o_ref,
                    dest_smem, rows_ref, sem_idx, sem_rows, *, tm):
    step = pl.program_id(0)
    idx_copy = pltpu.make_async_copy(dest_hbm.at[step], dest_smem, sem_idx)
    idx_copy.start()
    idx_copy.wait()

    def row_copy(k, t, d):
        return pltpu.make_async_copy(ys_hbm.at[pl.ds(d, 1)], rows_ref.at[k, pl.ds(t, 1)], sem_rows)

    def issue(t, c):
        for k in range(TOP_K):
            row_copy(k, t, dest_smem[k * tm + t]).start()
        return c

    lax.fori_loop(0, tm, issue, 0)

    x = x_ref[...]
    xb = x.astype(BF16)
    gate = jnp.dot(xb, sg_ref[...], preferred_element_type=F32)
    up = jnp.dot(xb, su_ref[...], preferred_element_type=F32)
    h = (gate * jax.nn.sigmoid(gate) * up).astype(BF16)
    f = jnp.dot(h, sd_ref[...], preferred_element_type=F32)

    def drain(t, c):
        for _ in range(TOP_K):
            row_copy(0, 0, 0).wait()
        return c

    lax.fori_loop(0, tm, drain, 0)

    w = w_ref[...]
    for k in range(TOP_K):
        f = f + w[:, k:k + 1] * rows_ref[k]
    o_ref[...] = _layer_norm(DN_ALPHA * x + f, g_ref[...], b_ref[...])


def _combine(x, ys, dest_tiles, w_tok, ws_gate, ws_up, ws_down, g, b, tm=128):
    t, d = x.shape
    f = ws_gate.shape[-1]
    row = pl.BlockSpec((tm, d), lambda i: (i, 0))
    vec = pl.BlockSpec((1, d), lambda i: (0, 0))
    return pl.pallas_call(
        functools.partial(_combine_kernel, tm=tm),
        grid=(t // tm,),
        in_specs=[
            pl.BlockSpec(memory_space=pl.ANY),
            pl.BlockSpec(memory_space=pl.ANY),
            row,
            pl.BlockSpec((tm, TOP_K), lambda i: (i, 0)),
            pl.BlockSpec((d, f), lambda i: (0, 0)),
            pl.BlockSpec((d, f), lambda i: (0, 0)),
            pl.BlockSpec((f, d), lambda i: (0, 0)),
            vec,
            vec,
        ],
        out_specs=row,
        out_shape=jax.ShapeDtypeStruct((t, d), F32),
        scratch_shapes=[
            pltpu.SMEM((TOP_K * tm,), jnp.int32),
            pltpu.VMEM((TOP_K, tm, d), F32),
            pltpu.SemaphoreType.DMA,
            pltpu.SemaphoreType.DMA,
        ],
        compiler_params=_cparams(1),
        name="combine",
    )(dest_tiles, ys, x, w_tok, ws_gate, ws_up, ws_down, g.reshape(1, d), b.reshape(1, d))


def _tile_major(a, tm):
    k, t = a.shape
    return a.reshape(k, t // tm, tm).transpose(1, 0, 2).reshape(t // tm, k * tm)


def _moe_ln(x, w_router, router_bias, we_gate, we_up, we_down, ws_gate, ws_up, ws_down, g, b):
    t, d = x.shape
    rb = EXPERT_ROW_BLOCK
    top_e, w, pos, counts = _router(x, w_router, router_bias)

    cnt = counts[:, 0].astype(jnp.int32)
    pcnt = (cnt + rb - 1) // rb * rb
    pend = jnp.cumsum(pcnt)
    pstart = pend - pcnt
    dest = pstart[top_e] + pos
    n_blocks = t * TOP_K // rb + N_EXPERTS
    n_real = (pend[-1] // rb).astype(jnp.int32)
    blk_ids = jnp.minimum(jnp.arange(n_blocks, dtype=jnp.int32), jnp.maximum(n_real - 1, 0))
    blk_e = jnp.minimum(
        jnp.searchsorted(pend, blk_ids * rb, side="right"), N_EXPERTS - 1
    ).astype(jnp.int32)

    xs = _dispatch(x, _tile_major(dest, 256), n_blocks * rb, tm=256)
    ys = _experts(xs, we_gate, we_up, we_down, blk_e, blk_ids, n_real.reshape(1))
    return _combine(
        x, ys, _tile_major(dest, 128), w.T, ws_gate.astype(BF16), ws_up.astype(BF16),
        ws_down.astype(BF16), g, b, tm=128,
    )


def kernel(x, w_qkv_a, w_qkv_b, w_o, ln_mix_g, ln_mix_b, w_router, router_bias, we_gate, we_up,
           we_down, ws_gate, ws_up, ws_down, ln_ffn_g, ln_ffn_b):
    bsz, s, d = x.shape
    t = bsz * s
    slopes = _alibi_slopes()
    xt = x.reshape(t, d)
    for i in range(DEPTH):
        if i % N_MIXERS == 0:
            qkv = _proj(xt, w_qkv_a[i // N_MIXERS].astype(BF16))
            mix = _moba_attention(qkv.reshape(bsz, s, -1), slopes).reshape(t, d)
        else:
            qkv = _proj(xt, w_qkv_b[i // N_MIXERS].astype(BF16)).reshape(bsz, s, -1)
            outs, lses = [], []
            for g, (_, dil) in enumerate(DIL_GROUPS):
                o, lse = _dilated_group(qkv, slopes, g, dil)
                outs.append(o.reshape(t, d))
                lses.append(lse.reshape(t, LANES))
            mix = _merge_groups(outs, lses)
        x1 = _wo_ln(mix, w_o[i].astype(BF16), xt, ln_mix_g[i], ln_mix_b[i])
        xt = _moe_ln(x1, w_router[i], router_bias[i], we_gate[i], we_up[i], we_down[i],
                     ws_gate[i], ws_up[i], ws_down[i], ln_ffn_g[i], ln_ffn_b[i])
    return xt.reshape(bsz, s, d)
```

```python
import functools
import math

import jax
import jax.numpy as jnp
import numpy as np
from jax import lax
from jax.experimental import pallas as pl
from jax.experimental.pallas import tpu as pltpu

N_HEADS = 16
HEAD_DIM = 64
DEPTH = 2
N_MIXERS = 2
MOBA_BLOCK = 256
MOBA_TOPK = 3
MOBA_KEY_BLOCKS_PER_STEP = 4
DIL_GROUPS = ((128, 1), (512, 4), (2048, 16))
DIL_BLOCK = 128
N_EXPERTS = 256
TOP_K = 8
N_GROUPS = 8
TOPK_GROUPS = 4
ROUTED_SCALE = 2.5
DN_ALPHA = (2 * DEPTH) ** 0.25
LN_EPS = 1e-5
NEG = -1e30

LANES = 128
HEADS_PER_SLAB = LANES // HEAD_DIM
N_SLABS = N_HEADS // HEADS_PER_SLAB

EXPERT_ROW_BLOCK = 256

BF16 = jnp.bfloat16
F32 = jnp.float32
_HI = lax.Precision.HIGHEST

_ARB = "arbitrary"


def _cparams(n_axes, vmem_mb=48):
    return pltpu.CompilerParams(
        dimension_semantics=(_ARB,) * n_axes, vmem_limit_bytes=vmem_mb * 1024 * 1024
    )


def _alibi_slopes():
    return 2.0 ** (-8.0 * jnp.arange(1, N_HEADS + 1, dtype=F32) / N_HEADS)


def _proj_kernel(x_ref, w_ref, o_ref):
    o_ref[...] = jnp.dot(
        x_ref[...].astype(BF16), w_ref[...], preferred_element_type=F32
    ).astype(o_ref.dtype)


def _proj(x, w, tm=512, tn=1024):
    t, k = x.shape
    n = w.shape[1]
    return pl.pallas_call(
        _proj_kernel,
        grid=(t // tm, n // tn),
        in_specs=[
            pl.BlockSpec((tm, k), lambda i, j: (i, 0)),
            pl.BlockSpec((k, tn), lambda i, j: (0, j)),
        ],
        out_specs=pl.BlockSpec((tm, tn), lambda i, j: (i, j)),
        out_shape=jax.ShapeDtypeStruct((t, n), BF16),
        compiler_params=_cparams(2),
        name="proj",
    )(x, w)


def _moba_kernel(slopes_ref, q_ref, k_ref, v_ref, o_ref, vt_ref, km_ref, sel_ref, *, n_blocks):
    blk = MOBA_BLOCK
    unroll = MOBA_KEY_BLOCKS_PER_STEP
    span = unroll * blk
    h2 = pl.program_id(1)
    i = pl.program_id(2)

    @pl.when(i == 0)
    def _prepare_keys():
        def body(n, c):
            r0 = pl.multiple_of(n * blk, blk)
            vb = v_ref[0, pl.ds(r0, blk), :].astype(F32)
            vt_ref[n] = vb.T.astype(BF16)
            kb = k_ref[0, pl.ds(r0, blk), :].astype(F32)
            km_ref[pl.ds(n, 1), :] = jnp.mean(kb, axis=0, keepdims=True)
            return c

        lax.fori_loop(0, n_blocks, body, 0)

    qt = q_ref[0].astype(F32).T
    feat = lax.broadcasted_iota(jnp.int32, qt.shape, 0)
    kmean = km_ref[...]
    km_lane = lax.broadcasted_iota(jnp.int32, kmean.shape, 1)
    blk_row = lax.broadcasted_iota(jnp.int32, (n_blocks, blk), 0)
    span_row_f = lax.broadcasted_iota(jnp.int32, (span, blk), 0).astype(F32)

    heads = range(HEADS_PER_SLAB)
    slopes = [slopes_ref[h2 * HEADS_PER_SLAB + hh] for hh in heads]
    qt_s = []
    for hh in heads:
        in_head = (feat >= hh * HEAD_DIM) & (feat < (hh + 1) * HEAD_DIM)
        qt_h = jnp.where(in_head, qt, 0.0)
        qt_s.append((qt_h * (HEAD_DIM ** -0.5)).astype(BF16))

        km_h = jnp.where((km_lane >= hh * HEAD_DIM) & (km_lane < (hh + 1) * HEAD_DIM), kmean, 0.0)
        gate = jnp.dot(km_h, qt_h, preferred_element_type=F32, precision=_HI)
        g = jnp.where(blk_row < i, gate, -jnp.inf)
        sel = jnp.zeros(g.shape, F32)
        for _ in range(MOBA_TOPK):
            gmax = jnp.max(g, axis=0, keepdims=True)
            first = jnp.min(jnp.where(g == gmax, blk_row, n_blocks), axis=0, keepdims=True)
            hit = blk_row == first
            sel = jnp.where(hit & (gmax > -jnp.inf), 1.0, sel)
            g = jnp.where(hit, -jnp.inf, g)
        sel_ref[hh] = sel

    bias = [slopes[hh] * span_row_f for hh in heads]

    def body(it, carry):
        j0 = it * unroll
        kk = k_ref[0, pl.ds(pl.multiple_of(j0 * blk, span), span), :]
        base = (blk * (j0 - i)).astype(F32)
        out = []
        for hh in heads:
            m, l, acc = carry[hh]
            s = jnp.dot(kk, qt_s[hh], preferred_element_type=F32) + bias[hh]
            c = slopes[hh] * base
            chosen, m_new = [], m
            for u in range(unroll):
                ch = sel_ref[hh, pl.ds(j0 + u, 1), :] > 0.5
                bm = jnp.max(s[u * blk:(u + 1) * blk], axis=0, keepdims=True) + c
                m_new = jnp.where(ch, jnp.maximum(m_new, bm), m_new)
                chosen.append(ch)
            alpha = jnp.exp(m - m_new)
            p = jnp.concatenate(
                [jnp.exp(s[u * blk:(u + 1) * blk] - jnp.where(chosen[u], m_new - c, -NEG)) for u in range(unroll)],
                axis=0)
            l = alpha * l + jnp.sum(p, axis=0, keepdims=True)
            vt = jnp.concatenate(
                [vt_ref[j0 + u][hh * HEAD_DIM:(hh + 1) * HEAD_DIM] for u in range(unroll)], axis=1)
            acc = alpha * acc + jnp.dot(vt, p.astype(BF16), preferred_element_type=F32)
            out.append((m_new, l, acc))
        return tuple(out)

    init = tuple(
        (jnp.full((1, blk), NEG, F32), jnp.zeros((1, blk), F32), jnp.zeros((HEAD_DIM, blk), F32)) for _ in heads)
    state = lax.fori_loop(0, (i + unroll - 1) // unroll, body, init)

    key_row = lax.broadcasted_iota(jnp.int32, (blk, blk), 0)
    qry_col = lax.broadcasted_iota(jnp.int32, (blk, blk), 1)
    k_own = k_ref[0, pl.ds(pl.multiple_of(i * blk, blk), blk), :]
    vt_own = vt_ref[i]
    outs = []
    for hh in heads:
        m, l, acc = state[hh]
        s = jnp.dot(k_own, qt_s[hh], preferred_element_type=F32) + slopes[hh] * key_row.astype(F32)
        s = jnp.where(key_row <= qry_col, s, NEG)
        m_new = jnp.maximum(m, jnp.max(s, axis=0, keepdims=True))
        alpha = jnp.exp(m - m_new)
        p = jnp.exp(s - m_new)
        l = alpha * l + jnp.sum(p, axis=0, keepdims=True)
        vt = vt_own[hh * HEAD_DIM:(hh + 1) * HEAD_DIM]
        acc = alpha * acc + jnp.dot(vt, p.astype(BF16), preferred_element_type=F32)
        outs.append(acc / l)

    o_ref[0] = jnp.concatenate(outs, axis=0).T.astype(o_ref.dtype)


def _moba_attention(qkv, slopes):
    b, s, _ = qkv.shape
    blk = MOBA_BLOCK
    n_blocks = s // blk
    grid_spec = pltpu.PrefetchScalarGridSpec(
        num_scalar_prefetch=1,
        grid=(b, N_SLABS, n_blocks),
        in_specs=[
            pl.BlockSpec((1, blk, LANES), lambda bi, h, i, sl: (bi, i, h)),
            pl.BlockSpec((1, s, LANES), lambda bi, h, i, sl: (bi, 0, N_SLABS + h)),
            pl.BlockSpec((1, s, LANES), lambda bi, h, i, sl: (bi, 0, 2 * N_SLABS + h)),
        ],
        out_specs=pl.BlockSpec((1, blk, LANES), lambda bi, h, i, sl: (bi, i, h)),
        scratch_shapes=[
            pltpu.VMEM((n_blocks, LANES, blk), BF16),
            pltpu.VMEM((n_blocks, LANES), F32),
            pltpu.VMEM((HEADS_PER_SLAB, n_blocks, blk), F32),
        ],
    )
    return pl.pallas_call(
        functools.partial(_moba_kernel, n_blocks=n_blocks),
        grid_spec=grid_spec,
        out_shape=jax.ShapeDtypeStruct((b, s, N_HEADS * HEAD_DIM), BF16),
        compiler_params=_cparams(3),
        name="moba",
    )(slopes, qkv, qkv, qkv)


def _dilated_kernel(bias_ref, q_ref, kc_ref, kp_ref, vc_ref, vp_ref, o_ref, lse_ref, lse_s):
    blk = DIL_BLOCK
    n = pl.program_id(2)
    band_row = lax.broadcasted_iota(jnp.int32, (2 * blk, blk), 0)
    prev_invalid = (band_row < blk) & (n == 0)
    lse_s[...] = jnp.zeros(lse_s.shape, F32)
    for h2 in range(N_SLABS):
        cols = slice(h2 * LANES, (h2 + 1) * LANES)
        qt = q_ref[0, 0, :, cols].astype(F32).T * (HEAD_DIM ** -0.5)
        feat = lax.broadcasted_iota(jnp.int32, qt.shape, 0)
        kband = jnp.concatenate([kp_ref[0, 0, :, cols], kc_ref[0, 0, :, cols]], axis=0)
        vband = jnp.concatenate([vp_ref[0, 0, :, cols], vc_ref[0, 0, :, cols]], axis=0)
        vt = vband.astype(F32).T.astype(BF16)
        outs = []
        for hh in range(HEADS_PER_SLAB):
            h = h2 * HEADS_PER_SLAB + hh
            in_head = (feat >= hh * HEAD_DIM) & (feat < (hh + 1) * HEAD_DIM)
            qt_h = jnp.where(in_head, qt, 0.0).astype(BF16)
            s = jnp.dot(kband, qt_h, preferred_element_type=F32) + bias_ref[h]
            s = jnp.where(prev_invalid, NEG, s)
            m = jnp.max(s, axis=0, keepdims=True)
            p = jnp.exp(s - m)
            l = jnp.sum(p, axis=0, keepdims=True)
            pv = jnp.dot(vt[hh * HEAD_DIM:(hh + 1) * HEAD_DIM], p.astype(BF16), preferred_element_type=F32)
            outs.append(pv / l)
            lse_s[pl.ds(h, 1), :] = m + jnp.log(l)
        o_ref[0, :, cols] = jnp.concatenate(outs, axis=0).T.astype(o_ref.dtype)
    lse_ref[0] = lse_s[...].T


def _dilated_bias(slopes, dil):
    blk = DIL_BLOCK
    steps = jnp.arange(blk)[None, :] - (jnp.arange(2 * blk)[:, None] - blk)
    steps_max = blk
    valid = (steps >= 0) & (steps <= steps_max)
    bias = -slopes[:, None, None] * (steps * dil).astype(F32)[None]
    return jnp.where(valid[None], bias, NEG).astype(F32)


def _proj_residue_kernel(x_ref, w_ref, o_ref):
    o_ref[0, 0] = jnp.dot(
        x_ref[0].astype(BF16), w_ref[...], preferred_element_type=F32
    ).astype(o_ref.dtype)


def _proj_residue(x, w, bsz, dil, tm=512, tn=1024):
    t, k = x.shape
    n = w.shape[1]
    l_sub = t // bsz // dil
    tm = min(tm, l_sub)
    view = x.reshape(bsz, l_sub, dil * k)
    return pl.pallas_call(
        _proj_residue_kernel,
        grid=(bsz, dil, l_sub // tm, n // tn),
        in_specs=[
            pl.BlockSpec((1, tm, k), lambda b, r, m, j: (b, m, r)),
            pl.BlockSpec((k, tn), lambda b, r, m, j: (0, j)),
        ],
        out_specs=pl.BlockSpec((1, 1, tm, tn), lambda b, r, m, j: (b, r, m, j)),
        out_shape=jax.ShapeDtypeStruct((bsz, dil, l_sub, n), BF16),
        compiler_params=_cparams(4),
        name=f"proj_residue_d{dil}",
    )(view, w)


def _dilated_group(qkv, slopes, g, window, dil):
    b, _, l_sub, _ = qkv.shape
    d = N_HEADS * HEAD_DIM
    blk = DIL_BLOCK
    assert window // dil == blk
    s = l_sub * dil
    nb = l_sub // blk
    bias = _dilated_bias(slopes, dil)

    def col(j):
        return lambda bi, r, n: (bi, r, n, j)

    def col_prev(j):
        return lambda bi, r, n: (bi, r, jnp.maximum(n - 1, 0), j)

    o, lse = pl.pallas_call(
        _dilated_kernel,
        grid=(b, dil, nb),
        in_specs=[
            pl.BlockSpec((N_HEADS, 2 * blk, blk), lambda bi, r, n: (0, 0, 0)),
            pl.BlockSpec((1, 1, blk, d), col(0)),
            pl.BlockSpec((1, 1, blk, d), col(1)),
            pl.BlockSpec((1, 1, blk, d), col_prev(1)),
            pl.BlockSpec((1, 1, blk, d), col(2)),
            pl.BlockSpec((1, 1, blk, d), col_prev(2)),
        ],
        out_specs=[
            pl.BlockSpec((1, blk, d), lambda bi, r, n: (bi, n, r)),
            pl.BlockSpec((1, blk, LANES), lambda bi, r, n: (bi, n, r)),
        ],
        out_shape=[
            jax.ShapeDtypeStruct((b, l_sub, dil * d), BF16),
            jax.ShapeDtypeStruct((b, l_sub, dil * LANES), F32),
        ],
        scratch_shapes=[pltpu.VMEM((LANES, blk), F32)],
        compiler_params=_cparams(3),
        name=f"dilated_g{g}",
    )(bias, qkv, qkv, qkv, qkv, qkv)
    return o.reshape(b, s, d), lse.reshape(b, s, LANES)


def _merge_kernel(e_ref, o1, o2, o3, l1, l2, l3, out_ref):
    ls = [l1[...], l2[...], l3[...]]
    mx = jnp.maximum(jnp.maximum(ls[0], ls[1]), ls[2])
    es = [jnp.exp(x - mx) for x in ls]
    den = es[0] + es[1] + es[2]
    acc = None
    for e, o in zip(es, (o1, o2, o3)):
        w = jnp.dot(e / den, e_ref[...], preferred_element_type=F32, precision=_HI)
        term = w * o[...].astype(F32)
        acc = term if acc is None else acc + term
    out_ref[...] = acc.astype(out_ref.dtype)


def _merge_groups(os, lses, tm=512):
    t, d = os[0].shape
    expand = (jnp.arange(LANES)[:, None] == (jnp.arange(d)[None, :] // HEAD_DIM)).astype(F32)
    row = lambda w: pl.BlockSpec((tm, w), lambda i: (i, 0))
    return pl.pallas_call(
        _merge_kernel,
        grid=(t // tm,),
        in_specs=[pl.BlockSpec((LANES, d), lambda i: (0, 0))] + [row(d)] * 3 + [row(LANES)] * 3,
        out_specs=row(d),
        out_shape=jax.ShapeDtypeStruct((t, d), BF16),
        compiler_params=_cparams(1),
        name="merge_groups",
    )(expand, *os, *lses)


def _layer_norm(y, g, b):
    mu = jnp.mean(y, axis=-1, keepdims=True)
    yc = y - mu
    var = jnp.mean(yc * yc, axis=-1, keepdims=True)
    return yc * lax.rsqrt(var + LN_EPS) * g + b


def _wo_ln_kernel(mix_ref, wo_ref, x_ref, g_ref, b_ref, o_ref):
    f = jnp.dot(mix_ref[...], wo_ref[...], preferred_element_type=F32)
    o_ref[...] = _layer_norm(DN_ALPHA * x_ref[...] + f, g_ref[...], b_ref[...])


def _wo_ln(mix, wo, x, g, b, tm=256):
    t, d = x.shape
    row = pl.BlockSpec((tm, d), lambda i: (i, 0))
    vec = pl.BlockSpec((1, d), lambda i: (0, 0))
    return pl.pallas_call(
        _wo_ln_kernel,
        grid=(t // tm,),
        in_specs=[row, pl.BlockSpec((d, d), lambda i: (0, 0)), row, vec, vec],
        out_specs=row,
        out_shape=jax.ShapeDtypeStruct((t, d), F32),
        compiler_params=_cparams(1),
        name="wo_ln",
    )(mix, wo, x, g.reshape(1, d), b.reshape(1, d))


def _router_kernel(wr_ref, rb_ref, x_ref, e_ref, w_ref, pos_ref, cnt_ref, run_ref, *, tm):
    step = pl.program_id(0)
    ne = N_EXPERTS
    gs = ne // N_GROUPS

    @pl.when(step == 0)
    def _():
        run_ref[...] = jnp.zeros(run_ref.shape, F32)

    logits = lax.dot_general(
        wr_ref[...], x_ref[...], (((1,), (1,)), ((), ())), preferred_element_type=F32, precision=_HI
    )
    scores = jax.nn.sigmoid(logits)
    choice = scores + rb_ref[...][:, :1]

    grp_rows = lax.broadcasted_iota(jnp.int32, (gs, tm), 0)
    blocks, gscore = [], []
    for gi in range(N_GROUPS):
        cb = choice[gi * gs:(gi + 1) * gs]
        m1 = jnp.max(cb, axis=0, keepdims=True)
        i1 = jnp.min(jnp.where(cb == m1, grp_rows, gs), axis=0, keepdims=True)
        m2 = jnp.max(jnp.where(grp_rows == i1, -jnp.inf, cb), axis=0, keepdims=True)
        blocks.append(cb)
        gscore.append(m1 + m2)
    masked = []
    for gi in range(N_GROUPS):
        beaten = jnp.zeros((1, tm), jnp.int32)
        for gj in range(N_GROUPS):
            if gj == gi:
                continue
            wins = (gscore[gj] > gscore[gi]) | ((gscore[gj] == gscore[gi]) & (gj < gi))
            beaten = beaten + wins.astype(jnp.int32)
        masked.append(jnp.where(beaten < TOPK_GROUPS, blocks[gi], NEG))
    cand = jnp.concatenate(masked, axis=0)

    rows = lax.broadcasted_iota(jnp.int32, (ne, tm), 0)
    member = jnp.zeros((ne, tm), F32)
    picks, raw_w = [], []
    for _ in range(TOP_K):
        cmax = jnp.max(cand, axis=0, keepdims=True)
        first = jnp.min(jnp.where(cand == cmax, rows, ne), axis=0, keepdims=True)
        hit = rows == first
        raw_w.append(jnp.sum(jnp.where(hit, scores, 0.0), axis=0, keepdims=True))
        member = jnp.where(hit, 1.0, member)
        cand = jnp.where(hit, -jnp.inf, cand)
        picks.append(first)
    wsum = raw_w[0]
    for r in raw_w[1:]:
        wsum = wsum + r

    tri = (lax.broadcasted_iota(jnp.int32, (tm, tm), 0) < lax.broadcasted_iota(jnp.int32, (tm, tm), 1))
    member_b = member.astype(BF16)
    before = jnp.dot(member_b, tri.astype(BF16), preferred_element_type=F32)
    run = run_ref[...]
    rank = before + jnp.concatenate([run] * (tm // LANES), axis=1)
    for k in range(TOP_K):
        hit = rows == picks[k]
        e_ref[pl.ds(k, 1), :] = picks[k]
        w_ref[pl.ds(k, 1), :] = raw_w[k] / wsum * ROUTED_SCALE
        pos_ref[pl.ds(k, 1), :] = jnp.sum(jnp.where(hit, rank, 0.0), axis=0, keepdims=True).astype(jnp.int32)
    run = run + jnp.dot(member_b, jnp.ones((tm, LANES), BF16), preferred_element_type=F32)
    run_ref[...] = run
    cnt_ref[...] = run


def _router(x, w_router, router_bias, tm=256):
    t, d = x.shape
    ne = N_EXPERTS
    wr_t = w_router.T
    rb = jnp.broadcast_to(router_bias.astype(F32)[:, None], (ne, LANES))
    tok = pl.BlockSpec((TOP_K, tm), lambda i: (0, i))
    return pl.pallas_call(
        functools.partial(_router_kernel, tm=tm),
        grid=(t // tm,),
        in_specs=[
            pl.BlockSpec((ne, d), lambda i: (0, 0)),
            pl.BlockSpec((ne, LANES), lambda i: (0, 0)),
            pl.BlockSpec((tm, d), lambda i: (i, 0)),
        ],
        out_specs=[tok, tok, tok, pl.BlockSpec((ne, LANES), lambda i: (0, 0))],
        out_shape=[
            jax.ShapeDtypeStruct((TOP_K, t), jnp.int32),
            jax.ShapeDtypeStruct((TOP_K, t), F32),
            jax.ShapeDtypeStruct((TOP_K, t), jnp.int32),
            jax.ShapeDtypeStruct((ne, LANES), F32),
        ],
        scratch_shapes=[pltpu.VMEM((ne, LANES), F32)],
        compiler_params=_cparams(1),
        name="router",
    )(wr_t, rb, x)


def _dispatch_kernel(dest_hbm, x_ref, xs_hbm, dest_smem, sem_idx, sem_rows, *, tm):
    step = pl.program_id(0)
    idx_copy = pltpu.make_async_copy(dest_hbm.at[step], dest_smem, sem_idx)
    idx_copy.start()
    idx_copy.wait()

    def row_copy(t, d):
        return pltpu.make_async_copy(x_ref.at[pl.ds(t, 1)], xs_hbm.at[pl.ds(d, 1)], sem_rows)

    def issue(t, c):
        for k in range(TOP_K):
            row_copy(t, dest_smem[k * tm + t]).start()
        return c

    lax.fori_loop(0, tm, issue, 0)

    def drain(t, c):
        for _ in range(TOP_K):
            row_copy(0, 0).wait()
        return c

    lax.fori_loop(0, tm, drain, 0)


def _dispatch(x, dest_tiles, n_rows, tm=256):
    t, d = x.shape
    return pl.pallas_call(
        functools.partial(_dispatch_kernel, tm=tm),
        grid=(t // tm,),
        in_specs=[pl.BlockSpec(memory_space=pl.ANY), pl.BlockSpec((tm, d), lambda i: (i, 0))],
        out_specs=pl.BlockSpec(memory_space=pl.ANY),
        out_shape=jax.ShapeDtypeStruct((n_rows, d), x.dtype),
        scratch_shapes=[
            pltpu.SMEM((TOP_K * tm,), jnp.int32),
            pltpu.SemaphoreType.DMA,
            pltpu.SemaphoreType.DMA,
        ],
        compiler_params=_cparams(1),
        name="dispatch",
    )(dest_tiles, x)


def _experts_kernel(blk_e_ref, blk_row_ref, n_real_ref, xs_ref, wg_ref, wu_ref, wd_ref, ys_ref):
    @pl.when(pl.program_id(0) < n_real_ref[0])
    def _():
        xb = xs_ref[...].astype(BF16)
        gate = jnp.dot(xb, wg_ref[0, 0].astype(BF16), preferred_element_type=F32)
        up = jnp.dot(xb, wu_ref[0, 0].astype(BF16), preferred_element_type=F32)
        h = (gate * jax.nn.sigmoid(gate) * up).astype(BF16)
        ys_ref[...] = jnp.dot(h, wd_ref[0, 0].astype(BF16), preferred_element_type=F32)


def _experts(xs, we_gate, we_up, we_down, layer, blk_e, blk_row, n_real):
    p, d = xs.shape
    f = we_gate.shape[-1]
    rb = EXPERT_ROW_BLOCK
    grid_spec = pltpu.PrefetchScalarGridSpec(
        num_scalar_prefetch=3,
        grid=(p // rb,),
        in_specs=[
            pl.BlockSpec((rb, d), lambda i, be, br, nr: (br[i], 0)),
            pl.BlockSpec((1, 1, d, f), lambda i, be, br, nr: (layer, be[i], 0, 0)),
            pl.BlockSpec((1, 1, d, f), lambda i, be, br, nr: (layer, be[i], 0, 0)),
            pl.BlockSpec((1, 1, f, d), lambda i, be, br, nr: (layer, be[i], 0, 0)),
        ],
        out_specs=pl.BlockSpec((rb, d), lambda i, be, br, nr: (br[i], 0)),
    )
    return pl.pallas_call(
        _experts_kernel,
        grid_spec=grid_spec,
        out_shape=jax.ShapeDtypeStruct((p, d), F32),
        compiler_params=_cparams(1),
        name="experts",
    )(blk_e, blk_row, n_real, xs, we_gate, we_up, we_down)


def _combine_kernel(dest_hbm, ys_hbm, x_ref, w_ref, sg_ref, su_ref, sd_ref, g_ref, b_ref, o_ref,
                    dest_smem, rows_ref, sem_idx, sem_rows, *, tm):
    step = pl.program_id(0)
    idx_copy = pltpu.make_async_copy(dest_hbm.at[step], dest_smem, sem_idx)
    idx_copy.start()
    idx_copy.wait()

    def row_copy(k, t, d):
        return pltpu.make_async_copy(ys_hbm.at[pl.ds(d, 1)], rows_ref.at[k, pl.ds(t, 1)], sem_rows)

    def issue(t, c):
        for k in range(TOP_K):
            row_copy(k, t, dest_smem[k * tm + t]).start()
        return c

    lax.fori_loop(0, tm, issue, 0)

    x = x_ref[...]
    xb = x.astype(BF16)
    gate = jnp.dot(xb, sg_ref[...], preferred_element_type=F32)
    up = jnp.dot(xb, su_ref[...], preferred_element_type=F32)
    h = (gate * jax.nn.sigmoid(gate) * up).astype(BF16)
    f = jnp.dot(h, sd_ref[...], preferred_element_type=F32)

    def drain(t, c):
        for _ in range(TOP_K):
            row_copy(0, 0, 0).wait()
        return c

    lax.fori_loop(0, tm, drain, 0)

    w = w_ref[...]
    for k in range(TOP_K):
        f = f + w[:, k:k + 1] * rows_ref[k]
    o_ref[...] = _layer_norm(DN_ALPHA * x + f, g_ref[...], b_ref[...])


def _combine(x, ys, dest_tiles, w_tok, ws_gate, ws_up, ws_down, g, b, tm=128):
    t, d = x.shape
    f = ws_gate.shape[-1]
    row = pl.BlockSpec((tm, d), lambda i: (i, 0))
    vec = pl.BlockSpec((1, d), lambda i: (0, 0))
    return pl.pallas_call(
        functools.partial(_combine_kernel, tm=tm),
        grid=(t // tm,),
        in_specs=[
            pl.BlockSpec(memory_space=pl.ANY),
            pl.BlockSpec(memory_space=pl.ANY),
            row,
            pl.BlockSpec((tm, TOP_K), lambda i: (i, 0)),
            pl.BlockSpec((d, f), lambda i: (0, 0)),
            pl.BlockSpec((d, f), lambda i: (0, 0)),
            pl.BlockSpec((f, d), lambda i: (0, 0)),
            vec,
            vec,
        ],
        out_specs=row,
        out_shape=jax.ShapeDtypeStruct((t, d), F32),
        scratch_shapes=[
            pltpu.SMEM((TOP_K * tm,), jnp.int32),
            pltpu.VMEM((TOP_K, tm, d), F32),
            pltpu.SemaphoreType.DMA,
            pltpu.SemaphoreType.DMA,
        ],
        compiler_params=_cparams(1),
        name="combine",
    )(dest_tiles, ys, x, w_tok, ws_gate, ws_up, ws_down, g.reshape(1, d), b.reshape(1, d))


def _tile_major(a, tm):
    k, t = a.shape
    return a.reshape(k, t // tm, tm).transpose(1, 0, 2).reshape(t // tm, k * tm)


def _dest_kernel(cnt_ref, e_ref, pos_ref, dest_ref, *, tm):
    ne = N_EXPERTS
    rb = float(EXPERT_ROW_BLOCK)
    padded = jnp.floor((cnt_ref[...] + (rb - 1.0)) * (1.0 / rb)) * rb
    lower = (lax.broadcasted_iota(jnp.int32, (ne, ne), 1) < lax.broadcasted_iota(jnp.int32, (ne, ne), 0))
    start = jnp.dot(lower.astype(F32), padded, preferred_element_type=F32, precision=_HI)
    start = jnp.concatenate([start] * (tm // LANES), axis=1)
    rows = lax.broadcasted_iota(jnp.int32, (ne, tm), 0)
    for k in range(TOP_K):
        hit = rows == e_ref[pl.ds(k, 1), :]
        first = jnp.sum(jnp.where(hit, start, 0.0), axis=0, keepdims=True)
        dest_ref[pl.ds(k, 1), :] = first.astype(jnp.int32) + pos_ref[pl.ds(k, 1), :]


def _dest(counts, top_e, pos, tm=512):
    t = top_e.shape[1]
    tok = pl.BlockSpec((TOP_K, tm), lambda i: (0, i))
    return pl.pallas_call(
        functools.partial(_dest_kernel, tm=tm),
        grid=(t // tm,),
        in_specs=[pl.BlockSpec((N_EXPERTS, LANES), lambda i: (0, 0)), tok, tok],
        out_specs=tok,
        out_shape=jax.ShapeDtypeStruct((TOP_K, t), jnp.int32),
        compiler_params=_cparams(1),
        name="dest",
    )(counts, top_e, pos)


def _moe_ln(x, layer, w_router, router_bias, we_gate, we_up, we_down, ws_gate, ws_up, ws_down, g, b):
    t, d = x.shape
    rb = EXPERT_ROW_BLOCK
    top_e, w, pos, counts = _router(x, w_router, router_bias)
    dest = _dest(counts, top_e, pos)

    cnt = counts[:, 0].astype(jnp.int32)
    pend = jnp.cumsum((cnt + rb - 1) // rb * rb)
    n_blocks = t * TOP_K // rb + N_EXPERTS
    n_real = (pend[-1] // rb).astype(jnp.int32)
    blk_ids = jnp.minimum(jnp.arange(n_blocks, dtype=jnp.int32), jnp.maximum(n_real - 1, 0))
    blk_e = jnp.sum((pend[None, :] <= (blk_ids * rb)[:, None]).astype(jnp.int32), axis=1)
    blk_e = jnp.minimum(blk_e, N_EXPERTS - 1)

    xs = _dispatch(x, _tile_major(dest, 256), n_blocks * rb, tm=256)
    ys = _experts(xs, we_gate, we_up, we_down, layer, blk_e, blk_ids, n_real.reshape(1))
    return _combine(
        x, ys, _tile_major(dest, 128), w.T, ws_gate.astype(BF16), ws_up.astype(BF16),
        ws_down.astype(BF16), g, b, tm=128,
    )


def kernel(x, w_qkv_a, w_qkv_b, w_o, ln_mix_g, ln_mix_b, w_router, router_bias, we_gate, we_up,
           we_down, ws_gate, ws_up, ws_down, ln_ffn_g, ln_ffn_b):
    bsz, s, d = x.shape
    t = bsz * s
    slopes = _alibi_slopes()
    xt = x.reshape(t, d)
    for i in range(DEPTH):
        if i % N_MIXERS == 0:
            qkv = _proj(xt, w_qkv_a[i // N_MIXERS].astype(BF16))
            mix = _moba_attention(qkv.reshape(bsz, s, -1), slopes).reshape(t, d)
        else:
            w_b = w_qkv_b[i // N_MIXERS].astype(BF16)
            outs, lses = [], []
            for g, (window, dil) in enumerate(DIL_GROUPS):
                qkv = _proj_residue(xt, w_b[:, g * 3 * d:(g + 1) * 3 * d], bsz, dil)
                o, lse = _dilated_group(qkv, slopes, g, window, dil)
                outs.append(o.reshape(t, d))
                lses.append(lse.reshape(t, LANES))
            mix = _merge_groups(outs, lses)
        x1 = _wo_ln(mix, w_o[i].astype(BF16), xt, ln_mix_g[i], ln_mix_b[i])
        xt = _moe_ln(x1, i, w_router[i], router_bias[i], we_gate, we_up, we_down,
                     ws_gate[i], ws_up[i], ws_down[i], ln_ffn_g[i], ln_ffn_b[i])
    return xt.reshape(bsz, s, d)
```

```python
import functools
import math

import jax
import jax.numpy as jnp
import numpy as np
from jax import lax
from jax.experimental import pallas as pl
from jax.experimental.pallas import tpu as pltpu

N_HEADS = 16
HEAD_DIM = 64
DEPTH = 2
N_MIXERS = 2
MOBA_BLOCK = 256
MOBA_TOPK = 3
MOBA_KEY_BLOCKS_PER_STEP = 4
MOBA_ACC_ROWS = 80
MOBA_BIAS_PARTS = 3
DIL_GROUPS = ((128, 1), (512, 4), (2048, 16))
DIL_BLOCK = 128
N_EXPERTS = 256
TOP_K = 8
N_GROUPS = 8
TOPK_GROUPS = 4
ROUTED_SCALE = 2.5
DN_ALPHA = (2 * DEPTH) ** 0.25
LN_EPS = 1e-5
NEG = -1e30

LANES = 128
HEADS_PER_SLAB = LANES // HEAD_DIM
N_SLABS = N_HEADS // HEADS_PER_SLAB

EXPERT_ROW_BLOCK = 256

BF16 = jnp.bfloat16
F32 = jnp.float32
_HI = lax.Precision.HIGHEST

_ARB = "arbitrary"


def _cparams(n_axes, vmem_mb=48):
    return pltpu.CompilerParams(
        dimension_semantics=(_ARB,) * n_axes, vmem_limit_bytes=vmem_mb * 1024 * 1024
    )


def _alibi_slopes():
    return 2.0 ** (-8.0 * jnp.arange(1, N_HEADS + 1, dtype=F32) / N_HEADS)


def _proj_kernel(x_ref, w_ref, o_ref):
    o_ref[...] = jnp.dot(
        x_ref[...].astype(BF16), w_ref[...], preferred_element_type=F32
    ).astype(o_ref.dtype)


def _proj(x, w, tm=512, tn=1024):
    t, k = x.shape
    n = w.shape[1]
    return pl.pallas_call(
        _proj_kernel,
        grid=(t // tm, n // tn),
        in_specs=[
            pl.BlockSpec((tm, k), lambda i, j: (i, 0)),
            pl.BlockSpec((k, tn), lambda i, j: (0, j)),
        ],
        out_specs=pl.BlockSpec((tm, tn), lambda i, j: (i, j)),
        out_shape=jax.ShapeDtypeStruct((t, n), BF16),
        compiler_params=_cparams(2),
        name="proj",
    )(x, w)


def _moba_kernel(slopes_ref, q_ref, k_ref, v_ref, wb_ref, o_ref, ka_ref, vt_ref, km_ref, sel_ref, s_ref,
                 *, n_blocks):
    blk = MOBA_BLOCK
    unroll = MOBA_KEY_BLOCKS_PER_STEP
    span = unroll * blk
    acc_rows = MOBA_ACC_ROWS
    h2 = pl.program_id(1)
    i = pl.program_id(2)
    heads = range(HEADS_PER_SLAB)

    @pl.when(i == 0)
    def _prepare_keys():
        lane = lax.broadcasted_iota(jnp.int32, (blk, LANES), 1)
        row_f = lax.broadcasted_iota(jnp.int32, (blk, LANES), 0).astype(F32)
        tail = lax.broadcasted_iota(jnp.int32, (acc_rows - HEAD_DIM, blk), 0)
        ones_row = jnp.where(tail == 0, 1.0, 0.0)

        def body(n, c):
            r0 = pl.multiple_of(n * blk, blk)
            vt = v_ref[0, pl.ds(r0, blk), :].astype(F32).T
            for hh in heads:
                vt_ref[n, hh] = jnp.concatenate(
                    [vt[hh * HEAD_DIM:(hh + 1) * HEAD_DIM], ones_row], axis=0).astype(BF16)
            kb = k_ref[0, pl.ds(r0, blk), :]
            km_ref[pl.ds(n, 1), :] = jnp.mean(kb.astype(F32), axis=0, keepdims=True)
            sub = (n % unroll).astype(F32)
            aux = jnp.where(lane < MOBA_BIAS_PARTS, sub, jnp.where(lane < 2 * MOBA_BIAS_PARTS, row_f, 0.0))
            ka_ref[pl.ds(r0, blk), :] = jnp.concatenate([kb, aux.astype(BF16)], axis=1)
            return c

        lax.fori_loop(0, n_blocks, body, 0)

    qt = q_ref[0].astype(F32).T
    feat = lax.broadcasted_iota(jnp.int32, qt.shape, 0)
    kmean = km_ref[...]
    km_lane = lax.broadcasted_iota(jnp.int32, kmean.shape, 1)
    blk_row = lax.broadcasted_iota(jnp.int32, (n_blocks, blk), 0)

    slopes = [slopes_ref[h2 * HEADS_PER_SLAB + hh] for hh in heads]
    qt_s = []
    for hh in heads:
        in_head = (feat >= hh * HEAD_DIM) & (feat < (hh + 1) * HEAD_DIM)
        qt_h = jnp.where(in_head, qt, 0.0)
        qt_s.append((qt_h * (HEAD_DIM ** -0.5)).astype(BF16))

        km_h = jnp.where((km_lane >= hh * HEAD_DIM) & (km_lane < (hh + 1) * HEAD_DIM), kmean, 0.0)
        gate = jnp.dot(km_h, qt_h, preferred_element_type=F32, precision=_HI)
        g = jnp.where(blk_row < i, gate, -jnp.inf)
        sel = jnp.zeros(g.shape, F32)
        for _ in range(MOBA_TOPK):
            gmax = jnp.max(g, axis=0, keepdims=True)
            first = jnp.min(jnp.where(g == gmax, blk_row, n_blocks), axis=0, keepdims=True)
            hit = blk_row == first
            sel = jnp.where(hit & (gmax > -jnp.inf), 1.0, sel)
            g = jnp.where(hit, -jnp.inf, g)
        sel_ref[hh] = sel

    w_all = jnp.concatenate(
        [jnp.concatenate(qt_s, axis=1), wb_ref[0]], axis=0)

    def scores(span_idx):
        r0 = pl.multiple_of(span_idx * span, span)
        return jnp.dot(ka_ref[pl.ds(r0, span), :], w_all, preferred_element_type=F32)

    key_row = lax.broadcasted_iota(jnp.int32, (blk, blk), 0)
    qry_col = lax.broadcasted_iota(jnp.int32, (blk, blk), 1)

    def consume(span_idx, state, last):
        j0 = span_idx * unroll
        base = (blk * (j0 - i)).astype(F32)
        out = []
        for hh in heads:
            m, acc = state[hh]
            c = slopes[hh] * base
            for u in range(unroll):
                j = j0 + u
                s = s_ref[u * blk:(u + 1) * blk, hh * blk:(hh + 1) * blk]
                picked = sel_ref[hh, pl.ds(j, 1), :]
                if last:
                    own = j == i
                    s = jnp.where(key_row - qry_col > jnp.where(own, 0, blk), NEG, s)
                    picked = picked + jnp.where(own, 1.0, 0.0)
                ch = picked > 0.5
                m_new = jnp.where(ch, jnp.maximum(m, jnp.max(s, axis=0, keepdims=True) + c), m)
                p = jnp.exp(s - jnp.where(ch, m_new - c, -NEG))
                acc = jnp.exp(m - m_new) * acc + jnp.dot(
                    vt_ref[j, hh], p.astype(BF16), preferred_element_type=F32)
                m = m_new
            out.append((m, acc))
        return tuple(out)

    n_full = i // unroll
    s_ref[...] = scores(0)

    def body(it, state):
        nxt = scores(it + 1)
        state = consume(it, state, last=False)
        s_ref[...] = nxt
        return state

    init = tuple((jnp.full((1, blk), NEG, F32), jnp.zeros((acc_rows, blk), F32)) for _ in heads)
    state = lax.fori_loop(0, n_full, body, init)
    state = consume(n_full, state, last=True)

    outs = [acc[:HEAD_DIM] / acc[HEAD_DIM:HEAD_DIM + 1] for _, acc in state]
    o_ref[0] = jnp.concatenate(outs, axis=0).T.astype(o_ref.dtype)


def _moba_attention(qkv, slopes):
    b, s, _ = qkv.shape
    blk = MOBA_BLOCK
    n_blocks = s // blk
    assert n_blocks % MOBA_KEY_BLOCKS_PER_STEP == 0
    wide = HEADS_PER_SLAB * blk
    grid_spec = pltpu.PrefetchScalarGridSpec(
        num_scalar_prefetch=1,
        grid=(b, N_SLABS, n_blocks),
        in_specs=[
            pl.BlockSpec((1, blk, LANES), lambda bi, h, i, sl: (bi, i, h)),
            pl.BlockSpec((1, s, LANES), lambda bi, h, i, sl: (bi, 0, N_SLABS + h)),
            pl.BlockSpec((1, s, LANES), lambda bi, h, i, sl: (bi, 0, 2 * N_SLABS + h)),
            pl.BlockSpec((1, LANES, wide), lambda bi, h, i, sl: (h, 0, 0)),
        ],
        out_specs=pl.BlockSpec((1, blk, LANES), lambda bi, h, i, sl: (bi, i, h)),
        scratch_shapes=[
            pltpu.VMEM((s, 2 * LANES), BF16),
            pltpu.VMEM((n_blocks, HEADS_PER_SLAB, MOBA_ACC_ROWS, blk), BF16),
            pltpu.VMEM((n_blocks, LANES), F32),
            pltpu.VMEM((HEADS_PER_SLAB, n_blocks, blk), F32),
            pltpu.VMEM((MOBA_KEY_BLOCKS_PER_STEP * blk, wide), F32),
        ],
    )
    return pl.pallas_call(
        functools.partial(_moba_kernel, n_blocks=n_blocks),
        grid_spec=grid_spec,
        out_shape=jax.ShapeDtypeStruct((b, s, N_HEADS * HEAD_DIM), BF16),
        compiler_params=_cparams(3),
        name="moba",
    )(slopes, qkv, qkv, qkv, _moba_slope_rows(slopes))


def _moba_slope_rows(slopes):
    blk = MOBA_BLOCK
    parts, rest = [], slopes
    for _ in range(MOBA_BIAS_PARTS):
        p = rest.astype(BF16).astype(F32)
        parts.append(p)
        rest = rest - p
    sp = jnp.stack(parts, axis=1)
    rows = jnp.concatenate([sp * blk, sp], axis=1).reshape(N_SLABS, HEADS_PER_SLAB, 2 * MOBA_BIAS_PARTS)
    wb = jnp.zeros((N_SLABS, LANES, HEADS_PER_SLAB * blk), F32)
    for hh in range(HEADS_PER_SLAB):
        wb = wb.at[:, :2 * MOBA_BIAS_PARTS, hh * blk:(hh + 1) * blk].set(rows[:, hh, :, None])
    return wb.astype(BF16)


def _dilated_kernel(bias_ref, q_ref, kc_ref, kp_ref, vc_ref, vp_ref, o_ref, lse_ref, lse_s):
    blk = DIL_BLOCK
    n = pl.program_id(2)
    band_row = lax.broadcasted_iota(jnp.int32, (2 * blk, blk), 0)
    prev_invalid = (band_row < blk) & (n == 0)
    lse_s[...] = jnp.zeros(lse_s.shape, F32)
    for h2 in range(N_SLABS):
        cols = slice(h2 * LANES, (h2 + 1) * LANES)
        qt = q_ref[0, 0, :, cols].astype(F32).T * (HEAD_DIM ** -0.5)
        feat = lax.broadcasted_iota(jnp.int32, qt.shape, 0)
        kband = jnp.concatenate([kp_ref[0, 0, :, cols], kc_ref[0, 0, :, cols]], axis=0)
        vband = jnp.concatenate([vp_ref[0, 0, :, cols], vc_ref[0, 0, :, cols]], axis=0)
        vt = vband.astype(F32).T.astype(BF16)
        qt_both = jnp.concatenate(
            [jnp.where((feat >= hh * HEAD_DIM) & (feat < (hh + 1) * HEAD_DIM), qt, 0.0)
             for hh in range(HEADS_PER_SLAB)], axis=1).astype(BF16)
        s_both = jnp.dot(kband, qt_both, preferred_element_type=F32)
        outs = []
        for hh in range(HEADS_PER_SLAB):
            h = h2 * HEADS_PER_SLAB + hh
            s = s_both[:, hh * blk:(hh + 1) * blk] + bias_ref[h]
            s = jnp.where(prev_invalid, NEG, s)
            m = jnp.max(s, axis=0, keepdims=True)
            p = jnp.exp(s - m)
            l = jnp.sum(p, axis=0, keepdims=True)
            pv = jnp.dot(vt[hh * HEAD_DIM:(hh + 1) * HEAD_DIM], p.astype(BF16), preferred_element_type=F32)
            outs.append(pv / l)
            lse_s[pl.ds(h, 1), :] = m + jnp.log(l)
        o_ref[0, :, cols] = jnp.concatenate(outs, axis=0).T.astype(o_ref.dtype)
    lse_ref[0] = lse_s[...].T


def _dilated_bias(slopes, dil):
    blk = DIL_BLOCK
    steps = jnp.arange(blk)[None, :] - (jnp.arange(2 * blk)[:, None] - blk)
    steps_max = blk
    valid = (steps >= 0) & (steps <= steps_max)
    bias = -slopes[:, None, None] * (steps * dil).astype(F32)[None]
    return jnp.where(valid[None], bias, NEG).astype(F32)


def _proj_residue_kernel(x_ref, w_ref, o_ref):
    o_ref[0, 0] = jnp.dot(
        x_ref[0].astype(BF16), w_ref[...], preferred_element_type=F32
    ).astype(o_ref.dtype)


def _proj_residue(x, w, bsz, dil, tm=512, tn=1024):
    t, k = x.shape
    n = w.shape[1]
    l_sub = t // bsz // dil
    tm = min(tm, l_sub)
    view = x.reshape(bsz, l_sub, dil * k)
    return pl.pallas_call(
        _proj_residue_kernel,
        grid=(bsz, dil, l_sub // tm, n // tn),
        in_specs=[
            pl.BlockSpec((1, tm, k), lambda b, r, m, j: (b, m, r)),
            pl.BlockSpec((k, tn), lambda b, r, m, j: (0, j)),
        ],
        out_specs=pl.BlockSpec((1, 1, tm, tn), lambda b, r, m, j: (b, r, m, j)),
        out_shape=jax.ShapeDtypeStruct((bsz, dil, l_sub, n), BF16),
        compiler_params=_cparams(4),
        name=f"proj_residue_d{dil}",
    )(view, w)


def _dilated_group(qkv, slopes, g, window, dil):
    b, _, l_sub, _ = qkv.shape
    d = N_HEADS * HEAD_DIM
    blk = DIL_BLOCK
    assert window // dil == blk
    s = l_sub * dil
    nb = l_sub // blk
    bias = _dilated_bias(slopes, dil)

    def col(j):
        return lambda bi, r, n: (bi, r, n, j)

    def col_prev(j):
        return lambda bi, r, n: (bi, r, jnp.maximum(n - 1, 0), j)

    o, lse = pl.pallas_call(
        _dilated_kernel,
        grid=(b, dil, nb),
        in_specs=[
            pl.BlockSpec((N_HEADS, 2 * blk, blk), lambda bi, r, n: (0, 0, 0)),
            pl.BlockSpec((1, 1, blk, d), col(0)),
            pl.BlockSpec((1, 1, blk, d), col(1)),
            pl.BlockSpec((1, 1, blk, d), col_prev(1)),
            pl.BlockSpec((1, 1, blk, d), col(2)),
            pl.BlockSpec((1, 1, blk, d), col_prev(2)),
        ],
        out_specs=[
            pl.BlockSpec((1, blk, d), lambda bi, r, n: (bi, n, r)),
            pl.BlockSpec((1, blk, LANES), lambda bi, r, n: (bi, n, r)),
        ],
        out_shape=[
            jax.ShapeDtypeStruct((b, l_sub, dil * d), BF16),
            jax.ShapeDtypeStruct((b, l_sub, dil * LANES), F32),
        ],
        scratch_shapes=[pltpu.VMEM((LANES, blk), F32)],
        compiler_params=_cparams(3),
        name=f"dilated_g{g}",
    )(bias, qkv, qkv, qkv, qkv, qkv)
    return o.reshape(b, s, d), lse.reshape(b, s, LANES)


def _merge_kernel(e_ref, o1, o2, o3, l1, l2, l3, out_ref):
    ls = [l1[...], l2[...], l3[...]]
    mx = jnp.maximum(jnp.maximum(ls[0], ls[1]), ls[2])
    es = [jnp.exp(x - mx) for x in ls]
    den = es[0] + es[1] + es[2]
    acc = None
    for e, o in zip(es, (o1, o2, o3)):
        w = jnp.dot(e / den, e_ref[...], preferred_element_type=F32, precision=_HI)
        term = w * o[...].astype(F32)
        acc = term if acc is None else acc + term
    out_ref[...] = acc.astype(out_ref.dtype)


def _merge_groups(os, lses, tm=512):
    t, d = os[0].shape
    expand = (jnp.arange(LANES)[:, None] == (jnp.arange(d)[None, :] // HEAD_DIM)).astype(F32)
    row = lambda w: pl.BlockSpec((tm, w), lambda i: (i, 0))
    return pl.pallas_call(
        _merge_kernel,
        grid=(t // tm,),
        in_specs=[pl.BlockSpec((LANES, d), lambda i: (0, 0))] + [row(d)] * 3 + [row(LANES)] * 3,
        out_specs=row(d),
        out_shape=jax.ShapeDtypeStruct((t, d), BF16),
        compiler_params=_cparams(1),
        name="merge_groups",
    )(expand, *os, *lses)


def _layer_norm(y, g, b):
    mu = jnp.mean(y, axis=-1, keepdims=True)
    yc = y - mu
    var = jnp.mean(yc * yc, axis=-1, keepdims=True)
    return yc * lax.rsqrt(var + LN_EPS) * g + b


def _wo_ln_kernel(mix_ref, wo_ref, x_ref, g_ref, b_ref, o_ref):
    f = jnp.dot(mix_ref[...], wo_ref[...], preferred_element_type=F32)
    o_ref[...] = _layer_norm(DN_ALPHA * x_ref[...] + f, g_ref[...], b_ref[...])


def _wo_ln(mix, wo, x, g, b, tm=256):
    t, d = x.shape
    row = pl.BlockSpec((tm, d), lambda i: (i, 0))
    vec = pl.BlockSpec((1, d), lambda i: (0, 0))
    return pl.pallas_call(
        _wo_ln_kernel,
        grid=(t // tm,),
        in_specs=[row, pl.BlockSpec((d, d), lambda i: (0, 0)), row, vec, vec],
        out_specs=row,
        out_shape=jax.ShapeDtypeStruct((t, d), F32),
        compiler_params=_cparams(1),
        name="wo_ln",
    )(mix, wo, x, g.reshape(1, d), b.reshape(1, d))


def _router_kernel(wr_ref, rb_ref, x_ref, e_ref, w_ref, pos_ref, cnt_ref, run_ref, *, tm):
    step = pl.program_id(0)
    ne = N_EXPERTS
    gs = ne // N_GROUPS

    @pl.when(step == 0)
    def _():
        run_ref[...] = jnp.zeros(run_ref.shape, F32)

    logits = lax.dot_general(
        wr_ref[...], x_ref[...], (((1,), (1,)), ((), ())), preferred_element_type=F32, precision=_HI
    )
    scores = jax.nn.sigmoid(logits)
    choice = scores + rb_ref[...][:, :1]

    grp_rows = lax.broadcasted_iota(jnp.int32, (gs, tm), 0)
    blocks, gscore = [], []
    for gi in range(N_GROUPS):
        cb = choice[gi * gs:(gi + 1) * gs]
        m1 = jnp.max(cb, axis=0, keepdims=True)
        i1 = jnp.min(jnp.where(cb == m1, grp_rows, gs), axis=0, keepdims=True)
        m2 = jnp.max(jnp.where(grp_rows == i1, -jnp.inf, cb), axis=0, keepdims=True)
        blocks.append(cb)
        gscore.append(m1 + m2)
    masked = []
    for gi in range(N_GROUPS):
        beaten = jnp.zeros((1, tm), jnp.int32)
        for gj in range(N_GROUPS):
            if gj == gi:
                continue
            wins = (gscore[gj] > gscore[gi]) | ((gscore[gj] == gscore[gi]) & (gj < gi))
            beaten = beaten + wins.astype(jnp.int32)
        masked.append(jnp.where(beaten < TOPK_GROUPS, blocks[gi], NEG))
    cand = jnp.concatenate(masked, axis=0)

    rows = lax.broadcasted_iota(jnp.int32, (ne, tm), 0)
    member = jnp.zeros((ne, tm), F32)
    picks, raw_w = [], []
    for _ in range(TOP_K):
        cmax = jnp.max(cand, axis=0, keepdims=True)
        first = jnp.min(jnp.where(cand == cmax, rows, ne), axis=0, keepdims=True)
        hit = rows == first
        raw_w.append(jnp.sum(jnp.where(hit, scores, 0.0), axis=0, keepdims=True))
        member = jnp.where(hit, 1.0, member)
        cand = jnp.where(hit, -jnp.inf, cand)
        picks.append(first)
    wsum = raw_w[0]
    for r in raw_w[1:]:
        wsum = wsum + r

    tri = (lax.broadcasted_iota(jnp.int32, (tm, tm), 0) < lax.broadcasted_iota(jnp.int32, (tm, tm), 1))
    member_b = member.astype(BF16)
    before = jnp.dot(member_b, tri.astype(BF16), preferred_element_type=F32)
    run = run_ref[...]
    rank = before + jnp.concatenate([run] * (tm // LANES), axis=1)
    for k in range(TOP_K):
        hit = rows == picks[k]
        e_ref[pl.ds(k, 1), :] = picks[k]
        w_ref[pl.ds(k, 1), :] = raw_w[k] / wsum * ROUTED_SCALE
        pos_ref[pl.ds(k, 1), :] = jnp.sum(jnp.where(hit, rank, 0.0), axis=0, keepdims=True).astype(jnp.int32)
    run = run + jnp.dot(member_b, jnp.ones((tm, LANES), BF16), preferred_element_type=F32)
    run_ref[...] = run
    cnt_ref[...] = run


def _router(x, w_router, router_bias, tm=256):
    t, d = x.shape
    ne = N_EXPERTS
    wr_t = w_router.T
    rb = jnp.broadcast_to(router_bias.astype(F32)[:, None], (ne, LANES))
    tok = pl.BlockSpec((TOP_K, tm), lambda i: (0, i))
    return pl.pallas_call(
        functools.partial(_router_kernel, tm=tm),
        grid=(t // tm,),
        in_specs=[
            pl.BlockSpec((ne, d), lambda i: (0, 0)),
            pl.BlockSpec((ne, LANES), lambda i: (0, 0)),
            pl.BlockSpec((tm, d), lambda i: (i, 0)),
        ],
        out_specs=[tok, tok, tok, pl.BlockSpec((ne, LANES), lambda i: (0, 0))],
        out_shape=[
            jax.ShapeDtypeStruct((TOP_K, t), jnp.int32),
            jax.ShapeDtypeStruct((TOP_K, t), F32),
            jax.ShapeDtypeStruct((TOP_K, t), jnp.int32),
            jax.ShapeDtypeStruct((ne, LANES), F32),
        ],
        scratch_shapes=[pltpu.VMEM((ne, LANES), F32)],
        compiler_params=_cparams(1),
        name="router",
    )(wr_t, rb, x)


def _dispatch_kernel(dest_hbm, x_ref, xs_hbm, dest_smem, sem_idx, sem_rows, *, tm):
    step = pl.program_id(0)
    idx_copy = pltpu.make_async_copy(dest_hbm.at[step], dest_smem, sem_idx)
    idx_copy.start()
    idx_copy.wait()

    def row_copy(t, d):
        return pltpu.make_async_copy(x_ref.at[pl.ds(t, 1)], xs_hbm.at[pl.ds(d, 1)], sem_rows)

    def issue(t, c):
        for k in range(TOP_K):
            row_copy(t, dest_smem[k * tm + t]).start()
        return c

    lax.fori_loop(0, tm, issue, 0)

    def drain(t, c):
        for _ in range(TOP_K):
            row_copy(0, 0).wait()
        return c

    lax.fori_loop(0, tm, drain, 0)


def _dispatch(x, dest_tiles, n_rows, tm=256):
    t, d = x.shape
    return pl.pallas_call(
        functools.partial(_dispatch_kernel, tm=tm),
        grid=(t // tm,),
        in_specs=[pl.BlockSpec(memory_space=pl.ANY), pl.BlockSpec((tm, d), lambda i: (i, 0))],
        out_specs=pl.BlockSpec(memory_space=pl.ANY),
        out_shape=jax.ShapeDtypeStruct((n_rows, d), x.dtype),
        scratch_shapes=[
            pltpu.SMEM((TOP_K * tm,), jnp.int32),
            pltpu.SemaphoreType.DMA,
            pltpu.SemaphoreType.DMA,
        ],
        compiler_params=_cparams(1),
        name="dispatch",
    )(dest_tiles, x)


def _experts_kernel(blk_e_ref, blk_row_ref, n_real_ref, xs_ref, wg_ref, wu_ref, wd_ref, ys_ref):
    @pl.when(pl.program_id(0) < n_real_ref[0])
    def _():
        xb = xs_ref[...].astype(BF16)
        gate = jnp.dot(xb, wg_ref[0, 0].astype(BF16), preferred_element_type=F32)
        up = jnp.dot(xb, wu_ref[0, 0].astype(BF16), preferred_element_type=F32)
        h = (gate * jax.nn.sigmoid(gate) * up).astype(BF16)
        ys_ref[...] = jnp.dot(h, wd_ref[0, 0].astype(BF16), preferred_element_type=F32)


def _experts(xs, we_gate, we_up, we_down, layer, blk_e, blk_row, n_real):
    p, d = xs.shape
    f = we_gate.shape[-1]
    rb = EXPERT_ROW_BLOCK
    grid_spec = pltpu.PrefetchScalarGridSpec(
        num_scalar_prefetch=3,
        grid=(p // rb,),
        in_specs=[
            pl.BlockSpec((rb, d), lambda i, be, br, nr: (br[i], 0)),
            pl.BlockSpec((1, 1, d, f), lambda i, be, br, nr: (layer, be[i], 0, 0)),
            pl.BlockSpec((1, 1, d, f), lambda i, be, br, nr: (layer, be[i], 0, 0)),
            pl.BlockSpec((1, 1, f, d), lambda i, be, br, nr: (layer, be[i], 0, 0)),
        ],
        out_specs=pl.BlockSpec((rb, d), lambda i, be, br, nr: (br[i], 0)),
    )
    return pl.pallas_call(
        _experts_kernel,
        grid_spec=grid_spec,
        out_shape=jax.ShapeDtypeStruct((p, d), F32),
        compiler_params=_cparams(1),
        name="experts",
    )(blk_e, blk_row, n_real, xs, we_gate, we_up, we_down)


def _combine_kernel(dest_hbm, ys_hbm, x_ref, w_ref, sg_ref, su_ref, sd_ref, g_ref, b_ref, o_ref,
                    dest_smem, rows_ref, sem_idx, sem_rows, *, tm):
    step = pl.program_id(0)
    idx_copy = pltpu.make_async_copy(dest_hbm.at[step], dest_smem, sem_idx)
    idx_copy.start()
    idx_copy.wait()

    def row_copy(k, t, d):
        return pltpu.make_async_copy(ys_hbm.at[pl.ds(d, 1)], rows_ref.at[k, pl.ds(t, 1)], sem_rows)

    def issue(t, c):
        for k in range(TOP_K):
            row_copy(k, t, dest_smem[k * tm + t]).start()
        return c

    lax.fori_loop(0, tm, issue, 0)

    x = x_ref[...]
    xb = x.astype(BF16)
    gate = jnp.dot(xb, sg_ref[...], preferred_element_type=F32)
    up = jnp.dot(xb, su_ref[...], preferred_element_type=F32)
    h = (gate * jax.nn.sigmoid(gate) * up).astype(BF16)
    f = jnp.dot(h, sd_ref[...], preferred_element_type=F32)

    def drain(t, c):
        for _ in range(TOP_K):
            row_copy(0, 0, 0).wait()
        return c

    lax.fori_loop(0, tm, drain, 0)

    w = w_ref[...]
    for k in range(TOP_K):
        f = f + w[:, k:k + 1] * rows_ref[k]
    o_ref[...] = _layer_norm(DN_ALPHA * x + f, g_ref[...], b_ref[...])


def _combine(x, ys, dest_tiles, w_tok, ws_gate, ws_up, ws_down, g, b, tm=128):
    t, d = x.shape
    f = ws_gate.shape[-1]
    row = pl.BlockSpec((tm, d), lambda i: (i, 0))
    vec = pl.BlockSpec((1, d), lambda i: (0, 0))
    return pl.pallas_call(
        functools.partial(_combine_kernel, tm=tm),
        grid=(t // tm,),
        in_specs=[
            pl.BlockSpec(memory_space=pl.ANY),
            pl.BlockSpec(memory_space=pl.ANY),
            row,
            pl.BlockSpec((tm, TOP_K), lambda i: (i, 0)),
            pl.BlockSpec((d, f), lambda i: (0, 0)),
            pl.BlockSpec((d, f), lambda i: (0, 0)),
            pl.BlockSpec((f, d), lambda i: (0, 0)),
            vec,
            vec,
        ],
        out_specs=row,
        out_shape=jax.ShapeDtypeStruct((t, d), F32),
        scratch_shapes=[
            pltpu.SMEM((TOP_K * tm,), jnp.int32),
            pltpu.VMEM((TOP_K, tm, d), F32),
            pltpu.SemaphoreType.DMA,
            pltpu.SemaphoreType.DMA,
        ],
        compiler_params=_cparams(1),
        name="combine",
    )(dest_tiles, ys, x, w_tok, ws_gate, ws_up, ws_down, g.reshape(1, d), b.reshape(1, d))


def _tile_major(a, tm):
    k, t = a.shape
    return a.reshape(k, t // tm, tm).transpose(1, 0, 2).reshape(t // tm, k * tm)


def _dest_kernel(cnt_ref, e_ref, pos_ref, dest_ref, *, tm):
    ne = N_EXPERTS
    rb = float(EXPERT_ROW_BLOCK)
    padded = jnp.floor((cnt_ref[...] + (rb - 1.0)) * (1.0 / rb)) * rb
    lower = (lax.broadcasted_iota(jnp.int32, (ne, ne), 1) < lax.broadcasted_iota(jnp.int32, (ne, ne), 0))
    start = jnp.dot(lower.astype(F32), padded, preferred_element_type=F32, precision=_HI)
    start = jnp.concatenate([start] * (tm // LANES), axis=1)
    rows = lax.broadcasted_iota(jnp.int32, (ne, tm), 0)
    for k in range(TOP_K):
        hit = rows == e_ref[pl.ds(k, 1), :]
        first = jnp.sum(jnp.where(hit, start, 0.0), axis=0, keepdims=True)
        dest_ref[pl.ds(k, 1), :] = first.astype(jnp.int32) + pos_ref[pl.ds(k, 1), :]


def _dest(counts, top_e, pos, tm=512):
    t = top_e.shape[1]
    tok = pl.BlockSpec((TOP_K, tm), lambda i: (0, i))
    return pl.pallas_call(
        functools.partial(_dest_kernel, tm=tm),
        grid=(t // tm,),
        in_specs=[pl.BlockSpec((N_EXPERTS, LANES), lambda i: (0, 0)), tok, tok],
        out_specs=tok,
        out_shape=jax.ShapeDtypeStruct((TOP_K, t), jnp.int32),
        compiler_params=_cparams(1),
        name="dest",
    )(counts, top_e, pos)


def _moe_ln(x, layer, w_router, router_bias, we_gate, we_up, we_down, ws_gate, ws_up, ws_down, g, b):
    t, d = x.shape
    rb = EXPERT_ROW_BLOCK
    top_e, w, pos, counts = _router(x, w_router, router_bias)
    dest = _dest(counts, top_e, pos)

    cnt = counts[:, 0].astype(jnp.int32)
    pend = jnp.cumsum((cnt + rb - 1) // rb * rb)
    n_blocks = t * TOP_K // rb + N_EXPERTS
    n_real = (pend[-1] // rb).astype(jnp.int32)
    blk_ids = jnp.minimum(jnp.arange(n_blocks, dtype=jnp.int32), jnp.maximum(n_real - 1, 0))
    blk_e = jnp.sum((pend[None, :] <= (blk_ids * rb)[:, None]).astype(jnp.int32), axis=1)
    blk_e = jnp.minimum(blk_e, N_EXPERTS - 1)

    xs = _dispatch(x, _tile_major(dest, 256), n_blocks * rb, tm=256)
    ys = _experts(xs, we_gate, we_up, we_down, layer, blk_e, blk_ids, n_real.reshape(1))
    return _combine(
        x, ys, _tile_major(dest, 128), w.T, ws_gate.astype(BF16), ws_up.astype(BF16),
        ws_down.astype(BF16), g, b, tm=128,
    )


def kernel(x, w_qkv_a, w_qkv_b, w_o, ln_mix_g, ln_mix_b, w_router, router_bias, we_gate, we_up,
           we_down, ws_gate, ws_up, ws_down, ln_ffn_g, ln_ffn_b):
    bsz, s, d = x.shape
    t = bsz * s
    slopes = _alibi_slopes()
    xt = x.reshape(t, d)
    for i in range(DEPTH):
        if i % N_MIXERS == 0:
            qkv = _proj(xt, w_qkv_a[i // N_MIXERS].astype(BF16))
            mix = _moba_attention(qkv.reshape(bsz, s, -1), slopes).reshape(t, d)
        else:
            w_b = w_qkv_b[i // N_MIXERS].astype(BF16)
            outs, lses = [], []
            for g, (window, dil) in enumerate(DIL_GROUPS):
                qkv = _proj_residue(xt, w_b[:, g * 3 * d:(g + 1) * 3 * d], bsz, dil)
                o, lse = _dilated_group(qkv, slopes, g, window, dil)
                outs.append(o.reshape(t, d))
                lses.append(lse.reshape(t, LANES))
            mix = _merge_groups(outs, lses)
        x1 = _wo_ln(mix, w_o[i].astype(BF16), xt, ln_mix_g[i], ln_mix_b[i])
        xt = _moe_ln(x1, i, w_router[i], router_bias[i], we_gate, we_up, we_down,
                     ws_gate[i], ws_up[i], ws_down[i], ln_ffn_g[i], ln_ffn_b[i])
    return xt.reshape(bsz, s, d)
```

```python
import functools
import math

import jax
import jax.numpy as jnp
import numpy as np
from jax import lax
from jax.experimental import pallas as pl
from jax.experimental.pallas import tpu as pltpu

N_HEADS = 16
HEAD_DIM = 64
DEPTH = 2
N_MIXERS = 2
MOBA_BLOCK = 256
MOBA_TOPK = 3
MOBA_KEY_BLOCKS_PER_STEP = 4
MOBA_ACC_ROWS = 80
MOBA_BIAS_PARTS = 3
DIL_GROUPS = ((128, 1), (512, 4), (2048, 16))
DIL_BLOCK = 128
N_EXPERTS = 256
TOP_K = 8
N_GROUPS = 8
TOPK_GROUPS = 4
ROUTED_SCALE = 2.5
DN_ALPHA = (2 * DEPTH) ** 0.25
LN_EPS = 1e-5
NEG = -1e30

LANES = 128
HEADS_PER_SLAB = LANES // HEAD_DIM
N_SLABS = N_HEADS // HEADS_PER_SLAB

EXPERT_ROW_BLOCK = 256
DISPATCH_TOKENS = 512
COMBINE_TOKENS = 128

BF16 = jnp.bfloat16
F32 = jnp.float32
_HI = lax.Precision.HIGHEST

_ARB = "arbitrary"


def _cparams(n_axes, vmem_mb=48):
    return pltpu.CompilerParams(
        dimension_semantics=(_ARB,) * n_axes, vmem_limit_bytes=vmem_mb * 1024 * 1024
    )


def _alibi_slopes():
    return 2.0 ** (-8.0 * jnp.arange(1, N_HEADS + 1, dtype=F32) / N_HEADS)


def _proj_kernel(x_ref, w_ref, o_ref):
    o_ref[...] = jnp.dot(
        x_ref[...].astype(BF16), w_ref[...], preferred_element_type=F32
    ).astype(o_ref.dtype)


def _proj(x, w, tm=512, tn=1024):
    t, k = x.shape
    n = w.shape[1]
    return pl.pallas_call(
        _proj_kernel,
        grid=(t // tm, n // tn),
        in_specs=[
            pl.BlockSpec((tm, k), lambda i, j: (i, 0)),
            pl.BlockSpec((k, tn), lambda i, j: (0, j)),
        ],
        out_specs=pl.BlockSpec((tm, tn), lambda i, j: (i, j)),
        out_shape=jax.ShapeDtypeStruct((t, n), BF16),
        compiler_params=_cparams(2),
        name="proj",
    )(x, w)


def _moba_kernel(slopes_ref, q_ref, k_ref, v_ref, wb_ref, o_ref, ka_ref, vt_ref, km_ref, sel_ref, s_ref,
                 *, n_blocks):
    blk = MOBA_BLOCK
    unroll = MOBA_KEY_BLOCKS_PER_STEP
    span = unroll * blk
    acc_rows = MOBA_ACC_ROWS
    h2 = pl.program_id(1)
    i = pl.program_id(2)
    heads = range(HEADS_PER_SLAB)

    @pl.when(i == 0)
    def _prepare_keys():
        lane = lax.broadcasted_iota(jnp.int32, (blk, LANES), 1)
        row_f = lax.broadcasted_iota(jnp.int32, (blk, LANES), 0).astype(F32)
        tail = lax.broadcasted_iota(jnp.int32, (acc_rows - HEAD_DIM, blk), 0)
        ones_row = jnp.where(tail == 0, 1.0, 0.0)

        def body(n, c):
            r0 = pl.multiple_of(n * blk, blk)
            vt = v_ref[0, pl.ds(r0, blk), :].astype(F32).T
            for hh in heads:
                vt_ref[n, hh] = jnp.concatenate(
                    [vt[hh * HEAD_DIM:(hh + 1) * HEAD_DIM], ones_row], axis=0).astype(BF16)
            kb = k_ref[0, pl.ds(r0, blk), :]
            km_ref[pl.ds(n, 1), :] = jnp.mean(kb.astype(F32), axis=0, keepdims=True)
            sub = jnp.asarray(n % unroll, F32)
            aux = jnp.where(lane < MOBA_BIAS_PARTS, sub, jnp.where(lane < 2 * MOBA_BIAS_PARTS, row_f, 0.0))
            ka_ref[pl.ds(r0, blk), :] = jnp.concatenate([kb, aux.astype(BF16)], axis=1)
            return c

        lax.fori_loop(0, n_blocks, body, 0)

    qt = q_ref[0].astype(F32).T
    feat = lax.broadcasted_iota(jnp.int32, qt.shape, 0)
    kmean = km_ref[...]
    km_lane = lax.broadcasted_iota(jnp.int32, kmean.shape, 1)
    blk_row = lax.broadcasted_iota(jnp.int32, (n_blocks, blk), 0)

    slopes = [slopes_ref[h2 * HEADS_PER_SLAB + hh] for hh in heads]
    qt_s = []
    for hh in heads:
        in_head = (feat >= hh * HEAD_DIM) & (feat < (hh + 1) * HEAD_DIM)
        qt_h = jnp.where(in_head, qt, 0.0)
        qt_s.append((qt_h * (HEAD_DIM ** -0.5)).astype(BF16))

        km_h = jnp.where((km_lane >= hh * HEAD_DIM) & (km_lane < (hh + 1) * HEAD_DIM), kmean, 0.0)
        gate = jnp.dot(km_h, qt_h, preferred_element_type=F32, precision=_HI)
        g = jnp.where(blk_row < i, gate, -jnp.inf)
        sel = jnp.zeros(g.shape, F32)
        for _ in range(MOBA_TOPK):
            gmax = jnp.max(g, axis=0, keepdims=True)
            first = jnp.min(jnp.where(g == gmax, blk_row, n_blocks), axis=0, keepdims=True)
            hit = blk_row == first
            sel = jnp.where(hit & (gmax > -jnp.inf), 1.0, sel)
            g = jnp.where(hit, -jnp.inf, g)
        sel_ref[hh] = sel

    w_all = jnp.concatenate(
        [jnp.concatenate(qt_s, axis=1), wb_ref[0]], axis=0)

    def scores(span_idx):
        r0 = pl.multiple_of(span_idx * span, span)
        return jnp.dot(ka_ref[pl.ds(r0, span), :], w_all, preferred_element_type=F32)

    key_row = lax.broadcasted_iota(jnp.int32, (blk, blk), 0)
    qry_col = lax.broadcasted_iota(jnp.int32, (blk, blk), 1)

    def consume(span_idx, state, last):
        j0 = span_idx * unroll
        base = jnp.asarray(blk * (j0 - i), F32)
        out = []
        for hh in heads:
            m, acc = state[hh]
            c = slopes[hh] * base
            for u in range(unroll):
                j = j0 + u
                s = s_ref[u * blk:(u + 1) * blk, hh * blk:(hh + 1) * blk]
                picked = sel_ref[hh, pl.ds(j, 1), :]
                if last:
                    own = j == i
                    s = jnp.where(key_row - qry_col > jnp.where(own, 0, blk), NEG, s)
                    picked = picked + jnp.where(own, 1.0, 0.0)
                ch = picked > 0.5
                m_new = jnp.where(ch, jnp.maximum(m, jnp.max(s, axis=0, keepdims=True) + c), m)
                p = jnp.exp(s - jnp.where(ch, m_new - c, -NEG))
                acc = jnp.exp(m - m_new) * acc + jnp.dot(
                    vt_ref[j, hh], p.astype(BF16), preferred_element_type=F32)
                m = m_new
            out.append((m, acc))
        return tuple(out)

    n_full = i // unroll
    s_ref[...] = scores(0)

    def body(it, state):
        nxt = scores(it + 1)
        state = consume(it, state, last=False)
        s_ref[...] = nxt
        return state

    init = tuple((jnp.full((1, blk), NEG, F32), jnp.zeros((acc_rows, blk), F32)) for _ in heads)
    state = lax.fori_loop(0, n_full, body, init)
    state = consume(n_full, state, last=True)

    outs = [acc[:HEAD_DIM] / acc[HEAD_DIM:HEAD_DIM + 1] for _, acc in state]
    o_ref[0] = jnp.concatenate(outs, axis=0).T.astype(o_ref.dtype)


def _moba_attention(qkv, slopes):
    b, s, _ = qkv.shape
    blk = MOBA_BLOCK
    n_blocks = s // blk
    assert n_blocks % MOBA_KEY_BLOCKS_PER_STEP == 0
    wide = HEADS_PER_SLAB * blk
    grid_spec = pltpu.PrefetchScalarGridSpec(
        num_scalar_prefetch=1,
        grid=(b, N_SLABS, n_blocks),
        in_specs=[
            pl.BlockSpec((1, blk, LANES), lambda bi, h, i, sl: (bi, i, h)),
            pl.BlockSpec((1, s, LANES), lambda bi, h, i, sl: (bi, 0, N_SLABS + h)),
            pl.BlockSpec((1, s, LANES), lambda bi, h, i, sl: (bi, 0, 2 * N_SLABS + h)),
            pl.BlockSpec((1, LANES, wide), lambda bi, h, i, sl: (h, 0, 0)),
        ],
        out_specs=pl.BlockSpec((1, blk, LANES), lambda bi, h, i, sl: (bi, i, h)),
        scratch_shapes=[
            pltpu.VMEM((s, 2 * LANES), BF16),
            pltpu.VMEM((n_blocks, HEADS_PER_SLAB, MOBA_ACC_ROWS, blk), BF16),
            pltpu.VMEM((n_blocks, LANES), F32),
            pltpu.VMEM((HEADS_PER_SLAB, n_blocks, blk), F32),
            pltpu.VMEM((MOBA_KEY_BLOCKS_PER_STEP * blk, wide), F32),
        ],
    )
    return pl.pallas_call(
        functools.partial(_moba_kernel, n_blocks=n_blocks),
        grid_spec=grid_spec,
        out_shape=jax.ShapeDtypeStruct((b, s, N_HEADS * HEAD_DIM), BF16),
        compiler_params=_cparams(3),
        name="moba",
    )(slopes, qkv, qkv, qkv, _moba_slope_rows(slopes))


def _moba_slope_rows(slopes):
    blk = MOBA_BLOCK
    parts, rest = [], slopes
    for _ in range(MOBA_BIAS_PARTS):
        p = rest.astype(BF16).astype(F32)
        parts.append(p)
        rest = rest - p
    sp = jnp.stack(parts, axis=1)
    rows = jnp.concatenate([sp * blk, sp], axis=1).reshape(N_SLABS, HEADS_PER_SLAB, 2 * MOBA_BIAS_PARTS)
    wb = jnp.zeros((N_SLABS, LANES, HEADS_PER_SLAB * blk), F32)
    for hh in range(HEADS_PER_SLAB):
        wb = wb.at[:, :2 * MOBA_BIAS_PARTS, hh * blk:(hh + 1) * blk].set(rows[:, hh, :, None])
    return wb.astype(BF16)


def _dilated_kernel(bias_ref, q_ref, kc_ref, kp_ref, vc_ref, vp_ref, o_ref, lse_ref, lse_s):
    blk = DIL_BLOCK
    n = pl.program_id(2)
    band_row = lax.broadcasted_iota(jnp.int32, (2 * blk, blk), 0)
    prev_invalid = (band_row < blk) & (n == 0)
    lse_s[...] = jnp.zeros(lse_s.shape, F32)
    for h2 in range(N_SLABS):
        cols = slice(h2 * LANES, (h2 + 1) * LANES)
        qt = q_ref[0, 0, :, cols].astype(F32).T * (HEAD_DIM ** -0.5)
        feat = lax.broadcasted_iota(jnp.int32, qt.shape, 0)
        kband = jnp.concatenate([kp_ref[0, 0, :, cols], kc_ref[0, 0, :, cols]], axis=0)
        vband = jnp.concatenate([vp_ref[0, 0, :, cols], vc_ref[0, 0, :, cols]], axis=0)
        vt = vband.astype(F32).T.astype(BF16)
        qt_both = jnp.concatenate(
            [jnp.where((feat >= hh * HEAD_DIM) & (feat < (hh + 1) * HEAD_DIM), qt, 0.0)
             for hh in range(HEADS_PER_SLAB)], axis=1).astype(BF16)
        s_both = jnp.dot(kband, qt_both, preferred_element_type=F32)
        outs = []
        for hh in range(HEADS_PER_SLAB):
            h = h2 * HEADS_PER_SLAB + hh
            s = s_both[:, hh * blk:(hh + 1) * blk] + bias_ref[h]
            s = jnp.where(prev_invalid, NEG, s)
            m = jnp.max(s, axis=0, keepdims=True)
            p = jnp.exp(s - m)
            l = jnp.sum(p, axis=0, keepdims=True)
            pv = jnp.dot(vt[hh * HEAD_DIM:(hh + 1) * HEAD_DIM], p.astype(BF16), preferred_element_type=F32)
            outs.append(pv / l)
            lse_s[pl.ds(h, 1), :] = m + jnp.log(l)
        o_ref[0, :, cols] = jnp.concatenate(outs, axis=0).T.astype(o_ref.dtype)
    lse_ref[0] = lse_s[...].T


def _dilated_bias(slopes, dil):
    blk = DIL_BLOCK
    steps = jnp.arange(blk)[None, :] - (jnp.arange(2 * blk)[:, None] - blk)
    steps_max = blk
    valid = (steps >= 0) & (steps <= steps_max)
    bias = -slopes[:, None, None] * (steps * dil).astype(F32)[None]
    return jnp.where(valid[None], bias, NEG).astype(F32)


def _proj_residue_kernel(x_ref, w_ref, o_ref):
    o_ref[0, 0] = jnp.dot(
        x_ref[0].astype(BF16), w_ref[...], preferred_element_type=F32
    ).astype(o_ref.dtype)


def _proj_residue(x, w, bsz, dil, tm=512, tn=1024):
    t, k = x.shape
    n = w.shape[1]
    l_sub = t // bsz // dil
    tm = min(tm, l_sub)
    view = x.reshape(bsz, l_sub, dil * k)
    return pl.pallas_call(
        _proj_residue_kernel,
        grid=(bsz, dil, l_sub // tm, n // tn),
        in_specs=[
            pl.BlockSpec((1, tm, k), lambda b, r, m, j: (b, m, r)),
            pl.BlockSpec((k, tn), lambda b, r, m, j: (0, j)),
        ],
        out_specs=pl.BlockSpec((1, 1, tm, tn), lambda b, r, m, j: (b, r, m, j)),
        out_shape=jax.ShapeDtypeStruct((bsz, dil, l_sub, n), BF16),
        compiler_params=_cparams(4),
        name=f"proj_residue_d{dil}",
    )(view, w)


def _dilated_group(qkv, slopes, g, window, dil):
    b, _, l_sub, _ = qkv.shape
    d = N_HEADS * HEAD_DIM
    blk = DIL_BLOCK
    assert window // dil == blk
    s = l_sub * dil
    nb = l_sub // blk
    bias = _dilated_bias(slopes, dil)

    def col(j):
        return lambda bi, r, n: (bi, r, n, j)

    def col_prev(j):
        return lambda bi, r, n: (bi, r, jnp.maximum(n - 1, 0), j)

    o, lse = pl.pallas_call(
        _dilated_kernel,
        grid=(b, dil, nb),
        in_specs=[
            pl.BlockSpec((N_HEADS, 2 * blk, blk), lambda bi, r, n: (0, 0, 0)),
            pl.BlockSpec((1, 1, blk, d), col(0)),
            pl.BlockSpec((1, 1, blk, d), col(1)),
            pl.BlockSpec((1, 1, blk, d), col_prev(1)),
            pl.BlockSpec((1, 1, blk, d), col(2)),
            pl.BlockSpec((1, 1, blk, d), col_prev(2)),
        ],
        out_specs=[
            pl.BlockSpec((1, blk, d), lambda bi, r, n: (bi, n, r)),
            pl.BlockSpec((1, blk, LANES), lambda bi, r, n: (bi, n, r)),
        ],
        out_shape=[
            jax.ShapeDtypeStruct((b, l_sub, dil * d), BF16),
            jax.ShapeDtypeStruct((b, l_sub, dil * LANES), F32),
        ],
        scratch_shapes=[pltpu.VMEM((LANES, blk), F32)],
        compiler_params=_cparams(3),
        name=f"dilated_g{g}",
    )(bias, qkv, qkv, qkv, qkv, qkv)
    return o.reshape(b, s, d), lse.reshape(b, s, LANES)


def _merge_kernel(e_ref, o1, o2, o3, l1, l2, l3, out_ref):
    ls = [l1[...], l2[...], l3[...]]
    mx = jnp.maximum(jnp.maximum(ls[0], ls[1]), ls[2])
    es = [jnp.exp(x - mx) for x in ls]
    den = es[0] + es[1] + es[2]
    acc = None
    for e, o in zip(es, (o1, o2, o3)):
        w = jnp.dot(e / den, e_ref[...], preferred_element_type=F32, precision=_HI)
        term = w * o[...].astype(F32)
        acc = term if acc is None else acc + term
    out_ref[...] = acc.astype(out_ref.dtype)


def _merge_groups(os, lses, tm=512):
    t, d = os[0].shape
    expand = (jnp.arange(LANES)[:, None] == (jnp.arange(d)[None, :] // HEAD_DIM)).astype(F32)
    row = lambda w: pl.BlockSpec((tm, w), lambda i: (i, 0))
    return pl.pallas_call(
        _merge_kernel,
        grid=(t // tm,),
        in_specs=[pl.BlockSpec((LANES, d), lambda i: (0, 0))] + [row(d)] * 3 + [row(LANES)] * 3,
        out_specs=row(d),
        out_shape=jax.ShapeDtypeStruct((t, d), BF16),
        compiler_params=_cparams(1),
        name="merge_groups",
    )(expand, *os, *lses)


def _layer_norm(y, g, b):
    mu = jnp.mean(y, axis=-1, keepdims=True)
    yc = y - mu
    var = jnp.mean(yc * yc, axis=-1, keepdims=True)
    return yc * lax.rsqrt(var + LN_EPS) * g + b


def _to_token_tiles(ref, y):
    n, d = y.shape
    nc = d // LANES
    for c in range(nc):
        ref[pl.ds(c, n, stride=nc), :] = y[:, c * LANES:(c + 1) * LANES]


def _from_token_tiles(ref, n, nc, lead=()):
    return jnp.concatenate([ref[lead + (pl.ds(c, n, stride=nc), slice(None))] for c in range(nc)], axis=1)


def _wo_ln_kernel(mix_ref, wo_ref, x_ref, g_ref, b_ref, o_ref, ot_ref):
    f = jnp.dot(mix_ref[...], wo_ref[...], preferred_element_type=F32)
    y = _layer_norm(DN_ALPHA * x_ref[...] + f, g_ref[...], b_ref[...])
    o_ref[...] = y
    _to_token_tiles(ot_ref, y)


def _wo_ln(mix, wo, x, g, b, tm=256):
    t, d = x.shape
    nc = d // LANES
    row = pl.BlockSpec((tm, d), lambda i: (i, 0))
    vec = pl.BlockSpec((1, d), lambda i: (0, 0))
    return pl.pallas_call(
        _wo_ln_kernel,
        grid=(t // tm,),
        in_specs=[row, pl.BlockSpec((d, d), lambda i: (0, 0)), row, vec, vec],
        out_specs=[row, pl.BlockSpec((tm * nc, LANES), lambda i: (i, 0))],
        out_shape=[jax.ShapeDtypeStruct((t, d), F32), jax.ShapeDtypeStruct((t * nc, LANES), F32)],
        compiler_params=_cparams(1),
        name="wo_ln",
    )(mix, wo, x, g.reshape(1, d), b.reshape(1, d))


def _router_kernel(wr_ref, rb_ref, x_ref, e_ref, w_ref, pos_ref, cnt_ref, run_ref, *, tm):
    step = pl.program_id(0)
    ne = N_EXPERTS
    gs = ne // N_GROUPS

    @pl.when(step == 0)
    def _():
        run_ref[...] = jnp.zeros(run_ref.shape, F32)

    logits = lax.dot_general(
        wr_ref[...], x_ref[...], (((1,), (1,)), ((), ())), preferred_element_type=F32, precision=_HI
    )
    scores = jax.nn.sigmoid(logits)
    choice = scores + rb_ref[...][:, :1]

    grp_rows = lax.broadcasted_iota(jnp.int32, (gs, tm), 0)
    blocks, gscore = [], []
    for gi in range(N_GROUPS):
        cb = choice[gi * gs:(gi + 1) * gs]
        m1 = jnp.max(cb, axis=0, keepdims=True)
        i1 = jnp.min(jnp.where(cb == m1, grp_rows, gs), axis=0, keepdims=True)
        m2 = jnp.max(jnp.where(grp_rows == i1, -jnp.inf, cb), axis=0, keepdims=True)
        blocks.append(cb)
        gscore.append(m1 + m2)
    masked = []
    for gi in range(N_GROUPS):
        beaten = jnp.zeros((1, tm), jnp.int32)
        for gj in range(N_GROUPS):
            if gj == gi:
                continue
            wins = (gscore[gj] > gscore[gi]) | ((gscore[gj] == gscore[gi]) & (gj < gi))
            beaten = beaten + wins.astype(jnp.int32)
        masked.append(jnp.where(beaten < TOPK_GROUPS, blocks[gi], NEG))
    cand = jnp.concatenate(masked, axis=0)

    rows = lax.broadcasted_iota(jnp.int32, (ne, tm), 0)
    member = jnp.zeros((ne, tm), F32)
    picks, raw_w = [], []
    for _ in range(TOP_K):
        cmax = jnp.max(cand, axis=0, keepdims=True)
        first = jnp.min(jnp.where(cand == cmax, rows, ne), axis=0, keepdims=True)
        hit = rows == first
        raw_w.append(jnp.sum(jnp.where(hit, scores, 0.0), axis=0, keepdims=True))
        member = jnp.where(hit, 1.0, member)
        cand = jnp.where(hit, -jnp.inf, cand)
        picks.append(first)
    wsum = raw_w[0]
    for r in raw_w[1:]:
        wsum = wsum + r

    tri = (lax.broadcasted_iota(jnp.int32, (tm, tm), 0) < lax.broadcasted_iota(jnp.int32, (tm, tm), 1))
    member_b = member.astype(BF16)
    before = jnp.dot(member_b, tri.astype(BF16), preferred_element_type=F32)
    run = run_ref[...]
    rank = before + jnp.concatenate([run] * (tm // LANES), axis=1)
    for k in range(TOP_K):
        hit = rows == picks[k]
        e_ref[pl.ds(k, 1), :] = picks[k]
        w_ref[pl.ds(k, 1), :] = raw_w[k] / wsum * ROUTED_SCALE
        pos_ref[pl.ds(k, 1), :] = jnp.sum(jnp.where(hit, rank, 0.0), axis=0, keepdims=True).astype(jnp.int32)
    run = run + jnp.dot(member_b, jnp.ones((tm, LANES), BF16), preferred_element_type=F32)
    run_ref[...] = run
    cnt_ref[...] = run


def _router(x, w_router, router_bias, tm=256):
    t, d = x.shape
    ne = N_EXPERTS
    wr_t = w_router.T
    rb = jnp.broadcast_to(router_bias.astype(F32)[:, None], (ne, LANES))
    tok = pl.BlockSpec((TOP_K, tm), lambda i: (0, i))
    return pl.pallas_call(
        functools.partial(_router_kernel, tm=tm),
        grid=(t // tm,),
        in_specs=[
            pl.BlockSpec((ne, d), lambda i: (0, 0)),
            pl.BlockSpec((ne, LANES), lambda i: (0, 0)),
            pl.BlockSpec((tm, d), lambda i: (i, 0)),
        ],
        out_specs=[tok, tok, tok, pl.BlockSpec((ne, LANES), lambda i: (0, 0))],
        out_shape=[
            jax.ShapeDtypeStruct((TOP_K, t), jnp.int32),
            jax.ShapeDtypeStruct((TOP_K, t), F32),
            jax.ShapeDtypeStruct((TOP_K, t), jnp.int32),
            jax.ShapeDtypeStruct((ne, LANES), F32),
        ],
        scratch_shapes=[pltpu.VMEM((ne, LANES), F32)],
        compiler_params=_cparams(1),
        name="router",
    )(wr_t, rb, x)


def _dispatch_kernel(dest_hbm, x_ref, xs_hbm, dest_smem, sem_idx, sem_rows, *, tm):
    step = pl.program_id(0)
    idx_copy = pltpu.make_async_copy(dest_hbm.at[step], dest_smem, sem_idx)
    idx_copy.start()
    idx_copy.wait()

    def row_copy(t, d):
        return pltpu.make_async_copy(x_ref.at[t], xs_hbm.at[d], sem_rows)

    def issue(t, c):
        for k in range(TOP_K):
            row_copy(t, dest_smem[k * tm + t]).start(priority=k % 2)
        return c

    lax.fori_loop(0, tm, issue, 0)

    def drain(t, c):
        for _ in range(TOP_K):
            row_copy(0, 0).wait()
        return c

    lax.fori_loop(0, tm, drain, 0)


def _dispatch(x_tiles, dest_tiles, n_rows, tm):
    t, nc, _ = x_tiles.shape
    return pl.pallas_call(
        functools.partial(_dispatch_kernel, tm=tm),
        grid=(t // tm,),
        in_specs=[pl.BlockSpec(memory_space=pl.ANY), pl.BlockSpec((tm, nc, LANES), lambda i: (i, 0, 0))],
        out_specs=pl.BlockSpec(memory_space=pl.ANY),
        out_shape=jax.ShapeDtypeStruct((n_rows, nc, LANES), x_tiles.dtype),
        scratch_shapes=[
            pltpu.SMEM((TOP_K * tm,), jnp.int32),
            pltpu.SemaphoreType.DMA,
            pltpu.SemaphoreType.DMA,
        ],
        compiler_params=_cparams(1),
        name="dispatch",
    )(dest_tiles, x_tiles)


def _experts_kernel(blk_e_ref, blk_row_ref, n_real_ref, xs_ref, wg_ref, wu_ref, wd_ref, ys_ref):
    @pl.when(pl.program_id(0) < n_real_ref[0])
    def _():
        rb = EXPERT_ROW_BLOCK
        nc = xs_ref.shape[0] // rb
        xb = _from_token_tiles(xs_ref, rb, nc).astype(BF16)
        gate = jnp.dot(xb, wg_ref[0, 0].astype(BF16), preferred_element_type=F32)
        up = jnp.dot(xb, wu_ref[0, 0].astype(BF16), preferred_element_type=F32)
        h = (gate * jax.nn.sigmoid(gate) * up).astype(BF16)
        _to_token_tiles(ys_ref, jnp.dot(h, wd_ref[0, 0].astype(BF16), preferred_element_type=F32))


def _experts(xs, we_gate, we_up, we_down, layer, blk_e, blk_row, n_real):
    d, f = we_gate.shape[-2:]
    nc = d // LANES
    rb = EXPERT_ROW_BLOCK
    p = xs.shape[0] // nc
    rows = pl.BlockSpec((rb * nc, LANES), lambda i, be, br, nr: (br[i], 0))
    grid_spec = pltpu.PrefetchScalarGridSpec(
        num_scalar_prefetch=3,
        grid=(p // rb,),
        in_specs=[
            rows,
            pl.BlockSpec((1, 1, d, f), lambda i, be, br, nr: (layer, be[i], 0, 0)),
            pl.BlockSpec((1, 1, d, f), lambda i, be, br, nr: (layer, be[i], 0, 0)),
            pl.BlockSpec((1, 1, f, d), lambda i, be, br, nr: (layer, be[i], 0, 0)),
        ],
        out_specs=rows,
    )
    return pl.pallas_call(
        _experts_kernel,
        grid_spec=grid_spec,
        out_shape=jax.ShapeDtypeStruct((p * nc, LANES), F32),
        compiler_params=_cparams(1),
        name="experts",
    )(blk_e, blk_row, n_real, xs, we_gate, we_up, we_down)


def _combine_kernel(dest_hbm, ys_hbm, x_ref, w_ref, sg_ref, su_ref, sd_ref, g_ref, b_ref, o_ref,
                    dest_smem, rows_ref, sem_idx, sem_rows, *, tm):
    step = pl.program_id(0)
    n_steps = pl.num_programs(0)
    nc = ys_hbm.shape[1]
    slot = step % 2

    def row_copy(sl, k, t, d):
        dst = rows_ref.at[sl, k, pl.ds(pl.multiple_of(t * nc, nc), nc)]
        return pltpu.make_async_copy(ys_hbm.at[d], dst, sem_rows.at[sl])

    def fetch(tile, sl):
        idx_copy = pltpu.make_async_copy(dest_hbm.at[tile], dest_smem.at[sl], sem_idx)
        idx_copy.start()
        idx_copy.wait()

        def issue(t, c):
            for k in range(TOP_K):
                row_copy(sl, k, t, dest_smem[sl, k * tm + t]).start(priority=k % 2)
            return c

        lax.fori_loop(0, tm, issue, 0)

    @pl.when(step == 0)
    def _():
        fetch(0, 0)

    @pl.when(step + 1 < n_steps)
    def _():
        fetch(step + 1, 1 - slot)

    x = x_ref[...]
    xb = x.astype(BF16)
    gate = jnp.dot(xb, sg_ref[...], preferred_element_type=F32)
    up = jnp.dot(xb, su_ref[...], preferred_element_type=F32)
    h = (gate * jax.nn.sigmoid(gate) * up).astype(BF16)
    f = jnp.dot(h, sd_ref[...], preferred_element_type=F32)

    def drain(t, c):
        for _ in range(TOP_K):
            row_copy(slot, 0, 0, 0).wait()
        return c

    lax.fori_loop(0, tm, drain, 0)

    w = w_ref[...]
    for k in range(TOP_K):
        f = f + w[:, k:k + 1] * _from_token_tiles(rows_ref, tm, nc, lead=(slot, k))
    o_ref[...] = _layer_norm(DN_ALPHA * x + f, g_ref[...], b_ref[...])


def _combine(x, ys, dest_tiles, w_tok, ws_gate, ws_up, ws_down, g, b, tm=128):
    t, d = x.shape
    nc = d // LANES
    f = ws_gate.shape[-1]
    row = pl.BlockSpec((tm, d), lambda i: (i, 0))
    vec = pl.BlockSpec((1, d), lambda i: (0, 0))
    return pl.pallas_call(
        functools.partial(_combine_kernel, tm=tm),
        grid=(t // tm,),
        in_specs=[
            pl.BlockSpec(memory_space=pl.ANY),
            pl.BlockSpec(memory_space=pl.ANY),
            row,
            pl.BlockSpec((tm, TOP_K), lambda i: (i, 0)),
            pl.BlockSpec((d, f), lambda i: (0, 0)),
            pl.BlockSpec((d, f), lambda i: (0, 0)),
            pl.BlockSpec((f, d), lambda i: (0, 0)),
            vec,
            vec,
        ],
        out_specs=row,
        out_shape=jax.ShapeDtypeStruct((t, d), F32),
        scratch_shapes=[
            pltpu.SMEM((2, TOP_K * tm), jnp.int32),
            pltpu.VMEM((2, TOP_K, tm * nc, LANES), F32),
            pltpu.SemaphoreType.DMA,
            pltpu.SemaphoreType.DMA((2,)),
        ],
        compiler_params=_cparams(1),
        name="combine",
    )(dest_tiles, ys, x, w_tok, ws_gate, ws_up, ws_down, g.reshape(1, d), b.reshape(1, d))


def _tile_major(a, tm):
    k, t = a.shape
    return a.reshape(k, t // tm, tm).transpose(1, 0, 2).reshape(t // tm, k * tm)


def _dest_kernel(cnt_ref, e_ref, pos_ref, dest_ref, *, tm):
    ne = N_EXPERTS
    rb = float(EXPERT_ROW_BLOCK)
    padded = jnp.floor((cnt_ref[...] + (rb - 1.0)) * (1.0 / rb)) * rb
    lower = (lax.broadcasted_iota(jnp.int32, (ne, ne), 1) < lax.broadcasted_iota(jnp.int32, (ne, ne), 0))
    start = jnp.dot(lower.astype(F32), padded, preferred_element_type=F32, precision=_HI)
    start = jnp.concatenate([start] * (tm // LANES), axis=1)
    rows = lax.broadcasted_iota(jnp.int32, (ne, tm), 0)
    for k in range(TOP_K):
        hit = rows == e_ref[pl.ds(k, 1), :]
        first = jnp.sum(jnp.where(hit, start, 0.0), axis=0, keepdims=True)
        dest_ref[pl.ds(k, 1), :] = first.astype(jnp.int32) + pos_ref[pl.ds(k, 1), :]


def _dest(counts, top_e, pos, tm=512):
    t = top_e.shape[1]
    tok = pl.BlockSpec((TOP_K, tm), lambda i: (0, i))
    return pl.pallas_call(
        functools.partial(_dest_kernel, tm=tm),
        grid=(t // tm,),
        in_specs=[pl.BlockSpec((N_EXPERTS, LANES), lambda i: (0, 0)), tok, tok],
        out_specs=tok,
        out_shape=jax.ShapeDtypeStruct((TOP_K, t), jnp.int32),
        compiler_params=_cparams(1),
        name="dest",
    )(counts, top_e, pos)


def _moe_ln(x, x_tiles, layer, w_router, router_bias, we_gate, we_up, we_down, ws_gate, ws_up, ws_down,
            g, b):
    t, d = x.shape
    rb = EXPERT_ROW_BLOCK
    top_e, w, pos, counts = _router(x, w_router, router_bias)
    dest = _dest(counts, top_e, pos)

    cnt = counts[:, 0].astype(jnp.int32)
    pend = jnp.cumsum((cnt + rb - 1) // rb * rb)
    n_blocks = t * TOP_K // rb + N_EXPERTS
    n_real = (pend[-1] // rb).astype(jnp.int32)
    blk_ids = jnp.minimum(jnp.arange(n_blocks, dtype=jnp.int32), jnp.maximum(n_real - 1, 0))
    blk_e = jnp.sum((pend[None, :] <= (blk_ids * rb)[:, None]).astype(jnp.int32), axis=1)
    blk_e = jnp.minimum(blk_e, N_EXPERTS - 1)

    nc = d // LANES
    p = n_blocks * rb
    xs = _dispatch(x_tiles.reshape(t, nc, LANES), _tile_major(dest, DISPATCH_TOKENS), p, tm=DISPATCH_TOKENS)
    ys = _experts(xs.reshape(p * nc, LANES), we_gate, we_up, we_down, layer, blk_e, blk_ids, n_real.reshape(1))
    return _combine(
        x, ys.reshape(p, nc, LANES), _tile_major(dest, COMBINE_TOKENS), w.T, ws_gate.astype(BF16),
        ws_up.astype(BF16), ws_down.astype(BF16), g, b, tm=COMBINE_TOKENS,
    )


def kernel(x, w_qkv_a, w_qkv_b, w_o, ln_mix_g, ln_mix_b, w_router, router_bias, we_gate, we_up,
           we_down, ws_gate, ws_up, ws_down, ln_ffn_g, ln_ffn_b):
    bsz, s, d = x.shape
    t = bsz * s
    slopes = _alibi_slopes()
    xt = x.reshape(t, d)
    for i in range(DEPTH):
        if i % N_MIXERS == 0:
            qkv = _proj(xt, w_qkv_a[i // N_MIXERS].astype(BF16))
            mix = _moba_attention(qkv.reshape(bsz, s, -1), slopes).reshape(t, d)
        else:
            w_b = w_qkv_b[i // N_MIXERS].astype(BF16)
            outs, lses = [], []
            for g, (window, dil) in enumerate(DIL_GROUPS):
                qkv = _proj_residue(xt, w_b[:, g * 3 * d:(g + 1) * 3 * d], bsz, dil)
                o, lse = _dilated_group(qkv, slopes, g, window, dil)
                outs.append(o.reshape(t, d))
                lses.append(lse.reshape(t, LANES))
            mix = _merge_groups(outs, lses)
        x1, x1_tiles = _wo_ln(mix, w_o[i].astype(BF16), xt, ln_mix_g[i], ln_mix_b[i])
        xt = _moe_ln(x1, x1_tiles, i, w_router[i], router_bias[i], we_gate, we_up, we_down,
                     ws_gate[i], ws_up[i], ws_down[i], ln_ffn_g[i], ln_ffn_b[i])
    return xt.reshape(bsz, s, d)
```

```python
import functools
import math

import jax
import jax.numpy as jnp
import numpy as np
from jax import lax
from jax.experimental import pallas as pl
from jax.experimental.pallas import tpu as pltpu

N_HEADS = 16
HEAD_DIM = 64
DEPTH = 2
N_MIXERS = 2
MOBA_BLOCK = 256
MOBA_TOPK = 3
MOBA_KEY_BLOCKS_PER_STEP = 4
MOBA_ACC_ROWS = 80
MOBA_BIAS_PARTS = 3
DIL_GROUPS = ((128, 1), (512, 4), (2048, 16))
DIL_BLOCK = 128
N_EXPERTS = 256
TOP_K = 8
N_GROUPS = 8
TOPK_GROUPS = 4
ROUTED_SCALE = 2.5
DN_ALPHA = (2 * DEPTH) ** 0.25
LN_EPS = 1e-5
NEG = -1e30

LANES = 128
HEADS_PER_SLAB = LANES // HEAD_DIM
N_SLABS = N_HEADS // HEADS_PER_SLAB

EXPERT_ROW_BLOCK = 256
DISPATCH_TOKENS = 512
COMBINE_TOKENS = 128

BF16 = jnp.bfloat16
F32 = jnp.float32
_HI = lax.Precision.HIGHEST

_ARB = "arbitrary"


def _cparams(n_axes, vmem_mb=48):
    return pltpu.CompilerParams(
        dimension_semantics=(_ARB,) * n_axes, vmem_limit_bytes=vmem_mb * 1024 * 1024
    )


def _alibi_slopes():
    return 2.0 ** (-8.0 * jnp.arange(1, N_HEADS + 1, dtype=F32) / N_HEADS)


def _proj_kernel(x_ref, w_ref, o_ref):
    o_ref[...] = jnp.dot(
        x_ref[...].astype(BF16), w_ref[...], preferred_element_type=F32
    ).astype(o_ref.dtype)


def _proj(x, w, tm=512, tn=1024):
    t, k = x.shape
    n = w.shape[1]
    return pl.pallas_call(
        _proj_kernel,
        grid=(t // tm, n // tn),
        in_specs=[
            pl.BlockSpec((tm, k), lambda i, j: (i, 0)),
            pl.BlockSpec((k, tn), lambda i, j: (0, j)),
        ],
        out_specs=pl.BlockSpec((tm, tn), lambda i, j: (i, j)),
        out_shape=jax.ShapeDtypeStruct((t, n), BF16),
        compiler_params=_cparams(2),
        name="proj",
    )(x, w)


def _moba_kernel(slopes_ref, q_ref, k_ref, v_ref, wb_ref, o_ref, ka_ref, vt_ref, km_ref, sel_ref, s_ref,
                 *, n_blocks):
    blk = MOBA_BLOCK
    unroll = MOBA_KEY_BLOCKS_PER_STEP
    span = unroll * blk
    acc_rows = MOBA_ACC_ROWS
    h2 = pl.program_id(1)
    i = pl.program_id(2)
    heads = range(HEADS_PER_SLAB)

    @pl.when(i == 0)
    def _prepare_keys():
        lane = lax.broadcasted_iota(jnp.int32, (blk, LANES), 1)
        row_f = lax.broadcasted_iota(jnp.int32, (blk, LANES), 0).astype(F32)
        tail = lax.broadcasted_iota(jnp.int32, (acc_rows - HEAD_DIM, blk), 0)
        ones_row = jnp.where(tail == 0, 1.0, 0.0)

        def body(n, c):
            r0 = pl.multiple_of(n * blk, blk)
            vt = v_ref[0, pl.ds(r0, blk), :].astype(F32).T
            for hh in heads:
                vt_ref[n, hh] = jnp.concatenate(
                    [vt[hh * HEAD_DIM:(hh + 1) * HEAD_DIM], ones_row], axis=0).astype(BF16)
            kb = k_ref[0, pl.ds(r0, blk), :]
            km_ref[pl.ds(n, 1), :] = jnp.mean(kb.astype(F32), axis=0, keepdims=True)
            sub = jnp.asarray(n % unroll, F32)
            aux = jnp.where(lane < MOBA_BIAS_PARTS, sub, jnp.where(lane < 2 * MOBA_BIAS_PARTS, row_f, 0.0))
            ka_ref[pl.ds(r0, blk), :] = jnp.concatenate([kb, aux.astype(BF16)], axis=1)
            return c

        lax.fori_loop(0, n_blocks, body, 0)

    qt = q_ref[0].astype(F32).T
    feat = lax.broadcasted_iota(jnp.int32, qt.shape, 0)
    kmean = km_ref[...]
    km_lane = lax.broadcasted_iota(jnp.int32, kmean.shape, 1)
    blk_row = lax.broadcasted_iota(jnp.int32, (n_blocks, blk), 0)

    slopes = [slopes_ref[h2 * HEADS_PER_SLAB + hh] for hh in heads]
    qt_s = []
    for hh in heads:
        in_head = (feat >= hh * HEAD_DIM) & (feat < (hh + 1) * HEAD_DIM)
        qt_h = jnp.where(in_head, qt, 0.0)
        qt_s.append((qt_h * (HEAD_DIM ** -0.5)).astype(BF16))

        km_h = jnp.where((km_lane >= hh * HEAD_DIM) & (km_lane < (hh + 1) * HEAD_DIM), kmean, 0.0)
        gate = jnp.dot(km_h, qt_h, preferred_element_type=F32, precision=_HI)
        g = jnp.where(blk_row < i, gate, -jnp.inf)
        sel = jnp.zeros(g.shape, F32)
        for _ in range(MOBA_TOPK):
            gmax = jnp.max(g, axis=0, keepdims=True)
            first = jnp.min(jnp.where(g == gmax, blk_row, n_blocks), axis=0, keepdims=True)
            hit = blk_row == first
            sel = jnp.where(hit & (gmax > -jnp.inf), 1.0, sel)
            g = jnp.where(hit, -jnp.inf, g)
        sel_ref[hh] = sel

    w_all = jnp.concatenate(
        [jnp.concatenate(qt_s, axis=1), wb_ref[0]], axis=0)

    def scores(span_idx):
        r0 = pl.multiple_of(span_idx * span, span)
        return jnp.dot(ka_ref[pl.ds(r0, span), :], w_all, preferred_element_type=F32)

    key_row = lax.broadcasted_iota(jnp.int32, (blk, blk), 0)
    qry_col = lax.broadcasted_iota(jnp.int32, (blk, blk), 1)

    def consume(span_idx, state, last):
        j0 = span_idx * unroll
        base = jnp.asarray(blk * (j0 - i), F32)
        out = []
        for hh in heads:
            m, acc = state[hh]
            c = slopes[hh] * base
            for u in range(unroll):
                j = j0 + u
                s = s_ref[u * blk:(u + 1) * blk, hh * blk:(hh + 1) * blk]
                picked = sel_ref[hh, pl.ds(j, 1), :]
                if last:
                    own = j == i
                    s = jnp.where(key_row - qry_col > jnp.where(own, 0, blk), NEG, s)
                    picked = picked + jnp.where(own, 1.0, 0.0)
                ch = picked > 0.5
                m_new = jnp.where(ch, jnp.maximum(m, jnp.max(s, axis=0, keepdims=True) + c), m)
                p = jnp.exp(s - jnp.where(ch, m_new - c, -NEG))
                acc = jnp.exp(m - m_new) * acc + jnp.dot(
                    vt_ref[j, hh], p.astype(BF16), preferred_element_type=F32)
                m = m_new
            out.append((m, acc))
        return tuple(out)

    n_full = i // unroll
    s_ref[...] = scores(0)

    def body(it, state):
        nxt = scores(it + 1)
        state = consume(it, state, last=False)
        s_ref[...] = nxt
        return state

    init = tuple((jnp.full((1, blk), NEG, F32), jnp.zeros((acc_rows, blk), F32)) for _ in heads)
    state = lax.fori_loop(0, n_full, body, init)
    state = consume(n_full, state, last=True)

    outs = [acc[:HEAD_DIM] / acc[HEAD_DIM:HEAD_DIM + 1] for _, acc in state]
    o_ref[0] = jnp.concatenate(outs, axis=0).T.astype(o_ref.dtype)


def _moba_attention(qkv, slopes):
    b, s, _ = qkv.shape
    blk = MOBA_BLOCK
    n_blocks = s // blk
    assert n_blocks % MOBA_KEY_BLOCKS_PER_STEP == 0
    wide = HEADS_PER_SLAB * blk
    grid_spec = pltpu.PrefetchScalarGridSpec(
        num_scalar_prefetch=1,
        grid=(b, N_SLABS, n_blocks),
        in_specs=[
            pl.BlockSpec((1, blk, LANES), lambda bi, h, i, sl: (bi, i, h)),
            pl.BlockSpec((1, s, LANES), lambda bi, h, i, sl: (bi, 0, N_SLABS + h)),
            pl.BlockSpec((1, s, LANES), lambda bi, h, i, sl: (bi, 0, 2 * N_SLABS + h)),
            pl.BlockSpec((1, LANES, wide), lambda bi, h, i, sl: (h, 0, 0)),
        ],
        out_specs=pl.BlockSpec((1, blk, LANES), lambda bi, h, i, sl: (bi, i, h)),
        scratch_shapes=[
            pltpu.VMEM((s, 2 * LANES), BF16),
            pltpu.VMEM((n_blocks, HEADS_PER_SLAB, MOBA_ACC_ROWS, blk), BF16),
            pltpu.VMEM((n_blocks, LANES), F32),
            pltpu.VMEM((HEADS_PER_SLAB, n_blocks, blk), F32),
            pltpu.VMEM((MOBA_KEY_BLOCKS_PER_STEP * blk, wide), F32),
        ],
    )
    return pl.pallas_call(
        functools.partial(_moba_kernel, n_blocks=n_blocks),
        grid_spec=grid_spec,
        out_shape=jax.ShapeDtypeStruct((b, s, N_HEADS * HEAD_DIM), BF16),
        compiler_params=_cparams(3),
        name="moba",
    )(slopes, qkv, qkv, qkv, _moba_slope_rows(slopes))


def _moba_slope_rows(slopes):
    blk = MOBA_BLOCK
    parts, rest = [], slopes
    for _ in range(MOBA_BIAS_PARTS):
        p = rest.astype(BF16).astype(F32)
        parts.append(p)
        rest = rest - p
    sp = jnp.stack(parts, axis=1)
    rows = jnp.concatenate([sp * blk, sp], axis=1).reshape(N_SLABS, HEADS_PER_SLAB, 2 * MOBA_BIAS_PARTS)
    wb = jnp.zeros((N_SLABS, LANES, HEADS_PER_SLAB * blk), F32)
    for hh in range(HEADS_PER_SLAB):
        wb = wb.at[:, :2 * MOBA_BIAS_PARTS, hh * blk:(hh + 1) * blk].set(rows[:, hh, :, None])
    return wb.astype(BF16)


def _dilated_kernel(bias_ref, q_ref, kc_ref, kp_ref, vc_ref, vp_ref, o_ref, lse_ref, lse_s):
    blk = DIL_BLOCK
    n = pl.program_id(2)
    band_row = lax.broadcasted_iota(jnp.int32, (2 * blk, blk), 0)
    prev_invalid = (band_row < blk) & (n == 0)
    lse_s[...] = jnp.zeros(lse_s.shape, F32)
    for h2 in range(N_SLABS):
        cols = slice(h2 * LANES, (h2 + 1) * LANES)
        qt = q_ref[0, 0, :, cols].astype(F32).T * (HEAD_DIM ** -0.5)
        feat = lax.broadcasted_iota(jnp.int32, qt.shape, 0)
        kband = jnp.concatenate([kp_ref[0, 0, :, cols], kc_ref[0, 0, :, cols]], axis=0)
        vband = jnp.concatenate([vp_ref[0, 0, :, cols], vc_ref[0, 0, :, cols]], axis=0)
        vt = vband.astype(F32).T.astype(BF16)
        qt_both = jnp.concatenate(
            [jnp.where((feat >= hh * HEAD_DIM) & (feat < (hh + 1) * HEAD_DIM), qt, 0.0)
             for hh in range(HEADS_PER_SLAB)], axis=1).astype(BF16)
        s_both = jnp.dot(kband, qt_both, preferred_element_type=F32)
        outs = []
        for hh in range(HEADS_PER_SLAB):
            h = h2 * HEADS_PER_SLAB + hh
            s = s_both[:, hh * blk:(hh + 1) * blk] + bias_ref[h]
            s = jnp.where(prev_invalid, NEG, s)
            m = jnp.max(s, axis=0, keepdims=True)
            p = jnp.exp(s - m)
            l = jnp.sum(p, axis=0, keepdims=True)
            pv = jnp.dot(vt[hh * HEAD_DIM:(hh + 1) * HEAD_DIM], p.astype(BF16), preferred_element_type=F32)
            outs.append(pv / l)
            lse_s[pl.ds(h, 1), :] = m + jnp.log(l)
        o_ref[0, :, cols] = jnp.concatenate(outs, axis=0).T.astype(o_ref.dtype)
    lse_ref[0] = lse_s[...].T


def _dilated_bias(slopes, dil):
    blk = DIL_BLOCK
    steps = jnp.arange(blk)[None, :] - (jnp.arange(2 * blk)[:, None] - blk)
    steps_max = blk
    valid = (steps >= 0) & (steps <= steps_max)
    bias = -slopes[:, None, None] * (steps * dil).astype(F32)[None]
    return jnp.where(valid[None], bias, NEG).astype(F32)


def _proj_residue_kernel(x_ref, w_ref, o_ref):
    o_ref[0, 0] = jnp.dot(
        x_ref[0].astype(BF16), w_ref[...], preferred_element_type=F32
    ).astype(o_ref.dtype)


def _proj_residue(x, w, bsz, dil, tm=512, tn=1024):
    t, k = x.shape
    n = w.shape[1]
    l_sub = t // bsz // dil
    tm = min(tm, l_sub)
    view = x.reshape(bsz, l_sub, dil * k)
    return pl.pallas_call(
        _proj_residue_kernel,
        grid=(bsz, dil, l_sub // tm, n // tn),
        in_specs=[
            pl.BlockSpec((1, tm, k), lambda b, r, m, j: (b, m, r)),
            pl.BlockSpec((k, tn), lambda b, r, m, j: (0, j)),
        ],
        out_specs=pl.BlockSpec((1, 1, tm, tn), lambda b, r, m, j: (b, r, m, j)),
        out_shape=jax.ShapeDtypeStruct((bsz, dil, l_sub, n), BF16),
        compiler_params=_cparams(4),
        name=f"proj_residue_d{dil}",
    )(view, w)


def _dilated_group(qkv, slopes, g, window, dil):
    b, _, l_sub, _ = qkv.shape
    d = N_HEADS * HEAD_DIM
    blk = DIL_BLOCK
    assert window // dil == blk
    s = l_sub * dil
    nb = l_sub // blk
    bias = _dilated_bias(slopes, dil)

    def col(j):
        return lambda bi, r, n: (bi, r, n, j)

    def col_prev(j):
        return lambda bi, r, n: (bi, r, jnp.maximum(n - 1, 0), j)

    o, lse = pl.pallas_call(
        _dilated_kernel,
        grid=(b, dil, nb),
        in_specs=[
            pl.BlockSpec((N_HEADS, 2 * blk, blk), lambda bi, r, n: (0, 0, 0)),
            pl.BlockSpec((1, 1, blk, d), col(0)),
            pl.BlockSpec((1, 1, blk, d), col(1)),
            pl.BlockSpec((1, 1, blk, d), col_prev(1)),
            pl.BlockSpec((1, 1, blk, d), col(2)),
            pl.BlockSpec((1, 1, blk, d), col_prev(2)),
        ],
        out_specs=[
            pl.BlockSpec((1, blk, d), lambda bi, r, n: (bi, n, r)),
            pl.BlockSpec((1, blk, LANES), lambda bi, r, n: (bi, n, r)),
        ],
        out_shape=[
            jax.ShapeDtypeStruct((b, l_sub, dil * d), BF16),
            jax.ShapeDtypeStruct((b, l_sub, dil * LANES), F32),
        ],
        scratch_shapes=[pltpu.VMEM((LANES, blk), F32)],
        compiler_params=_cparams(3),
        name=f"dilated_g{g}",
    )(bias, qkv, qkv, qkv, qkv, qkv)
    return o.reshape(b, s, d), lse.reshape(b, s, LANES)


def _merge_kernel(e_ref, o1, o2, o3, l1, l2, l3, out_ref):
    ls = [l1[...], l2[...], l3[...]]
    mx = jnp.maximum(jnp.maximum(ls[0], ls[1]), ls[2])
    es = [jnp.exp(x - mx) for x in ls]
    den = es[0] + es[1] + es[2]
    acc = None
    for e, o in zip(es, (o1, o2, o3)):
        w = jnp.dot(e / den, e_ref[...], preferred_element_type=F32, precision=_HI)
        term = w * o[...].astype(F32)
        acc = term if acc is None else acc + term
    out_ref[...] = acc.astype(out_ref.dtype)


def _merge_groups(os, lses, tm=512):
    t, d = os[0].shape
    expand = (jnp.arange(LANES)[:, None] == (jnp.arange(d)[None, :] // HEAD_DIM)).astype(F32)
    row = lambda w: pl.BlockSpec((tm, w), lambda i: (i, 0))
    return pl.pallas_call(
        _merge_kernel,
        grid=(t // tm,),
        in_specs=[pl.BlockSpec((LANES, d), lambda i: (0, 0))] + [row(d)] * 3 + [row(LANES)] * 3,
        out_specs=row(d),
        out_shape=jax.ShapeDtypeStruct((t, d), BF16),
        compiler_params=_cparams(1),
        name="merge_groups",
    )(expand, *os, *lses)


def _layer_norm(y, g, b):
    mu = jnp.mean(y, axis=-1, keepdims=True)
    yc = y - mu
    var = jnp.mean(yc * yc, axis=-1, keepdims=True)
    return yc * lax.rsqrt(var + LN_EPS) * g + b


U32 = jnp.uint32
TILE_WORDS = 2 * LANES
HIGH_HALF = np.uint32(0xFFFF0000)


def _pack_token_tiles(ref, y):
    n, d = y.shape
    npk = d // TILE_WORDS
    for c in range(npk):
        lo = y[:, (2 * c) * LANES:(2 * c + 1) * LANES].astype(BF16).astype(F32)
        hi = y[:, (2 * c + 1) * LANES:(2 * c + 2) * LANES].astype(BF16).astype(F32)
        word = (lax.bitcast_convert_type(lo, U32) >> 16) | (lax.bitcast_convert_type(hi, U32) & HIGH_HALF)
        ref[pl.ds(c, n, stride=npk), :] = word


def _unpack_token_tiles(ref, n, npk, lead=()):
    blocks = []
    for c in range(npk):
        word = ref[lead + (pl.ds(c, n, stride=npk), slice(None))]
        blocks.append(lax.bitcast_convert_type(word << 16, F32))
        blocks.append(lax.bitcast_convert_type(word & HIGH_HALF, F32))
    return jnp.concatenate(blocks, axis=1)


def _wo_ln_kernel(mix_ref, wo_ref, x_ref, g_ref, b_ref, o_ref, ot_ref):
    f = jnp.dot(mix_ref[...], wo_ref[...], preferred_element_type=F32)
    y = _layer_norm(DN_ALPHA * x_ref[...] + f, g_ref[...], b_ref[...])
    o_ref[...] = y
    _pack_token_tiles(ot_ref, y)


def _wo_ln(mix, wo, x, g, b, tm=256):
    t, d = x.shape
    npk = d // TILE_WORDS
    row = pl.BlockSpec((tm, d), lambda i: (i, 0))
    vec = pl.BlockSpec((1, d), lambda i: (0, 0))
    return pl.pallas_call(
        _wo_ln_kernel,
        grid=(t // tm,),
        in_specs=[row, pl.BlockSpec((d, d), lambda i: (0, 0)), row, vec, vec],
        out_specs=[row, pl.BlockSpec((tm * npk, LANES), lambda i: (i, 0))],
        out_shape=[jax.ShapeDtypeStruct((t, d), F32), jax.ShapeDtypeStruct((t * npk, LANES), U32)],
        compiler_params=_cparams(1),
        name="wo_ln",
    )(mix, wo, x, g.reshape(1, d), b.reshape(1, d))


def _router_kernel(wr_ref, rb_ref, x_ref, e_ref, w_ref, pos_ref, cnt_ref, run_ref, *, tm):
    step = pl.program_id(0)
    ne = N_EXPERTS
    gs = ne // N_GROUPS

    @pl.when(step == 0)
    def _():
        run_ref[...] = jnp.zeros(run_ref.shape, F32)

    logits = lax.dot_general(
        wr_ref[...], x_ref[...], (((1,), (1,)), ((), ())), preferred_element_type=F32, precision=_HI
    )
    scores = jax.nn.sigmoid(logits)
    choice = scores + rb_ref[...][:, :1]

    grp_rows = lax.broadcasted_iota(jnp.int32, (gs, tm), 0)
    blocks, gscore = [], []
    for gi in range(N_GROUPS):
        cb = choice[gi * gs:(gi + 1) * gs]
        m1 = jnp.max(cb, axis=0, keepdims=True)
        i1 = jnp.min(jnp.where(cb == m1, grp_rows, gs), axis=0, keepdims=True)
        m2 = jnp.max(jnp.where(grp_rows == i1, -jnp.inf, cb), axis=0, keepdims=True)
        blocks.append(cb)
        gscore.append(m1 + m2)
    masked = []
    for gi in range(N_GROUPS):
        beaten = jnp.zeros((1, tm), jnp.int32)
        for gj in range(N_GROUPS):
            if gj == gi:
                continue
            wins = (gscore[gj] > gscore[gi]) | ((gscore[gj] == gscore[gi]) & (gj < gi))
            beaten = beaten + wins.astype(jnp.int32)
        masked.append(jnp.where(beaten < TOPK_GROUPS, blocks[gi], NEG))
    cand = jnp.concatenate(masked, axis=0)

    rows = lax.broadcasted_iota(jnp.int32, (ne, tm), 0)
    member = jnp.zeros((ne, tm), F32)
    picks, raw_w = [], []
    for _ in range(TOP_K):
        cmax = jnp.max(cand, axis=0, keepdims=True)
        first = jnp.min(jnp.where(cand == cmax, rows, ne), axis=0, keepdims=True)
        hit = rows == first
        raw_w.append(jnp.sum(jnp.where(hit, scores, 0.0), axis=0, keepdims=True))
        member = jnp.where(hit, 1.0, member)
        cand = jnp.where(hit, -jnp.inf, cand)
        picks.append(first)
    wsum = raw_w[0]
    for r in raw_w[1:]:
        wsum = wsum + r

    tri = (lax.broadcasted_iota(jnp.int32, (tm, tm), 0) < lax.broadcasted_iota(jnp.int32, (tm, tm), 1))
    member_b = member.astype(BF16)
    before = jnp.dot(member_b, tri.astype(BF16), preferred_element_type=F32)
    run = run_ref[...]
    rank = before + jnp.concatenate([run] * (tm // LANES), axis=1)
    for k in range(TOP_K):
        hit = rows == picks[k]
        e_ref[pl.ds(k, 1), :] = picks[k]
        w_ref[pl.ds(k, 1), :] = raw_w[k] / wsum * ROUTED_SCALE
        pos_ref[pl.ds(k, 1), :] = jnp.sum(jnp.where(hit, rank, 0.0), axis=0, keepdims=True).astype(jnp.int32)
    run = run + jnp.dot(member_b, jnp.ones((tm, LANES), BF16), preferred_element_type=F32)
    run_ref[...] = run
    cnt_ref[...] = run


def _router(x, w_router, router_bias, tm=256):
    t, d = x.shape
    ne = N_EXPERTS
    wr_t = w_router.T
    rb = jnp.broadcast_to(router_bias.astype(F32)[:, None], (ne, LANES))
    tok = pl.BlockSpec((TOP_K, tm), lambda i: (0, i))
    return pl.pallas_call(
        functools.partial(_router_kernel, tm=tm),
        grid=(t // tm,),
        in_specs=[
            pl.BlockSpec((ne, d), lambda i: (0, 0)),
            pl.BlockSpec((ne, LANES), lambda i: (0, 0)),
            pl.BlockSpec((tm, d), lambda i: (i, 0)),
        ],
        out_specs=[tok, tok, tok, pl.BlockSpec((ne, LANES), lambda i: (0, 0))],
        out_shape=[
            jax.ShapeDtypeStruct((TOP_K, t), jnp.int32),
            jax.ShapeDtypeStruct((TOP_K, t), F32),
            jax.ShapeDtypeStruct((TOP_K, t), jnp.int32),
            jax.ShapeDtypeStruct((ne, LANES), F32),
        ],
        scratch_shapes=[pltpu.VMEM((ne, LANES), F32)],
        compiler_params=_cparams(1),
        name="router",
    )(wr_t, rb, x)


def _tile_rows(ref, i, npk):
    return ref.at[pl.ds(pl.multiple_of(i * npk, npk), npk)]


def _dispatch_kernel(dest_hbm, x_ref, xs_hbm, dest_smem, sem_idx, sem_rows, *, tm):
    step = pl.program_id(0)
    npk = x_ref.shape[0] // tm
    idx_copy = pltpu.make_async_copy(dest_hbm.at[step], dest_smem, sem_idx)
    idx_copy.start()
    idx_copy.wait()

    def row_copy(t, d):
        return pltpu.make_async_copy(_tile_rows(x_ref, t, npk), _tile_rows(xs_hbm, d, npk), sem_rows)

    def issue(t, c):
        for k in range(TOP_K):
            row_copy(t, dest_smem[k * tm + t]).start(priority=k % 2)
        return c

    lax.fori_loop(0, tm, issue, 0)

    def drain(t, c):
        for _ in range(TOP_K):
            row_copy(0, 0).wait()
        return c

    lax.fori_loop(0, tm, drain, 0)


def _dispatch(x_tiles, dest_tiles, n_rows, npk, tm):
    t = x_tiles.shape[0] // npk
    return pl.pallas_call(
        functools.partial(_dispatch_kernel, tm=tm),
        grid=(t // tm,),
        in_specs=[pl.BlockSpec(memory_space=pl.ANY), pl.BlockSpec((tm * npk, LANES), lambda i: (i, 0))],
        out_specs=pl.BlockSpec(memory_space=pl.ANY),
        out_shape=jax.ShapeDtypeStruct((n_rows * npk, LANES), x_tiles.dtype),
        scratch_shapes=[
            pltpu.SMEM((TOP_K * tm,), jnp.int32),
            pltpu.SemaphoreType.DMA,
            pltpu.SemaphoreType.DMA,
        ],
        compiler_params=_cparams(1),
        name="dispatch",
    )(dest_tiles, x_tiles)


def _experts_kernel(blk_e_ref, blk_row_ref, n_real_ref, xs_ref, wg_ref, wu_ref, wd_ref, ys_ref):
    @pl.when(pl.program_id(0) < n_real_ref[0])
    def _():
        rb = EXPERT_ROW_BLOCK
        npk = xs_ref.shape[0] // rb
        xb = _unpack_token_tiles(xs_ref, rb, npk).astype(BF16)
        gate = jnp.dot(xb, wg_ref[0, 0].astype(BF16), preferred_element_type=F32)
        up = jnp.dot(xb, wu_ref[0, 0].astype(BF16), preferred_element_type=F32)
        h = (gate * jax.nn.sigmoid(gate) * up).astype(BF16)
        _pack_token_tiles(ys_ref, jnp.dot(h, wd_ref[0, 0].astype(BF16), preferred_element_type=F32))


def _experts(xs, we_gate, we_up, we_down, layer, blk_e, blk_row, n_real):
    d, f = we_gate.shape[-2:]
    nc = d // TILE_WORDS
    rb = EXPERT_ROW_BLOCK
    p = xs.shape[0] // nc
    rows = pl.BlockSpec((rb * nc, LANES), lambda i, be, br, nr: (br[i], 0))
    grid_spec = pltpu.PrefetchScalarGridSpec(
        num_scalar_prefetch=3,
        grid=(p // rb,),
        in_specs=[
            rows,
            pl.BlockSpec((1, 1, d, f), lambda i, be, br, nr: (layer, be[i], 0, 0)),
            pl.BlockSpec((1, 1, d, f), lambda i, be, br, nr: (layer, be[i], 0, 0)),
            pl.BlockSpec((1, 1, f, d), lambda i, be, br, nr: (layer, be[i], 0, 0)),
        ],
        out_specs=rows,
    )
    return pl.pallas_call(
        _experts_kernel,
        grid_spec=grid_spec,
        out_shape=jax.ShapeDtypeStruct((p * nc, LANES), U32),
        compiler_params=_cparams(1),
        name="experts",
    )(blk_e, blk_row, n_real, xs, we_gate, we_up, we_down)


def _combine_kernel(dest_hbm, ys_hbm, x_ref, w_ref, sg_ref, su_ref, sd_ref, g_ref, b_ref, o_ref,
                    dest_smem0, dest_smem1, rows_ref, sem_idx, sem_rows, *, tm):
    dest_smems = (dest_smem0, dest_smem1)
    step = pl.program_id(0)
    n_steps = pl.num_programs(0)
    npk = rows_ref.shape[2] // tm
    slot = step % 2

    def row_copy(sl, k, t, d):
        return pltpu.make_async_copy(
            _tile_rows(ys_hbm, d, npk), _tile_rows(rows_ref.at[sl, k], t, npk), sem_rows.at[sl])

    def fetch(tile, sl):
        dest_smem = dest_smems[sl]
        idx_copy = pltpu.make_async_copy(dest_hbm.at[tile], dest_smem, sem_idx)
        idx_copy.start()
        idx_copy.wait()

        def issue(t, c):
            for k in range(TOP_K):
                row_copy(sl, k, t, dest_smem[k * tm + t]).start(priority=k % 2)
            return c

        lax.fori_loop(0, tm, issue, 0)

    @pl.when(step == 0)
    def _():
        fetch(0, 0)

    for sl in range(2):
        @pl.when((step + 1 < n_steps) & (slot == 1 - sl))
        def _(sl=sl):
            fetch(step + 1, sl)

    x = x_ref[...]
    xb = x.astype(BF16)
    gate = jnp.dot(xb, sg_ref[...], preferred_element_type=F32)
    up = jnp.dot(xb, su_ref[...], preferred_element_type=F32)
    h = (gate * jax.nn.sigmoid(gate) * up).astype(BF16)
    f = jnp.dot(h, sd_ref[...], preferred_element_type=F32)

    def drain(t, c):
        for _ in range(TOP_K):
            row_copy(slot, 0, 0, 0).wait()
        return c

    lax.fori_loop(0, tm, drain, 0)

    w = w_ref[...]
    for k in range(TOP_K):
        f = f + w[:, k:k + 1] * _unpack_token_tiles(rows_ref, tm, npk, lead=(slot, k))
    o_ref[...] = _layer_norm(DN_ALPHA * x + f, g_ref[...], b_ref[...])


def _combine(x, ys, dest_tiles, w_tok, ws_gate, ws_up, ws_down, g, b, tm=128):
    t, d = x.shape
    nc = d // TILE_WORDS
    f = ws_gate.shape[-1]
    row = pl.BlockSpec((tm, d), lambda i: (i, 0))
    vec = pl.BlockSpec((1, d), lambda i: (0, 0))
    return pl.pallas_call(
        functools.partial(_combine_kernel, tm=tm),
        grid=(t // tm,),
        in_specs=[
            pl.BlockSpec(memory_space=pl.ANY),
            pl.BlockSpec(memory_space=pl.ANY),
            row,
            pl.BlockSpec((tm, TOP_K), lambda i: (i, 0)),
            pl.BlockSpec((d, f), lambda i: (0, 0)),
            pl.BlockSpec((d, f), lambda i: (0, 0)),
            pl.BlockSpec((f, d), lambda i: (0, 0)),
            vec,
            vec,
        ],
        out_specs=row,
        out_shape=jax.ShapeDtypeStruct((t, d), F32),
        scratch_shapes=[
            pltpu.SMEM((TOP_K * tm,), jnp.int32),
            pltpu.SMEM((TOP_K * tm,), jnp.int32),
            pltpu.VMEM((2, TOP_K, tm * nc, LANES), U32),
            pltpu.SemaphoreType.DMA,
            pltpu.SemaphoreType.DMA((2,)),
        ],
        compiler_params=_cparams(1),
        name="combine",
    )(dest_tiles, ys, x, w_tok, ws_gate, ws_up, ws_down, g.reshape(1, d), b.reshape(1, d))


def _tile_major(a, tm):
    k, t = a.shape
    return a.reshape(k, t // tm, tm).transpose(1, 0, 2).reshape(t // tm, k * tm)


def _dest_kernel(cnt_ref, e_ref, pos_ref, dest_ref, *, tm):
    ne = N_EXPERTS
    rb = float(EXPERT_ROW_BLOCK)
    padded = jnp.floor((cnt_ref[...] + (rb - 1.0)) * (1.0 / rb)) * rb
    lower = (lax.broadcasted_iota(jnp.int32, (ne, ne), 1) < lax.broadcasted_iota(jnp.int32, (ne, ne), 0))
    start = jnp.dot(lower.astype(F32), padded, preferred_element_type=F32, precision=_HI)
    start = jnp.concatenate([start] * (tm // LANES), axis=1)
    rows = lax.broadcasted_iota(jnp.int32, (ne, tm), 0)
    for k in range(TOP_K):
        hit = rows == e_ref[pl.ds(k, 1), :]
        first = jnp.sum(jnp.where(hit, start, 0.0), axis=0, keepdims=True)
        dest_ref[pl.ds(k, 1), :] = first.astype(jnp.int32) + pos_ref[pl.ds(k, 1), :]


def _dest(counts, top_e, pos, tm=512):
    t = top_e.shape[1]
    tok = pl.BlockSpec((TOP_K, tm), lambda i: (0, i))
    return pl.pallas_call(
        functools.partial(_dest_kernel, tm=tm),
        grid=(t // tm,),
        in_specs=[pl.BlockSpec((N_EXPERTS, LANES), lambda i: (0, 0)), tok, tok],
        out_specs=tok,
        out_shape=jax.ShapeDtypeStruct((TOP_K, t), jnp.int32),
        compiler_params=_cparams(1),
        name="dest",
    )(counts, top_e, pos)


def _moe_ln(x, x_tiles, layer, w_router, router_bias, we_gate, we_up, we_down, ws_gate, ws_up, ws_down,
            g, b):
    t, d = x.shape
    rb = EXPERT_ROW_BLOCK
    top_e, w, pos, counts = _router(x, w_router, router_bias)
    dest = _dest(counts, top_e, pos)

    cnt = counts[:, 0].astype(jnp.int32)
    pend = jnp.cumsum((cnt + rb - 1) // rb * rb)
    n_blocks = t * TOP_K // rb + N_EXPERTS
    n_real = (pend[-1] // rb).astype(jnp.int32)
    blk_ids = jnp.minimum(jnp.arange(n_blocks, dtype=jnp.int32), jnp.maximum(n_real - 1, 0))
    blk_e = jnp.sum((pend[None, :] <= (blk_ids * rb)[:, None]).astype(jnp.int32), axis=1)
    blk_e = jnp.minimum(blk_e, N_EXPERTS - 1)

    xs = _dispatch(x_tiles, _tile_major(dest, DISPATCH_TOKENS), n_blocks * rb, d // TILE_WORDS,
                   tm=DISPATCH_TOKENS)
    ys = _experts(xs, we_gate, we_up, we_down, layer, blk_e, blk_ids, n_real.reshape(1))
    return _combine(
        x, ys, _tile_major(dest, COMBINE_TOKENS), w.T, ws_gate.astype(BF16),
        ws_up.astype(BF16), ws_down.astype(BF16), g, b, tm=COMBINE_TOKENS,
    )


def kernel(x, w_qkv_a, w_qkv_b, w_o, ln_mix_g, ln_mix_b, w_router, router_bias, we_gate, we_up,
           we_down, ws_gate, ws_up, ws_down, ln_ffn_g, ln_ffn_b):
    bsz, s, d = x.shape
    t = bsz * s
    slopes = _alibi_slopes()
    xt = x.reshape(t, d)
    for i in range(DEPTH):
        if i % N_MIXERS == 0:
            qkv = _proj(xt, w_qkv_a[i // N_MIXERS].astype(BF16))
            mix = _moba_attention(qkv.reshape(bsz, s, -1), slopes).reshape(t, d)
        else:
            w_b = w_qkv_b[i // N_MIXERS].astype(BF16)
            outs, lses = [], []
            for g, (window, dil) in enumerate(DIL_GROUPS):
                qkv = _proj_residue(xt, w_b[:, g * 3 * d:(g + 1) * 3 * d], bsz, dil)
                o, lse = _dilated_group(qkv, slopes, g, window, dil)
                outs.append(o.reshape(t, d))
                lses.append(lse.reshape(t, LANES))
            mix = _merge_groups(outs, lses)
        x1, x1_tiles = _wo_ln(mix, w_o[i].astype(BF16), xt, ln_mix_g[i], ln_mix_b[i])
        xt = _moe_ln(x1, x1_tiles, i, w_router[i], router_bias[i], we_gate, we_up, we_down,
                     ws_gate[i], ws_up[i], ws_down[i], ln_ffn_g[i], ln_ffn_b[i])
    return xt.reshape(bsz, s, d)
```

```python
import functools
import math

import jax
import jax.numpy as jnp
import numpy as np
from jax import lax
from jax.experimental import pallas as pl
from jax.experimental.pallas import tpu as pltpu

N_HEADS = 16
HEAD_DIM = 64
DEPTH = 2
N_MIXERS = 2
MOBA_BLOCK = 256
MOBA_TOPK = 3
MOBA_KEY_BLOCKS_PER_STEP = 4
MOBA_ACC_ROWS = 80
MOBA_BIAS_PARTS = 3
DIL_GROUPS = ((128, 1), (512, 4), (2048, 16))
DIL_BLOCK = 128
N_EXPERTS = 256
TOP_K = 8
N_GROUPS = 8
TOPK_GROUPS = 4
ROUTED_SCALE = 2.5
DN_ALPHA = (2 * DEPTH) ** 0.25
LN_EPS = 1e-5
NEG = -1e30
LOG2E = math.log2(math.e)

LANES = 128
HEADS_PER_SLAB = LANES // HEAD_DIM
N_SLABS = N_HEADS // HEADS_PER_SLAB

DISPATCH_TOKENS = 512
COMBINE_TOKENS = 128

BF16 = jnp.bfloat16
F32 = jnp.float32
_HI = lax.Precision.HIGHEST

_ARB = "arbitrary"


def _cparams(n_axes, vmem_mb=48):
    return pltpu.CompilerParams(
        dimension_semantics=(_ARB,) * n_axes, vmem_limit_bytes=vmem_mb * 1024 * 1024
    )


def _alibi_slopes():
    return 2.0 ** (-8.0 * jnp.arange(1, N_HEADS + 1, dtype=F32) / N_HEADS)


def _proj_kernel(x_ref, w_ref, o_ref):
    o_ref[...] = jnp.dot(
        x_ref[...].astype(BF16), w_ref[...], preferred_element_type=F32
    ).astype(o_ref.dtype)


def _proj(x, w, tm=512, tn=1024):
    t, k = x.shape
    n = w.shape[1]
    return pl.pallas_call(
        _proj_kernel,
        grid=(t // tm, n // tn),
        in_specs=[
            pl.BlockSpec((tm, k), lambda i, j: (i, 0)),
            pl.BlockSpec((k, tn), lambda i, j: (0, j)),
        ],
        out_specs=pl.BlockSpec((tm, tn), lambda i, j: (i, j)),
        out_shape=jax.ShapeDtypeStruct((t, n), BF16),
        compiler_params=_cparams(2),
        name="proj",
    )(x, w)


def _moba_kernel(slopes_ref, q_ref, k_ref, v_ref, wb_ref, o_ref, ka_ref, vt_ref, km_ref, sel_ref, s_ref,
                 *, n_blocks):
    blk = MOBA_BLOCK
    unroll = MOBA_KEY_BLOCKS_PER_STEP
    span = unroll * blk
    acc_rows = MOBA_ACC_ROWS
    h2 = pl.program_id(1)
    i = pl.program_id(2)
    heads = range(HEADS_PER_SLAB)

    @pl.when(i == 0)
    def _prepare_keys():
        lane = lax.broadcasted_iota(jnp.int32, (blk, LANES), 1)
        row_f = lax.broadcasted_iota(jnp.int32, (blk, LANES), 0).astype(F32)
        tail = lax.broadcasted_iota(jnp.int32, (acc_rows - HEAD_DIM, blk), 0)
        ones_row = jnp.where(tail == 0, 1.0, 0.0)

        def body(n, c):
            r0 = pl.multiple_of(n * blk, blk)
            vt = v_ref[0, pl.ds(r0, blk), :].astype(F32).T
            for hh in heads:
                vt_ref[n, hh] = jnp.concatenate(
                    [vt[hh * HEAD_DIM:(hh + 1) * HEAD_DIM], ones_row], axis=0).astype(BF16)
            kb = k_ref[0, pl.ds(r0, blk), :]
            km_ref[pl.ds(n, 1), :] = jnp.mean(kb.astype(F32), axis=0, keepdims=True)
            sub = jnp.asarray(n % unroll, F32)
            aux = jnp.where(lane < MOBA_BIAS_PARTS, sub, jnp.where(lane < 2 * MOBA_BIAS_PARTS, row_f, 0.0))
            ka_ref[pl.ds(r0, blk), :] = jnp.concatenate([kb, aux.astype(BF16)], axis=1)
            return c

        lax.fori_loop(0, n_blocks, body, 0)

    qt = q_ref[0].astype(F32).T
    feat = lax.broadcasted_iota(jnp.int32, qt.shape, 0)
    kmean = km_ref[...]
    km_lane = lax.broadcasted_iota(jnp.int32, kmean.shape, 1)
    blk_row = lax.broadcasted_iota(jnp.int32, (n_blocks, blk), 0)

    slopes = [slopes_ref[h2 * HEADS_PER_SLAB + hh] for hh in heads]
    qt_s = []
    for hh in heads:
        in_head = (feat >= hh * HEAD_DIM) & (feat < (hh + 1) * HEAD_DIM)
        qt_h = jnp.where(in_head, qt, 0.0)
        qt_s.append((qt_h * (HEAD_DIM ** -0.5 * LOG2E)).astype(BF16))

        km_h = jnp.where((km_lane >= hh * HEAD_DIM) & (km_lane < (hh + 1) * HEAD_DIM), kmean, 0.0)
        gate = jnp.dot(km_h, qt_h, preferred_element_type=F32, precision=_HI)
        g = jnp.where(blk_row < i, gate, -jnp.inf)
        sel = jnp.zeros(g.shape, F32)
        for _ in range(MOBA_TOPK):
            gmax = jnp.max(g, axis=0, keepdims=True)
            first = jnp.min(jnp.where(g == gmax, blk_row, n_blocks), axis=0, keepdims=True)
            hit = blk_row == first
            sel = jnp.where(hit & (gmax > -jnp.inf), 1.0, sel)
            g = jnp.where(hit, -jnp.inf, g)
        sel_ref[hh] = sel

    w_all = jnp.concatenate(
        [jnp.concatenate(qt_s, axis=1), wb_ref[0]], axis=0)

    def scores(span_idx):
        r0 = pl.multiple_of(span_idx * span, span)
        return jnp.dot(ka_ref[pl.ds(r0, span), :], w_all, preferred_element_type=F32)

    key_row = lax.broadcasted_iota(jnp.int32, (blk, blk), 0)
    qry_col = lax.broadcasted_iota(jnp.int32, (blk, blk), 1)

    def consume(span_idx, state, last):
        j0 = span_idx * unroll
        base = jnp.asarray(blk * (j0 - i), F32)
        out = []
        for hh in heads:
            m, acc = state[hh]
            c = slopes[hh] * LOG2E * base
            for u in range(unroll):
                j = j0 + u
                s = s_ref[u * blk:(u + 1) * blk, hh * blk:(hh + 1) * blk]
                picked = sel_ref[hh, pl.ds(j, 1), :]
                if last:
                    own = j == i
                    s = jnp.where(key_row - qry_col > jnp.where(own, 0, blk), NEG, s)
                    picked = picked + jnp.where(own, 1.0, 0.0)
                ch = picked > 0.5
                m_new = jnp.where(ch, jnp.maximum(m, jnp.max(s, axis=0, keepdims=True) + c), m)
                p = jnp.exp2(s - jnp.where(ch, m_new - c, -NEG))
                acc = jnp.exp2(m - m_new) * acc + jnp.dot(
                    vt_ref[j, hh], p.astype(BF16), preferred_element_type=F32)
                m = m_new
            out.append((m, acc))
        return tuple(out)

    n_full = i // unroll
    s_ref[...] = scores(0)

    def body(it, state):
        nxt = scores(it + 1)
        state = consume(it, state, last=False)
        s_ref[...] = nxt
        return state

    init = tuple((jnp.full((1, blk), NEG, F32), jnp.zeros((acc_rows, blk), F32)) for _ in heads)
    state = lax.fori_loop(0, n_full, body, init)
    state = consume(n_full, state, last=True)

    outs = [acc[:HEAD_DIM] / acc[HEAD_DIM:HEAD_DIM + 1] for _, acc in state]
    o_ref[0] = jnp.concatenate(outs, axis=0).T.astype(o_ref.dtype)


def _moba_attention(qkv, slopes):
    b, s, _ = qkv.shape
    blk = MOBA_BLOCK
    n_blocks = s // blk
    assert n_blocks % MOBA_KEY_BLOCKS_PER_STEP == 0
    wide = HEADS_PER_SLAB * blk
    grid_spec = pltpu.PrefetchScalarGridSpec(
        num_scalar_prefetch=1,
        grid=(b, N_SLABS, n_blocks),
        in_specs=[
            pl.BlockSpec((1, blk, LANES), lambda bi, h, i, sl: (bi, i, h)),
            pl.BlockSpec((1, s, LANES), lambda bi, h, i, sl: (bi, 0, N_SLABS + h)),
            pl.BlockSpec((1, s, LANES), lambda bi, h, i, sl: (bi, 0, 2 * N_SLABS + h)),
            pl.BlockSpec((1, LANES, wide), lambda bi, h, i, sl: (h, 0, 0)),
        ],
        out_specs=pl.BlockSpec((1, blk, LANES), lambda bi, h, i, sl: (bi, i, h)),
        scratch_shapes=[
            pltpu.VMEM((s, 2 * LANES), BF16),
            pltpu.VMEM((n_blocks, HEADS_PER_SLAB, MOBA_ACC_ROWS, blk), BF16),
            pltpu.VMEM((n_blocks, LANES), F32),
            pltpu.VMEM((HEADS_PER_SLAB, n_blocks, blk), F32),
            pltpu.VMEM((MOBA_KEY_BLOCKS_PER_STEP * blk, wide), F32),
        ],
    )
    return pl.pallas_call(
        functools.partial(_moba_kernel, n_blocks=n_blocks),
        grid_spec=grid_spec,
        out_shape=jax.ShapeDtypeStruct((b, s, N_HEADS * HEAD_DIM), BF16),
        compiler_params=_cparams(3),
        name="moba",
    )(slopes, qkv, qkv, qkv, _moba_slope_rows(slopes))


def _moba_slope_rows(slopes):
    blk = MOBA_BLOCK
    parts, rest = [], slopes * LOG2E
    for _ in range(MOBA_BIAS_PARTS):
        p = rest.astype(BF16).astype(F32)
        parts.append(p)
        rest = rest - p
    sp = jnp.stack(parts, axis=1)
    rows = jnp.concatenate([sp * blk, sp], axis=1).reshape(N_SLABS, HEADS_PER_SLAB, 2 * MOBA_BIAS_PARTS)
    wb = jnp.zeros((N_SLABS, LANES, HEADS_PER_SLAB * blk), F32)
    for hh in range(HEADS_PER_SLAB):
        wb = wb.at[:, :2 * MOBA_BIAS_PARTS, hh * blk:(hh + 1) * blk].set(rows[:, hh, :, None])
    return wb.astype(BF16)


def _dilated_kernel(bias_ref, q_ref, kc_ref, kp_ref, vc_ref, vp_ref, o_ref, lse_ref, lse_s):
    blk = DIL_BLOCK
    n = pl.program_id(2)
    band_row = lax.broadcasted_iota(jnp.int32, (2 * blk, blk), 0)
    prev_invalid = (band_row < blk) & (n == 0)
    lse_s[...] = jnp.zeros(lse_s.shape, F32)
    for h2 in range(N_SLABS):
        cols = slice(h2 * LANES, (h2 + 1) * LANES)
        qt = q_ref[0, 0, :, cols].astype(F32).T * (HEAD_DIM ** -0.5)
        feat = lax.broadcasted_iota(jnp.int32, qt.shape, 0)
        kband = jnp.concatenate([kp_ref[0, 0, :, cols], kc_ref[0, 0, :, cols]], axis=0)
        vband = jnp.concatenate([vp_ref[0, 0, :, cols], vc_ref[0, 0, :, cols]], axis=0)
        vt = vband.astype(F32).T.astype(BF16)
        qt_both = jnp.concatenate(
            [jnp.where((feat >= hh * HEAD_DIM) & (feat < (hh + 1) * HEAD_DIM), qt, 0.0)
             for hh in range(HEADS_PER_SLAB)], axis=1).astype(BF16)
        s_both = jnp.dot(kband, qt_both, preferred_element_type=F32)
        outs = []
        for hh in range(HEADS_PER_SLAB):
            h = h2 * HEADS_PER_SLAB + hh
            s = s_both[:, hh * blk:(hh + 1) * blk] + bias_ref[h]
            s = jnp.where(prev_invalid, NEG, s)
            m = jnp.max(s, axis=0, keepdims=True)
            p = jnp.exp(s - m)
            l = jnp.sum(p, axis=0, keepdims=True)
            pv = jnp.dot(vt[hh * HEAD_DIM:(hh + 1) * HEAD_DIM], p.astype(BF16), preferred_element_type=F32)
            outs.append(pv / l)
            lse_s[pl.ds(h, 1), :] = m + jnp.log(l)
        o_ref[0, :, cols] = jnp.concatenate(outs, axis=0).T.astype(o_ref.dtype)
    lse_ref[0] = lse_s[...].T


def _dilated_bias(slopes, dil):
    blk = DIL_BLOCK
    steps = jnp.arange(blk)[None, :] - (jnp.arange(2 * blk)[:, None] - blk)
    steps_max = blk
    valid = (steps >= 0) & (steps <= steps_max)
    bias = -slopes[:, None, None] * (steps * dil).astype(F32)[None]
    return jnp.where(valid[None], bias, NEG).astype(F32)


def _proj_residue_kernel(x_ref, w_ref, o_ref):
    o_ref[0, 0] = jnp.dot(
        x_ref[0].astype(BF16), w_ref[...], preferred_element_type=F32
    ).astype(o_ref.dtype)


def _proj_residue(x, w, bsz, dil, tm=512, tn=1024):
    t, k = x.shape
    n = w.shape[1]
    l_sub = t // bsz // dil
    tm = min(tm, l_sub)
    view = x.reshape(bsz, l_sub, dil * k)
    return pl.pallas_call(
        _proj_residue_kernel,
        grid=(bsz, dil, l_sub // tm, n // tn),
        in_specs=[
            pl.BlockSpec((1, tm, k), lambda b, r, m, j: (b, m, r)),
            pl.BlockSpec((k, tn), lambda b, r, m, j: (0, j)),
        ],
        out_specs=pl.BlockSpec((1, 1, tm, tn), lambda b, r, m, j: (b, r, m, j)),
        out_shape=jax.ShapeDtypeStruct((bsz, dil, l_sub, n), BF16),
        compiler_params=_cparams(4),
        name=f"proj_residue_d{dil}",
    )(view, w)


def _dilated_group(qkv, slopes, g, window, dil):
    b, _, l_sub, _ = qkv.shape
    d = N_HEADS * HEAD_DIM
    blk = DIL_BLOCK
    assert window // dil == blk
    s = l_sub * dil
    nb = l_sub // blk
    bias = _dilated_bias(slopes, dil)

    def col(j):
        return lambda bi, r, n: (bi, r, n, j)

    def col_prev(j):
        return lambda bi, r, n: (bi, r, jnp.maximum(n - 1, 0), j)

    o, lse = pl.pallas_call(
        _dilated_kernel,
        grid=(b, dil, nb),
        in_specs=[
            pl.BlockSpec((N_HEADS, 2 * blk, blk), lambda bi, r, n: (0, 0, 0)),
            pl.BlockSpec((1, 1, blk, d), col(0)),
            pl.BlockSpec((1, 1, blk, d), col(1)),
            pl.BlockSpec((1, 1, blk, d), col_prev(1)),
            pl.BlockSpec((1, 1, blk, d), col(2)),
            pl.BlockSpec((1, 1, blk, d), col_prev(2)),
        ],
        out_specs=[
            pl.BlockSpec((1, blk, d), lambda bi, r, n: (bi, n, r)),
            pl.BlockSpec((1, blk, LANES), lambda bi, r, n: (bi, n, r)),
        ],
        out_shape=[
            jax.ShapeDtypeStruct((b, l_sub, dil * d), BF16),
            jax.ShapeDtypeStruct((b, l_sub, dil * LANES), F32),
        ],
        scratch_shapes=[pltpu.VMEM((LANES, blk), F32)],
        compiler_params=_cparams(3),
        name=f"dilated_g{g}",
    )(bias, qkv, qkv, qkv, qkv, qkv)
    return o.reshape(b, s, d), lse.reshape(b, s, LANES)


def _merge_kernel(e_ref, o1, o2, o3, l1, l2, l3, out_ref):
    ls = [l1[...], l2[...], l3[...]]
    mx = jnp.maximum(jnp.maximum(ls[0], ls[1]), ls[2])
    es = [jnp.exp(x - mx) for x in ls]
    den = es[0] + es[1] + es[2]
    acc = None
    for e, o in zip(es, (o1, o2, o3)):
        w = jnp.dot(e / den, e_ref[...], preferred_element_type=F32, precision=_HI)
        term = w * o[...].astype(F32)
        acc = term if acc is None else acc + term
    out_ref[...] = acc.astype(out_ref.dtype)


def _merge_groups(os, lses, tm=512):
    t, d = os[0].shape
    expand = (jnp.arange(LANES)[:, None] == (jnp.arange(d)[None, :] // HEAD_DIM)).astype(F32)
    row = lambda w: pl.BlockSpec((tm, w), lambda i: (i, 0))
    return pl.pallas_call(
        _merge_kernel,
        grid=(t // tm,),
        in_specs=[pl.BlockSpec((LANES, d), lambda i: (0, 0))] + [row(d)] * 3 + [row(LANES)] * 3,
        out_specs=row(d),
        out_shape=jax.ShapeDtypeStruct((t, d), BF16),
        compiler_params=_cparams(1),
        name="merge_groups",
    )(expand, *os, *lses)


def _layer_norm(y, g, b):
    mu = jnp.mean(y, axis=-1, keepdims=True)
    yc = y - mu
    var = jnp.mean(yc * yc, axis=-1, keepdims=True)
    return yc * lax.rsqrt(var + LN_EPS) * g + b


U32 = jnp.uint32
TILE_WORDS = 2 * LANES
HIGH_HALF = np.uint32(0xFFFF0000)


def _pack_token_tiles(ref, y):
    n, d = y.shape
    npk = d // TILE_WORDS
    for c in range(npk):
        lo = y[:, (2 * c) * LANES:(2 * c + 1) * LANES].astype(BF16).astype(F32)
        hi = y[:, (2 * c + 1) * LANES:(2 * c + 2) * LANES].astype(BF16).astype(F32)
        word = (lax.bitcast_convert_type(lo, U32) >> 16) | (lax.bitcast_convert_type(hi, U32) & HIGH_HALF)
        ref[pl.ds(c, n, stride=npk), :] = word


def _unpack_token_tiles(ref, n, npk, lead=()):
    blocks = []
    for c in range(npk):
        word = ref[lead + (pl.ds(c, n, stride=npk), slice(None))]
        blocks.append(lax.bitcast_convert_type(word << 16, F32))
        blocks.append(lax.bitcast_convert_type(word & HIGH_HALF, F32))
    return jnp.concatenate(blocks, axis=1)


def _wo_ln_kernel(mix_ref, wo_ref, x_ref, g_ref, b_ref, o_ref, ot_ref):
    f = jnp.dot(mix_ref[...], wo_ref[...], preferred_element_type=F32)
    y = _layer_norm(DN_ALPHA * x_ref[...] + f, g_ref[...], b_ref[...])
    o_ref[...] = y
    _pack_token_tiles(ot_ref, y)


def _wo_ln(mix, wo, x, g, b, tm=256):
    t, d = x.shape
    npk = d // TILE_WORDS
    row = pl.BlockSpec((tm, d), lambda i: (i, 0))
    vec = pl.BlockSpec((1, d), lambda i: (0, 0))
    return pl.pallas_call(
        _wo_ln_kernel,
        grid=(t // tm,),
        in_specs=[row, pl.BlockSpec((d, d), lambda i: (0, 0)), row, vec, vec],
        out_specs=[row, pl.BlockSpec((tm * npk, LANES), lambda i: (i, 0))],
        out_shape=[jax.ShapeDtypeStruct((t, d), F32), jax.ShapeDtypeStruct((t * npk, LANES), U32)],
        compiler_params=_cparams(1),
        name="wo_ln",
    )(mix, wo, x, g.reshape(1, d), b.reshape(1, d))


def _router_kernel(wr_ref, rb_ref, x_ref, e_ref, w_ref, pos_ref, cnt_ref, run_ref, *, tm):
    step = pl.program_id(0)
    ne = N_EXPERTS
    gs = ne // N_GROUPS

    @pl.when(step == 0)
    def _():
        run_ref[...] = jnp.zeros(run_ref.shape, F32)

    logits = lax.dot_general(
        wr_ref[...], x_ref[...], (((1,), (1,)), ((), ())), preferred_element_type=F32, precision=_HI
    )
    scores = jax.nn.sigmoid(logits)
    choice = scores + rb_ref[...][:, :1]

    grp_rows = lax.broadcasted_iota(jnp.int32, (gs, tm), 0)
    blocks, gscore = [], []
    for gi in range(N_GROUPS):
        cb = choice[gi * gs:(gi + 1) * gs]
        m1 = jnp.max(cb, axis=0, keepdims=True)
        i1 = jnp.min(jnp.where(cb == m1, grp_rows, gs), axis=0, keepdims=True)
        m2 = jnp.max(jnp.where(grp_rows == i1, -jnp.inf, cb), axis=0, keepdims=True)
        blocks.append(cb)
        gscore.append(m1 + m2)
    masked = []
    for gi in range(N_GROUPS):
        beaten = jnp.zeros((1, tm), jnp.int32)
        for gj in range(N_GROUPS):
            if gj == gi:
                continue
            wins = (gscore[gj] > gscore[gi]) | ((gscore[gj] == gscore[gi]) & (gj < gi))
            beaten = beaten + wins.astype(jnp.int32)
        masked.append(jnp.where(beaten < TOPK_GROUPS, blocks[gi], NEG))
    cand = jnp.concatenate(masked, axis=0)

    rows = lax.broadcasted_iota(jnp.int32, (ne, tm), 0)
    member = jnp.zeros((ne, tm), F32)
    picks, raw_w = [], []
    for _ in range(TOP_K):
        cmax = jnp.max(cand, axis=0, keepdims=True)
        first = jnp.min(jnp.where(cand == cmax, rows, ne), axis=0, keepdims=True)
        hit = rows == first
        raw_w.append(jnp.sum(jnp.where(hit, scores, 0.0), axis=0, keepdims=True))
        member = jnp.where(hit, 1.0, member)
        cand = jnp.where(hit, -jnp.inf, cand)
        picks.append(first)
    wsum = raw_w[0]
    for r in raw_w[1:]:
        wsum = wsum + r

    tri = (lax.broadcasted_iota(jnp.int32, (tm, tm), 0) < lax.broadcasted_iota(jnp.int32, (tm, tm), 1))
    member_b = member.astype(BF16)
    before = jnp.dot(member_b, tri.astype(BF16), preferred_element_type=F32)
    run = run_ref[...]
    rank = before + jnp.concatenate([run] * (tm // LANES), axis=1)
    for k in range(TOP_K):
        hit = rows == picks[k]
        e_ref[pl.ds(k, 1), :] = picks[k]
        w_ref[pl.ds(k, 1), :] = raw_w[k] / wsum * ROUTED_SCALE
        pos_ref[pl.ds(k, 1), :] = jnp.sum(jnp.where(hit, rank, 0.0), axis=0, keepdims=True).astype(jnp.int32)
    run = run + jnp.dot(member_b, jnp.ones((tm, LANES), BF16), preferred_element_type=F32)
    run_ref[...] = run
    cnt_ref[...] = run


def _router(x, w_router, router_bias, tm=256):
    t, d = x.shape
    ne = N_EXPERTS
    wr_t = w_router.T
    rb = jnp.broadcast_to(router_bias.astype(F32)[:, None], (ne, LANES))
    tok = pl.BlockSpec((TOP_K, tm), lambda i: (0, i))
    return pl.pallas_call(
        functools.partial(_router_kernel, tm=tm),
        grid=(t // tm,),
        in_specs=[
            pl.BlockSpec((ne, d), lambda i: (0, 0)),
            pl.BlockSpec((ne, LANES), lambda i: (0, 0)),
            pl.BlockSpec((tm, d), lambda i: (i, 0)),
        ],
        out_specs=[tok, tok, tok, pl.BlockSpec((ne, LANES), lambda i: (0, 0))],
        out_shape=[
            jax.ShapeDtypeStruct((TOP_K, t), jnp.int32),
            jax.ShapeDtypeStruct((TOP_K, t), F32),
            jax.ShapeDtypeStruct((TOP_K, t), jnp.int32),
            jax.ShapeDtypeStruct((ne, LANES), F32),
        ],
        scratch_shapes=[pltpu.VMEM((ne, LANES), F32)],
        compiler_params=_cparams(1),
        name="router",
    )(wr_t, rb, x)


def _tile_rows(ref, i, npk):
    return ref.at[pl.ds(pl.multiple_of(i * npk, npk), npk)]


def _dispatch_kernel(dest_hbm, x_ref, xs_hbm, dest_smem, sem_idx, sem_rows, *, tm):
    step = pl.program_id(0)
    npk = x_ref.shape[0] // tm
    idx_copy = pltpu.make_async_copy(dest_hbm.at[step], dest_smem, sem_idx)
    idx_copy.start()
    idx_copy.wait()

    def row_copy(t, d):
        return pltpu.make_async_copy(_tile_rows(x_ref, t, npk), _tile_rows(xs_hbm, d, npk), sem_rows)

    def issue(t, c):
        for k in range(TOP_K):
            row_copy(t, dest_smem[k * tm + t]).start(priority=k % 2)
        return c

    lax.fori_loop(0, tm, issue, 0)

    def drain(t, c):
        for _ in range(TOP_K):
            row_copy(0, 0).wait()
        return c

    lax.fori_loop(0, tm, drain, 0)


def _dispatch(x_tiles, dest_tiles, n_rows, npk, tm):
    t = x_tiles.shape[0] // npk
    return pl.pallas_call(
        functools.partial(_dispatch_kernel, tm=tm),
        grid=(t // tm,),
        in_specs=[pl.BlockSpec(memory_space=pl.ANY), pl.BlockSpec((tm * npk, LANES), lambda i: (i, 0))],
        out_specs=pl.BlockSpec(memory_space=pl.ANY),
        out_shape=jax.ShapeDtypeStruct((n_rows * npk, LANES), x_tiles.dtype),
        scratch_shapes=[
            pltpu.SMEM((TOP_K * tm,), jnp.int32),
            pltpu.SemaphoreType.DMA,
            pltpu.SemaphoreType.DMA,
        ],
        compiler_params=_cparams(1),
        name="dispatch",
    )(dest_tiles, x_tiles)


def _experts_kernel(blk_e_ref, blk_row_ref, n_real_ref, xs_ref, wg_ref, wu_ref, wd_ref, ys_ref, *, rb):
    @pl.when(pl.program_id(0) < n_real_ref[0])
    def _():
        npk = xs_ref.shape[0] // rb
        xb = _unpack_token_tiles(xs_ref, rb, npk).astype(BF16)
        gate = jnp.dot(xb, wg_ref[0, 0].astype(BF16), preferred_element_type=F32)
        up = jnp.dot(xb, wu_ref[0, 0].astype(BF16), preferred_element_type=F32)
        h = (gate * jax.nn.sigmoid(gate) * up).astype(BF16)
        _pack_token_tiles(ys_ref, jnp.dot(h, wd_ref[0, 0].astype(BF16), preferred_element_type=F32))


def _experts(xs, we_gate, we_up, we_down, layer, rb, blk_e, blk_row, n_real):
    d, f = we_gate.shape[-2:]
    nc = d // TILE_WORDS
    p = xs.shape[0] // nc
    rows = pl.BlockSpec((rb * nc, LANES), lambda i, be, br, nr: (br[i], 0))
    grid_spec = pltpu.PrefetchScalarGridSpec(
        num_scalar_prefetch=3,
        grid=(p // rb,),
        in_specs=[
            rows,
            pl.BlockSpec((1, 1, d, f), lambda i, be, br, nr: (layer, be[i], 0, 0)),
            pl.BlockSpec((1, 1, d, f), lambda i, be, br, nr: (layer, be[i], 0, 0)),
            pl.BlockSpec((1, 1, f, d), lambda i, be, br, nr: (layer, be[i], 0, 0)),
        ],
        out_specs=rows,
    )
    return pl.pallas_call(
        functools.partial(_experts_kernel, rb=rb),
        grid_spec=grid_spec,
        out_shape=jax.ShapeDtypeStruct((p * nc, LANES), U32),
        compiler_params=_cparams(1),
        name="experts",
    )(blk_e, blk_row, n_real, xs, we_gate, we_up, we_down)


def _combine_kernel(dest_hbm, ys_hbm, x_ref, w_ref, sg_ref, su_ref, sd_ref, g_ref, b_ref, o_ref,
                    dest_smem0, dest_smem1, rows_ref, sem_idx, sem_rows, *, tm):
    dest_smems = (dest_smem0, dest_smem1)
    step = pl.program_id(0)
    n_steps = pl.num_programs(0)
    npk = rows_ref.shape[2] // tm
    slot = step % 2

    def row_copy(sl, k, t, d):
        return pltpu.make_async_copy(
            _tile_rows(ys_hbm, d, npk), _tile_rows(rows_ref.at[sl, k], t, npk), sem_rows.at[sl])

    def fetch(tile, sl):
        dest_smem = dest_smems[sl]
        idx_copy = pltpu.make_async_copy(dest_hbm.at[tile], dest_smem, sem_idx)
        idx_copy.start()
        idx_copy.wait()

        def issue(t, c):
            for k in range(TOP_K):
                row_copy(sl, k, t, dest_smem[k * tm + t]).start(priority=k % 2)
            return c

        lax.fori_loop(0, tm, issue, 0)

    @pl.when(step == 0)
    def _():
        fetch(0, 0)

    for sl in range(2):
        @pl.when((step + 1 < n_steps) & (slot == 1 - sl))
        def _(sl=sl):
            fetch(step + 1, sl)

    x = x_ref[...]
    xb = x.astype(BF16)
    gate = jnp.dot(xb, sg_ref[...], preferred_element_type=F32)
    up = jnp.dot(xb, su_ref[...], preferred_element_type=F32)
    h = (gate * jax.nn.sigmoid(gate) * up).astype(BF16)
    f = jnp.dot(h, sd_ref[...], preferred_element_type=F32)

    def drain(t, c):
        for _ in range(TOP_K):
            row_copy(slot, 0, 0, 0).wait()
        return c

    lax.fori_loop(0, tm, drain, 0)

    w = w_ref[...]
    for k in range(TOP_K):
        f = f + w[:, k:k + 1] * _unpack_token_tiles(rows_ref, tm, npk, lead=(slot, k))
    o_ref[...] = _layer_norm(DN_ALPHA * x + f, g_ref[...], b_ref[...])


def _combine(x, ys, dest_tiles, w_tok, ws_gate, ws_up, ws_down, g, b, tm=128):
    t, d = x.shape
    nc = d // TILE_WORDS
    f = ws_gate.shape[-1]
    row = pl.BlockSpec((tm, d), lambda i: (i, 0))
    vec = pl.BlockSpec((1, d), lambda i: (0, 0))
    return pl.pallas_call(
        functools.partial(_combine_kernel, tm=tm),
        grid=(t // tm,),
        in_specs=[
            pl.BlockSpec(memory_space=pl.ANY),
            pl.BlockSpec(memory_space=pl.ANY),
            row,
            pl.BlockSpec((tm, TOP_K), lambda i: (i, 0)),
            pl.BlockSpec((d, f), lambda i: (0, 0)),
            pl.BlockSpec((d, f), lambda i: (0, 0)),
            pl.BlockSpec((f, d), lambda i: (0, 0)),
            vec,
            vec,
        ],
        out_specs=row,
        out_shape=jax.ShapeDtypeStruct((t, d), F32),
        scratch_shapes=[
            pltpu.SMEM((TOP_K * tm,), jnp.int32),
            pltpu.SMEM((TOP_K * tm,), jnp.int32),
            pltpu.VMEM((2, TOP_K, tm * nc, LANES), U32),
            pltpu.SemaphoreType.DMA,
            pltpu.SemaphoreType.DMA((2,)),
        ],
        compiler_params=_cparams(1),
        name="combine",
    )(dest_tiles, ys, x, w_tok, ws_gate, ws_up, ws_down, g.reshape(1, d), b.reshape(1, d))


def _tile_major(a, tm):
    k, t = a.shape
    return a.reshape(k, t // tm, tm).transpose(1, 0, 2).reshape(t // tm, k * tm)


def _dest_kernel(start_ref, e_ref, pos_ref, dest_ref, *, tm):
    ne = N_EXPERTS
    start = jnp.concatenate([start_ref[...]] * (tm // LANES), axis=1)
    rows = lax.broadcasted_iota(jnp.int32, (ne, tm), 0)
    for k in range(TOP_K):
        hit = rows == e_ref[pl.ds(k, 1), :]
        first = jnp.sum(jnp.where(hit, start, 0), axis=0, keepdims=True)
        dest_ref[pl.ds(k, 1), :] = first + pos_ref[pl.ds(k, 1), :]


def _dest(starts, top_e, pos, tm=512):
    t = top_e.shape[1]
    tok = pl.BlockSpec((TOP_K, tm), lambda i: (0, i))
    return pl.pallas_call(
        functools.partial(_dest_kernel, tm=tm),
        grid=(t // tm,),
        in_specs=[pl.BlockSpec((N_EXPERTS, LANES), lambda i: (0, 0)), tok, tok],
        out_specs=tok,
        out_shape=jax.ShapeDtypeStruct((TOP_K, t), jnp.int32),
        compiler_params=_cparams(1),
        name="dest",
    )(jnp.broadcast_to(starts[:, None], (N_EXPERTS, LANES)), top_e, pos)


def _expert_row_block(t):
    mean_rows = t * TOP_K // N_EXPERTS
    return max(LANES, -(-(mean_rows + mean_rows // 4) // LANES) * LANES)


def _moe_ln(x, x_tiles, layer, w_router, router_bias, we_gate, we_up, we_down, ws_gate, ws_up, ws_down,
            g, b):
    t, d = x.shape
    rb = _expert_row_block(t)
    top_e, w, pos, counts = _router(x, w_router, router_bias)

    cnt = counts[:, 0].astype(jnp.int32)
    padded = (cnt + rb - 1) // rb * rb
    pend = jnp.cumsum(padded)
    n_blocks = t * TOP_K // rb + N_EXPERTS
    n_real = (pend[-1] // rb).astype(jnp.int32)
    blk_ids = jnp.minimum(jnp.arange(n_blocks, dtype=jnp.int32), jnp.maximum(n_real - 1, 0))
    blk_e = jnp.sum((pend[None, :] <= (blk_ids * rb)[:, None]).astype(jnp.int32), axis=1)
    blk_e = jnp.minimum(blk_e, N_EXPERTS - 1)

    dest = _dest(pend - padded, top_e, pos)
    xs = _dispatch(x_tiles, _tile_major(dest, DISPATCH_TOKENS), n_blocks * rb, d // TILE_WORDS,
                   tm=DISPATCH_TOKENS)
    ys = _experts(xs, we_gate, we_up, we_down, layer, rb, blk_e, blk_ids, n_real.reshape(1))
    return _combine(
        x, ys, _tile_major(dest, COMBINE_TOKENS), w.T, ws_gate.astype(BF16),
        ws_up.astype(BF16), ws_down.astype(BF16), g, b, tm=COMBINE_TOKENS,
    )


def kernel(x, w_qkv_a, w_qkv_b, w_o, ln_mix_g, ln_mix_b, w_router, router_bias, we_gate, we_up,
           we_down, ws_gate, ws_up, ws_down, ln_ffn_g, ln_ffn_b):
    bsz, s, d = x.shape
    t = bsz * s
    slopes = _alibi_slopes()
    xt = x.reshape(t, d)
    for i in range(DEPTH):
        if i % N_MIXERS == 0:
            qkv = _proj(xt, w_qkv_a[i // N_MIXERS].astype(BF16))
            mix = _moba_attention(qkv.reshape(bsz, s, -1), slopes).reshape(t, d)
        else:
            w_b = w_qkv_b[i // N_MIXERS].astype(BF16)
            outs, lses = [], []
            for g, (window, dil) in enumerate(DIL_GROUPS):
                qkv = _proj_residue(xt, w_b[:, g * 3 * d:(g + 1) * 3 * d], bsz, dil)
                o, lse = _dilated_group(qkv, slopes, g, window, dil)
                outs.append(o.reshape(t, d))
                lses.append(lse.reshape(t, LANES))
            mix = _merge_groups(outs, lses)
        x1, x1_tiles = _wo_ln(mix, w_o[i].astype(BF16), xt, ln_mix_g[i], ln_mix_b[i])
        xt = _moe_ln(x1, x1_tiles, i, w_router[i], router_bias[i], we_gate, we_up, we_down,
                     ws_gate[i], ws_up[i], ws_down[i], ln_ffn_g[i], ln_ffn_b[i])
    return xt.reshape(bsz, s, d)
```

```python
import functools
import math

import jax
import jax.numpy as jnp
import numpy as np
from jax import lax
from jax.experimental import pallas as pl
from jax.experimental.pallas import tpu as pltpu

N_HEADS = 16
HEAD_DIM = 64
DEPTH = 2
N_MIXERS = 2
MOBA_BLOCK = 256
MOBA_TOPK = 3
MOBA_KEY_BLOCKS_PER_STEP = 4
MOBA_ACC_ROWS = 80
MOBA_BIAS_PARTS = 3
DIL_GROUPS = ((128, 1), (512, 4), (2048, 16))
DIL_BLOCK = 128
N_EXPERTS = 256
TOP_K = 8
N_GROUPS = 8
TOPK_GROUPS = 4
ROUTED_SCALE = 2.5
DN_ALPHA = (2 * DEPTH) ** 0.25
LN_EPS = 1e-5
NEG = -1e30
LOG2E = math.log2(math.e)

LANES = 128
HEADS_PER_SLAB = LANES // HEAD_DIM
N_SLABS = N_HEADS // HEADS_PER_SLAB

DISPATCH_TOKENS = 512
COMBINE_TOKENS = 128

BF16 = jnp.bfloat16
F32 = jnp.float32
_HI = lax.Precision.HIGHEST

_ARB = "arbitrary"


def _cparams(n_axes, vmem_mb=48):
    return pltpu.CompilerParams(
        dimension_semantics=(_ARB,) * n_axes, vmem_limit_bytes=vmem_mb * 1024 * 1024
    )


def _alibi_slopes():
    return 2.0 ** (-8.0 * jnp.arange(1, N_HEADS + 1, dtype=F32) / N_HEADS)


def _proj_kernel(x_ref, w_ref, o_ref):
    o_ref[...] = jnp.dot(
        x_ref[...].astype(BF16), w_ref[...], preferred_element_type=F32
    ).astype(o_ref.dtype)


def _proj(x, w, tm=512, tn=1024):
    t, k = x.shape
    n = w.shape[1]
    return pl.pallas_call(
        _proj_kernel,
        grid=(t // tm, n // tn),
        in_specs=[
            pl.BlockSpec((tm, k), lambda i, j: (i, 0)),
            pl.BlockSpec((k, tn), lambda i, j: (0, j)),
        ],
        out_specs=pl.BlockSpec((tm, tn), lambda i, j: (i, j)),
        out_shape=jax.ShapeDtypeStruct((t, n), BF16),
        compiler_params=_cparams(2),
        name="proj",
    )(x, w)


def _moba_kernel(slopes_ref, q_ref, k_ref, v_ref, wb_ref, o_ref, ka_ref, vt_ref, km_ref, sel_ref, s_ref,
                 *, n_blocks):
    blk = MOBA_BLOCK
    unroll = MOBA_KEY_BLOCKS_PER_STEP
    span = unroll * blk
    acc_rows = MOBA_ACC_ROWS
    h2 = pl.program_id(1)
    i = pl.program_id(2)
    heads = range(HEADS_PER_SLAB)

    @pl.when(i == 0)
    def _prepare_keys():
        lane = lax.broadcasted_iota(jnp.int32, (blk, LANES), 1)
        row_f = lax.broadcasted_iota(jnp.int32, (blk, LANES), 0).astype(F32)
        tail = lax.broadcasted_iota(jnp.int32, (acc_rows - HEAD_DIM, blk), 0)
        ones_row = jnp.where(tail == 0, 1.0, 0.0)

        def body(n, c):
            r0 = pl.multiple_of(n * blk, blk)
            vt = v_ref[0, pl.ds(r0, blk), :].astype(F32).T
            for hh in heads:
                vt_ref[n, hh] = jnp.concatenate(
                    [vt[hh * HEAD_DIM:(hh + 1) * HEAD_DIM], ones_row], axis=0).astype(BF16)
            kb = k_ref[0, pl.ds(r0, blk), :]
            km_ref[pl.ds(n, 1), :] = jnp.mean(kb.astype(F32), axis=0, keepdims=True)
            sub = jnp.asarray(n % unroll, F32)
            aux = jnp.where(lane < MOBA_BIAS_PARTS, sub, jnp.where(lane < 2 * MOBA_BIAS_PARTS, row_f, 0.0))
            ka_ref[pl.ds(r0, blk), :] = jnp.concatenate([kb, aux.astype(BF16)], axis=1)
            return c

        lax.fori_loop(0, n_blocks, body, 0)

    qt = q_ref[0].astype(F32).T
    feat = lax.broadcasted_iota(jnp.int32, qt.shape, 0)
    kmean = km_ref[...]
    km_lane = lax.broadcasted_iota(jnp.int32, kmean.shape, 1)
    blk_row = lax.broadcasted_iota(jnp.int32, (n_blocks, blk), 0)

    slopes = [slopes_ref[h2 * HEADS_PER_SLAB + hh] for hh in heads]
    qt_s = []
    for hh in heads:
        in_head = (feat >= hh * HEAD_DIM) & (feat < (hh + 1) * HEAD_DIM)
        qt_h = jnp.where(in_head, qt, 0.0)
        qt_s.append((qt_h * (HEAD_DIM ** -0.5 * LOG2E)).astype(BF16))

        km_h = jnp.where((km_lane >= hh * HEAD_DIM) & (km_lane < (hh + 1) * HEAD_DIM), kmean, 0.0)
        gate = jnp.dot(km_h, qt_h, preferred_element_type=F32, precision=_HI)
        g = jnp.where(blk_row < i, gate, -jnp.inf)
        sel = jnp.zeros(g.shape, F32)
        for _ in range(MOBA_TOPK):
            gmax = jnp.max(g, axis=0, keepdims=True)
            first = jnp.min(jnp.where(g == gmax, blk_row, n_blocks), axis=0, keepdims=True)
            hit = blk_row == first
            sel = jnp.where(hit & (gmax > -jnp.inf), 1.0, sel)
            g = jnp.where(hit, -jnp.inf, g)
        sel_ref[hh] = sel

    w_all = jnp.concatenate(
        [jnp.concatenate(qt_s, axis=1), wb_ref[0]], axis=0)

    def scores(span_idx):
        r0 = pl.multiple_of(span_idx * span, span)
        return jnp.dot(ka_ref[pl.ds(r0, span), :], w_all, preferred_element_type=F32)

    key_row = lax.broadcasted_iota(jnp.int32, (blk, blk), 0)
    qry_col = lax.broadcasted_iota(jnp.int32, (blk, blk), 1)

    def consume(span_idx, state, last):
        j0 = span_idx * unroll
        base = jnp.asarray(blk * (j0 - i), F32)
        out = []
        for hh in heads:
            m, acc = state[hh]
            c = slopes[hh] * LOG2E * base
            for u in range(unroll):
                j = j0 + u
                s = s_ref[u * blk:(u + 1) * blk, hh * blk:(hh + 1) * blk]
                picked = sel_ref[hh, pl.ds(j, 1), :]
                if last:
                    own = j == i
                    s = jnp.where(key_row - qry_col > jnp.where(own, 0, blk), NEG, s)
                    picked = picked + jnp.where(own, 1.0, 0.0)
                ch = picked > 0.5
                m_new = jnp.where(ch, jnp.maximum(m, jnp.max(s, axis=0, keepdims=True) + c), m)
                p = jnp.exp2(s - jnp.where(ch, m_new - c, -NEG))
                acc = jnp.exp2(m - m_new) * acc + jnp.dot(
                    vt_ref[j, hh], p.astype(BF16), preferred_element_type=F32)
                m = m_new
            out.append((m, acc))
        return tuple(out)

    n_full = i // unroll
    s_ref[...] = scores(0)

    def body(it, state):
        nxt = scores(it + 1)
        state = consume(it, state, last=False)
        s_ref[...] = nxt
        return state

    init = tuple((jnp.full((1, blk), NEG, F32), jnp.zeros((acc_rows, blk), F32)) for _ in heads)
    state = lax.fori_loop(0, n_full, body, init)
    state = consume(n_full, state, last=True)

    outs = [acc[:HEAD_DIM] / acc[HEAD_DIM:HEAD_DIM + 1] for _, acc in state]
    o_ref[0] = jnp.concatenate(outs, axis=0).T.astype(o_ref.dtype)


def _moba_attention(qkv, slopes):
    b, s, _ = qkv.shape
    blk = MOBA_BLOCK
    n_blocks = s // blk
    assert n_blocks % MOBA_KEY_BLOCKS_PER_STEP == 0
    wide = HEADS_PER_SLAB * blk
    grid_spec = pltpu.PrefetchScalarGridSpec(
        num_scalar_prefetch=1,
        grid=(b, N_SLABS, n_blocks),
        in_specs=[
            pl.BlockSpec((1, blk, LANES), lambda bi, h, i, sl: (bi, i, h)),
            pl.BlockSpec((1, s, LANES), lambda bi, h, i, sl: (bi, 0, N_SLABS + h)),
            pl.BlockSpec((1, s, LANES), lambda bi, h, i, sl: (bi, 0, 2 * N_SLABS + h)),
            pl.BlockSpec((1, LANES, wide), lambda bi, h, i, sl: (h, 0, 0)),
        ],
        out_specs=pl.BlockSpec((1, blk, LANES), lambda bi, h, i, sl: (bi, i, h)),
        scratch_shapes=[
            pltpu.VMEM((s, 2 * LANES), BF16),
            pltpu.VMEM((n_blocks, HEADS_PER_SLAB, MOBA_ACC_ROWS, blk), BF16),
            pltpu.VMEM((n_blocks, LANES), F32),
            pltpu.VMEM((HEADS_PER_SLAB, n_blocks, blk), F32),
            pltpu.VMEM((MOBA_KEY_BLOCKS_PER_STEP * blk, wide), F32),
        ],
    )
    return pl.pallas_call(
        functools.partial(_moba_kernel, n_blocks=n_blocks),
        grid_spec=grid_spec,
        out_shape=jax.ShapeDtypeStruct((b, s, N_HEADS * HEAD_DIM), BF16),
        compiler_params=_cparams(3),
        name="moba",
    )(slopes, qkv, qkv, qkv, _moba_slope_rows(slopes))


def _moba_slope_rows(slopes):
    blk = MOBA_BLOCK
    parts, rest = [], slopes * LOG2E
    for _ in range(MOBA_BIAS_PARTS):
        p = rest.astype(BF16).astype(F32)
        parts.append(p)
        rest = rest - p
    sp = jnp.stack(parts, axis=1)
    rows = jnp.concatenate([sp * blk, sp], axis=1).reshape(N_SLABS, HEADS_PER_SLAB, 2 * MOBA_BIAS_PARTS)
    wb = jnp.zeros((N_SLABS, LANES, HEADS_PER_SLAB * blk), F32)
    for hh in range(HEADS_PER_SLAB):
        wb = wb.at[:, :2 * MOBA_BIAS_PARTS, hh * blk:(hh + 1) * blk].set(rows[:, hh, :, None])
    return wb.astype(BF16)


def _dilated_kernel(bias_ref, qt_ref, kc_ref, kp_ref, vtc_ref, vtp_ref, o_ref, lse_ref, lse_s, s_scr):
    blk = DIL_BLOCK
    lse_s[...] = jnp.zeros(lse_s.shape, F32)
    tail = lax.broadcasted_iota(jnp.int32, (MOBA_ACC_ROWS - HEAD_DIM, 2 * blk), 0)
    ones_row = jnp.where(tail == 0, 1.0, 0.0).astype(BF16)
    feat = lax.broadcasted_iota(jnp.int32, (LANES, blk), 0)
    for h2 in range(N_SLABS):
        cols = slice(h2 * LANES, (h2 + 1) * LANES)
        qt = qt_ref[0, 0, cols, :].astype(F32) * (HEAD_DIM ** -0.5 * LOG2E)
        kband = jnp.concatenate([kp_ref[0, 0, :, cols], kc_ref[0, 0, :, cols]], axis=0)
        qt_both = jnp.concatenate(
            [jnp.where((feat >= hh * HEAD_DIM) & (feat < (hh + 1) * HEAD_DIM), qt, 0.0)
             for hh in range(HEADS_PER_SLAB)], axis=1).astype(BF16)
        s_scr[h2] = jnp.dot(kband, qt_both, preferred_element_type=F32)
    for h2 in range(N_SLABS):
        cols = slice(h2 * LANES, (h2 + 1) * LANES)
        vt = jnp.concatenate([vtp_ref[0, 0, cols, :], vtc_ref[0, 0, cols, :]], axis=1)
        outs = []
        for hh in range(HEADS_PER_SLAB):
            h = h2 * HEADS_PER_SLAB + hh
            s = s_scr[h2, :, hh * blk:(hh + 1) * blk] + bias_ref[0, h]
            m = jnp.max(s, axis=0, keepdims=True)
            p = jnp.exp2(s - m).astype(BF16)
            vt_h = jnp.concatenate([vt[hh * HEAD_DIM:(hh + 1) * HEAD_DIM], ones_row], axis=0)
            acc = jnp.dot(vt_h, p, preferred_element_type=F32)
            l = acc[HEAD_DIM:HEAD_DIM + 1]
            outs.append(acc[:HEAD_DIM] / l)
            lse_s[pl.ds(h, 1), :] = (m + jnp.log2(l)) * (1.0 / LOG2E)
        o_ref[0, :, cols] = jnp.concatenate(outs, axis=0).T.astype(o_ref.dtype)
    lse_ref[0] = lse_s[...].T


def _dilated_bias(slopes, dil):
    blk = DIL_BLOCK
    key = jnp.arange(2 * blk)[:, None]
    steps = jnp.arange(blk)[None, :] - (key - blk)
    steps_max = blk
    valid = (steps >= 0) & (steps <= steps_max)
    bias = -(slopes * LOG2E)[:, None, None] * (steps * dil).astype(F32)[None]
    rest = jnp.where(valid[None], bias, NEG)
    first = jnp.where((key >= blk)[None], rest, NEG)
    return jnp.stack([first, rest]).astype(F32)


def _proj_residue_kernel(x_ref, wqt_ref, wk_ref, wvt_ref, qt_ref, k_ref, vt_ref):
    xb = x_ref[0].astype(BF16)
    nt = (((1,), (1,)), ((), ()))
    qt_ref[0, 0] = lax.dot_general(wqt_ref[...], xb, nt, preferred_element_type=F32).astype(qt_ref.dtype)
    k_ref[0, 0] = jnp.dot(xb, wk_ref[...], preferred_element_type=F32).astype(k_ref.dtype)
    vt_ref[0, 0] = lax.dot_general(wvt_ref[...], xb, nt, preferred_element_type=F32).astype(vt_ref.dtype)


def _proj_residue(x, w, bsz, dil, tm=512):
    t, k = x.shape
    d = w.shape[1] // 3
    l_sub = t // bsz // dil
    tm = min(tm, l_sub)
    view = x.reshape(bsz, l_sub, dil * k)
    const = lambda shape: pl.BlockSpec(shape, lambda b, r, m: (0, 0))
    tr = pl.BlockSpec((1, 1, d, tm), lambda b, r, m: (b, r, 0, m))
    return pl.pallas_call(
        _proj_residue_kernel,
        grid=(bsz, dil, l_sub // tm),
        in_specs=[pl.BlockSpec((1, tm, k), lambda b, r, m: (b, m, r)), const((d, k)), const((k, d)), const((d, k))],
        out_specs=[tr, pl.BlockSpec((1, 1, tm, d), lambda b, r, m: (b, r, m, 0)), tr],
        out_shape=[
            jax.ShapeDtypeStruct((bsz, dil, d, l_sub), BF16),
            jax.ShapeDtypeStruct((bsz, dil, l_sub, d), BF16),
            jax.ShapeDtypeStruct((bsz, dil, d, l_sub), BF16),
        ],
        compiler_params=_cparams(3),
        name=f"proj_residue_d{dil}",
    )(view, w[:, :d].T, w[:, d:2 * d], w[:, 2 * d:].T)


def _dilated_group(qt, k, vt, slopes, g, window, dil):
    b, _, l_sub, d = k.shape
    blk = DIL_BLOCK
    assert window // dil == blk
    s = l_sub * dil
    nb = l_sub // blk
    bias = _dilated_bias(slopes, dil)
    prev = lambda n: jnp.maximum(n - 1, 0)

    o, lse = pl.pallas_call(
        _dilated_kernel,
        grid=(b, dil, nb),
        in_specs=[
            pl.BlockSpec((1, N_HEADS, 2 * blk, blk), lambda bi, r, n: (jnp.minimum(n, 1), 0, 0, 0)),
            pl.BlockSpec((1, 1, d, blk), lambda bi, r, n: (bi, r, 0, n)),
            pl.BlockSpec((1, 1, blk, d), lambda bi, r, n: (bi, r, n, 0)),
            pl.BlockSpec((1, 1, blk, d), lambda bi, r, n: (bi, r, prev(n), 0)),
            pl.BlockSpec((1, 1, d, blk), lambda bi, r, n: (bi, r, 0, n)),
            pl.BlockSpec((1, 1, d, blk), lambda bi, r, n: (bi, r, 0, prev(n))),
        ],
        out_specs=[
            pl.BlockSpec((1, blk, d), lambda bi, r, n: (bi, n, r)),
            pl.BlockSpec((1, blk, LANES), lambda bi, r, n: (bi, n, r)),
        ],
        out_shape=[
            jax.ShapeDtypeStruct((b, l_sub, dil * d), BF16),
            jax.ShapeDtypeStruct((b, l_sub, dil * LANES), F32),
        ],
        scratch_shapes=[pltpu.VMEM((LANES, blk), F32), pltpu.VMEM((N_SLABS, 2 * blk, 2 * blk), F32)],
        compiler_params=_cparams(3),
        name=f"dilated_g{g}",
    )(bias, qt, k, k, vt, vt)
    return o.reshape(b, s, d), lse.reshape(b, s, LANES)


def _merge_kernel(e_ref, o1, o2, o3, l1, l2, l3, out_ref):
    ls = [l1[...], l2[...], l3[...]]
    mx = jnp.maximum(jnp.maximum(ls[0], ls[1]), ls[2])
    es = [jnp.exp(x - mx) for x in ls]
    den = es[0] + es[1] + es[2]
    acc = None
    for e, o in zip(es, (o1, o2, o3)):
        w = jnp.dot(e / den, e_ref[...], preferred_element_type=F32, precision=_HI)
        term = w * o[...].astype(F32)
        acc = term if acc is None else acc + term
    out_ref[...] = acc.astype(out_ref.dtype)


def _merge_groups(os, lses, tm=512):
    t, d = os[0].shape
    expand = (jnp.arange(LANES)[:, None] == (jnp.arange(d)[None, :] // HEAD_DIM)).astype(F32)
    row = lambda w: pl.BlockSpec((tm, w), lambda i: (i, 0))
    return pl.pallas_call(
        _merge_kernel,
        grid=(t // tm,),
        in_specs=[pl.BlockSpec((LANES, d), lambda i: (0, 0))] + [row(d)] * 3 + [row(LANES)] * 3,
        out_specs=row(d),
        out_shape=jax.ShapeDtypeStruct((t, d), BF16),
        compiler_params=_cparams(1),
        name="merge_groups",
    )(expand, *os, *lses)


def _layer_norm(y, g, b):
    mu = jnp.mean(y, axis=-1, keepdims=True)
    yc = y - mu
    var = jnp.mean(yc * yc, axis=-1, keepdims=True)
    return yc * lax.rsqrt(var + LN_EPS) * g + b


U32 = jnp.uint32
TILE_WORDS = 2 * LANES
HIGH_HALF = np.uint32(0xFFFF0000)


def _pack_token_tiles(ref, y):
    n, d = y.shape
    npk = d // TILE_WORDS
    for c in range(npk):
        lo = y[:, (2 * c) * LANES:(2 * c + 1) * LANES].astype(BF16).astype(F32)
        hi = y[:, (2 * c + 1) * LANES:(2 * c + 2) * LANES].astype(BF16).astype(F32)
        word = (lax.bitcast_convert_type(lo, U32) >> 16) | (lax.bitcast_convert_type(hi, U32) & HIGH_HALF)
        ref[pl.ds(c, n, stride=npk), :] = word


def _unpack_token_tiles(ref, n, npk, lead=()):
    blocks = []
    for c in range(npk):
        word = ref[lead + (pl.ds(c, n, stride=npk), slice(None))]
        blocks.append(lax.bitcast_convert_type(word << 16, F32))
        blocks.append(lax.bitcast_convert_type(word & HIGH_HALF, F32))
    return jnp.concatenate(blocks, axis=1)


def _wo_ln_kernel(mix_ref, wo_ref, x_ref, g_ref, b_ref, o_ref, ot_ref):
    f = jnp.dot(mix_ref[...], wo_ref[...], preferred_element_type=F32)
    y = _layer_norm(DN_ALPHA * x_ref[...] + f, g_ref[...], b_ref[...])
    o_ref[...] = y
    _pack_token_tiles(ot_ref, y)


def _wo_ln(mix, wo, x, g, b, tm=256):
    t, d = x.shape
    npk = d // TILE_WORDS
    row = pl.BlockSpec((tm, d), lambda i: (i, 0))
    vec = pl.BlockSpec((1, d), lambda i: (0, 0))
    return pl.pallas_call(
        _wo_ln_kernel,
        grid=(t // tm,),
        in_specs=[row, pl.BlockSpec((d, d), lambda i: (0, 0)), row, vec, vec],
        out_specs=[row, pl.BlockSpec((tm * npk, LANES), lambda i: (i, 0))],
        out_shape=[jax.ShapeDtypeStruct((t, d), F32), jax.ShapeDtypeStruct((t * npk, LANES), U32)],
        compiler_params=_cparams(1),
        name="wo_ln",
    )(mix, wo, x, g.reshape(1, d), b.reshape(1, d))


def _router_kernel(wr_ref, rb_ref, x_ref, e_ref, w_ref, pos_ref, cnt_ref, run_ref, *, tm):
    step = pl.program_id(0)
    ne = N_EXPERTS
    gs = ne // N_GROUPS

    @pl.when(step == 0)
    def _():
        run_ref[...] = jnp.zeros(run_ref.shape, F32)

    logits = lax.dot_general(
        wr_ref[...], x_ref[...], (((1,), (1,)), ((), ())), preferred_element_type=F32, precision=_HI
    )
    scores = jax.nn.sigmoid(logits)
    choice = scores + rb_ref[...][:, :1]

    grp_rows = lax.broadcasted_iota(jnp.int32, (gs, tm), 0)
    blocks, gscore = [], []
    for gi in range(N_GROUPS):
        cb = choice[gi * gs:(gi + 1) * gs]
        m1 = jnp.max(cb, axis=0, keepdims=True)
        i1 = jnp.min(jnp.where(cb == m1, grp_rows, gs), axis=0, keepdims=True)
        m2 = jnp.max(jnp.where(grp_rows == i1, -jnp.inf, cb), axis=0, keepdims=True)
        blocks.append(cb)
        gscore.append(m1 + m2)
    masked = []
    for gi in range(N_GROUPS):
        beaten = jnp.zeros((1, tm), jnp.int32)
        for gj in range(N_GROUPS):
            if gj == gi:
                continue
            wins = (gscore[gj] > gscore[gi]) | ((gscore[gj] == gscore[gi]) & (gj < gi))
            beaten = beaten + wins.astype(jnp.int32)
        masked.append(jnp.where(beaten < TOPK_GROUPS, blocks[gi], NEG))
    cand = jnp.concatenate(masked, axis=0)

    rows = lax.broadcasted_iota(jnp.int32, (ne, tm), 0)
    member = jnp.zeros((ne, tm), F32)
    picks, raw_w = [], []
    for _ in range(TOP_K):
        cmax = jnp.max(cand, axis=0, keepdims=True)
        first = jnp.min(jnp.where(cand == cmax, rows, ne), axis=0, keepdims=True)
        hit = rows == first
        raw_w.append(jnp.sum(jnp.where(hit, scores, 0.0), axis=0, keepdims=True))
        member = jnp.where(hit, 1.0, member)
        cand = jnp.where(hit, -jnp.inf, cand)
        picks.append(first)
    wsum = raw_w[0]
    for r in raw_w[1:]:
        wsum = wsum + r

    tri = (lax.broadcasted_iota(jnp.int32, (tm, tm), 0) < lax.broadcasted_iota(jnp.int32, (tm, tm), 1))
    member_b = member.astype(BF16)
    before = jnp.dot(member_b, tri.astype(BF16), preferred_element_type=F32)
    run = run_ref[...]
    rank = before + jnp.concatenate([run] * (tm // LANES), axis=1)
    for k in range(TOP_K):
        hit = rows == picks[k]
        e_ref[pl.ds(k, 1), :] = picks[k]
        w_ref[pl.ds(k, 1), :] = raw_w[k] / wsum * ROUTED_SCALE
        pos_ref[pl.ds(k, 1), :] = jnp.sum(jnp.where(hit, rank, 0.0), axis=0, keepdims=True).astype(jnp.int32)
    run = run + jnp.dot(member_b, jnp.ones((tm, LANES), BF16), preferred_element_type=F32)
    run_ref[...] = run
    cnt_ref[...] = run


def _router(x, w_router, router_bias, tm=256):
    t, d = x.shape
    ne = N_EXPERTS
    wr_t = w_router.T
    rb = jnp.broadcast_to(router_bias.astype(F32)[:, None], (ne, LANES))
    tok = pl.BlockSpec((TOP_K, tm), lambda i: (0, i))
    return pl.pallas_call(
        functools.partial(_router_kernel, tm=tm),
        grid=(t // tm,),
        in_specs=[
            pl.BlockSpec((ne, d), lambda i: (0, 0)),
            pl.BlockSpec((ne, LANES), lambda i: (0, 0)),
            pl.BlockSpec((tm, d), lambda i: (i, 0)),
        ],
        out_specs=[tok, tok, tok, pl.BlockSpec((ne, LANES), lambda i: (0, 0))],
        out_shape=[
            jax.ShapeDtypeStruct((TOP_K, t), jnp.int32),
            jax.ShapeDtypeStruct((TOP_K, t), F32),
            jax.ShapeDtypeStruct((TOP_K, t), jnp.int32),
            jax.ShapeDtypeStruct((ne, LANES), F32),
        ],
        scratch_shapes=[pltpu.VMEM((ne, LANES), F32)],
        compiler_params=_cparams(1),
        name="router",
    )(wr_t, rb, x)


def _tile_rows(ref, i, npk):
    return ref.at[pl.ds(pl.multiple_of(i * npk, npk), npk)]


def _dispatch_kernel(dest_hbm, x_ref, xs_hbm, dest_smem, sem_idx, sem_rows, *, tm):
    step = pl.program_id(0)
    npk = x_ref.shape[0] // tm
    idx_copy = pltpu.make_async_copy(dest_hbm.at[step], dest_smem, sem_idx)
    idx_copy.start()
    idx_copy.wait()

    def row_copy(t, d):
        return pltpu.make_async_copy(_tile_rows(x_ref, t, npk), _tile_rows(xs_hbm, d, npk), sem_rows)

    def issue(t, c):
        for k in range(TOP_K):
            row_copy(t, dest_smem[k * tm + t]).start(priority=k % 2)
        return c

    lax.fori_loop(0, tm, issue, 0)

    def drain(t, c):
        for _ in range(TOP_K):
            row_copy(0, 0).wait()
        return c

    lax.fori_loop(0, tm, drain, 0)


def _dispatch(x_tiles, dest_tiles, n_rows, npk, tm):
    t = x_tiles.shape[0] // npk
    return pl.pallas_call(
        functools.partial(_dispatch_kernel, tm=tm),
        grid=(t // tm,),
        in_specs=[pl.BlockSpec(memory_space=pl.ANY), pl.BlockSpec((tm * npk, LANES), lambda i: (i, 0))],
        out_specs=pl.BlockSpec(memory_space=pl.ANY),
        out_shape=jax.ShapeDtypeStruct((n_rows * npk, LANES), x_tiles.dtype),
        scratch_shapes=[
            pltpu.SMEM((TOP_K * tm,), jnp.int32),
            pltpu.SemaphoreType.DMA,
            pltpu.SemaphoreType.DMA,
        ],
        compiler_params=_cparams(1),
        name="dispatch",
    )(dest_tiles, x_tiles)


def _experts_kernel(blk_e_ref, blk_row_ref, n_real_ref, xs_ref, wg_ref, wu_ref, wd_ref, ys_ref, *, rb):
    @pl.when(pl.program_id(0) < n_real_ref[0])
    def _():
        npk = xs_ref.shape[0] // rb
        xb = _unpack_token_tiles(xs_ref, rb, npk).astype(BF16)
        gate = jnp.dot(xb, wg_ref[0, 0].astype(BF16), preferred_element_type=F32)
        up = jnp.dot(xb, wu_ref[0, 0].astype(BF16), preferred_element_type=F32)
        h = (gate * jax.nn.sigmoid(gate) * up).astype(BF16)
        _pack_token_tiles(ys_ref, jnp.dot(h, wd_ref[0, 0].astype(BF16), preferred_element_type=F32))


def _experts(xs, we_gate, we_up, we_down, layer, rb, blk_e, blk_row, n_real):
    d, f = we_gate.shape[-2:]
    nc = d // TILE_WORDS
    p = xs.shape[0] // nc
    rows = pl.BlockSpec((rb * nc, LANES), lambda i, be, br, nr: (br[i], 0))
    grid_spec = pltpu.PrefetchScalarGridSpec(
        num_scalar_prefetch=3,
        grid=(p // rb,),
        in_specs=[
            rows,
            pl.BlockSpec((1, 1, d, f), lambda i, be, br, nr: (layer, be[i], 0, 0)),
            pl.BlockSpec((1, 1, d, f), lambda i, be, br, nr: (layer, be[i], 0, 0)),
            pl.BlockSpec((1, 1, f, d), lambda i, be, br, nr: (layer, be[i], 0, 0)),
        ],
        out_specs=rows,
    )
    return pl.pallas_call(
        functools.partial(_experts_kernel, rb=rb),
        grid_spec=grid_spec,
        out_shape=jax.ShapeDtypeStruct((p * nc, LANES), U32),
        compiler_params=_cparams(1),
        name="experts",
    )(blk_e, blk_row, n_real, xs, we_gate, we_up, we_down)


def _combine_kernel(dest_hbm, ys_hbm, x_ref, w_ref, sg_ref, su_ref, sd_ref, g_ref, b_ref, o_ref,
                    dest_smem0, dest_smem1, rows_ref, sem_idx, sem_rows, *, tm):
    dest_smems = (dest_smem0, dest_smem1)
    step = pl.program_id(0)
    n_steps = pl.num_programs(0)
    npk = rows_ref.shape[2] // tm
    slot = step % 2

    def row_copy(sl, k, t, d):
        return pltpu.make_async_copy(
            _tile_rows(ys_hbm, d, npk), _tile_rows(rows_ref.at[sl, k], t, npk), sem_rows.at[sl])

    def fetch(tile, sl):
        dest_smem = dest_smems[sl]
        idx_copy = pltpu.make_async_copy(dest_hbm.at[tile], dest_smem, sem_idx)
        idx_copy.start()
        idx_copy.wait()

        def issue(t, c):
            for k in range(TOP_K):
                row_copy(sl, k, t, dest_smem[k * tm + t]).start(priority=k % 2)
            return c

        lax.fori_loop(0, tm, issue, 0)

    @pl.when(step == 0)
    def _():
        fetch(0, 0)

    for sl in range(2):
        @pl.when((step + 1 < n_steps) & (slot == 1 - sl))
        def _(sl=sl):
            fetch(step + 1, sl)

    x = x_ref[...]
    xb = x.astype(BF16)
    gate = jnp.dot(xb, sg_ref[...], preferred_element_type=F32)
    up = jnp.dot(xb, su_ref[...], preferred_element_type=F32)
    h = (gate * jax.nn.sigmoid(gate) * up).astype(BF16)
    f = jnp.dot(h, sd_ref[...], preferred_element_type=F32)

    def drain(t, c):
        for _ in range(TOP_K):
            row_copy(slot, 0, 0, 0).wait()
        return c

    lax.fori_loop(0, tm, drain, 0)

    w = w_ref[...]
    for k in range(TOP_K):
        f = f + w[:, k:k + 1] * _unpack_token_tiles(rows_ref, tm, npk, lead=(slot, k))
    o_ref[...] = _layer_norm(DN_ALPHA * x + f, g_ref[...], b_ref[...])


def _combine(x, ys, dest_tiles, w_tok, ws_gate, ws_up, ws_down, g, b, tm=128):
    t, d = x.shape
    nc = d // TILE_WORDS
    f = ws_gate.shape[-1]
    row = pl.BlockSpec((tm, d), lambda i: (i, 0))
    vec = pl.BlockSpec((1, d), lambda i: (0, 0))
    return pl.pallas_call(
        functools.partial(_combine_kernel, tm=tm),
        grid=(t // tm,),
        in_specs=[
            pl.BlockSpec(memory_space=pl.ANY),
            pl.BlockSpec(memory_space=pl.ANY),
            row,
            pl.BlockSpec((tm, TOP_K), lambda i: (i, 0)),
            pl.BlockSpec((d, f), lambda i: (0, 0)),
            pl.BlockSpec((d, f), lambda i: (0, 0)),
            pl.BlockSpec((f, d), lambda i: (0, 0)),
            vec,
            vec,
        ],
        out_specs=row,
        out_shape=jax.ShapeDtypeStruct((t, d), F32),
        scratch_shapes=[
            pltpu.SMEM((TOP_K * tm,), jnp.int32),
            pltpu.SMEM((TOP_K * tm,), jnp.int32),
            pltpu.VMEM((2, TOP_K, tm * nc, LANES), U32),
            pltpu.SemaphoreType.DMA,
            pltpu.SemaphoreType.DMA((2,)),
        ],
        compiler_params=_cparams(1),
        name="combine",
    )(dest_tiles, ys, x, w_tok, ws_gate, ws_up, ws_down, g.reshape(1, d), b.reshape(1, d))


def _tile_major(a, tm):
    k, t = a.shape
    return a.reshape(k, t // tm, tm).transpose(1, 0, 2).reshape(t // tm, k * tm)


def _dest_kernel(start_ref, e_ref, pos_ref, dest_ref, *, tm):
    ne = N_EXPERTS
    start = jnp.concatenate([start_ref[...]] * (tm // LANES), axis=1)
    rows = lax.broadcasted_iota(jnp.int32, (ne, tm), 0)
    for k in range(TOP_K):
        hit = rows == e_ref[pl.ds(k, 1), :]
        first = jnp.sum(jnp.where(hit, start, 0), axis=0, keepdims=True)
        dest_ref[pl.ds(k, 1), :] = first + pos_ref[pl.ds(k, 1), :]


def _dest(starts, top_e, pos, tm=512):
    t = top_e.shape[1]
    tok = pl.BlockSpec((TOP_K, tm), lambda i: (0, i))
    return pl.pallas_call(
        functools.partial(_dest_kernel, tm=tm),
        grid=(t // tm,),
        in_specs=[pl.BlockSpec((N_EXPERTS, LANES), lambda i: (0, 0)), tok, tok],
        out_specs=tok,
        out_shape=jax.ShapeDtypeStruct((TOP_K, t), jnp.int32),
        compiler_params=_cparams(1),
        name="dest",
    )(jnp.broadcast_to(starts[:, None], (N_EXPERTS, LANES)), top_e, pos)


def _expert_row_block(t):
    mean_rows = t * TOP_K // N_EXPERTS
    return max(LANES, -(-(mean_rows + mean_rows // 4) // LANES) * LANES)


def _moe_ln(x, x_tiles, layer, w_router, router_bias, we_gate, we_up, we_down, ws_gate, ws_up, ws_down,
            g, b):
    t, d = x.shape
    rb = _expert_row_block(t)
    top_e, w, pos, counts = _router(x, w_router, router_bias)

    cnt = counts[:, 0].astype(jnp.int32)
    padded = (cnt + rb - 1) // rb * rb
    pend = jnp.cumsum(padded)
    n_blocks = t * TOP_K // rb + N_EXPERTS
    n_real = (pend[-1] // rb).astype(jnp.int32)
    blk_ids = jnp.minimum(jnp.arange(n_blocks, dtype=jnp.int32), jnp.maximum(n_real - 1, 0))
    blk_e = jnp.sum((pend[None, :] <= (blk_ids * rb)[:, None]).astype(jnp.int32), axis=1)
    blk_e = jnp.minimum(blk_e, N_EXPERTS - 1)

    dest = _dest(pend - padded, top_e, pos)
    xs = _dispatch(x_tiles, _tile_major(dest, DISPATCH_TOKENS), n_blocks * rb, d // TILE_WORDS,
                   tm=DISPATCH_TOKENS)
    ys = _experts(xs, we_gate, we_up, we_down, layer, rb, blk_e, blk_ids, n_real.reshape(1))
    return _combine(
        x, ys, _tile_major(dest, COMBINE_TOKENS), w.T, ws_gate.astype(BF16),
        ws_up.astype(BF16), ws_down.astype(BF16), g, b, tm=COMBINE_TOKENS,
    )


def kernel(x, w_qkv_a, w_qkv_b, w_o, ln_mix_g, ln_mix_b, w_router, router_bias, we_gate, we_up,
           we_down, ws_gate, ws_up, ws_down, ln_ffn_g, ln_ffn_b):
    bsz, s, d = x.shape
    t = bsz * s
    slopes = _alibi_slopes()
    xt = x.reshape(t, d)
    for i in range(DEPTH):
        if i % N_MIXERS == 0:
            qkv = _proj(xt, w_qkv_a[i // N_MIXERS].astype(BF16))
            mix = _moba_attention(qkv.reshape(bsz, s, -1), slopes).reshape(t, d)
        else:
            w_b = w_qkv_b[i // N_MIXERS].astype(BF16)
            outs, lses = [], []
            for g, (window, dil) in enumerate(DIL_GROUPS):
                qt, kk, vt = _proj_residue(xt, w_b[:, g * 3 * d:(g + 1) * 3 * d], bsz, dil)
                o, lse = _dilated_group(qt, kk, vt, slopes, g, window, dil)
                outs.append(o.reshape(t, d))
                lses.append(lse.reshape(t, LANES))
            mix = _merge_groups(outs, lses)
        x1, x1_tiles = _wo_ln(mix, w_o[i].astype(BF16), xt, ln_mix_g[i], ln_mix_b[i])
        xt = _moe_ln(x1, x1_tiles, i, w_router[i], router_bias[i], we_gate, we_up, we_down,
                     ws_gate[i], ws_up[i], ws_down[i], ln_ffn_g[i], ln_ffn_b[i])
    return xt.reshape(bsz, s, d)
```

```python
import functools
import math

import jax
import jax.numpy as jnp
import numpy as np
from jax import lax
from jax.experimental import pallas as pl
from jax.experimental.pallas import tpu as pltpu

N_HEADS = 16
HEAD_DIM = 64
DEPTH = 2
N_MIXERS = 2
MOBA_BLOCK = 256
MOBA_TOPK = 3
MOBA_KEY_BLOCKS_PER_STEP = 4
MOBA_QUERY_BLOCKS_PER_STEP = 2
MOBA_ACC_ROWS = 80
MOBA_BIAS_PARTS = 3
DIL_GROUPS = ((128, 1), (512, 4), (2048, 16))
DIL_BLOCK = 128
N_EXPERTS = 256
TOP_K = 8
N_GROUPS = 8
TOPK_GROUPS = 4
ROUTED_SCALE = 2.5
DN_ALPHA = (2 * DEPTH) ** 0.25
LN_EPS = 1e-5
NEG = -1e30
LOG2E = math.log2(math.e)

LANES = 128
HEADS_PER_SLAB = LANES // HEAD_DIM
N_SLABS = N_HEADS // HEADS_PER_SLAB

DISPATCH_TOKENS = 512
COMBINE_TOKENS = 128

BF16 = jnp.bfloat16
F32 = jnp.float32
_HI = lax.Precision.HIGHEST

_ARB = "arbitrary"


def _cparams(n_axes, vmem_mb=48):
    return pltpu.CompilerParams(
        dimension_semantics=(_ARB,) * n_axes, vmem_limit_bytes=vmem_mb * 1024 * 1024
    )


def _alibi_slopes():
    return 2.0 ** (-8.0 * jnp.arange(1, N_HEADS + 1, dtype=F32) / N_HEADS)


def _bf16_parts(x, n):
    parts, rest = [], x
    for _ in range(n):
        top = lax.bitcast_convert_type(lax.bitcast_convert_type(rest, jnp.uint32) & np.uint32(0xFFFF0000), F32)
        parts.append(top.astype(BF16))
        rest = rest - top
    return parts


def _proj_kernel(x_ref, w_ref, o_ref):
    o_ref[...] = jnp.dot(
        x_ref[...].astype(BF16), w_ref[...], preferred_element_type=F32
    ).astype(o_ref.dtype)


def _proj(x, w, tm=512, tn=1024):
    t, k = x.shape
    n = w.shape[1]
    return pl.pallas_call(
        _proj_kernel,
        grid=(t // tm, n // tn),
        in_specs=[
            pl.BlockSpec((tm, k), lambda i, j: (i, 0)),
            pl.BlockSpec((k, tn), lambda i, j: (0, j)),
        ],
        out_specs=pl.BlockSpec((tm, tn), lambda i, j: (i, j)),
        out_shape=jax.ShapeDtypeStruct((t, n), BF16),
        compiler_params=_cparams(2),
        name="proj",
    )(x, w)


def _moba_kernel(slopes_ref, q_ref, k_ref, v_ref, wb_ref, o_ref, ka_ref, vt_ref, km_ref, sel_ref, s_ref,
                 *, n_blocks):
    blk = MOBA_BLOCK
    unroll = MOBA_KEY_BLOCKS_PER_STEP
    span = unroll * blk
    acc_rows = MOBA_ACC_ROWS
    h2 = pl.program_id(1)
    i = pl.program_id(2)
    heads = range(HEADS_PER_SLAB)

    @pl.when(i == 0)
    def _prepare_keys():
        lane = lax.broadcasted_iota(jnp.int32, (blk, LANES), 1)
        row_f = lax.broadcasted_iota(jnp.int32, (blk, LANES), 0).astype(F32)
        tail = lax.broadcasted_iota(jnp.int32, (acc_rows - HEAD_DIM, blk), 0)
        ones_row = jnp.where(tail == 0, 1.0, 0.0)

        def body(n, c):
            r0 = pl.multiple_of(n * blk, blk)
            vt = v_ref[0, pl.ds(r0, blk), :].astype(F32).T
            for hh in heads:
                vt_ref[n, hh] = jnp.concatenate(
                    [vt[hh * HEAD_DIM:(hh + 1) * HEAD_DIM], ones_row], axis=0).astype(BF16)
            kb = k_ref[0, pl.ds(r0, blk), :]
            km_ref[pl.ds(n, 1), :] = jnp.mean(kb.astype(F32), axis=0, keepdims=True)
            sub = jnp.asarray(n % unroll, F32)
            aux = jnp.where(lane < MOBA_BIAS_PARTS, sub, jnp.where(lane < 2 * MOBA_BIAS_PARTS, row_f, 0.0))
            ka_ref[pl.ds(r0, blk), :] = jnp.concatenate([kb, aux.astype(BF16)], axis=1)
            return c

        lax.fori_loop(0, n_blocks, body, 0)

    qb = MOBA_QUERY_BLOCKS_PER_STEP
    wide = qb * blk
    i_first = pl.program_id(2) * qb
    qt = q_ref[0].astype(F32).T
    feat = lax.broadcasted_iota(jnp.int32, qt.shape, 0)
    kmean = km_ref[...]
    km_lane = lax.broadcasted_iota(jnp.int32, kmean.shape, 1)
    blk_row = lax.broadcasted_iota(jnp.int32, (n_blocks, wide), 0)
    own_blk = i_first + lax.broadcasted_iota(jnp.int32, (n_blocks, wide), 1) // blk

    slopes = [slopes_ref[h2 * HEADS_PER_SLAB + hh] for hh in heads]
    qt_s = []
    for hh in heads:
        in_head = (feat >= hh * HEAD_DIM) & (feat < (hh + 1) * HEAD_DIM)
        qt_h = jnp.where(in_head, qt, 0.0)
        qt_s.append((qt_h * (HEAD_DIM ** -0.5 * LOG2E)).astype(BF16))

        km_h = jnp.where((km_lane >= hh * HEAD_DIM) & (km_lane < (hh + 1) * HEAD_DIM), kmean, 0.0)
        gate = jnp.dot(km_h, qt_h, preferred_element_type=F32, precision=_HI)
        g = jnp.where(blk_row < own_blk, gate, -jnp.inf)
        sel = jnp.zeros(g.shape, F32)
        for _ in range(MOBA_TOPK):
            gmax = jnp.max(g, axis=0, keepdims=True)
            first = jnp.min(jnp.where(g == gmax, blk_row, n_blocks), axis=0, keepdims=True)
            hit = blk_row == first
            sel = jnp.where(hit & (gmax > -jnp.inf), 1.0, sel)
            g = jnp.where(hit, -jnp.inf, g)
        sel_ref[hh] = sel

    w_all = jnp.concatenate(
        [jnp.concatenate(qt_s, axis=1), wb_ref[0]], axis=0)

    def scores(span_idx):
        r0 = pl.multiple_of(span_idx * span, span)
        return jnp.dot(ka_ref[pl.ds(r0, span), :], w_all, preferred_element_type=F32)

    key_row = lax.broadcasted_iota(jnp.int32, (blk, blk), 0)
    qry_col = lax.broadcasted_iota(jnp.int32, (blk, blk), 1)
    groups = [(hh, q) for hh in heads for q in range(qb)]

    def consume(s_ref, span_idx, state, last):
        j0 = span_idx * unroll
        out = []
        for gi, (hh, q) in enumerate(groups):
            m, acc = state[gi]
            i_q = i_first + q
            c = slopes[hh] * LOG2E * jnp.asarray(blk * (j0 - i_q), F32)
            c0 = (hh * qb + q) * blk
            for u in range(unroll):
                j = j0 + u
                s = s_ref[u * blk:(u + 1) * blk, c0:c0 + blk]
                picked = sel_ref[hh, pl.ds(j, 1), q * blk:(q + 1) * blk]
                if last:
                    own = j == i_q
                    s = jnp.where(key_row - qry_col > jnp.where(own, 0, blk), NEG, s)
                    picked = picked + jnp.where(own, 1.0, 0.0)
                ch = picked > 0.5
                m_new = jnp.where(ch, jnp.maximum(m, jnp.max(s, axis=0, keepdims=True) + c), m)
                p = jnp.exp2(s - jnp.where(ch, m_new - c, -NEG))
                acc = jnp.exp2(m - m_new) * acc + jnp.dot(
                    vt_ref[j, hh], p.astype(BF16), preferred_element_type=F32)
                m = m_new
            out.append((m, acc))
        return tuple(out)

    n_full = i_first // unroll
    s_ref[...] = scores(0)

    def body(it, state):
        nxt = scores(it + 1)
        state = consume(s_ref, it, state, last=False)
        s_ref[...] = nxt
        return state

    init = tuple((jnp.full((1, blk), NEG, F32), jnp.zeros((acc_rows, blk), F32)) for _ in groups)
    state = lax.fori_loop(0, n_full, body, init)
    state = consume(s_ref, n_full, state, last=True)

    outs = [acc[:HEAD_DIM] / acc[HEAD_DIM:HEAD_DIM + 1] for _, acc in state]
    per_block = [jnp.concatenate([outs[hh * qb + q] for hh in heads], axis=0) for q in range(qb)]
    o_ref[0] = jnp.concatenate(per_block, axis=1).T.astype(o_ref.dtype)


def _moba_attention(qkv, slopes):
    b, s, _ = qkv.shape
    blk = MOBA_BLOCK
    n_blocks = s // blk
    qb = MOBA_QUERY_BLOCKS_PER_STEP
    assert n_blocks % MOBA_KEY_BLOCKS_PER_STEP == 0 and MOBA_KEY_BLOCKS_PER_STEP % qb == 0
    wide = HEADS_PER_SLAB * qb * blk
    grid_spec = pltpu.PrefetchScalarGridSpec(
        num_scalar_prefetch=1,
        grid=(b, N_SLABS, n_blocks // qb),
        in_specs=[
            pl.BlockSpec((1, qb * blk, LANES), lambda bi, h, i, sl: (bi, i, h)),
            pl.BlockSpec((1, s, LANES), lambda bi, h, i, sl: (bi, 0, N_SLABS + h)),
            pl.BlockSpec((1, s, LANES), lambda bi, h, i, sl: (bi, 0, 2 * N_SLABS + h)),
            pl.BlockSpec((1, LANES, wide), lambda bi, h, i, sl: (h, 0, 0)),
        ],
        out_specs=pl.BlockSpec((1, qb * blk, LANES), lambda bi, h, i, sl: (bi, i, h)),
        scratch_shapes=[
            pltpu.VMEM((s, 2 * LANES), BF16),
            pltpu.VMEM((n_blocks, HEADS_PER_SLAB, MOBA_ACC_ROWS, blk), BF16),
            pltpu.VMEM((n_blocks, LANES), F32),
            pltpu.VMEM((HEADS_PER_SLAB, n_blocks, qb * blk), F32),
            pltpu.VMEM((MOBA_KEY_BLOCKS_PER_STEP * blk, wide), F32),
        ],
    )
    return pl.pallas_call(
        functools.partial(_moba_kernel, n_blocks=n_blocks),
        grid_spec=grid_spec,
        out_shape=jax.ShapeDtypeStruct((b, s, N_HEADS * HEAD_DIM), BF16),
        compiler_params=_cparams(3),
        name="moba",
    )(slopes, qkv, qkv, qkv, _moba_slope_rows(slopes))


def _moba_slope_rows(slopes):
    blk = MOBA_BLOCK
    wide = MOBA_QUERY_BLOCKS_PER_STEP * blk
    parts = _bf16_parts(slopes * LOG2E, MOBA_BIAS_PARTS)
    sp = jnp.stack([p.astype(F32) for p in parts], axis=1)
    rows = jnp.concatenate([sp * blk, sp], axis=1).reshape(N_SLABS, HEADS_PER_SLAB, 2 * MOBA_BIAS_PARTS)
    wb = jnp.zeros((N_SLABS, LANES, HEADS_PER_SLAB * wide), F32)
    for hh in range(HEADS_PER_SLAB):
        wb = wb.at[:, :2 * MOBA_BIAS_PARTS, hh * wide:(hh + 1) * wide].set(rows[:, hh, :, None])
    return wb.astype(BF16)


def _dilated_kernel(bias_ref, qt_ref, kc_ref, kp_ref, vtc_ref, vtp_ref, o_ref, lse_ref, lse_s, s_scr):
    blk = DIL_BLOCK
    lse_s[...] = jnp.zeros(lse_s.shape, F32)
    tail = lax.broadcasted_iota(jnp.int32, (MOBA_ACC_ROWS - HEAD_DIM, 2 * blk), 0)
    ones_row = jnp.where(tail == 0, 1.0, 0.0).astype(BF16)
    feat = lax.broadcasted_iota(jnp.int32, (LANES, blk), 0)
    for h2 in range(N_SLABS):
        cols = slice(h2 * LANES, (h2 + 1) * LANES)
        qt = qt_ref[0, 0, cols, :].astype(F32) * (HEAD_DIM ** -0.5 * LOG2E)
        kband = jnp.concatenate([kp_ref[0, 0, :, cols], kc_ref[0, 0, :, cols]], axis=0)
        qt_both = jnp.concatenate(
            [jnp.where((feat >= hh * HEAD_DIM) & (feat < (hh + 1) * HEAD_DIM), qt, 0.0)
             for hh in range(HEADS_PER_SLAB)], axis=1).astype(BF16)
        s_scr[h2] = jnp.dot(kband, qt_both, preferred_element_type=F32)
    for h2 in range(N_SLABS):
        cols = slice(h2 * LANES, (h2 + 1) * LANES)
        vt = jnp.concatenate([vtp_ref[0, 0, cols, :], vtc_ref[0, 0, cols, :]], axis=1)
        outs = []
        for hh in range(HEADS_PER_SLAB):
            h = h2 * HEADS_PER_SLAB + hh
            s = s_scr[h2, :, hh * blk:(hh + 1) * blk] + bias_ref[0, h]
            m = jnp.max(s, axis=0, keepdims=True)
            p = jnp.exp2(s - m).astype(BF16)
            vt_h = jnp.concatenate([vt[hh * HEAD_DIM:(hh + 1) * HEAD_DIM], ones_row], axis=0)
            acc = jnp.dot(vt_h, p, preferred_element_type=F32)
            l = acc[HEAD_DIM:HEAD_DIM + 1]
            outs.append(acc[:HEAD_DIM] / l)
            lse_s[pl.ds(h, 1), :] = (m + jnp.log2(l)) * (1.0 / LOG2E)
        o_ref[0, :, cols] = jnp.concatenate(outs, axis=0).T.astype(o_ref.dtype)
    lse_ref[0] = lse_s[...].T


def _dilated_bias(slopes, dil):
    blk = DIL_BLOCK
    key = jnp.arange(2 * blk)[:, None]
    steps = jnp.arange(blk)[None, :] - (key - blk)
    steps_max = blk
    valid = (steps >= 0) & (steps <= steps_max)
    bias = -(slopes * LOG2E)[:, None, None] * (steps * dil).astype(F32)[None]
    rest = jnp.where(valid[None], bias, NEG)
    first = jnp.where((key >= blk)[None], rest, NEG)
    return jnp.stack([first, rest]).astype(F32)


def _proj_residue_kernel(x_ref, wqt_ref, wk_ref, wvt_ref, qt_ref, k_ref, vt_ref):
    xb = x_ref[0].astype(BF16)
    nt = (((1,), (1,)), ((), ()))
    qt_ref[0, 0] = lax.dot_general(wqt_ref[...], xb, nt, preferred_element_type=F32).astype(qt_ref.dtype)
    k_ref[0, 0] = jnp.dot(xb, wk_ref[...], preferred_element_type=F32).astype(k_ref.dtype)
    vt_ref[0, 0] = lax.dot_general(wvt_ref[...], xb, nt, preferred_element_type=F32).astype(vt_ref.dtype)


def _proj_residue(x, w, bsz, dil, tm=512):
    t, k = x.shape
    d = w.shape[1] // 3
    l_sub = t // bsz // dil
    tm = min(tm, l_sub)
    view = x.reshape(bsz, l_sub, dil * k)
    const = lambda shape: pl.BlockSpec(shape, lambda b, r, m: (0, 0))
    tr = pl.BlockSpec((1, 1, d, tm), lambda b, r, m: (b, r, 0, m))
    return pl.pallas_call(
        _proj_residue_kernel,
        grid=(bsz, dil, l_sub // tm),
        in_specs=[pl.BlockSpec((1, tm, k), lambda b, r, m: (b, m, r)), const((d, k)), const((k, d)), const((d, k))],
        out_specs=[tr, pl.BlockSpec((1, 1, tm, d), lambda b, r, m: (b, r, m, 0)), tr],
        out_shape=[
            jax.ShapeDtypeStruct((bsz, dil, d, l_sub), BF16),
            jax.ShapeDtypeStruct((bsz, dil, l_sub, d), BF16),
            jax.ShapeDtypeStruct((bsz, dil, d, l_sub), BF16),
        ],
        compiler_params=_cparams(3),
        name=f"proj_residue_d{dil}",
    )(view, w[:, :d].T, w[:, d:2 * d], w[:, 2 * d:].T)


def _dilated_group(qt, k, vt, slopes, g, window, dil):
    b, _, l_sub, d = k.shape
    blk = DIL_BLOCK
    assert window // dil == blk
    s = l_sub * dil
    nb = l_sub // blk
    bias = _dilated_bias(slopes, dil)
    prev = lambda n: jnp.maximum(n - 1, 0)

    o, lse = pl.pallas_call(
        _dilated_kernel,
        grid=(b, dil, nb),
        in_specs=[
            pl.BlockSpec((1, N_HEADS, 2 * blk, blk), lambda bi, r, n: (jnp.minimum(n, 1), 0, 0, 0)),
            pl.BlockSpec((1, 1, d, blk), lambda bi, r, n: (bi, r, 0, n)),
            pl.BlockSpec((1, 1, blk, d), lambda bi, r, n: (bi, r, n, 0)),
            pl.BlockSpec((1, 1, blk, d), lambda bi, r, n: (bi, r, prev(n), 0)),
            pl.BlockSpec((1, 1, d, blk), lambda bi, r, n: (bi, r, 0, n)),
            pl.BlockSpec((1, 1, d, blk), lambda bi, r, n: (bi, r, 0, prev(n))),
        ],
        out_specs=[
            pl.BlockSpec((1, blk, d), lambda bi, r, n: (bi, n, r)),
            pl.BlockSpec((1, blk, LANES), lambda bi, r, n: (bi, n, r)),
        ],
        out_shape=[
            jax.ShapeDtypeStruct((b, l_sub, dil * d), BF16),
            jax.ShapeDtypeStruct((b, l_sub, dil * LANES), F32),
        ],
        scratch_shapes=[pltpu.VMEM((LANES, blk), F32), pltpu.VMEM((N_SLABS, 2 * blk, 2 * blk), F32)],
        compiler_params=_cparams(3),
        name=f"dilated_g{g}",
    )(bias, qt, k, k, vt, vt)
    return o.reshape(b, s, d), lse.reshape(b, s, LANES)


def _merge_kernel(e_ref, o1, o2, o3, l1, l2, l3, out_ref):
    ls = [l1[...], l2[...], l3[...]]
    mx = jnp.maximum(jnp.maximum(ls[0], ls[1]), ls[2])
    es = [jnp.exp(x - mx) for x in ls]
    den = es[0] + es[1] + es[2]
    acc = None
    for e, o in zip(es, (o1, o2, o3)):
        w = jnp.dot(e / den, e_ref[...], preferred_element_type=F32, precision=_HI)
        term = w * o[...].astype(F32)
        acc = term if acc is None else acc + term
    out_ref[...] = acc.astype(out_ref.dtype)


def _merge_groups(os, lses, tm=512):
    t, d = os[0].shape
    expand = (jnp.arange(LANES)[:, None] == (jnp.arange(d)[None, :] // HEAD_DIM)).astype(F32)
    row = lambda w: pl.BlockSpec((tm, w), lambda i: (i, 0))
    return pl.pallas_call(
        _merge_kernel,
        grid=(t // tm,),
        in_specs=[pl.BlockSpec((LANES, d), lambda i: (0, 0))] + [row(d)] * 3 + [row(LANES)] * 3,
        out_specs=row(d),
        out_shape=jax.ShapeDtypeStruct((t, d), BF16),
        compiler_params=_cparams(1),
        name="merge_groups",
    )(expand, *os, *lses)


def _layer_norm(y, g, b):
    mu = jnp.mean(y, axis=-1, keepdims=True)
    yc = y - mu
    var = jnp.mean(yc * yc, axis=-1, keepdims=True)
    return yc * lax.rsqrt(var + LN_EPS) * g + b


U32 = jnp.uint32
TILE_WORDS = 2 * LANES
HIGH_HALF = np.uint32(0xFFFF0000)


def _pack_token_tiles(ref, y):
    n, d = y.shape
    npk = d // TILE_WORDS
    for c in range(npk):
        lo = y[:, (2 * c) * LANES:(2 * c + 1) * LANES].astype(BF16).astype(F32)
        hi = y[:, (2 * c + 1) * LANES:(2 * c + 2) * LANES].astype(BF16).astype(F32)
        word = (lax.bitcast_convert_type(lo, U32) >> 16) | (lax.bitcast_convert_type(hi, U32) & HIGH_HALF)
        ref[pl.ds(c, n, stride=npk), :] = word


def _unpack_token_tiles(ref, n, npk, lead=()):
    blocks = []
    for c in range(npk):
        word = ref[lead + (pl.ds(c, n, stride=npk), slice(None))]
        blocks.append(lax.bitcast_convert_type(word << 16, F32))
        blocks.append(lax.bitcast_convert_type(word & HIGH_HALF, F32))
    return jnp.concatenate(blocks, axis=1)


def _wo_ln_kernel(mix_ref, wo_ref, x_ref, g_ref, b_ref, o_ref, ot_ref):
    f = jnp.dot(mix_ref[...], wo_ref[...], preferred_element_type=F32)
    y = _layer_norm(DN_ALPHA * x_ref[...] + f, g_ref[...], b_ref[...])
    o_ref[...] = y
    _pack_token_tiles(ot_ref, y)


def _wo_ln(mix, wo, x, g, b, tm=256):
    t, d = x.shape
    npk = d // TILE_WORDS
    row = pl.BlockSpec((tm, d), lambda i: (i, 0))
    vec = pl.BlockSpec((1, d), lambda i: (0, 0))
    return pl.pallas_call(
        _wo_ln_kernel,
        grid=(t // tm,),
        in_specs=[row, pl.BlockSpec((d, d), lambda i: (0, 0)), row, vec, vec],
        out_specs=[row, pl.BlockSpec((tm * npk, LANES), lambda i: (i, 0))],
        out_shape=[jax.ShapeDtypeStruct((t, d), F32), jax.ShapeDtypeStruct((t * npk, LANES), U32)],
        compiler_params=_cparams(1),
        name="wo_ln",
    )(mix, wo, x, g.reshape(1, d), b.reshape(1, d))


def _router_kernel(wr_ref, rb_ref, x_ref, e_ref, w_ref, pos_ref, cnt_ref, run_ref, *, tm):
    step = pl.program_id(0)
    ne = N_EXPERTS
    gs = ne // N_GROUPS

    @pl.when(step == 0)
    def _():
        run_ref[...] = jnp.zeros(run_ref.shape, F32)

    x = x_ref[...]
    x_hi = x.astype(BF16)
    x_lo = (x - x_hi.astype(F32)).astype(BF16)
    nt = (((1,), (1,)), ((), ()))
    logits = (lax.dot_general(wr_ref[0], x_hi, nt, preferred_element_type=F32)
              + lax.dot_general(wr_ref[0], x_lo, nt, preferred_element_type=F32)
              + lax.dot_general(wr_ref[1], x_hi, nt, preferred_element_type=F32))
    scores = jax.nn.sigmoid(logits)
    choice = scores + rb_ref[...][:, :1]

    grp_rows = lax.broadcasted_iota(jnp.int32, (gs, tm), 0)
    blocks, gscore = [], []
    for gi in range(N_GROUPS):
        cb = choice[gi * gs:(gi + 1) * gs]
        m1 = jnp.max(cb, axis=0, keepdims=True)
        i1 = jnp.min(jnp.where(cb == m1, grp_rows, gs), axis=0, keepdims=True)
        m2 = jnp.max(jnp.where(grp_rows == i1, -jnp.inf, cb), axis=0, keepdims=True)
        blocks.append(cb)
        gscore.append(m1 + m2)
    masked = []
    for gi in range(N_GROUPS):
        beaten = jnp.zeros((1, tm), jnp.int32)
        for gj in range(N_GROUPS):
            if gj == gi:
                continue
            wins = (gscore[gj] > gscore[gi]) | ((gscore[gj] == gscore[gi]) & (gj < gi))
            beaten = beaten + wins.astype(jnp.int32)
        masked.append(jnp.where(beaten < TOPK_GROUPS, blocks[gi], NEG))
    cand = jnp.concatenate(masked, axis=0)

    rows = lax.broadcasted_iota(jnp.int32, (ne, tm), 0)
    member = jnp.zeros((ne, tm), F32)
    picks, raw_w = [], []
    for _ in range(TOP_K):
        cmax = jnp.max(cand, axis=0, keepdims=True)
        first = jnp.min(jnp.where(cand == cmax, rows, ne), axis=0, keepdims=True)
        hit = rows == first
        raw_w.append(jnp.sum(jnp.where(hit, scores, 0.0), axis=0, keepdims=True))
        member = jnp.where(hit, 1.0, member)
        cand = jnp.where(hit, -jnp.inf, cand)
        picks.append(first)
    wsum = raw_w[0]
    for r in raw_w[1:]:
        wsum = wsum + r

    tri = (lax.broadcasted_iota(jnp.int32, (tm, tm), 0) < lax.broadcasted_iota(jnp.int32, (tm, tm), 1))
    member_b = member.astype(BF16)
    before = jnp.dot(member_b, tri.astype(BF16), preferred_element_type=F32)
    run = run_ref[...]
    rank = before + jnp.concatenate([run] * (tm // LANES), axis=1)
    for k in range(TOP_K):
        hit = rows == picks[k]
        e_ref[pl.ds(k, 1), :] = picks[k]
        w_ref[pl.ds(k, 1), :] = raw_w[k] / wsum * ROUTED_SCALE
        pos_ref[pl.ds(k, 1), :] = jnp.sum(jnp.where(hit, rank, 0.0), axis=0, keepdims=True).astype(jnp.int32)
    run = run + jnp.dot(member_b, jnp.ones((tm, LANES), BF16), preferred_element_type=F32)
    run_ref[...] = run
    cnt_ref[...] = run


def _router(x, w_router, router_bias, tm=256):
    t, d = x.shape
    ne = N_EXPERTS
    wr_parts = jnp.stack(_bf16_parts(w_router.T.astype(F32), 2))
    rb = jnp.broadcast_to(router_bias.astype(F32)[:, None], (ne, LANES))
    tok = pl.BlockSpec((TOP_K, tm), lambda i: (0, i))
    return pl.pallas_call(
        functools.partial(_router_kernel, tm=tm),
        grid=(t // tm,),
        in_specs=[
            pl.BlockSpec((2, ne, d), lambda i: (0, 0, 0)),
            pl.BlockSpec((ne, LANES), lambda i: (0, 0)),
            pl.BlockSpec((tm, d), lambda i: (i, 0)),
        ],
        out_specs=[tok, tok, tok, pl.BlockSpec((ne, LANES), lambda i: (0, 0))],
        out_shape=[
            jax.ShapeDtypeStruct((TOP_K, t), jnp.int32),
            jax.ShapeDtypeStruct((TOP_K, t), F32),
            jax.ShapeDtypeStruct((TOP_K, t), jnp.int32),
            jax.ShapeDtypeStruct((ne, LANES), F32),
        ],
        scratch_shapes=[pltpu.VMEM((ne, LANES), F32)],
        compiler_params=_cparams(1),
        name="router",
    )(wr_parts, rb, x)


def _tile_rows(ref, i, npk):
    return ref.at[pl.ds(pl.multiple_of(i * npk, npk), npk)]


def _dispatch_kernel(dest_hbm, x_ref, xs_hbm, dest_smem, sem_idx, sem_rows, *, tm):
    step = pl.program_id(0)
    npk = x_ref.shape[0] // tm
    idx_copy = pltpu.make_async_copy(dest_hbm.at[step], dest_smem, sem_idx)
    idx_copy.start()
    idx_copy.wait()

    def row_copy(t, d):
        return pltpu.make_async_copy(_tile_rows(x_ref, t, npk), _tile_rows(xs_hbm, d, npk), sem_rows)

    def issue(t, c):
        for k in range(TOP_K):
            row_copy(t, dest_smem[k * tm + t]).start(priority=k % 2)
        return c

    lax.fori_loop(0, tm, issue, 0)

    def drain(t, c):
        for _ in range(TOP_K):
            row_copy(0, 0).wait()
        return c

    lax.fori_loop(0, tm, drain, 0)


def _dispatch(x_tiles, dest_tiles, n_rows, npk, tm):
    t = x_tiles.shape[0] // npk
    return pl.pallas_call(
        functools.partial(_dispatch_kernel, tm=tm),
        grid=(t // tm,),
        in_specs=[pl.BlockSpec(memory_space=pl.ANY), pl.BlockSpec((tm * npk, LANES), lambda i: (i, 0))],
        out_specs=pl.BlockSpec(memory_space=pl.ANY),
        out_shape=jax.ShapeDtypeStruct((n_rows * npk, LANES), x_tiles.dtype),
        scratch_shapes=[
            pltpu.SMEM((TOP_K * tm,), jnp.int32),
            pltpu.SemaphoreType.DMA,
            pltpu.SemaphoreType.DMA,
        ],
        compiler_params=_cparams(1),
        name="dispatch",
    )(dest_tiles, x_tiles)


def _experts_kernel(blk_e_ref, blk_row_ref, n_real_ref, xs_ref, wg_ref, wu_ref, wd_ref, ys_ref, *, rb):
    @pl.when(pl.program_id(0) < n_real_ref[0])
    def _():
        npk = xs_ref.shape[0] // rb
        xb = _unpack_token_tiles(xs_ref, rb, npk).astype(BF16)
        gate = jnp.dot(xb, wg_ref[0, 0].astype(BF16), preferred_element_type=F32)
        up = jnp.dot(xb, wu_ref[0, 0].astype(BF16), preferred_element_type=F32)
        h = (gate * jax.nn.sigmoid(gate) * up).astype(BF16)
        _pack_token_tiles(ys_ref, jnp.dot(h, wd_ref[0, 0].astype(BF16), preferred_element_type=F32))


def _experts(xs, we_gate, we_up, we_down, layer, rb, blk_e, blk_row, n_real):
    d, f = we_gate.shape[-2:]
    nc = d // TILE_WORDS
    p = xs.shape[0] // nc
    rows = pl.BlockSpec((rb * nc, LANES), lambda i, be, br, nr: (br[i], 0))
    grid_spec = pltpu.PrefetchScalarGridSpec(
        num_scalar_prefetch=3,
        grid=(p // rb,),
        in_specs=[
            rows,
            pl.BlockSpec((1, 1, d, f), lambda i, be, br, nr: (layer, be[i], 0, 0)),
            pl.BlockSpec((1, 1, d, f), lambda i, be, br, nr: (layer, be[i], 0, 0)),
            pl.BlockSpec((1, 1, f, d), lambda i, be, br, nr: (layer, be[i], 0, 0)),
        ],
        out_specs=rows,
    )
    return pl.pallas_call(
        functools.partial(_experts_kernel, rb=rb),
        grid_spec=grid_spec,
        out_shape=jax.ShapeDtypeStruct((p * nc, LANES), U32),
        compiler_params=_cparams(1),
        name="experts",
    )(blk_e, blk_row, n_real, xs, we_gate, we_up, we_down)


def _combine_kernel(dest_hbm, ys_hbm, x_ref, w_ref, sg_ref, su_ref, sd_ref, g_ref, b_ref, o_ref,
                    dest_smem0, dest_smem1, rows_ref, sem_idx, sem_rows, *, tm):
    dest_smems = (dest_smem0, dest_smem1)
    step = pl.program_id(0)
    n_steps = pl.num_programs(0)
    npk = rows_ref.shape[2] // tm
    slot = step % 2

    def row_copy(sl, k, t, d):
        return pltpu.make_async_copy(
            _tile_rows(ys_hbm, d, npk), _tile_rows(rows_ref.at[sl, k], t, npk), sem_rows.at[sl])

    def idx_copy(tile, sl):
        return pltpu.make_async_copy(dest_hbm.at[tile], dest_smems[sl], sem_idx.at[sl])

    def issue_rows(sl):
        dest_smem = dest_smems[sl]

        def issue(t, c):
            for k in range(TOP_K):
                row_copy(sl, k, t, dest_smem[k * tm + t]).start(priority=k % 2)
            return c

        lax.fori_loop(0, tm, issue, 0)

    @pl.when(step == 0)
    def _():
        idx_copy(0, 0).start()
        idx_copy(0, 0).wait()
        issue_rows(0)

        @pl.when(n_steps > 1)
        def _():
            idx_copy(1, 1).start()

    for sl in range(2):
        @pl.when((step + 1 < n_steps) & (slot == 1 - sl))
        def _(sl=sl):
            idx_copy(step + 1, sl).wait()

            @pl.when(step + 2 < n_steps)
            def _():
                idx_copy(step + 2, 1 - sl).start()

            issue_rows(sl)

    x = x_ref[...]
    xb = x.astype(BF16)
    gate = jnp.dot(xb, sg_ref[...], preferred_element_type=F32)
    up = jnp.dot(xb, su_ref[...], preferred_element_type=F32)
    h = (gate * jax.nn.sigmoid(gate) * up).astype(BF16)
    f = jnp.dot(h, sd_ref[...], preferred_element_type=F32)

    def drain(t, c):
        for _ in range(TOP_K):
            row_copy(slot, 0, 0, 0).wait()
        return c

    lax.fori_loop(0, tm, drain, 0)

    w = w_ref[...]
    for k in range(TOP_K):
        f = f + w[:, k:k + 1] * _unpack_token_tiles(rows_ref, tm, npk, lead=(slot, k))
    o_ref[...] = _layer_norm(DN_ALPHA * x + f, g_ref[...], b_ref[...])


def _combine(x, ys, dest_tiles, w_tok, ws_gate, ws_up, ws_down, g, b, tm=128):
    t, d = x.shape
    nc = d // TILE_WORDS
    f = ws_gate.shape[-1]
    row = pl.BlockSpec((tm, d), lambda i: (i, 0))
    vec = pl.BlockSpec((1, d), lambda i: (0, 0))
    return pl.pallas_call(
        functools.partial(_combine_kernel, tm=tm),
        grid=(t // tm,),
        in_specs=[
            pl.BlockSpec(memory_space=pl.ANY),
            pl.BlockSpec(memory_space=pl.ANY),
            row,
            pl.BlockSpec((tm, TOP_K), lambda i: (i, 0)),
            pl.BlockSpec((d, f), lambda i: (0, 0)),
            pl.BlockSpec((d, f), lambda i: (0, 0)),
            pl.BlockSpec((f, d), lambda i: (0, 0)),
            vec,
            vec,
        ],
        out_specs=row,
        out_shape=jax.ShapeDtypeStruct((t, d), F32),
        scratch_shapes=[
            pltpu.SMEM((TOP_K * tm,), jnp.int32),
            pltpu.SMEM((TOP_K * tm,), jnp.int32),
            pltpu.VMEM((2, TOP_K, tm * nc, LANES), U32),
            pltpu.SemaphoreType.DMA((2,)),
            pltpu.SemaphoreType.DMA((2,)),
        ],
        compiler_params=_cparams(1),
        name="combine",
    )(dest_tiles, ys, x, w_tok, ws_gate, ws_up, ws_down, g.reshape(1, d), b.reshape(1, d))


def _tile_major(a, tm):
    k, t = a.shape
    return a.reshape(k, t // tm, tm).transpose(1, 0, 2).reshape(t // tm, k * tm)


def _dest_kernel(start_ref, e_ref, pos_ref, dest_ref, *, tm):
    ne = N_EXPERTS
    start = jnp.concatenate([start_ref[...]] * (tm // LANES), axis=1)
    rows = lax.broadcasted_iota(jnp.int32, (ne, tm), 0)
    for k in range(TOP_K):
        hit = rows == e_ref[pl.ds(k, 1), :]
        first = jnp.sum(jnp.where(hit, start, 0), axis=0, keepdims=True)
        dest_ref[pl.ds(k, 1), :] = first + pos_ref[pl.ds(k, 1), :]


def _dest(starts, top_e, pos, tm=512):
    t = top_e.shape[1]
    tok = pl.BlockSpec((TOP_K, tm), lambda i: (0, i))
    return pl.pallas_call(
        functools.partial(_dest_kernel, tm=tm),
        grid=(t // tm,),
        in_specs=[pl.BlockSpec((N_EXPERTS, LANES), lambda i: (0, 0)), tok, tok],
        out_specs=tok,
        out_shape=jax.ShapeDtypeStruct((TOP_K, t), jnp.int32),
        compiler_params=_cparams(1),
        name="dest",
    )(jnp.broadcast_to(starts[:, None], (N_EXPERTS, LANES)), top_e, pos)


def _expert_row_block(t):
    mean_rows = t * TOP_K // N_EXPERTS
    return max(LANES, -(-(mean_rows + mean_rows // 4) // LANES) * LANES)


def _moe_ln(x, x_tiles, layer, w_router, router_bias, we_gate, we_up, we_down, ws_gate, ws_up, ws_down,
            g, b):
    t, d = x.shape
    rb = _expert_row_block(t)
    top_e, w, pos, counts = _router(x, w_router, router_bias)

    cnt = counts[:, 0].astype(jnp.int32)
    padded = (cnt + rb - 1) // rb * rb
    pend = jnp.cumsum(padded)
    n_blocks = t * TOP_K // rb + N_EXPERTS
    n_real = (pend[-1] // rb).astype(jnp.int32)
    blk_ids = jnp.minimum(jnp.arange(n_blocks, dtype=jnp.int32), jnp.maximum(n_real - 1, 0))
    blk_e = jnp.sum((pend[None, :] <= (blk_ids * rb)[:, None]).astype(jnp.int32), axis=1)
    blk_e = jnp.minimum(blk_e, N_EXPERTS - 1)

    dest = _dest(pend - padded, top_e, pos)
    xs = _dispatch(x_tiles, _tile_major(dest, DISPATCH_TOKENS), n_blocks * rb, d // TILE_WORDS,
                   tm=DISPATCH_TOKENS)
    ys = _experts(xs, we_gate, we_up, we_down, layer, rb, blk_e, blk_ids, n_real.reshape(1))
    return _combine(
        x, ys, _tile_major(dest, COMBINE_TOKENS), w.T, ws_gate.astype(BF16),
        ws_up.astype(BF16), ws_down.astype(BF16), g, b, tm=COMBINE_TOKENS,
    )


def kernel(x, w_qkv_a, w_qkv_b, w_o, ln_mix_g, ln_mix_b, w_router, router_bias, we_gate, we_up,
           we_down, ws_gate, ws_up, ws_down, ln_ffn_g, ln_ffn_b):
    bsz, s, d = x.shape
    t = bsz * s
    slopes = _alibi_slopes()
    xt = x.reshape(t, d)
    for i in range(DEPTH):
        if i % N_MIXERS == 0:
            qkv = _proj(xt, w_qkv_a[i // N_MIXERS].astype(BF16))
            mix = _moba_attention(qkv.reshape(bsz, s, -1), slopes).reshape(t, d)
        else:
            w_b = w_qkv_b[i // N_MIXERS].astype(BF16)
            outs, lses = [], []
            for g, (window, dil) in enumerate(DIL_GROUPS):
                qt, kk, vt = _proj_residue(xt, w_b[:, g * 3 * d:(g + 1) * 3 * d], bsz, dil)
                o, lse = _dilated_group(qt, kk, vt, slopes, g, window, dil)
                outs.append(o.reshape(t, d))
                lses.append(lse.reshape(t, LANES))
            mix = _merge_groups(outs, lses)
        x1, x1_tiles = _wo_ln(mix, w_o[i].astype(BF16), xt, ln_mix_g[i], ln_mix_b[i])
        xt = _moe_ln(x1, x1_tiles, i, w_router[i], router_bias[i], we_gate, we_up, we_down,
                     ws_gate[i], ws_up[i], ws_down[i], ln_ffn_g[i], ln_ffn_b[i])
    return xt.reshape(bsz, s, d)
```

```python
import functools
import math

import jax
import jax.numpy as jnp
import numpy as np
from jax import lax
from jax.experimental import pallas as pl
from jax.experimental.pallas import tpu as pltpu

N_HEADS = 16
HEAD_DIM = 64
DEPTH = 2
N_MIXERS = 2
MOBA_BLOCK = 256
MOBA_TOPK = 3
MOBA_KEY_BLOCKS_PER_STEP = 4
MOBA_QUERY_BLOCKS_PER_STEP = MOBA_KEY_BLOCKS_PER_STEP
MOBA_ACC_ROWS = 80
MOBA_BIAS_PARTS = 3
DIL_GROUPS = ((128, 1), (512, 4), (2048, 16))
DIL_BLOCK = 128
N_EXPERTS = 256
TOP_K = 8
N_GROUPS = 8
TOPK_GROUPS = 4
ROUTED_SCALE = 2.5
DN_ALPHA = (2 * DEPTH) ** 0.25
LN_EPS = 1e-5
NEG = -1e30
LOG2E = math.log2(math.e)

LANES = 128
HEADS_PER_SLAB = LANES // HEAD_DIM
N_SLABS = N_HEADS // HEADS_PER_SLAB

DISPATCH_TOKENS = 512
COMBINE_TOKENS = 128

BF16 = jnp.bfloat16
F32 = jnp.float32
_HI = lax.Precision.HIGHEST

_ARB = "arbitrary"


def _cparams(n_axes, vmem_mb=48):
    return pltpu.CompilerParams(
        dimension_semantics=(_ARB,) * n_axes, vmem_limit_bytes=vmem_mb * 1024 * 1024
    )


def _alibi_slopes():
    return 2.0 ** (-8.0 * jnp.arange(1, N_HEADS + 1, dtype=F32) / N_HEADS)


def _bf16_parts(x, n):
    parts, rest = [], x
    for _ in range(n):
        top = lax.bitcast_convert_type(lax.bitcast_convert_type(rest, jnp.uint32) & np.uint32(0xFFFF0000), F32)
        parts.append(top.astype(BF16))
        rest = rest - top
    return parts


def _proj_kernel(x_ref, w_ref, o_ref):
    o_ref[...] = jnp.dot(
        x_ref[...].astype(BF16), w_ref[...], preferred_element_type=F32
    ).astype(o_ref.dtype)


def _proj(x, w, tm=512, tn=1024):
    t, k = x.shape
    n = w.shape[1]
    return pl.pallas_call(
        _proj_kernel,
        grid=(t // tm, n // tn),
        in_specs=[
            pl.BlockSpec((tm, k), lambda i, j: (i, 0)),
            pl.BlockSpec((k, tn), lambda i, j: (0, j)),
        ],
        out_specs=pl.BlockSpec((tm, tn), lambda i, j: (i, j)),
        out_shape=jax.ShapeDtypeStruct((t, n), BF16),
        compiler_params=_cparams(2),
        name="proj",
    )(x, w)


def _moba_kernel(slopes_ref, q_ref, k_ref, v_ref, wb_ref, o_ref, ka_ref, vt_ref, km_ref, sel_ref, s_ref,
                 *, n_blocks):
    blk = MOBA_BLOCK
    unroll = MOBA_KEY_BLOCKS_PER_STEP
    span = unroll * blk
    acc_rows = MOBA_ACC_ROWS
    h2 = pl.program_id(1)
    i = pl.program_id(2)
    heads = range(HEADS_PER_SLAB)

    @pl.when(i == 0)
    def _prepare_keys():
        lane = lax.broadcasted_iota(jnp.int32, (blk, LANES), 1)
        row_f = lax.broadcasted_iota(jnp.int32, (blk, LANES), 0).astype(F32)
        tail = lax.broadcasted_iota(jnp.int32, (acc_rows - HEAD_DIM, blk), 0)
        ones_row = jnp.where(tail == 0, 1.0, 0.0)

        def body(n, c):
            r0 = pl.multiple_of(n * blk, blk)
            vt = v_ref[0, pl.ds(r0, blk), :].astype(F32).T
            for hh in heads:
                vt_ref[n, hh] = jnp.concatenate(
                    [vt[hh * HEAD_DIM:(hh + 1) * HEAD_DIM], ones_row], axis=0).astype(BF16)
            kb = k_ref[0, pl.ds(r0, blk), :]
            km_ref[pl.ds(n, 1), :] = jnp.mean(kb.astype(F32), axis=0, keepdims=True)
            sub = jnp.asarray(n % unroll, F32)
            aux = jnp.where(lane < MOBA_BIAS_PARTS, sub, jnp.where(lane < 2 * MOBA_BIAS_PARTS, row_f, 0.0))
            ka_ref[pl.ds(r0, blk), :] = jnp.concatenate([kb, aux.astype(BF16)], axis=1)
            return c

        lax.fori_loop(0, n_blocks, body, 0)

    qb = MOBA_QUERY_BLOCKS_PER_STEP
    wide = qb * blk
    i_first = pl.program_id(2) * qb
    qt = q_ref[0].astype(F32).T
    feat = lax.broadcasted_iota(jnp.int32, qt.shape, 0)
    kmean = km_ref[...]
    km_lane = lax.broadcasted_iota(jnp.int32, kmean.shape, 1)
    blk_row = lax.broadcasted_iota(jnp.int32, (n_blocks, wide), 0)
    own_blk = i_first + lax.broadcasted_iota(jnp.int32, (n_blocks, wide), 1) // blk

    slopes = [slopes_ref[h2 * HEADS_PER_SLAB + hh] for hh in heads]
    qt_s = []
    for hh in heads:
        in_head = (feat >= hh * HEAD_DIM) & (feat < (hh + 1) * HEAD_DIM)
        qt_h = jnp.where(in_head, qt, 0.0)
        qt_s.append((qt_h * (HEAD_DIM ** -0.5 * LOG2E)).astype(BF16))

        km_h = jnp.where((km_lane >= hh * HEAD_DIM) & (km_lane < (hh + 1) * HEAD_DIM), kmean, 0.0)
        gate = jnp.dot(km_h, qt_h, preferred_element_type=F32, precision=_HI)
        g = jnp.where(blk_row < own_blk, gate, -jnp.inf)
        sel = jnp.zeros(g.shape, F32)
        for _ in range(MOBA_TOPK):
            gmax = jnp.max(g, axis=0, keepdims=True)
            first = jnp.min(jnp.where(g == gmax, blk_row, n_blocks), axis=0, keepdims=True)
            hit = blk_row == first
            sel = jnp.where(hit & (gmax > -jnp.inf), 1.0, sel)
            g = jnp.where(hit, -jnp.inf, g)
        sel_ref[hh] = sel

    w_all = jnp.concatenate(
        [jnp.concatenate(qt_s, axis=1), wb_ref[0]], axis=0)

    groups = [(hh, q) for hh in heads for q in range(qb)]
    group_cols = [slice((hh * qb + q) * blk, (hh * qb + q + 1) * blk) for hh, q in groups]

    def scores(span_idx, cols):
        r0 = pl.multiple_of(span_idx * span, span)
        return jnp.dot(ka_ref[pl.ds(r0, span), :], w_all[:, cols], preferred_element_type=F32)

    key_row = lax.broadcasted_iota(jnp.int32, (blk, blk), 0)
    qry_col = lax.broadcasted_iota(jnp.int32, (blk, blk), 1)

    def consume(gi, span_idx, m, acc, last):
        hh, q = groups[gi]
        j0 = span_idx * unroll
        c = slopes[hh] * LOG2E * jnp.asarray(blk * (j0 - (i_first + q)), F32)
        for u in range(unroll):
            if last and u > q:
                continue
            j = j0 + u
            s = s_ref[u * blk:(u + 1) * blk, group_cols[gi]]
            if last and u == q:
                s = jnp.where(key_row <= qry_col, s, NEG)
                m_new = jnp.maximum(m, jnp.max(s, axis=0, keepdims=True) + c)
                p = jnp.exp2(s - (m_new - c))
            else:
                ch = sel_ref[hh, pl.ds(j, 1), q * blk:(q + 1) * blk] > 0.5
                m_new = jnp.where(ch, jnp.maximum(m, jnp.max(s, axis=0, keepdims=True) + c), m)
                p = jnp.exp2(s - jnp.where(ch, m_new - c, -NEG))
            acc = jnp.exp2(m - m_new) * acc + jnp.dot(
                vt_ref[j, hh], p.astype(BF16), preferred_element_type=F32)
            m = m_new
        return m, acc

    n_full = i_first // unroll
    s_ref[...] = scores(0, slice(None))

    def body(it, state):
        out = []
        for gi in range(len(groups)):
            out.append(consume(gi, it, *state[gi], last=False))
            s_ref[:, group_cols[gi]] = scores(it + 1, group_cols[gi])
        return tuple(out)

    init = tuple((jnp.full((1, blk), NEG, F32), jnp.zeros((acc_rows, blk), F32)) for _ in groups)
    state = lax.fori_loop(0, n_full, body, init)
    state = [consume(gi, n_full, *state[gi], last=True) for gi in range(len(groups))]

    outs = [acc[:HEAD_DIM] / acc[HEAD_DIM:HEAD_DIM + 1] for _, acc in state]
    per_block = [jnp.concatenate([outs[hh * qb + q] for hh in heads], axis=0) for q in range(qb)]
    o_ref[0] = jnp.concatenate(per_block, axis=1).T.astype(o_ref.dtype)


def _moba_attention(qkv, slopes):
    b, s, _ = qkv.shape
    blk = MOBA_BLOCK
    n_blocks = s // blk
    qb = MOBA_QUERY_BLOCKS_PER_STEP
    assert n_blocks % MOBA_KEY_BLOCKS_PER_STEP == 0 and MOBA_KEY_BLOCKS_PER_STEP == qb
    wide = HEADS_PER_SLAB * qb * blk
    grid_spec = pltpu.PrefetchScalarGridSpec(
        num_scalar_prefetch=1,
        grid=(b, N_SLABS, n_blocks // qb),
        in_specs=[
            pl.BlockSpec((1, qb * blk, LANES), lambda bi, h, i, sl: (bi, i, h)),
            pl.BlockSpec((1, s, LANES), lambda bi, h, i, sl: (bi, 0, N_SLABS + h)),
            pl.BlockSpec((1, s, LANES), lambda bi, h, i, sl: (bi, 0, 2 * N_SLABS + h)),
            pl.BlockSpec((1, LANES, wide), lambda bi, h, i, sl: (h, 0, 0)),
        ],
        out_specs=pl.BlockSpec((1, qb * blk, LANES), lambda bi, h, i, sl: (bi, i, h)),
        scratch_shapes=[
            pltpu.VMEM((s, 2 * LANES), BF16),
            pltpu.VMEM((n_blocks, HEADS_PER_SLAB, MOBA_ACC_ROWS, blk), BF16),
            pltpu.VMEM((n_blocks, LANES), F32),
            pltpu.VMEM((HEADS_PER_SLAB, n_blocks, qb * blk), F32),
            pltpu.VMEM((MOBA_KEY_BLOCKS_PER_STEP * blk, wide), F32),
        ],
    )
    return pl.pallas_call(
        functools.partial(_moba_kernel, n_blocks=n_blocks),
        grid_spec=grid_spec,
        out_shape=jax.ShapeDtypeStruct((b, s, N_HEADS * HEAD_DIM), BF16),
        compiler_params=_cparams(3),
        name="moba",
    )(slopes, qkv, qkv, qkv, _moba_slope_rows(slopes))


def _moba_slope_rows(slopes):
    blk = MOBA_BLOCK
    wide = MOBA_QUERY_BLOCKS_PER_STEP * blk
    parts = _bf16_parts(slopes * LOG2E, MOBA_BIAS_PARTS)
    sp = jnp.stack([p.astype(F32) for p in parts], axis=1)
    rows = jnp.concatenate([sp * blk, sp], axis=1).reshape(N_SLABS, HEADS_PER_SLAB, 2 * MOBA_BIAS_PARTS)
    wb = jnp.zeros((N_SLABS, LANES, HEADS_PER_SLAB * wide), F32)
    for hh in range(HEADS_PER_SLAB):
        wb = wb.at[:, :2 * MOBA_BIAS_PARTS, hh * wide:(hh + 1) * wide].set(rows[:, hh, :, None])
    return wb.astype(BF16)


def _dilated_kernel(bias_ref, qt_ref, kc_ref, kp_ref, vtc_ref, vtp_ref, o_ref, lse_ref, lse_s, s_scr):
    blk = DIL_BLOCK
    lse_s[...] = jnp.zeros(lse_s.shape, F32)
    tail = lax.broadcasted_iota(jnp.int32, (MOBA_ACC_ROWS - HEAD_DIM, 2 * blk), 0)
    ones_row = jnp.where(tail == 0, 1.0, 0.0).astype(BF16)
    feat = lax.broadcasted_iota(jnp.int32, (LANES, blk), 0)
    for h2 in range(N_SLABS):
        cols = slice(h2 * LANES, (h2 + 1) * LANES)
        qt = qt_ref[0, 0, cols, :].astype(F32) * (HEAD_DIM ** -0.5 * LOG2E)
        kband = jnp.concatenate([kp_ref[0, 0, :, cols], kc_ref[0, 0, :, cols]], axis=0)
        qt_both = jnp.concatenate(
            [jnp.where((feat >= hh * HEAD_DIM) & (feat < (hh + 1) * HEAD_DIM), qt, 0.0)
             for hh in range(HEADS_PER_SLAB)], axis=1).astype(BF16)
        s_scr[h2] = jnp.dot(kband, qt_both, preferred_element_type=F32)
    for h2 in range(N_SLABS):
        cols = slice(h2 * LANES, (h2 + 1) * LANES)
        vt = jnp.concatenate([vtp_ref[0, 0, cols, :], vtc_ref[0, 0, cols, :]], axis=1)
        outs = []
        for hh in range(HEADS_PER_SLAB):
            h = h2 * HEADS_PER_SLAB + hh
            s = s_scr[h2, :, hh * blk:(hh + 1) * blk] + bias_ref[0, h]
            m = jnp.max(s, axis=0, keepdims=True)
            p = jnp.exp2(s - m).astype(BF16)
            vt_h = jnp.concatenate([vt[hh * HEAD_DIM:(hh + 1) * HEAD_DIM], ones_row], axis=0)
            acc = jnp.dot(vt_h, p, preferred_element_type=F32)
            l = acc[HEAD_DIM:HEAD_DIM + 1]
            outs.append(acc[:HEAD_DIM] / l)
            lse_s[pl.ds(h, 1), :] = (m + jnp.log2(l)) * (1.0 / LOG2E)
        o_ref[0, :, cols] = jnp.concatenate(outs, axis=0).T.astype(o_ref.dtype)
    lse_ref[0] = lse_s[...].T


def _dilated_bias(slopes, dil):
    blk = DIL_BLOCK
    key = jnp.arange(2 * blk)[:, None]
    steps = jnp.arange(blk)[None, :] - (key - blk)
    steps_max = blk
    valid = (steps >= 0) & (steps <= steps_max)
    bias = -(slopes * LOG2E)[:, None, None] * (steps * dil).astype(F32)[None]
    rest = jnp.where(valid[None], bias, NEG)
    first = jnp.where((key >= blk)[None], rest, NEG)
    return jnp.stack([first, rest]).astype(F32)


def _proj_residue_kernel(x_ref, wqt_ref, wk_ref, wvt_ref, qt_ref, k_ref, vt_ref):
    xb = x_ref[0].astype(BF16)
    nt = (((1,), (1,)), ((), ()))
    qt_ref[0, 0] = lax.dot_general(wqt_ref[...], xb, nt, preferred_element_type=F32).astype(qt_ref.dtype)
    k_ref[0, 0] = jnp.dot(xb, wk_ref[...], preferred_element_type=F32).astype(k_ref.dtype)
    vt_ref[0, 0] = lax.dot_general(wvt_ref[...], xb, nt, preferred_element_type=F32).astype(vt_ref.dtype)


def _proj_residue(x, w, bsz, dil, tm=512):
    t, k = x.shape
    d = w.shape[1] // 3
    l_sub = t // bsz // dil
    tm = min(tm, l_sub)
    view = x.reshape(bsz, l_sub, dil * k)
    const = lambda shape: pl.BlockSpec(shape, lambda b, r, m: (0, 0))
    tr = pl.BlockSpec((1, 1, d, tm), lambda b, r, m: (b, r, 0, m))
    return pl.pallas_call(
        _proj_residue_kernel,
        grid=(bsz, dil, l_sub // tm),
        in_specs=[pl.BlockSpec((1, tm, k), lambda b, r, m: (b, m, r)), const((d, k)), const((k, d)), const((d, k))],
        out_specs=[tr, pl.BlockSpec((1, 1, tm, d), lambda b, r, m: (b, r, m, 0)), tr],
        out_shape=[
            jax.ShapeDtypeStruct((bsz, dil, d, l_sub), BF16),
            jax.ShapeDtypeStruct((bsz, dil, l_sub, d), BF16),
            jax.ShapeDtypeStruct((bsz, dil, d, l_sub), BF16),
        ],
        compiler_params=_cparams(3),
        name=f"proj_residue_d{dil}",
    )(view, w[:, :d].T, w[:, d:2 * d], w[:, 2 * d:].T)


def _dilated_group(qt, k, vt, slopes, g, window, dil):
    b, _, l_sub, d = k.shape
    blk = DIL_BLOCK
    assert window // dil == blk
    s = l_sub * dil
    nb = l_sub // blk
    bias = _dilated_bias(slopes, dil)
    prev = lambda n: jnp.maximum(n - 1, 0)

    o, lse = pl.pallas_call(
        _dilated_kernel,
        grid=(b, dil, nb),
        in_specs=[
            pl.BlockSpec((1, N_HEADS, 2 * blk, blk), lambda bi, r, n: (jnp.minimum(n, 1), 0, 0, 0)),
            pl.BlockSpec((1, 1, d, blk), lambda bi, r, n: (bi, r, 0, n)),
            pl.BlockSpec((1, 1, blk, d), lambda bi, r, n: (bi, r, n, 0)),
            pl.BlockSpec((1, 1, blk, d), lambda bi, r, n: (bi, r, prev(n), 0)),
            pl.BlockSpec((1, 1, d, blk), lambda bi, r, n: (bi, r, 0, n)),
            pl.BlockSpec((1, 1, d, blk), lambda bi, r, n: (bi, r, 0, prev(n))),
        ],
        out_specs=[
            pl.BlockSpec((1, blk, d), lambda bi, r, n: (bi, n, r)),
            pl.BlockSpec((1, blk, LANES), lambda bi, r, n: (bi, n, r)),
        ],
        out_shape=[
            jax.ShapeDtypeStruct((b, l_sub, dil * d), BF16),
            jax.ShapeDtypeStruct((b, l_sub, dil * LANES), F32),
        ],
        scratch_shapes=[pltpu.VMEM((LANES, blk), F32), pltpu.VMEM((N_SLABS, 2 * blk, 2 * blk), F32)],
        compiler_params=_cparams(3),
        name=f"dilated_g{g}",
    )(bias, qt, k, k, vt, vt)
    return o.reshape(b, s, d), lse.reshape(b, s, LANES)


def _merge_kernel(e_ref, o1, o2, o3, l1, l2, l3, out_ref):
    ls = [l1[...], l2[...], l3[...]]
    mx = jnp.maximum(jnp.maximum(ls[0], ls[1]), ls[2])
    es = [jnp.exp(x - mx) for x in ls]
    den = es[0] + es[1] + es[2]
    acc = None
    for e, o in zip(es, (o1, o2, o3)):
        w = jnp.dot(e / den, e_ref[...], preferred_element_type=F32, precision=_HI)
        term = w * o[...].astype(F32)
        acc = term if acc is None else acc + term
    out_ref[...] = acc.astype(out_ref.dtype)


def _merge_groups(os, lses, tm=512):
    t, d = os[0].shape
    expand = (jnp.arange(LANES)[:, None] == (jnp.arange(d)[None, :] // HEAD_DIM)).astype(F32)
    row = lambda w: pl.BlockSpec((tm, w), lambda i: (i, 0))
    return pl.pallas_call(
        _merge_kernel,
        grid=(t // tm,),
        in_specs=[pl.BlockSpec((LANES, d), lambda i: (0, 0))] + [row(d)] * 3 + [row(LANES)] * 3,
        out_specs=row(d),
        out_shape=jax.ShapeDtypeStruct((t, d), BF16),
        compiler_params=_cparams(1),
        name="merge_groups",
    )(expand, *os, *lses)


def _layer_norm(y, g, b):
    mu = jnp.mean(y, axis=-1, keepdims=True)
    yc = y - mu
    var = jnp.mean(yc * yc, axis=-1, keepdims=True)
    return yc * lax.rsqrt(var + LN_EPS) * g + b


U32 = jnp.uint32
TILE_WORDS = 2 * LANES
HIGH_HALF = np.uint32(0xFFFF0000)


def _pack_token_tiles(ref, y):
    n, d = y.shape
    npk = d // TILE_WORDS
    for c in range(npk):
        lo = y[:, (2 * c) * LANES:(2 * c + 1) * LANES].astype(BF16).astype(F32)
        hi = y[:, (2 * c + 1) * LANES:(2 * c + 2) * LANES].astype(BF16).astype(F32)
        word = (lax.bitcast_convert_type(lo, U32) >> 16) | (lax.bitcast_convert_type(hi, U32) & HIGH_HALF)
        ref[pl.ds(c, n, stride=npk), :] = word


def _unpack_token_tiles(ref, n, npk, lead=()):
    blocks = []
    for c in range(npk):
        word = ref[lead + (pl.ds(c, n, stride=npk), slice(None))]
        blocks.append(lax.bitcast_convert_type(word << 16, F32))
        blocks.append(lax.bitcast_convert_type(word & HIGH_HALF, F32))
    return jnp.concatenate(blocks, axis=1)


def _wo_ln_kernel(mix_ref, wo_ref, x_ref, g_ref, b_ref, o_ref, ot_ref):
    f = jnp.dot(mix_ref[...], wo_ref[...], preferred_element_type=F32)
    y = _layer_norm(DN_ALPHA * x_ref[...] + f, g_ref[...], b_ref[...])
    o_ref[...] = y
    _pack_token_tiles(ot_ref, y)


def _wo_ln(mix, wo, x, g, b, tm=256):
    t, d = x.shape
    npk = d // TILE_WORDS
    row = pl.BlockSpec((tm, d), lambda i: (i, 0))
    vec = pl.BlockSpec((1, d), lambda i: (0, 0))
    return pl.pallas_call(
        _wo_ln_kernel,
        grid=(t // tm,),
        in_specs=[row, pl.BlockSpec((d, d), lambda i: (0, 0)), row, vec, vec],
        out_specs=[row, pl.BlockSpec((tm * npk, LANES), lambda i: (i, 0))],
        out_shape=[jax.ShapeDtypeStruct((t, d), F32), jax.ShapeDtypeStruct((t * npk, LANES), U32)],
        compiler_params=_cparams(1),
        name="wo_ln",
    )(mix, wo, x, g.reshape(1, d), b.reshape(1, d))


def _router_kernel(wr_ref, rb_ref, x_ref, e_ref, w_ref, pos_ref, cnt_ref, run_ref, *, tm):
    step = pl.program_id(0)
    ne = N_EXPERTS
    gs = ne // N_GROUPS

    @pl.when(step == 0)
    def _():
        run_ref[...] = jnp.zeros(run_ref.shape, F32)

    x = x_ref[...]
    x_hi = x.astype(BF16)
    x_lo = (x - x_hi.astype(F32)).astype(BF16)
    nt = (((1,), (1,)), ((), ()))
    logits = (lax.dot_general(wr_ref[0], x_hi, nt, preferred_element_type=F32)
              + lax.dot_general(wr_ref[0], x_lo, nt, preferred_element_type=F32)
              + lax.dot_general(wr_ref[1], x_hi, nt, preferred_element_type=F32))
    scores = jax.nn.sigmoid(logits)
    choice = scores + rb_ref[...][:, :1]

    grp_rows = lax.broadcasted_iota(jnp.int32, (gs, tm), 0)
    blocks, gscore = [], []
    for gi in range(N_GROUPS):
        cb = choice[gi * gs:(gi + 1) * gs]
        m1 = jnp.max(cb, axis=0, keepdims=True)
        i1 = jnp.min(jnp.where(cb == m1, grp_rows, gs), axis=0, keepdims=True)
        m2 = jnp.max(jnp.where(grp_rows == i1, -jnp.inf, cb), axis=0, keepdims=True)
        blocks.append(cb)
        gscore.append(m1 + m2)
    masked = []
    for gi in range(N_GROUPS):
        beaten = jnp.zeros((1, tm), jnp.int32)
        for gj in range(N_GROUPS):
            if gj == gi:
                continue
            wins = (gscore[gj] > gscore[gi]) | ((gscore[gj] == gscore[gi]) & (gj < gi))
            beaten = beaten + wins.astype(jnp.int32)
        masked.append(jnp.where(beaten < TOPK_GROUPS, blocks[gi], NEG))
    cand = jnp.concatenate(masked, axis=0)

    rows = lax.broadcasted_iota(jnp.int32, (ne, tm), 0)
    member = jnp.zeros((ne, tm), F32)
    picks, raw_w = [], []
    for _ in range(TOP_K):
        cmax = jnp.max(cand, axis=0, keepdims=True)
        first = jnp.min(jnp.where(cand == cmax, rows, ne), axis=0, keepdims=True)
        hit = rows == first
        raw_w.append(jnp.sum(jnp.where(hit, scores, 0.0), axis=0, keepdims=True))
        member = jnp.where(hit, 1.0, member)
        cand = jnp.where(hit, -jnp.inf, cand)
        picks.append(first)
    wsum = raw_w[0]
    for r in raw_w[1:]:
        wsum = wsum + r

    tri = (lax.broadcasted_iota(jnp.int32, (tm, tm), 0) < lax.broadcasted_iota(jnp.int32, (tm, tm), 1))
    member_b = member.astype(BF16)
    before = jnp.dot(member_b, tri.astype(BF16), preferred_element_type=F32)
    run = run_ref[...]
    rank = before + jnp.concatenate([run] * (tm // LANES), axis=1)
    for k in range(TOP_K):
        hit = rows == picks[k]
        e_ref[pl.ds(k, 1), :] = picks[k]
        w_ref[pl.ds(k, 1), :] = raw_w[k] / wsum * ROUTED_SCALE
        pos_ref[pl.ds(k, 1), :] = jnp.sum(jnp.where(hit, rank, 0.0), axis=0, keepdims=True).astype(jnp.int32)
    run = run + jnp.dot(member_b, jnp.ones((tm, LANES), BF16), preferred_element_type=F32)
    run_ref[...] = run
    cnt_ref[...] = run


def _router(x, w_router, router_bias, tm=256):
    t, d = x.shape
    ne = N_EXPERTS
    wr_parts = jnp.stack(_bf16_parts(w_router.T.astype(F32), 2))
    rb = jnp.broadcast_to(router_bias.astype(F32)[:, None], (ne, LANES))
    tok = pl.BlockSpec((TOP_K, tm), lambda i: (0, i))
    return pl.pallas_call(
        functools.partial(_router_kernel, tm=tm),
        grid=(t // tm,),
        in_specs=[
            pl.BlockSpec((2, ne, d), lambda i: (0, 0, 0)),
            pl.BlockSpec((ne, LANES), lambda i: (0, 0)),
            pl.BlockSpec((tm, d), lambda i: (i, 0)),
        ],
        out_specs=[tok, tok, tok, pl.BlockSpec((ne, LANES), lambda i: (0, 0))],
        out_shape=[
            jax.ShapeDtypeStruct((TOP_K, t), jnp.int32),
            jax.ShapeDtypeStruct((TOP_K, t), F32),
            jax.ShapeDtypeStruct((TOP_K, t), jnp.int32),
            jax.ShapeDtypeStruct((ne, LANES), F32),
        ],
        scratch_shapes=[pltpu.VMEM((ne, LANES), F32)],
        compiler_params=_cparams(1),
        name="router",
    )(wr_parts, rb, x)


def _tile_rows(ref, i, npk):
    return ref.at[pl.ds(pl.multiple_of(i * npk, npk), npk)]


def _dispatch_kernel(dest_hbm, x_ref, xs_hbm, dest_smem, sem_idx, sem_rows, *, tm):
    step = pl.program_id(0)
    npk = x_ref.shape[0] // tm
    idx_copy = pltpu.make_async_copy(dest_hbm.at[step], dest_smem, sem_idx)
    idx_copy.start()
    idx_copy.wait()

    def row_copy(t, d):
        return pltpu.make_async_copy(_tile_rows(x_ref, t, npk), _tile_rows(xs_hbm, d, npk), sem_rows)

    def issue(t, c):
        for k in range(TOP_K):
            row_copy(t, dest_smem[k * tm + t]).start(priority=k % 2)
        return c

    lax.fori_loop(0, tm, issue, 0)

    def drain(t, c):
        for _ in range(TOP_K):
            row_copy(0, 0).wait()
        return c

    lax.fori_loop(0, tm, drain, 0)


def _dispatch(x_tiles, dest_tiles, n_rows, npk, tm):
    t = x_tiles.shape[0] // npk
    return pl.pallas_call(
        functools.partial(_dispatch_kernel, tm=tm),
        grid=(t // tm,),
        in_specs=[pl.BlockSpec(memory_space=pl.ANY), pl.BlockSpec((tm * npk, LANES), lambda i: (i, 0))],
        out_specs=pl.BlockSpec(memory_space=pl.ANY),
        out_shape=jax.ShapeDtypeStruct((n_rows * npk, LANES), x_tiles.dtype),
        scratch_shapes=[
            pltpu.SMEM((TOP_K * tm,), jnp.int32),
            pltpu.SemaphoreType.DMA,
            pltpu.SemaphoreType.DMA,
        ],
        compiler_params=_cparams(1),
        name="dispatch",
    )(dest_tiles, x_tiles)


def _experts_kernel(blk_e_ref, blk_row_ref, n_real_ref, xs_ref, wg_ref, wu_ref, wd_ref, ys_ref, *, rb):
    @pl.when(pl.program_id(0) < n_real_ref[0])
    def _():
        npk = xs_ref.shape[0] // rb
        xb = _unpack_token_tiles(xs_ref, rb, npk).astype(BF16)
        gate = jnp.dot(xb, wg_ref[0, 0].astype(BF16), preferred_element_type=F32)
        up = jnp.dot(xb, wu_ref[0, 0].astype(BF16), preferred_element_type=F32)
        h = (gate * jax.nn.sigmoid(gate) * up).astype(BF16)
        _pack_token_tiles(ys_ref, jnp.dot(h, wd_ref[0, 0].astype(BF16), preferred_element_type=F32))


def _experts(xs, we_gate, we_up, we_down, layer, rb, blk_e, blk_row, n_real):
    d, f = we_gate.shape[-2:]
    nc = d // TILE_WORDS
    p = xs.shape[0] // nc
    rows = pl.BlockSpec((rb * nc, LANES), lambda i, be, br, nr: (br[i], 0))
    grid_spec = pltpu.PrefetchScalarGridSpec(
        num_scalar_prefetch=3,
        grid=(p // rb,),
        in_specs=[
            rows,
            pl.BlockSpec((1, 1, d, f), lambda i, be, br, nr: (layer, be[i], 0, 0)),
            pl.BlockSpec((1, 1, d, f), lambda i, be, br, nr: (layer, be[i], 0, 0)),
            pl.BlockSpec((1, 1, f, d), lambda i, be, br, nr: (layer, be[i], 0, 0)),
        ],
        out_specs=rows,
    )
    return pl.pallas_call(
        functools.partial(_experts_kernel, rb=rb),
        grid_spec=grid_spec,
        out_shape=jax.ShapeDtypeStruct((p * nc, LANES), U32),
        compiler_params=_cparams(1),
        name="experts",
    )(blk_e, blk_row, n_real, xs, we_gate, we_up, we_down)


def _combine_kernel(dest_hbm, ys_hbm, x_ref, w_ref, sg_ref, su_ref, sd_ref, g_ref, b_ref, o_ref,
                    dest_smem0, dest_smem1, rows_ref, sem_idx, sem_rows, *, tm):
    dest_smems = (dest_smem0, dest_smem1)
    step = pl.program_id(0)
    n_steps = pl.num_programs(0)
    npk = rows_ref.shape[2] // tm
    slot = step % 2

    def row_copy(sl, k, t, d):
        return pltpu.make_async_copy(
            _tile_rows(ys_hbm, d, npk), _tile_rows(rows_ref.at[sl, k], t, npk), sem_rows.at[sl])

    def idx_copy(tile, sl):
        return pltpu.make_async_copy(dest_hbm.at[tile], dest_smems[sl], sem_idx.at[sl])

    def issue_rows(sl):
        dest_smem = dest_smems[sl]

        def issue(t, c):
            for k in range(TOP_K):
                row_copy(sl, k, t, dest_smem[k * tm + t]).start(priority=k % 2)
            return c

        lax.fori_loop(0, tm, issue, 0)

    @pl.when(step == 0)
    def _():
        idx_copy(0, 0).start()
        idx_copy(0, 0).wait()
        issue_rows(0)

        @pl.when(n_steps > 1)
        def _():
            idx_copy(1, 1).start()

    for sl in range(2):
        @pl.when((step + 1 < n_steps) & (slot == 1 - sl))
        def _(sl=sl):
            idx_copy(step + 1, sl).wait()

            @pl.when(step + 2 < n_steps)
            def _():
                idx_copy(step + 2, 1 - sl).start()

            issue_rows(sl)

    x = x_ref[...]
    xb = x.astype(BF16)
    gate = jnp.dot(xb, sg_ref[...], preferred_element_type=F32)
    up = jnp.dot(xb, su_ref[...], preferred_element_type=F32)
    h = (gate * jax.nn.sigmoid(gate) * up).astype(BF16)
    f = jnp.dot(h, sd_ref[...], preferred_element_type=F32)

    def drain(t, c):
        for _ in range(TOP_K):
            row_copy(slot, 0, 0, 0).wait()
        return c

    lax.fori_loop(0, tm, drain, 0)

    w = w_ref[...]
    for k in range(TOP_K):
        f = f + w[:, k:k + 1] * _unpack_token_tiles(rows_ref, tm, npk, lead=(slot, k))
    o_ref[...] = _layer_norm(DN_ALPHA * x + f, g_ref[...], b_ref[...])


def _combine(x, ys, dest_tiles, w_tok, ws_gate, ws_up, ws_down, g, b, tm=128):
    t, d = x.shape
    nc = d // TILE_WORDS
    f = ws_gate.shape[-1]
    row = pl.BlockSpec((tm, d), lambda i: (i, 0))
    vec = pl.BlockSpec((1, d), lambda i: (0, 0))
    return pl.pallas_call(
        functools.partial(_combine_kernel, tm=tm),
        grid=(t // tm,),
        in_specs=[
            pl.BlockSpec(memory_space=pl.ANY),
            pl.BlockSpec(memory_space=pl.ANY),
            row,
            pl.BlockSpec((tm, TOP_K), lambda i: (i, 0)),
            pl.BlockSpec((d, f), lambda i: (0, 0)),
            pl.BlockSpec((d, f), lambda i: (0, 0)),
            pl.BlockSpec((f, d), lambda i: (0, 0)),
            vec,
            vec,
        ],
        out_specs=row,
        out_shape=jax.ShapeDtypeStruct((t, d), F32),
        scratch_shapes=[
            pltpu.SMEM((TOP_K * tm,), jnp.int32),
            pltpu.SMEM((TOP_K * tm,), jnp.int32),
            pltpu.VMEM((2, TOP_K, tm * nc, LANES), U32),
            pltpu.SemaphoreType.DMA((2,)),
            pltpu.SemaphoreType.DMA((2,)),
        ],
        compiler_params=_cparams(1),
        name="combine",
    )(dest_tiles, ys, x, w_tok, ws_gate, ws_up, ws_down, g.reshape(1, d), b.reshape(1, d))


def _tile_major(a, tm):
    k, t = a.shape
    return a.reshape(k, t // tm, tm).transpose(1, 0, 2).reshape(t // tm, k * tm)


def _dest_kernel(start_ref, e_ref, pos_ref, dest_ref, *, tm):
    ne = N_EXPERTS
    start = jnp.concatenate([start_ref[...]] * (tm // LANES), axis=1)
    rows = lax.broadcasted_iota(jnp.int32, (ne, tm), 0)
    for k in range(TOP_K):
        hit = rows == e_ref[pl.ds(k, 1), :]
        first = jnp.sum(jnp.where(hit, start, 0), axis=0, keepdims=True)
        dest_ref[pl.ds(k, 1), :] = first + pos_ref[pl.ds(k, 1), :]


def _dest(starts, top_e, pos, tm=512):
    t = top_e.shape[1]
    tok = pl.BlockSpec((TOP_K, tm), lambda i: (0, i))
    return pl.pallas_call(
        functools.partial(_dest_kernel, tm=tm),
        grid=(t // tm,),
        in_specs=[pl.BlockSpec((N_EXPERTS, LANES), lambda i: (0, 0)), tok, tok],
        out_specs=tok,
        out_shape=jax.ShapeDtypeStruct((TOP_K, t), jnp.int32),
        compiler_params=_cparams(1),
        name="dest",
    )(jnp.broadcast_to(starts[:, None], (N_EXPERTS, LANES)), top_e, pos)


def _expert_row_block(t):
    mean_rows = t * TOP_K // N_EXPERTS
    return max(LANES, -(-(mean_rows + mean_rows // 4) // LANES) * LANES)


def _moe_ln(x, x_tiles, layer, w_router, router_bias, we_gate, we_up, we_down, ws_gate, ws_up, ws_down,
            g, b):
    t, d = x.shape
    rb = _expert_row_block(t)
    top_e, w, pos, counts = _router(x, w_router, router_bias)

    cnt = counts[:, 0].astype(jnp.int32)
    padded = (cnt + rb - 1) // rb * rb
    pend = jnp.cumsum(padded)
    n_blocks = t * TOP_K // rb + N_EXPERTS
    n_real = (pend[-1] // rb).astype(jnp.int32)
    blk_ids = jnp.minimum(jnp.arange(n_blocks, dtype=jnp.int32), jnp.maximum(n_real - 1, 0))
    blk_e = jnp.sum((pend[None, :] <= (blk_ids * rb)[:, None]).astype(jnp.int32), axis=1)
    blk_e = jnp.minimum(blk_e, N_EXPERTS - 1)

    dest = _dest(pend - padded, top_e, pos)
    xs = _dispatch(x_tiles, _tile_major(dest, DISPATCH_TOKENS), n_blocks * rb, d // TILE_WORDS,
                   tm=DISPATCH_TOKENS)
    ys = _experts(xs, we_gate, we_up, we_down, layer, rb, blk_e, blk_ids, n_real.reshape(1))
    return _combine(
        x, ys, _tile_major(dest, COMBINE_TOKENS), w.T, ws_gate.astype(BF16),
        ws_up.astype(BF16), ws_down.astype(BF16), g, b, tm=COMBINE_TOKENS,
    )


def kernel(x, w_qkv_a, w_qkv_b, w_o, ln_mix_g, ln_mix_b, w_router, router_bias, we_gate, we_up,
           we_down, ws_gate, ws_up, ws_down, ln_ffn_g, ln_ffn_b):
    bsz, s, d = x.shape
    t = bsz * s
    slopes = _alibi_slopes()
    xt = x.reshape(t, d)
    for i in range(DEPTH):
        if i % N_MIXERS == 0:
            qkv = _proj(xt, w_qkv_a[i // N_MIXERS].astype(BF16))
            mix = _moba_attention(qkv.reshape(bsz, s, -1), slopes).reshape(t, d)
        else:
            w_b = w_qkv_b[i // N_MIXERS].astype(BF16)
            outs, lses = [], []
            for g, (window, dil) in enumerate(DIL_GROUPS):
                qt, kk, vt = _proj_residue(xt, w_b[:, g * 3 * d:(g + 1) * 3 * d], bsz, dil)
                o, lse = _dilated_group(qt, kk, vt, slopes, g, window, dil)
                outs.append(o.reshape(t, d))
                lses.append(lse.reshape(t, LANES))
            mix = _merge_groups(outs, lses)
        x1, x1_tiles = _wo_ln(mix, w_o[i].astype(BF16), xt, ln_mix_g[i], ln_mix_b[i])
        xt = _moe_ln(x1, x1_tiles, i, w_router[i], router_bias[i], we_gate, we_up, we_down,
                     ws_gate[i], ws_up[i], ws_down[i], ln_ffn_g[i], ln_ffn_b[i])
    return xt.reshape(bsz, s, d)
```

```python
import functools
import math

import jax
import jax.numpy as jnp
import numpy as np
from jax import lax
from jax.experimental import pallas as pl
from jax.experimental.pallas import tpu as pltpu

N_HEADS = 16
HEAD_DIM = 64
DEPTH = 2
N_MIXERS = 2
MOBA_BLOCK = 256
MOBA_TOPK = 3
MOBA_KEY_BLOCKS_PER_STEP = 4
MOBA_QUERY_BLOCKS_PER_STEP = MOBA_KEY_BLOCKS_PER_STEP
MOBA_ACC_ROWS = 80
MOBA_BIAS_PARTS = 3
DIL_GROUPS = ((128, 1), (512, 4), (2048, 16))
DIL_BLOCK = 128
N_EXPERTS = 256
TOP_K = 8
N_GROUPS = 8
TOPK_GROUPS = 4
ROUTED_SCALE = 2.5
DN_ALPHA = (2 * DEPTH) ** 0.25
LN_EPS = 1e-5
NEG = -1e30
LOG2E = math.log2(math.e)

LANES = 128
HEADS_PER_SLAB = LANES // HEAD_DIM
N_SLABS = N_HEADS // HEADS_PER_SLAB

DISPATCH_TOKENS = 512
COMBINE_TOKENS = 128

BF16 = jnp.bfloat16
F32 = jnp.float32

_ARB = "arbitrary"


def _cparams(n_axes, vmem_mb=48):
    return pltpu.CompilerParams(
        dimension_semantics=(_ARB,) * n_axes, vmem_limit_bytes=vmem_mb * 1024 * 1024
    )


def _alibi_slopes():
    return 2.0 ** (-8.0 * jnp.arange(1, N_HEADS + 1, dtype=F32) / N_HEADS)


def _bf16_parts(x, n):
    parts, rest = [], x
    for _ in range(n):
        top = lax.bitcast_convert_type(lax.bitcast_convert_type(rest, jnp.uint32) & np.uint32(0xFFFF0000), F32)
        parts.append(top.astype(BF16))
        rest = rest - top
    return parts


def _proj_kernel(x_ref, w_ref, o_ref):
    o_ref[...] = jnp.dot(
        x_ref[...].astype(BF16), w_ref[...], preferred_element_type=F32
    ).astype(o_ref.dtype)


def _proj(x, w, tm=512, tn=1024):
    t, k = x.shape
    n = w.shape[1]
    return pl.pallas_call(
        _proj_kernel,
        grid=(t // tm, n // tn),
        in_specs=[
            pl.BlockSpec((tm, k), lambda i, j: (i, 0)),
            pl.BlockSpec((k, tn), lambda i, j: (0, j)),
        ],
        out_specs=pl.BlockSpec((tm, tn), lambda i, j: (i, j)),
        out_shape=jax.ShapeDtypeStruct((t, n), BF16),
        compiler_params=_cparams(2),
        name="proj",
    )(x, w)


def _moba_kernel(slopes_ref, q_ref, qn_ref, k_ref, v_ref, wb_ref, o_ref, ka_ref, vt_ref, km_ref, sel_ref,
                 s_ref, *, n_blocks):
    blk = MOBA_BLOCK
    unroll = MOBA_KEY_BLOCKS_PER_STEP
    span = unroll * blk
    acc_rows = MOBA_ACC_ROWS
    h2 = pl.program_id(1)
    i = pl.program_id(2)
    heads = range(HEADS_PER_SLAB)

    @pl.when(i == 0)
    def _prepare_keys():
        lane = lax.broadcasted_iota(jnp.int32, (blk, LANES), 1)
        row_f = lax.broadcasted_iota(jnp.int32, (blk, LANES), 0).astype(F32)
        tail = lax.broadcasted_iota(jnp.int32, (acc_rows - HEAD_DIM, blk), 0)
        ones_row = jnp.where(tail == 0, 1.0, 0.0)

        def body(n, c):
            r0 = pl.multiple_of(n * blk, blk)
            vt = v_ref[0, pl.ds(r0, blk), :].astype(F32).T
            for hh in heads:
                vt_ref[n, hh] = jnp.concatenate(
                    [vt[hh * HEAD_DIM:(hh + 1) * HEAD_DIM], ones_row], axis=0).astype(BF16)
            kb = k_ref[0, pl.ds(r0, blk), :]
            km_ref[pl.ds(n, 1), :] = jnp.mean(kb.astype(F32), axis=0, keepdims=True)
            sub = jnp.asarray(n % unroll, F32)
            aux = jnp.where(lane < MOBA_BIAS_PARTS, sub, jnp.where(lane < 2 * MOBA_BIAS_PARTS, row_f, 0.0))
            ka_ref[pl.ds(r0, blk), :] = jnp.concatenate([kb, aux.astype(BF16)], axis=1)
            return c

        lax.fori_loop(0, n_blocks, body, 0)

    qb = MOBA_QUERY_BLOCKS_PER_STEP
    wide = qb * blk
    i_first = pl.program_id(2) * qb
    feat = lax.broadcasted_iota(jnp.int32, (LANES, wide), 0)
    kmean = km_ref[...]
    km_lane = lax.broadcasted_iota(jnp.int32, kmean.shape, 1)
    blk_row = lax.broadcasted_iota(jnp.int32, (n_blocks, wide), 0)
    own_blk = i_first + lax.broadcasted_iota(jnp.int32, (n_blocks, wide), 1) // blk

    def per_head_qt(ref):
        qt = ref[0].astype(F32).T
        return [jnp.where((feat >= hh * HEAD_DIM) & (feat < (hh + 1) * HEAD_DIM), qt, 0.0) for hh in heads]

    def score_operand(qt_hs):
        scaled = [(x * (HEAD_DIM ** -0.5 * LOG2E)).astype(BF16) for x in qt_hs]
        return jnp.concatenate([jnp.concatenate(scaled, axis=1), wb_ref[0]], axis=0)

    slopes = [slopes_ref[h2 * HEADS_PER_SLAB + hh] for hh in heads]
    qt_hs = per_head_qt(q_ref)
    for hh in heads:
        qt_h = qt_hs[hh]
        km_h = jnp.where((km_lane >= hh * HEAD_DIM) & (km_lane < (hh + 1) * HEAD_DIM), kmean, 0.0)
        qt_b = qt_h.astype(BF16)
        gate = sum(jnp.dot(part, qt_b, preferred_element_type=F32) for part in _bf16_parts(km_h, 3))
        g = jnp.where(blk_row < own_blk, gate, -jnp.inf)
        sel = jnp.zeros(g.shape, F32)
        for _ in range(MOBA_TOPK):
            gmax = jnp.max(g, axis=0, keepdims=True)
            first = jnp.min(jnp.where(g == gmax, blk_row, n_blocks), axis=0, keepdims=True)
            hit = blk_row == first
            sel = jnp.where(hit & (gmax > -jnp.inf), 1.0, sel)
            g = jnp.where(hit, -jnp.inf, g)
        sel_ref[hh] = sel

    w_all = score_operand(qt_hs)
    w_next = score_operand(per_head_qt(qn_ref))

    groups = [(hh, q) for hh in heads for q in range(qb)]
    group_cols = [slice((hh * qb + q) * blk, (hh * qb + q + 1) * blk) for hh, q in groups]

    def scores(span_idx, cols, w=w_all):
        r0 = pl.multiple_of(span_idx * span, span)
        return jnp.dot(ka_ref[pl.ds(r0, span), :], w[:, cols], preferred_element_type=F32)

    key_row = lax.broadcasted_iota(jnp.int32, (blk, blk), 0)
    qry_col = lax.broadcasted_iota(jnp.int32, (blk, blk), 1)

    def consume(gi, span_idx, m, acc, last):
        hh, q = groups[gi]
        j0 = span_idx * unroll
        c = slopes[hh] * LOG2E * jnp.asarray(blk * (j0 - (i_first + q)), F32)
        for u in range(unroll):
            if last and u > q:
                continue
            j = j0 + u
            s = s_ref[u * blk:(u + 1) * blk, group_cols[gi]]
            if last and u == q:
                s = jnp.where(key_row <= qry_col, s, NEG)
                m_new = jnp.maximum(m, jnp.max(s, axis=0, keepdims=True) + c)
                p = jnp.exp2(s - (m_new - c))
            else:
                ch = sel_ref[hh, pl.ds(j, 1), q * blk:(q + 1) * blk] > 0.5
                m_new = jnp.where(ch, jnp.maximum(m, jnp.max(s, axis=0, keepdims=True) + c), m)
                p = jnp.exp2(s - jnp.where(ch, m_new - c, -NEG))
            acc = jnp.exp2(m - m_new) * acc + jnp.dot(
                vt_ref[j, hh], p.astype(BF16), preferred_element_type=F32)
            m = m_new
        return m, acc

    n_full = i_first // unroll

    @pl.when(i == 0)
    def _first_scores():
        s_ref[...] = scores(0, slice(None))

    def body(it, state):
        out = []
        for gi in range(len(groups)):
            out.append(consume(gi, it, *state[gi], last=False))
            s_ref[:, group_cols[gi]] = scores(it + 1, group_cols[gi])
        return tuple(out)

    init = tuple((jnp.full((1, blk), NEG, F32), jnp.zeros((acc_rows, blk), F32)) for _ in groups)
    state = lax.fori_loop(0, n_full, body, init)
    final = []
    for gi in range(len(groups)):
        final.append(consume(gi, n_full, *state[gi], last=True))
        s_ref[:, group_cols[gi]] = scores(0, group_cols[gi], w_next)
    state = final

    outs = [acc[:HEAD_DIM] / acc[HEAD_DIM:HEAD_DIM + 1] for _, acc in state]
    per_block = [jnp.concatenate([outs[hh * qb + q] for hh in heads], axis=0) for q in range(qb)]
    o_ref[0] = jnp.concatenate(per_block, axis=1).T.astype(o_ref.dtype)


def _moba_attention(qkv, slopes):
    b, s, _ = qkv.shape
    blk = MOBA_BLOCK
    n_blocks = s // blk
    qb = MOBA_QUERY_BLOCKS_PER_STEP
    assert n_blocks % MOBA_KEY_BLOCKS_PER_STEP == 0 and MOBA_KEY_BLOCKS_PER_STEP == qb
    wide = HEADS_PER_SLAB * qb * blk
    grid_spec = pltpu.PrefetchScalarGridSpec(
        num_scalar_prefetch=1,
        grid=(b, N_SLABS, n_blocks // qb),
        in_specs=[
            pl.BlockSpec((1, qb * blk, LANES), lambda bi, h, i, sl: (bi, i, h)),
            pl.BlockSpec((1, qb * blk, LANES), lambda bi, h, i, sl: (bi, jnp.minimum(i + 1, n_blocks // qb - 1), h)),
            pl.BlockSpec((1, s, LANES), lambda bi, h, i, sl: (bi, 0, N_SLABS + h)),
            pl.BlockSpec((1, s, LANES), lambda bi, h, i, sl: (bi, 0, 2 * N_SLABS + h)),
            pl.BlockSpec((1, LANES, wide), lambda bi, h, i, sl: (h, 0, 0)),
        ],
        out_specs=pl.BlockSpec((1, qb * blk, LANES), lambda bi, h, i, sl: (bi, i, h)),
        scratch_shapes=[
            pltpu.VMEM((s, 2 * LANES), BF16),
            pltpu.VMEM((n_blocks, HEADS_PER_SLAB, MOBA_ACC_ROWS, blk), BF16),
            pltpu.VMEM((n_blocks, LANES), F32),
            pltpu.VMEM((HEADS_PER_SLAB, n_blocks, qb * blk), F32),
            pltpu.VMEM((MOBA_KEY_BLOCKS_PER_STEP * blk, wide), F32),
        ],
    )
    return pl.pallas_call(
        functools.partial(_moba_kernel, n_blocks=n_blocks),
        grid_spec=grid_spec,
        out_shape=jax.ShapeDtypeStruct((b, s, N_HEADS * HEAD_DIM), BF16),
        compiler_params=_cparams(3),
        name="moba",
    )(slopes, qkv, qkv, qkv, qkv, _moba_slope_rows(slopes))


def _moba_slope_rows(slopes):
    blk = MOBA_BLOCK
    wide = MOBA_QUERY_BLOCKS_PER_STEP * blk
    parts = _bf16_parts(slopes * LOG2E, MOBA_BIAS_PARTS)
    sp = jnp.stack([p.astype(F32) for p in parts], axis=1)
    rows = jnp.concatenate([sp * blk, sp], axis=1).reshape(N_SLABS, HEADS_PER_SLAB, 2 * MOBA_BIAS_PARTS)
    wb = jnp.zeros((N_SLABS, LANES, HEADS_PER_SLAB * wide), F32)
    for hh in range(HEADS_PER_SLAB):
        wb = wb.at[:, :2 * MOBA_BIAS_PARTS, hh * wide:(hh + 1) * wide].set(rows[:, hh, :, None])
    return wb.astype(BF16)


def _dilated_kernel(bias_ref, qt_ref, kc_ref, kp_ref, vtc_ref, vtp_ref, o_ref, lse_ref, lse_s, s_scr):
    blk = DIL_BLOCK
    lse_s[...] = jnp.zeros(lse_s.shape, F32)
    tail = lax.broadcasted_iota(jnp.int32, (MOBA_ACC_ROWS - HEAD_DIM, 2 * blk), 0)
    ones_row = jnp.where(tail == 0, 1.0, 0.0).astype(BF16)
    feat = lax.broadcasted_iota(jnp.int32, (LANES, blk), 0)
    for h2 in range(N_SLABS):
        cols = slice(h2 * LANES, (h2 + 1) * LANES)
        qt = qt_ref[0, 0, cols, :].astype(F32) * (HEAD_DIM ** -0.5 * LOG2E)
        kband = jnp.concatenate([kp_ref[0, 0, :, cols], kc_ref[0, 0, :, cols]], axis=0)
        qt_both = jnp.concatenate(
            [jnp.where((feat >= hh * HEAD_DIM) & (feat < (hh + 1) * HEAD_DIM), qt, 0.0)
             for hh in range(HEADS_PER_SLAB)], axis=1).astype(BF16)
        s_scr[h2] = jnp.dot(kband, qt_both, preferred_element_type=F32)
    for h2 in range(N_SLABS):
        cols = slice(h2 * LANES, (h2 + 1) * LANES)
        vt = jnp.concatenate([vtp_ref[0, 0, cols, :], vtc_ref[0, 0, cols, :]], axis=1)
        outs = []
        for hh in range(HEADS_PER_SLAB):
            h = h2 * HEADS_PER_SLAB + hh
            s = s_scr[h2, :, hh * blk:(hh + 1) * blk] + bias_ref[0, h]
            m = jnp.max(s, axis=0, keepdims=True)
            p = jnp.exp2(s - m).astype(BF16)
            vt_h = jnp.concatenate([vt[hh * HEAD_DIM:(hh + 1) * HEAD_DIM], ones_row], axis=0)
            acc = jnp.dot(vt_h, p, preferred_element_type=F32)
            l = acc[HEAD_DIM:HEAD_DIM + 1]
            outs.append(acc[:HEAD_DIM] / l)
            lse_s[pl.ds(h, 1), :] = (m + jnp.log2(l)) * (1.0 / LOG2E)
        o_ref[0, :, cols] = jnp.concatenate(outs, axis=0).T.astype(o_ref.dtype)
    lse_ref[0] = lse_s[...].T


def _dilated_bias(slopes, dil):
    blk = DIL_BLOCK
    key = jnp.arange(2 * blk)[:, None]
    steps = jnp.arange(blk)[None, :] - (key - blk)
    steps_max = blk
    valid = (steps >= 0) & (steps <= steps_max)
    bias = -(slopes * LOG2E)[:, None, None] * (steps * dil).astype(F32)[None]
    rest = jnp.where(valid[None], bias, NEG)
    first = jnp.where((key >= blk)[None], rest, NEG)
    return jnp.stack([first, rest]).astype(F32)


def _proj_residue_kernel(x_ref, wqt_ref, wk_ref, wvt_ref, qt_ref, k_ref, vt_ref):
    xb = x_ref[0].astype(BF16)
    nt = (((1,), (1,)), ((), ()))
    qt_ref[0, 0] = lax.dot_general(wqt_ref[...], xb, nt, preferred_element_type=F32).astype(qt_ref.dtype)
    k_ref[0, 0] = jnp.dot(xb, wk_ref[...], preferred_element_type=F32).astype(k_ref.dtype)
    vt_ref[0, 0] = lax.dot_general(wvt_ref[...], xb, nt, preferred_element_type=F32).astype(vt_ref.dtype)


def _proj_residue(x, w, bsz, dil, tm=512):
    t, k = x.shape
    d = w.shape[1] // 3
    l_sub = t // bsz // dil
    tm = min(tm, l_sub)
    view = x.reshape(bsz, l_sub, dil * k)
    const = lambda shape: pl.BlockSpec(shape, lambda b, r, m: (0, 0))
    tr = pl.BlockSpec((1, 1, d, tm), lambda b, r, m: (b, r, 0, m))
    return pl.pallas_call(
        _proj_residue_kernel,
        grid=(bsz, dil, l_sub // tm),
        in_specs=[pl.BlockSpec((1, tm, k), lambda b, r, m: (b, m, r)), const((d, k)), const((k, d)), const((d, k))],
        out_specs=[tr, pl.BlockSpec((1, 1, tm, d), lambda b, r, m: (b, r, m, 0)), tr],
        out_shape=[
            jax.ShapeDtypeStruct((bsz, dil, d, l_sub), BF16),
            jax.ShapeDtypeStruct((bsz, dil, l_sub, d), BF16),
            jax.ShapeDtypeStruct((bsz, dil, d, l_sub), BF16),
        ],
        compiler_params=_cparams(3),
        name=f"proj_residue_d{dil}",
    )(view, w[:, :d].T, w[:, d:2 * d], w[:, 2 * d:].T)


def _dilated_group(qt, k, vt, slopes, g, window, dil):
    b, _, l_sub, d = k.shape
    blk = DIL_BLOCK
    assert window // dil == blk
    s = l_sub * dil
    nb = l_sub // blk
    bias = _dilated_bias(slopes, dil)
    prev = lambda n: jnp.maximum(n - 1, 0)

    o, lse = pl.pallas_call(
        _dilated_kernel,
        grid=(b, dil, nb),
        in_specs=[
            pl.BlockSpec((1, N_HEADS, 2 * blk, blk), lambda bi, r, n: (jnp.minimum(n, 1), 0, 0, 0)),
            pl.BlockSpec((1, 1, d, blk), lambda bi, r, n: (bi, r, 0, n)),
            pl.BlockSpec((1, 1, blk, d), lambda bi, r, n: (bi, r, n, 0)),
            pl.BlockSpec((1, 1, blk, d), lambda bi, r, n: (bi, r, prev(n), 0)),
            pl.BlockSpec((1, 1, d, blk), lambda bi, r, n: (bi, r, 0, n)),
            pl.BlockSpec((1, 1, d, blk), lambda bi, r, n: (bi, r, 0, prev(n))),
        ],
        out_specs=[
            pl.BlockSpec((1, blk, d), lambda bi, r, n: (bi, n, r)),
            pl.BlockSpec((1, blk, LANES), lambda bi, r, n: (bi, n, r)),
        ],
        out_shape=[
            jax.ShapeDtypeStruct((b, l_sub, dil * d), BF16),
            jax.ShapeDtypeStruct((b, l_sub, dil * LANES), F32),
        ],
        scratch_shapes=[pltpu.VMEM((LANES, blk), F32), pltpu.VMEM((N_SLABS, 2 * blk, 2 * blk), F32)],
        compiler_params=_cparams(3),
        name=f"dilated_g{g}",
    )(bias, qt, k, k, vt, vt)
    return o.reshape(b, s, d), lse.reshape(b, s, LANES)


def _merge_kernel(e_ref, o1, o2, o3, l1, l2, l3, out_ref):
    ls = [l1[...], l2[...], l3[...]]
    mx = jnp.maximum(jnp.maximum(ls[0], ls[1]), ls[2])
    es = [jnp.exp(x - mx) for x in ls]
    den = es[0] + es[1] + es[2]
    acc = None
    for e, o in zip(es, (o1, o2, o3)):
        w = sum(jnp.dot(part, e_ref[...], preferred_element_type=F32) for part in _bf16_parts(e / den, 2))
        term = w * o[...].astype(F32)
        acc = term if acc is None else acc + term
    out_ref[...] = acc.astype(out_ref.dtype)


def _merge_groups(os, lses, tm=512):
    t, d = os[0].shape
    expand = (jnp.arange(LANES)[:, None] == (jnp.arange(d)[None, :] // HEAD_DIM)).astype(BF16)
    row = lambda w: pl.BlockSpec((tm, w), lambda i: (i, 0))
    return pl.pallas_call(
        _merge_kernel,
        grid=(t // tm,),
        in_specs=[pl.BlockSpec((LANES, d), lambda i: (0, 0))] + [row(d)] * 3 + [row(LANES)] * 3,
        out_specs=row(d),
        out_shape=jax.ShapeDtypeStruct((t, d), BF16),
        compiler_params=_cparams(1),
        name="merge_groups",
    )(expand, *os, *lses)


def _layer_norm(y, g, b):
    mu = jnp.mean(y, axis=-1, keepdims=True)
    yc = y - mu
    var = jnp.mean(yc * yc, axis=-1, keepdims=True)
    return yc * lax.rsqrt(var + LN_EPS) * g + b


U32 = jnp.uint32
TILE_WORDS = 2 * LANES
HIGH_HALF = np.uint32(0xFFFF0000)


def _pack_token_tiles(ref, y):
    n, d = y.shape
    npk = d // TILE_WORDS
    for c in range(npk):
        lo = y[:, (2 * c) * LANES:(2 * c + 1) * LANES].astype(BF16).astype(F32)
        hi = y[:, (2 * c + 1) * LANES:(2 * c + 2) * LANES].astype(BF16).astype(F32)
        word = (lax.bitcast_convert_type(lo, U32) >> 16) | (lax.bitcast_convert_type(hi, U32) & HIGH_HALF)
        ref[pl.ds(c, n, stride=npk), :] = word


def _unpack_token_tiles(ref, n, npk, lead=()):
    blocks = []
    for c in range(npk):
        word = ref[lead + (pl.ds(c, n, stride=npk), slice(None))]
        blocks.append(lax.bitcast_convert_type(word << 16, F32))
        blocks.append(lax.bitcast_convert_type(word & HIGH_HALF, F32))
    return jnp.concatenate(blocks, axis=1)


def _wo_ln_kernel(mix_ref, wo_ref, x_ref, g_ref, b_ref, o_ref, ot_ref):
    f = jnp.dot(mix_ref[...], wo_ref[...], preferred_element_type=F32)
    y = _layer_norm(DN_ALPHA * x_ref[...] + f, g_ref[...], b_ref[...])
    o_ref[...] = y
    _pack_token_tiles(ot_ref, y)


def _wo_ln(mix, wo, x, g, b, tm=256):
    t, d = x.shape
    npk = d // TILE_WORDS
    row = pl.BlockSpec((tm, d), lambda i: (i, 0))
    vec = pl.BlockSpec((1, d), lambda i: (0, 0))
    return pl.pallas_call(
        _wo_ln_kernel,
        grid=(t // tm,),
        in_specs=[row, pl.BlockSpec((d, d), lambda i: (0, 0)), row, vec, vec],
        out_specs=[row, pl.BlockSpec((tm * npk, LANES), lambda i: (i, 0))],
        out_shape=[jax.ShapeDtypeStruct((t, d), F32), jax.ShapeDtypeStruct((t * npk, LANES), U32)],
        compiler_params=_cparams(1),
        name="wo_ln",
    )(mix, wo, x, g.reshape(1, d), b.reshape(1, d))


def _router_kernel(wr_ref, rb_ref, x_ref, e_ref, w_ref, pos_ref, cnt_ref, run_ref, *, tm):
    step = pl.program_id(0)
    ne = N_EXPERTS
    gs = ne // N_GROUPS

    @pl.when(step == 0)
    def _():
        run_ref[...] = jnp.zeros(run_ref.shape, F32)

    x = x_ref[...]
    x_hi = x.astype(BF16)
    x_lo = (x - x_hi.astype(F32)).astype(BF16)
    nt = (((1,), (1,)), ((), ()))
    logits = (lax.dot_general(wr_ref[0], x_hi, nt, preferred_element_type=F32)
              + lax.dot_general(wr_ref[0], x_lo, nt, preferred_element_type=F32)
              + lax.dot_general(wr_ref[1], x_hi, nt, preferred_element_type=F32))
    scores = jax.nn.sigmoid(logits)
    choice = scores + rb_ref[...][:, :1]

    grp_rows = lax.broadcasted_iota(jnp.int32, (gs, tm), 0)
    blocks, gscore = [], []
    for gi in range(N_GROUPS):
        cb = choice[gi * gs:(gi + 1) * gs]
        m1 = jnp.max(cb, axis=0, keepdims=True)
        i1 = jnp.min(jnp.where(cb == m1, grp_rows, gs), axis=0, keepdims=True)
        m2 = jnp.max(jnp.where(grp_rows == i1, -jnp.inf, cb), axis=0, keepdims=True)
        blocks.append(cb)
        gscore.append(m1 + m2)
    masked = []
    for gi in range(N_GROUPS):
        beaten = jnp.zeros((1, tm), jnp.int32)
        for gj in range(N_GROUPS):
            if gj == gi:
                continue
            wins = (gscore[gj] > gscore[gi]) | ((gscore[gj] == gscore[gi]) & (gj < gi))
            beaten = beaten + wins.astype(jnp.int32)
        masked.append(jnp.where(beaten < TOPK_GROUPS, blocks[gi], NEG))
    cand = jnp.concatenate(masked, axis=0)

    rows = lax.broadcasted_iota(jnp.int32, (ne, tm), 0)
    member = jnp.zeros((ne, tm), F32)
    picks, raw_w = [], []
    for _ in range(TOP_K):
        cmax = jnp.max(cand, axis=0, keepdims=True)
        first = jnp.min(jnp.where(cand == cmax, rows, ne), axis=0, keepdims=True)
        hit = rows == first
        raw_w.append(jnp.sum(jnp.where(hit, scores, 0.0), axis=0, keepdims=True))
        member = jnp.where(hit, 1.0, member)
        cand = jnp.where(hit, -jnp.inf, cand)
        picks.append(first)
    wsum = raw_w[0]
    for r in raw_w[1:]:
        wsum = wsum + r

    tri = (lax.broadcasted_iota(jnp.int32, (tm, tm), 0) < lax.broadcasted_iota(jnp.int32, (tm, tm), 1))
    member_b = member.astype(BF16)
    before = jnp.dot(member_b, tri.astype(BF16), preferred_element_type=F32)
    run = run_ref[...]
    rank = before + jnp.concatenate([run] * (tm // LANES), axis=1)
    for k in range(TOP_K):
        hit = rows == picks[k]
        e_ref[pl.ds(k, 1), :] = picks[k]
        w_ref[pl.ds(k, 1), :] = raw_w[k] / wsum * ROUTED_SCALE
        pos_ref[pl.ds(k, 1), :] = jnp.sum(jnp.where(hit, rank, 0.0), axis=0, keepdims=True).astype(jnp.int32)
    run = run + jnp.dot(member_b, jnp.ones((tm, LANES), BF16), preferred_element_type=F32)
    run_ref[...] = run
    cnt_ref[...] = run


def _router(x, w_router, router_bias, tm=256):
    t, d = x.shape
    ne = N_EXPERTS
    wr_parts = jnp.stack(_bf16_parts(w_router.T.astype(F32), 2))
    rb = jnp.broadcast_to(router_bias.astype(F32)[:, None], (ne, LANES))
    tok = pl.BlockSpec((TOP_K, tm), lambda i: (0, i))
    return pl.pallas_call(
        functools.partial(_router_kernel, tm=tm),
        grid=(t // tm,),
        in_specs=[
            pl.BlockSpec((2, ne, d), lambda i: (0, 0, 0)),
            pl.BlockSpec((ne, LANES), lambda i: (0, 0)),
            pl.BlockSpec((tm, d), lambda i: (i, 0)),
        ],
        out_specs=[tok, tok, tok, pl.BlockSpec((ne, LANES), lambda i: (0, 0))],
        out_shape=[
            jax.ShapeDtypeStruct((TOP_K, t), jnp.int32),
            jax.ShapeDtypeStruct((TOP_K, t), F32),
            jax.ShapeDtypeStruct((TOP_K, t), jnp.int32),
            jax.ShapeDtypeStruct((ne, LANES), F32),
        ],
        scratch_shapes=[pltpu.VMEM((ne, LANES), F32)],
        compiler_params=_cparams(1),
        name="router",
    )(wr_parts, rb, x)


def _tile_rows(ref, i, npk):
    return ref.at[pl.ds(pl.multiple_of(i * npk, npk), npk)]


def _dispatch_kernel(dest_hbm, x_ref, xs_hbm, dest_smem, sem_idx, sem_rows, *, tm):
    step = pl.program_id(0)
    npk = x_ref.shape[0] // tm
    idx_copy = pltpu.make_async_copy(dest_hbm.at[step], dest_smem, sem_idx)
    idx_copy.start()
    idx_copy.wait()

    def row_copy(t, d):
        return pltpu.make_async_copy(_tile_rows(x_ref, t, npk), _tile_rows(xs_hbm, d, npk), sem_rows)

    def issue(t, c):
        for k in range(TOP_K):
            row_copy(t, dest_smem[k * tm + t]).start(priority=k % 2)
        return c

    lax.fori_loop(0, tm, issue, 0)

    def drain(t, c):
        for _ in range(TOP_K):
            row_copy(0, 0).wait()
        return c

    lax.fori_loop(0, tm, drain, 0)


def _dispatch(x_tiles, dest_tiles, n_rows, npk, tm):
    t = x_tiles.shape[0] // npk
    return pl.pallas_call(
        functools.partial(_dispatch_kernel, tm=tm),
        grid=(t // tm,),
        in_specs=[pl.BlockSpec(memory_space=pl.ANY), pl.BlockSpec((tm * npk, LANES), lambda i: (i, 0))],
        out_specs=pl.BlockSpec(memory_space=pl.ANY),
        out_shape=jax.ShapeDtypeStruct((n_rows * npk, LANES), x_tiles.dtype),
        scratch_shapes=[
            pltpu.SMEM((TOP_K * tm,), jnp.int32),
            pltpu.SemaphoreType.DMA,
            pltpu.SemaphoreType.DMA,
        ],
        compiler_params=_cparams(1),
        name="dispatch",
    )(dest_tiles, x_tiles)


def _experts_kernel(blk_e_ref, blk_row_ref, n_real_ref, xs_ref, wg_ref, wu_ref, wd_ref, ys_ref, *, rb):
    @pl.when(pl.program_id(0) < n_real_ref[0])
    def _():
        npk = xs_ref.shape[0] // rb
        xb = _unpack_token_tiles(xs_ref, rb, npk).astype(BF16)
        gate = jnp.dot(xb, wg_ref[0, 0].astype(BF16), preferred_element_type=F32)
        up = jnp.dot(xb, wu_ref[0, 0].astype(BF16), preferred_element_type=F32)
        h = (gate * jax.nn.sigmoid(gate) * up).astype(BF16)
        _pack_token_tiles(ys_ref, jnp.dot(h, wd_ref[0, 0].astype(BF16), preferred_element_type=F32))


def _experts(xs, we_gate, we_up, we_down, layer, rb, blk_e, blk_row, n_real):
    d, f = we_gate.shape[-2:]
    nc = d // TILE_WORDS
    p = xs.shape[0] // nc
    rows = pl.BlockSpec((rb * nc, LANES), lambda i, be, br, nr: (br[i], 0))
    grid_spec = pltpu.PrefetchScalarGridSpec(
        num_scalar_prefetch=3,
        grid=(p // rb,),
        in_specs=[
            rows,
            pl.BlockSpec((1, 1, d, f), lambda i, be, br, nr: (layer, be[i], 0, 0)),
            pl.BlockSpec((1, 1, d, f), lambda i, be, br, nr: (layer, be[i], 0, 0)),
            pl.BlockSpec((1, 1, f, d), lambda i, be, br, nr: (layer, be[i], 0, 0)),
        ],
        out_specs=rows,
    )
    return pl.pallas_call(
        functools.partial(_experts_kernel, rb=rb),
        grid_spec=grid_spec,
        out_shape=jax.ShapeDtypeStruct((p * nc, LANES), U32),
        compiler_params=_cparams(1),
        name="experts",
    )(blk_e, blk_row, n_real, xs, we_gate, we_up, we_down)


def _combine_kernel(dest_hbm, ys_hbm, x_ref, w_ref, sg_ref, su_ref, sd_ref, g_ref, b_ref, o_ref,
                    dest_smem0, dest_smem1, rows_ref, sem_idx, sem_rows, *, tm):
    dest_smems = (dest_smem0, dest_smem1)
    step = pl.program_id(0)
    n_steps = pl.num_programs(0)
    npk = rows_ref.shape[2] // tm
    slot = step % 2

    def row_copy(sl, k, t, d):
        return pltpu.make_async_copy(
            _tile_rows(ys_hbm, d, npk), _tile_rows(rows_ref.at[sl, k], t, npk), sem_rows.at[sl])

    def idx_copy(tile, sl):
        return pltpu.make_async_copy(dest_hbm.at[tile], dest_smems[sl], sem_idx.at[sl])

    def issue_rows(sl):
        dest_smem = dest_smems[sl]

        def issue(t, c):
            for k in range(TOP_K):
                row_copy(sl, k, t, dest_smem[k * tm + t]).start(priority=k % 2)
            return c

        lax.fori_loop(0, tm, issue, 0)

    @pl.when(step == 0)
    def _():
        idx_copy(0, 0).start()
        idx_copy(0, 0).wait()
        issue_rows(0)

        @pl.when(n_steps > 1)
        def _():
            idx_copy(1, 1).start()

    for sl in range(2):
        @pl.when((step + 1 < n_steps) & (slot == 1 - sl))
        def _(sl=sl):
            idx_copy(step + 1, sl).wait()

            @pl.when(step + 2 < n_steps)
            def _():
                idx_copy(step + 2, 1 - sl).start()

            issue_rows(sl)

    x = x_ref[...]
    xb = x.astype(BF16)
    gate = jnp.dot(xb, sg_ref[...], preferred_element_type=F32)
    up = jnp.dot(xb, su_ref[...], preferred_element_type=F32)
    h = (gate * jax.nn.sigmoid(gate) * up).astype(BF16)
    f = jnp.dot(h, sd_ref[...], preferred_element_type=F32)

    def drain(t, c):
        for _ in range(TOP_K):
            row_copy(slot, 0, 0, 0).wait()
        return c

    lax.fori_loop(0, tm, drain, 0)

    w = w_ref[...]
    for k in range(TOP_K):
        f = f + w[:, k:k + 1] * _unpack_token_tiles(rows_ref, tm, npk, lead=(slot, k))
    o_ref[...] = _layer_norm(DN_ALPHA * x + f, g_ref[...], b_ref[...])


def _combine(x, ys, dest_tiles, w_tok, ws_gate, ws_up, ws_down, g, b, tm=128):
    t, d = x.shape
    nc = d // TILE_WORDS
    f = ws_gate.shape[-1]
    row = pl.BlockSpec((tm, d), lambda i: (i, 0))
    vec = pl.BlockSpec((1, d), lambda i: (0, 0))
    return pl.pallas_call(
        functools.partial(_combine_kernel, tm=tm),
        grid=(t // tm,),
        in_specs=[
            pl.BlockSpec(memory_space=pl.ANY),
            pl.BlockSpec(memory_space=pl.ANY),
            row,
            pl.BlockSpec((tm, TOP_K), lambda i: (i, 0)),
            pl.BlockSpec((d, f), lambda i: (0, 0)),
            pl.BlockSpec((d, f), lambda i: (0, 0)),
            pl.BlockSpec((f, d), lambda i: (0, 0)),
            vec,
            vec,
        ],
        out_specs=row,
        out_shape=jax.ShapeDtypeStruct((t, d), F32),
        scratch_shapes=[
            pltpu.SMEM((TOP_K * tm,), jnp.int32),
            pltpu.SMEM((TOP_K * tm,), jnp.int32),
            pltpu.VMEM((2, TOP_K, tm * nc, LANES), U32),
            pltpu.SemaphoreType.DMA((2,)),
            pltpu.SemaphoreType.DMA((2,)),
        ],
        compiler_params=_cparams(1),
        name="combine",
    )(dest_tiles, ys, x, w_tok, ws_gate, ws_up, ws_down, g.reshape(1, d), b.reshape(1, d))


def _tile_major(a, tm):
    k, t = a.shape
    return a.reshape(k, t // tm, tm).transpose(1, 0, 2).reshape(t // tm, k * tm)


def _dest_kernel(start_ref, e_ref, pos_ref, dest_ref, *, tm):
    ne = N_EXPERTS
    start = jnp.concatenate([start_ref[...]] * (tm // LANES), axis=1)
    rows = lax.broadcasted_iota(jnp.int32, (ne, tm), 0)
    for k in range(TOP_K):
        hit = rows == e_ref[pl.ds(k, 1), :]
        first = jnp.sum(jnp.where(hit, start, 0), axis=0, keepdims=True)
        dest_ref[pl.ds(k, 1), :] = first + pos_ref[pl.ds(k, 1), :]


def _dest(starts, top_e, pos, tm=512):
    t = top_e.shape[1]
    tok = pl.BlockSpec((TOP_K, tm), lambda i: (0, i))
    return pl.pallas_call(
        functools.partial(_dest_kernel, tm=tm),
        grid=(t // tm,),
        in_specs=[pl.BlockSpec((N_EXPERTS, LANES), lambda i: (0, 0)), tok, tok],
        out_specs=tok,
        out_shape=jax.ShapeDtypeStruct((TOP_K, t), jnp.int32),
        compiler_params=_cparams(1),
        name="dest",
    )(jnp.broadcast_to(starts[:, None], (N_EXPERTS, LANES)), top_e, pos)


def _expert_row_block(t):
    mean_rows = t * TOP_K // N_EXPERTS
    return max(LANES, -(-(mean_rows + mean_rows // 4) // LANES) * LANES)


def _moe_ln(x, x_tiles, layer, w_router, router_bias, we_gate, we_up, we_down, ws_gate, ws_up, ws_down,
            g, b):
    t, d = x.shape
    rb = _expert_row_block(t)
    top_e, w, pos, counts = _router(x, w_router, router_bias)

    cnt = counts[:, 0].astype(jnp.int32)
    padded = (cnt + rb - 1) // rb * rb
    pend = jnp.cumsum(padded)
    n_blocks = t * TOP_K // rb + N_EXPERTS
    n_real = (pend[-1] // rb).astype(jnp.int32)
    blk_ids = jnp.minimum(jnp.arange(n_blocks, dtype=jnp.int32), jnp.maximum(n_real - 1, 0))
    blk_e = jnp.sum((pend[None, :] <= (blk_ids * rb)[:, None]).astype(jnp.int32), axis=1)
    blk_e = jnp.minimum(blk_e, N_EXPERTS - 1)

    dest = _dest(pend - padded, top_e, pos)
    xs = _dispatch(x_tiles, _tile_major(dest, DISPATCH_TOKENS), n_blocks * rb, d // TILE_WORDS,
                   tm=DISPATCH_TOKENS)
    ys = _experts(xs, we_gate, we_up, we_down, layer, rb, blk_e, blk_ids, n_real.reshape(1))
    return _combine(
        x, ys, _tile_major(dest, COMBINE_TOKENS), w.T, ws_gate.astype(BF16),
        ws_up.astype(BF16), ws_down.astype(BF16), g, b, tm=COMBINE_TOKENS,
    )


def kernel(x, w_qkv_a, w_qkv_b, w_o, ln_mix_g, ln_mix_b, w_router, router_bias, we_gate, we_up,
           we_down, ws_gate, ws_up, ws_down, ln_ffn_g, ln_ffn_b):
    bsz, s, d = x.shape
    t = bsz * s
    slopes = _alibi_slopes()
    xt = x.reshape(t, d)
    for i in range(DEPTH):
        if i % N_MIXERS == 0:
            qkv = _proj(xt, w_qkv_a[i // N_MIXERS].astype(BF16))
            mix = _moba_attention(qkv.reshape(bsz, s, -1), slopes).reshape(t, d)
        else:
            w_b = w_qkv_b[i // N_MIXERS].astype(BF16)
            outs, lses = [], []
            for g, (window, dil) in enumerate(DIL_GROUPS):
                qt, kk, vt = _proj_residue(xt, w_b[:, g * 3 * d:(g + 1) * 3 * d], bsz, dil)
                o, lse = _dilated_group(qt, kk, vt, slopes, g, window, dil)
                outs.append(o.reshape(t, d))
                lses.append(lse.reshape(t, LANES))
            mix = _merge_groups(outs, lses)
        x1, x1_tiles = _wo_ln(mix, w_o[i].astype(BF16), xt, ln_mix_g[i], ln_mix_b[i])
        xt = _moe_ln(x1, x1_tiles, i, w_router[i], router_bias[i], we_gate, we_up, we_down,
                     ws_gate[i], ws_up[i], ws_down[i], ln_ffn_g[i], ln_ffn_b[i])
    return xt.reshape(bsz, s, d)
```

```python
import functools
import math

import jax
import jax.numpy as jnp
import numpy as np
from jax import lax
from jax.experimental import pallas as pl
from jax.experimental.pallas import tpu as pltpu

N_HEADS = 16
HEAD_DIM = 64
DEPTH = 2
N_MIXERS = 2
MOBA_BLOCK = 256
MOBA_TOPK = 3
MOBA_KEY_BLOCKS_PER_STEP = 4
MOBA_QUERY_BLOCKS_PER_STEP = MOBA_KEY_BLOCKS_PER_STEP
MOBA_ACC_ROWS = 80
MOBA_BIAS_PARTS = 3
DIL_GROUPS = ((128, 1), (512, 4), (2048, 16))
DIL_BLOCK = 128
N_EXPERTS = 256
TOP_K = 8
N_GROUPS = 8
TOPK_GROUPS = 4
ROUTED_SCALE = 2.5
DN_ALPHA = (2 * DEPTH) ** 0.25
LN_EPS = 1e-5
NEG = -1e30
LOG2E = math.log2(math.e)

LANES = 128
HEADS_PER_SLAB = LANES // HEAD_DIM
N_SLABS = N_HEADS // HEADS_PER_SLAB

DISPATCH_TOKENS = 512
COMBINE_TOKENS = 256

BF16 = jnp.bfloat16
F32 = jnp.float32

_ARB = "arbitrary"


def _cparams(n_axes, vmem_mb=48):
    return pltpu.CompilerParams(
        dimension_semantics=(_ARB,) * n_axes, vmem_limit_bytes=vmem_mb * 1024 * 1024
    )


def _alibi_slopes():
    return 2.0 ** (-8.0 * jnp.arange(1, N_HEADS + 1, dtype=F32) / N_HEADS)


def _bf16_parts(x, n):
    parts, rest = [], x
    for _ in range(n):
        top = lax.bitcast_convert_type(lax.bitcast_convert_type(rest, jnp.uint32) & np.uint32(0xFFFF0000), F32)
        parts.append(top.astype(BF16))
        rest = rest - top
    return parts


def _proj_kernel(x_ref, w_ref, o_ref):
    o_ref[...] = jnp.dot(
        x_ref[...].astype(BF16), w_ref[...], preferred_element_type=F32
    ).astype(o_ref.dtype)


def _proj(x, w, tm=512, tn=1024):
    t, k = x.shape
    n = w.shape[1]
    return pl.pallas_call(
        _proj_kernel,
        grid=(t // tm, n // tn),
        in_specs=[
            pl.BlockSpec((tm, k), lambda i, j: (i, 0)),
            pl.BlockSpec((k, tn), lambda i, j: (0, j)),
        ],
        out_specs=pl.BlockSpec((tm, tn), lambda i, j: (i, j)),
        out_shape=jax.ShapeDtypeStruct((t, n), BF16),
        compiler_params=_cparams(2),
        name="proj",
    )(x, w)


def _moba_kernel(slopes_ref, q_ref, qn_ref, k_ref, v_ref, wb_ref, o_ref, ka_ref, vt_ref, km_ref, sel_ref,
                 s_ref, *, n_blocks):
    blk = MOBA_BLOCK
    unroll = MOBA_KEY_BLOCKS_PER_STEP
    span = unroll * blk
    acc_rows = MOBA_ACC_ROWS
    h2 = pl.program_id(1)
    i = pl.program_id(2)
    heads = range(HEADS_PER_SLAB)

    @pl.when(i == 0)
    def _prepare_keys():
        lane = lax.broadcasted_iota(jnp.int32, (blk, LANES), 1)
        row_f = lax.broadcasted_iota(jnp.int32, (blk, LANES), 0).astype(F32)
        tail = lax.broadcasted_iota(jnp.int32, (acc_rows - HEAD_DIM, blk), 0)
        ones_row = jnp.where(tail == 0, 1.0, 0.0)

        def body(n, c):
            r0 = pl.multiple_of(n * blk, blk)
            vt = v_ref[0, pl.ds(r0, blk), :].astype(F32).T
            for hh in heads:
                vt_ref[n, hh] = jnp.concatenate(
                    [vt[hh * HEAD_DIM:(hh + 1) * HEAD_DIM], ones_row], axis=0).astype(BF16)
            kb = k_ref[0, pl.ds(r0, blk), :]
            km_ref[pl.ds(n, 1), :] = jnp.mean(kb.astype(F32), axis=0, keepdims=True)
            sub = jnp.asarray(n % unroll, F32)
            aux = jnp.where(lane < MOBA_BIAS_PARTS, sub, jnp.where(lane < 2 * MOBA_BIAS_PARTS, row_f, 0.0))
            ka_ref[pl.ds(r0, blk), :] = jnp.concatenate([kb, aux.astype(BF16)], axis=1)
            return c

        lax.fori_loop(0, n_blocks, body, 0)

    qb = MOBA_QUERY_BLOCKS_PER_STEP
    wide = qb * blk
    i_first = pl.program_id(2) * qb
    feat = lax.broadcasted_iota(jnp.int32, (LANES, wide), 0)
    kmean = km_ref[...]
    km_lane = lax.broadcasted_iota(jnp.int32, kmean.shape, 1)
    blk_row = lax.broadcasted_iota(jnp.int32, (n_blocks, wide), 0)
    own_blk = i_first + lax.broadcasted_iota(jnp.int32, (n_blocks, wide), 1) // blk

    def per_head_qt(ref):
        qt = ref[0].astype(F32).T
        return [jnp.where((feat >= hh * HEAD_DIM) & (feat < (hh + 1) * HEAD_DIM), qt, 0.0) for hh in heads]

    def score_operand(qt_hs):
        scaled = [(x * (HEAD_DIM ** -0.5 * LOG2E)).astype(BF16) for x in qt_hs]
        return jnp.concatenate([jnp.concatenate(scaled, axis=1), wb_ref[0]], axis=0)

    slopes = [slopes_ref[h2 * HEADS_PER_SLAB + hh] for hh in heads]
    qt_hs = per_head_qt(q_ref)
    for hh in heads:
        qt_h = qt_hs[hh]
        km_h = jnp.where((km_lane >= hh * HEAD_DIM) & (km_lane < (hh + 1) * HEAD_DIM), kmean, 0.0)
        qt_b = qt_h.astype(BF16)
        gate = sum(jnp.dot(part, qt_b, preferred_element_type=F32) for part in _bf16_parts(km_h, 3))
        g = jnp.where(blk_row < own_blk, gate, -jnp.inf)
        sel = jnp.zeros(g.shape, F32)
        for _ in range(MOBA_TOPK):
            gmax = jnp.max(g, axis=0, keepdims=True)
            first = jnp.min(jnp.where(g == gmax, blk_row, n_blocks), axis=0, keepdims=True)
            hit = blk_row == first
            sel = jnp.where(hit & (gmax > -jnp.inf), 1.0, sel)
            g = jnp.where(hit, -jnp.inf, g)
        sel_ref[hh] = sel

    w_all = score_operand(qt_hs)
    w_next = score_operand(per_head_qt(qn_ref))

    groups = [(hh, q) for hh in heads for q in range(qb)]
    group_cols = [slice((hh * qb + q) * blk, (hh * qb + q + 1) * blk) for hh, q in groups]

    def scores(span_idx, cols, w=w_all):
        r0 = pl.multiple_of(span_idx * span, span)
        return jnp.dot(ka_ref[pl.ds(r0, span), :], w[:, cols], preferred_element_type=F32)

    key_row = lax.broadcasted_iota(jnp.int32, (blk, blk), 0)
    qry_col = lax.broadcasted_iota(jnp.int32, (blk, blk), 1)

    def consume(gi, span_idx, m, acc, last):
        hh, q = groups[gi]
        j0 = span_idx * unroll
        c = slopes[hh] * LOG2E * jnp.asarray(blk * (j0 - (i_first + q)), F32)
        for u in range(unroll):
            if last and u > q:
                continue
            j = j0 + u
            s = s_ref[u * blk:(u + 1) * blk, group_cols[gi]]
            if last and u == q:
                s = jnp.where(key_row <= qry_col, s, NEG)
                m_new = jnp.maximum(m, jnp.max(s, axis=0, keepdims=True) + c)
                p = jnp.exp2(s - (m_new - c))
            else:
                ch = sel_ref[hh, pl.ds(j, 1), q * blk:(q + 1) * blk] > 0.5
                m_new = jnp.where(ch, jnp.maximum(m, jnp.max(s, axis=0, keepdims=True) + c), m)
                p = jnp.exp2(s - jnp.where(ch, m_new - c, -NEG))
            acc = jnp.exp2(m - m_new) * acc + jnp.dot(
                vt_ref[j, hh], p.astype(BF16), preferred_element_type=F32)
            m = m_new
        return m, acc

    n_full = i_first // unroll

    @pl.when(i == 0)
    def _first_scores():
        s_ref[...] = scores(0, slice(None))

    def body(it, state):
        out = []
        for gi in range(len(groups)):
            out.append(consume(gi, it, *state[gi], last=False))
            s_ref[:, group_cols[gi]] = scores(it + 1, group_cols[gi])
        return tuple(out)

    init = tuple((jnp.full((1, blk), NEG, F32), jnp.zeros((acc_rows, blk), F32)) for _ in groups)
    state = lax.fori_loop(0, n_full, body, init)
    final = []
    for gi in range(len(groups)):
        final.append(consume(gi, n_full, *state[gi], last=True))
        s_ref[:, group_cols[gi]] = scores(0, group_cols[gi], w_next)
    state = final

    outs = [acc[:HEAD_DIM] / acc[HEAD_DIM:HEAD_DIM + 1] for _, acc in state]
    per_block = [jnp.concatenate([outs[hh * qb + q] for hh in heads], axis=0) for q in range(qb)]
    o_ref[0] = jnp.concatenate(per_block, axis=1).T.astype(o_ref.dtype)


def _moba_attention(qkv, slopes):
    b, s, _ = qkv.shape
    blk = MOBA_BLOCK
    n_blocks = s // blk
    qb = MOBA_QUERY_BLOCKS_PER_STEP
    assert n_blocks % MOBA_KEY_BLOCKS_PER_STEP == 0 and MOBA_KEY_BLOCKS_PER_STEP == qb
    wide = HEADS_PER_SLAB * qb * blk
    grid_spec = pltpu.PrefetchScalarGridSpec(
        num_scalar_prefetch=1,
        grid=(b, N_SLABS, n_blocks // qb),
        in_specs=[
            pl.BlockSpec((1, qb * blk, LANES), lambda bi, h, i, sl: (bi, i, h)),
            pl.BlockSpec((1, qb * blk, LANES), lambda bi, h, i, sl: (bi, jnp.minimum(i + 1, n_blocks // qb - 1), h)),
            pl.BlockSpec((1, s, LANES), lambda bi, h, i, sl: (bi, 0, N_SLABS + h)),
            pl.BlockSpec((1, s, LANES), lambda bi, h, i, sl: (bi, 0, 2 * N_SLABS + h)),
            pl.BlockSpec((1, LANES, wide), lambda bi, h, i, sl: (h, 0, 0)),
        ],
        out_specs=pl.BlockSpec((1, qb * blk, LANES), lambda bi, h, i, sl: (bi, i, h)),
        scratch_shapes=[
            pltpu.VMEM((s, 2 * LANES), BF16),
            pltpu.VMEM((n_blocks, HEADS_PER_SLAB, MOBA_ACC_ROWS, blk), BF16),
            pltpu.VMEM((n_blocks, LANES), F32),
            pltpu.VMEM((HEADS_PER_SLAB, n_blocks, qb * blk), F32),
            pltpu.VMEM((MOBA_KEY_BLOCKS_PER_STEP * blk, wide), F32),
        ],
    )
    return pl.pallas_call(
        functools.partial(_moba_kernel, n_blocks=n_blocks),
        grid_spec=grid_spec,
        out_shape=jax.ShapeDtypeStruct((b, s, N_HEADS * HEAD_DIM), BF16),
        compiler_params=_cparams(3),
        name="moba",
    )(slopes, qkv, qkv, qkv, qkv, _moba_slope_rows(slopes))


def _moba_slope_rows(slopes):
    blk = MOBA_BLOCK
    wide = MOBA_QUERY_BLOCKS_PER_STEP * blk
    parts = _bf16_parts(slopes * LOG2E, MOBA_BIAS_PARTS)
    sp = jnp.stack([p.astype(F32) for p in parts], axis=1)
    rows = jnp.concatenate([sp * blk, sp], axis=1).reshape(N_SLABS, HEADS_PER_SLAB, 2 * MOBA_BIAS_PARTS)
    wb = jnp.zeros((N_SLABS, LANES, HEADS_PER_SLAB * wide), F32)
    for hh in range(HEADS_PER_SLAB):
        wb = wb.at[:, :2 * MOBA_BIAS_PARTS, hh * wide:(hh + 1) * wide].set(rows[:, hh, :, None])
    return wb.astype(BF16)


def _dilated_kernel(bias_ref, qt_ref, kc_ref, kp_ref, vtc_ref, vtp_ref, o_ref, lse_ref, lse_s, s_scr):
    blk = DIL_BLOCK
    lse_s[...] = jnp.zeros(lse_s.shape, F32)
    tail = lax.broadcasted_iota(jnp.int32, (MOBA_ACC_ROWS - HEAD_DIM, 2 * blk), 0)
    ones_row = jnp.where(tail == 0, 1.0, 0.0).astype(BF16)
    feat = lax.broadcasted_iota(jnp.int32, (LANES, blk), 0)
    for h2 in range(N_SLABS):
        cols = slice(h2 * LANES, (h2 + 1) * LANES)
        qt = qt_ref[0, 0, cols, :].astype(F32) * (HEAD_DIM ** -0.5 * LOG2E)
        kband = jnp.concatenate([kp_ref[0, 0, :, cols], kc_ref[0, 0, :, cols]], axis=0)
        qt_both = jnp.concatenate(
            [jnp.where((feat >= hh * HEAD_DIM) & (feat < (hh + 1) * HEAD_DIM), qt, 0.0)
             for hh in range(HEADS_PER_SLAB)], axis=1).astype(BF16)
        s_scr[h2] = jnp.dot(kband, qt_both, preferred_element_type=F32)
    for h2 in range(N_SLABS):
        cols = slice(h2 * LANES, (h2 + 1) * LANES)
        vt = jnp.concatenate([vtp_ref[0, 0, cols, :], vtc_ref[0, 0, cols, :]], axis=1)
        outs = []
        for hh in range(HEADS_PER_SLAB):
            h = h2 * HEADS_PER_SLAB + hh
            s = s_scr[h2, :, hh * blk:(hh + 1) * blk] + bias_ref[0, h]
            m = jnp.max(s, axis=0, keepdims=True)
            p = jnp.exp2(s - m).astype(BF16)
            vt_h = jnp.concatenate([vt[hh * HEAD_DIM:(hh + 1) * HEAD_DIM], ones_row], axis=0)
            acc = jnp.dot(vt_h, p, preferred_element_type=F32)
            l = acc[HEAD_DIM:HEAD_DIM + 1]
            outs.append(acc[:HEAD_DIM] / l)
            lse_s[pl.ds(h, 1), :] = (m + jnp.log2(l)) * (1.0 / LOG2E)
        o_ref[0, :, cols] = jnp.concatenate(outs, axis=0).T.astype(o_ref.dtype)
    lse_ref[0] = lse_s[...].T


def _dilated_bias(slopes, dil):
    blk = DIL_BLOCK
    key = jnp.arange(2 * blk)[:, None]
    steps = jnp.arange(blk)[None, :] - (key - blk)
    steps_max = blk
    valid = (steps >= 0) & (steps <= steps_max)
    bias = -(slopes * LOG2E)[:, None, None] * (steps * dil).astype(F32)[None]
    rest = jnp.where(valid[None], bias, NEG)
    first = jnp.where((key >= blk)[None], rest, NEG)
    return jnp.stack([first, rest]).astype(F32)


def _proj_residue_kernel(x_ref, wqt_ref, wk_ref, wvt_ref, qt_ref, k_ref, vt_ref):
    xb = x_ref[0].astype(BF16)
    nt = (((1,), (1,)), ((), ()))
    qt_ref[0, 0] = lax.dot_general(wqt_ref[...], xb, nt, preferred_element_type=F32).astype(qt_ref.dtype)
    k_ref[0, 0] = jnp.dot(xb, wk_ref[...], preferred_element_type=F32).astype(k_ref.dtype)
    vt_ref[0, 0] = lax.dot_general(wvt_ref[...], xb, nt, preferred_element_type=F32).astype(vt_ref.dtype)


def _proj_residue(x, w, bsz, dil, tm=512):
    t, k = x.shape
    d = w.shape[1] // 3
    l_sub = t // bsz // dil
    tm = min(tm, l_sub)
    view = x.reshape(bsz, l_sub, dil * k)
    const = lambda shape: pl.BlockSpec(shape, lambda b, r, m: (0, 0))
    tr = pl.BlockSpec((1, 1, d, tm), lambda b, r, m: (b, r, 0, m))
    return pl.pallas_call(
        _proj_residue_kernel,
        grid=(bsz, dil, l_sub // tm),
        in_specs=[pl.BlockSpec((1, tm, k), lambda b, r, m: (b, m, r)), const((d, k)), const((k, d)), const((d, k))],
        out_specs=[tr, pl.BlockSpec((1, 1, tm, d), lambda b, r, m: (b, r, m, 0)), tr],
        out_shape=[
            jax.ShapeDtypeStruct((bsz, dil, d, l_sub), BF16),
            jax.ShapeDtypeStruct((bsz, dil, l_sub, d), BF16),
            jax.ShapeDtypeStruct((bsz, dil, d, l_sub), BF16),
        ],
        compiler_params=_cparams(3),
        name=f"proj_residue_d{dil}",
    )(view, w[:, :d].T, w[:, d:2 * d], w[:, 2 * d:].T)


def _dilated_group(qt, k, vt, slopes, g, window, dil):
    b, _, l_sub, d = k.shape
    blk = DIL_BLOCK
    assert window // dil == blk
    s = l_sub * dil
    nb = l_sub // blk
    bias = _dilated_bias(slopes, dil)
    prev = lambda n: jnp.maximum(n - 1, 0)

    o, lse = pl.pallas_call(
        _dilated_kernel,
        grid=(b, dil, nb),
        in_specs=[
            pl.BlockSpec((1, N_HEADS, 2 * blk, blk), lambda bi, r, n: (jnp.minimum(n, 1), 0, 0, 0)),
            pl.BlockSpec((1, 1, d, blk), lambda bi, r, n: (bi, r, 0, n)),
            pl.BlockSpec((1, 1, blk, d), lambda bi, r, n: (bi, r, n, 0)),
            pl.BlockSpec((1, 1, blk, d), lambda bi, r, n: (bi, r, prev(n), 0)),
            pl.BlockSpec((1, 1, d, blk), lambda bi, r, n: (bi, r, 0, n)),
            pl.BlockSpec((1, 1, d, blk), lambda bi, r, n: (bi, r, 0, prev(n))),
        ],
        out_specs=[
            pl.BlockSpec((1, blk, d), lambda bi, r, n: (bi, n, r)),
            pl.BlockSpec((1, blk, LANES), lambda bi, r, n: (bi, n, r)),
        ],
        out_shape=[
            jax.ShapeDtypeStruct((b, l_sub, dil * d), BF16),
            jax.ShapeDtypeStruct((b, l_sub, dil * LANES), F32),
        ],
        scratch_shapes=[pltpu.VMEM((LANES, blk), F32), pltpu.VMEM((N_SLABS, 2 * blk, 2 * blk), F32)],
        compiler_params=_cparams(3),
        name=f"dilated_g{g}",
    )(bias, qt, k, k, vt, vt)
    return o.reshape(b, s, d), lse.reshape(b, s, LANES)


def _merge_kernel(e_ref, o1, o2, o3, l1, l2, l3, out_ref):
    ls = [l1[...], l2[...], l3[...]]
    mx = jnp.maximum(jnp.maximum(ls[0], ls[1]), ls[2])
    es = [jnp.exp(x - mx) for x in ls]
    den = es[0] + es[1] + es[2]
    acc = None
    for e, o in zip(es, (o1, o2, o3)):
        w = sum(jnp.dot(part, e_ref[...], preferred_element_type=F32) for part in _bf16_parts(e / den, 2))
        term = w * o[...].astype(F32)
        acc = term if acc is None else acc + term
    out_ref[...] = acc.astype(out_ref.dtype)


def _merge_groups(os, lses, tm=512):
    t, d = os[0].shape
    expand = (jnp.arange(LANES)[:, None] == (jnp.arange(d)[None, :] // HEAD_DIM)).astype(BF16)
    row = lambda w: pl.BlockSpec((tm, w), lambda i: (i, 0))
    return pl.pallas_call(
        _merge_kernel,
        grid=(t // tm,),
        in_specs=[pl.BlockSpec((LANES, d), lambda i: (0, 0))] + [row(d)] * 3 + [row(LANES)] * 3,
        out_specs=row(d),
        out_shape=jax.ShapeDtypeStruct((t, d), BF16),
        compiler_params=_cparams(1),
        name="merge_groups",
    )(expand, *os, *lses)


def _layer_norm(y, g, b):
    mu = jnp.mean(y, axis=-1, keepdims=True)
    yc = y - mu
    var = jnp.mean(yc * yc, axis=-1, keepdims=True)
    return yc * lax.rsqrt(var + LN_EPS) * g + b


U32 = jnp.uint32
TILE_WORDS = 2 * LANES
HIGH_HALF = np.uint32(0xFFFF0000)


def _pack_token_tiles(ref, y):
    n, d = y.shape
    npk = d // TILE_WORDS
    for c in range(npk):
        lo = y[:, (2 * c) * LANES:(2 * c + 1) * LANES].astype(BF16).astype(F32)
        hi = y[:, (2 * c + 1) * LANES:(2 * c + 2) * LANES].astype(BF16).astype(F32)
        word = (lax.bitcast_convert_type(lo, U32) >> 16) | (lax.bitcast_convert_type(hi, U32) & HIGH_HALF)
        ref[pl.ds(c, n, stride=npk), :] = word


def _unpack_token_tiles(ref, n, npk, lead=()):
    blocks = []
    for c in range(npk):
        word = ref[lead + (pl.ds(c, n, stride=npk), slice(None))]
        blocks.append(lax.bitcast_convert_type(word << 16, F32))
        blocks.append(lax.bitcast_convert_type(word & HIGH_HALF, F32))
    return jnp.concatenate(blocks, axis=1)


def _wo_ln_kernel(mix_ref, wo_ref, x_ref, g_ref, b_ref, o_ref, ot_ref):
    f = jnp.dot(mix_ref[...], wo_ref[...], preferred_element_type=F32)
    y = _layer_norm(DN_ALPHA * x_ref[...] + f, g_ref[...], b_ref[...])
    o_ref[...] = y
    _pack_token_tiles(ot_ref, y)


def _wo_ln(mix, wo, x, g, b, tm=512):
    t, d = x.shape
    npk = d // TILE_WORDS
    row = pl.BlockSpec((tm, d), lambda i: (i, 0))
    vec = pl.BlockSpec((1, d), lambda i: (0, 0))
    return pl.pallas_call(
        _wo_ln_kernel,
        grid=(t // tm,),
        in_specs=[row, pl.BlockSpec((d, d), lambda i: (0, 0)), row, vec, vec],
        out_specs=[row, pl.BlockSpec((tm * npk, LANES), lambda i: (i, 0))],
        out_shape=[jax.ShapeDtypeStruct((t, d), F32), jax.ShapeDtypeStruct((t * npk, LANES), U32)],
        compiler_params=_cparams(1),
        name="wo_ln",
    )(mix, wo, x, g.reshape(1, d), b.reshape(1, d))


def _router_kernel(wr_ref, rb_ref, x_ref, e_ref, w_ref, pos_ref, cnt_ref, run_ref, *, tm):
    step = pl.program_id(0)
    ne = N_EXPERTS
    gs = ne // N_GROUPS

    @pl.when(step == 0)
    def _():
        run_ref[...] = jnp.zeros(run_ref.shape, F32)

    x = x_ref[...]
    x_hi = x.astype(BF16)
    x_lo = (x - x_hi.astype(F32)).astype(BF16)
    nt = (((1,), (1,)), ((), ()))
    logits = (lax.dot_general(wr_ref[0], x_hi, nt, preferred_element_type=F32)
              + lax.dot_general(wr_ref[0], x_lo, nt, preferred_element_type=F32)
              + lax.dot_general(wr_ref[1], x_hi, nt, preferred_element_type=F32))
    scores = jax.nn.sigmoid(logits)
    choice = scores + rb_ref[...][:, :1]

    grp_rows = lax.broadcasted_iota(jnp.int32, (gs, tm), 0)
    blocks, gscore = [], []
    for gi in range(N_GROUPS):
        cb = choice[gi * gs:(gi + 1) * gs]
        m1 = jnp.max(cb, axis=0, keepdims=True)
        i1 = jnp.min(jnp.where(cb == m1, grp_rows, gs), axis=0, keepdims=True)
        m2 = jnp.max(jnp.where(grp_rows == i1, -jnp.inf, cb), axis=0, keepdims=True)
        blocks.append(cb)
        gscore.append(m1 + m2)
    masked = []
    for gi in range(N_GROUPS):
        beaten = jnp.zeros((1, tm), jnp.int32)
        for gj in range(N_GROUPS):
            if gj == gi:
                continue
            wins = (gscore[gj] > gscore[gi]) | ((gscore[gj] == gscore[gi]) & (gj < gi))
            beaten = beaten + wins.astype(jnp.int32)
        masked.append(jnp.where(beaten < TOPK_GROUPS, blocks[gi], NEG))
    cand = jnp.concatenate(masked, axis=0)

    rows = lax.broadcasted_iota(jnp.int32, (ne, tm), 0)
    member = jnp.zeros((ne, tm), F32)
    picks, raw_w = [], []
    for _ in range(TOP_K):
        cmax = jnp.max(cand, axis=0, keepdims=True)
        first = jnp.min(jnp.where(cand == cmax, rows, ne), axis=0, keepdims=True)
        hit = rows == first
        raw_w.append(jnp.sum(jnp.where(hit, scores, 0.0), axis=0, keepdims=True))
        member = jnp.where(hit, 1.0, member)
        cand = jnp.where(hit, -jnp.inf, cand)
        picks.append(first)
    wsum = raw_w[0]
    for r in raw_w[1:]:
        wsum = wsum + r

    tri = (lax.broadcasted_iota(jnp.int32, (tm, tm), 0) < lax.broadcasted_iota(jnp.int32, (tm, tm), 1))
    member_b = member.astype(BF16)
    before = jnp.dot(member_b, tri.astype(BF16), preferred_element_type=F32)
    run = run_ref[...]
    rank = before + jnp.concatenate([run] * (tm // LANES), axis=1)
    for k in range(TOP_K):
        hit = rows == picks[k]
        e_ref[pl.ds(k, 1), :] = picks[k]
        w_ref[pl.ds(k, 1), :] = raw_w[k] / wsum * ROUTED_SCALE
        pos_ref[pl.ds(k, 1), :] = jnp.sum(jnp.where(hit, rank, 0.0), axis=0, keepdims=True).astype(jnp.int32)
    run = run + jnp.dot(member_b, jnp.ones((tm, LANES), BF16), preferred_element_type=F32)
    run_ref[...] = run
    cnt_ref[...] = run


def _router(x, w_router, router_bias, tm=256):
    t, d = x.shape
    ne = N_EXPERTS
    wr_parts = jnp.stack(_bf16_parts(w_router.T.astype(F32), 2))
    rb = jnp.broadcast_to(router_bias.astype(F32)[:, None], (ne, LANES))
    tok = pl.BlockSpec((TOP_K, tm), lambda i: (0, i))
    return pl.pallas_call(
        functools.partial(_router_kernel, tm=tm),
        grid=(t // tm,),
        in_specs=[
            pl.BlockSpec((2, ne, d), lambda i: (0, 0, 0)),
            pl.BlockSpec((ne, LANES), lambda i: (0, 0)),
            pl.BlockSpec((tm, d), lambda i: (i, 0)),
        ],
        out_specs=[tok, tok, tok, pl.BlockSpec((ne, LANES), lambda i: (0, 0))],
        out_shape=[
            jax.ShapeDtypeStruct((TOP_K, t), jnp.int32),
            jax.ShapeDtypeStruct((TOP_K, t), F32),
            jax.ShapeDtypeStruct((TOP_K, t), jnp.int32),
            jax.ShapeDtypeStruct((ne, LANES), F32),
        ],
        scratch_shapes=[pltpu.VMEM((ne, LANES), F32)],
        compiler_params=_cparams(1),
        name="router",
    )(wr_parts, rb, x)


def _tile_rows(ref, i, npk):
    return ref.at[pl.ds(pl.multiple_of(i * npk, npk), npk)]


def _dispatch_kernel(dest_hbm, x_hbm, x_ref, xs_hbm, dest_smem, sem_idx, sem_rows, *, tm):
    step = pl.program_id(0)
    npk = x_ref.shape[0] // tm
    idx_copy = pltpu.make_async_copy(dest_hbm.at[step], dest_smem, sem_idx)
    idx_copy.start()
    idx_copy.wait()

    def row_copy(t, d, from_hbm=False):
        src = _tile_rows(x_hbm, step * tm + t, npk) if from_hbm else _tile_rows(x_ref, t, npk)
        return pltpu.make_async_copy(src, _tile_rows(xs_hbm, d, npk), sem_rows)

    def issue(t, c):
        for k in range(TOP_K):
            row_copy(t, dest_smem[k * tm + t], from_hbm=k >= TOP_K // 2).start(priority=k % 2)
        return c

    lax.fori_loop(0, tm, issue, 0)

    def drain(t, c):
        for _ in range(TOP_K):
            row_copy(0, 0).wait()
        return c

    lax.fori_loop(0, tm, drain, 0)


def _dispatch(x_tiles, dest_tiles, n_rows, npk, tm):
    t = x_tiles.shape[0] // npk
    return pl.pallas_call(
        functools.partial(_dispatch_kernel, tm=tm),
        grid=(t // tm,),
        in_specs=[pl.BlockSpec(memory_space=pl.ANY), pl.BlockSpec(memory_space=pl.ANY),
                  pl.BlockSpec((tm * npk, LANES), lambda i: (i, 0))],
        out_specs=pl.BlockSpec(memory_space=pl.ANY),
        out_shape=jax.ShapeDtypeStruct((n_rows * npk, LANES), x_tiles.dtype),
        scratch_shapes=[
            pltpu.SMEM((TOP_K * tm,), jnp.int32),
            pltpu.SemaphoreType.DMA,
            pltpu.SemaphoreType.DMA,
        ],
        compiler_params=_cparams(1),
        name="dispatch",
    )(dest_tiles, x_tiles, x_tiles)


def _experts_kernel(blk_e_ref, blk_row_ref, n_real_ref, xs_ref, wg_ref, wu_ref, wd_ref, ys_ref, *, rb):
    @pl.when(pl.program_id(0) < n_real_ref[0])
    def _():
        npk = xs_ref.shape[0] // rb
        xb = _unpack_token_tiles(xs_ref, rb, npk).astype(BF16)
        gate = jnp.dot(xb, wg_ref[0, 0].astype(BF16), preferred_element_type=F32)
        up = jnp.dot(xb, wu_ref[0, 0].astype(BF16), preferred_element_type=F32)
        h = (gate * jax.nn.sigmoid(gate) * up).astype(BF16)
        _pack_token_tiles(ys_ref, jnp.dot(h, wd_ref[0, 0].astype(BF16), preferred_element_type=F32))


def _experts(xs, we_gate, we_up, we_down, layer, rb, blk_e, blk_row, n_real):
    d, f = we_gate.shape[-2:]
    nc = d // TILE_WORDS
    p = xs.shape[0] // nc
    rows = pl.BlockSpec((rb * nc, LANES), lambda i, be, br, nr: (br[i], 0))
    grid_spec = pltpu.PrefetchScalarGridSpec(
        num_scalar_prefetch=3,
        grid=(p // rb,),
        in_specs=[
            rows,
            pl.BlockSpec((1, 1, d, f), lambda i, be, br, nr: (layer, be[i], 0, 0)),
            pl.BlockSpec((1, 1, d, f), lambda i, be, br, nr: (layer, be[i], 0, 0)),
            pl.BlockSpec((1, 1, f, d), lambda i, be, br, nr: (layer, be[i], 0, 0)),
        ],
        out_specs=rows,
    )
    return pl.pallas_call(
        functools.partial(_experts_kernel, rb=rb),
        grid_spec=grid_spec,
        out_shape=jax.ShapeDtypeStruct((p * nc, LANES), U32),
        compiler_params=_cparams(1),
        name="experts",
    )(blk_e, blk_row, n_real, xs, we_gate, we_up, we_down)


def _combine_kernel(dest_hbm, ys_hbm, x_ref, w_ref, sg_ref, su_ref, sd_ref, g_ref, b_ref, o_ref,
                    dest_smem0, dest_smem1, rows_ref, sem_idx, sem_rows, *, tm):
    dest_smems = (dest_smem0, dest_smem1)
    step = pl.program_id(0)
    n_steps = pl.num_programs(0)
    npk = rows_ref.shape[2] // tm
    slot = step % 2

    def row_copy(sl, k, t, d):
        return pltpu.make_async_copy(
            _tile_rows(ys_hbm, d, npk), _tile_rows(rows_ref.at[sl, k], t, npk), sem_rows.at[sl])

    def idx_copy(tile, sl):
        return pltpu.make_async_copy(dest_hbm.at[tile], dest_smems[sl], sem_idx.at[sl])

    def issue_rows(sl):
        dest_smem = dest_smems[sl]

        def issue(t, c):
            for k in range(TOP_K):
                row_copy(sl, k, t, dest_smem[k * tm + t]).start(priority=k % 2)
            return c

        lax.fori_loop(0, tm, issue, 0)

    @pl.when(step == 0)
    def _():
        idx_copy(0, 0).start()
        idx_copy(0, 0).wait()
        issue_rows(0)

        @pl.when(n_steps > 1)
        def _():
            idx_copy(1, 1).start()

    for sl in range(2):
        @pl.when((step + 1 < n_steps) & (slot == 1 - sl))
        def _(sl=sl):
            idx_copy(step + 1, sl).wait()

            @pl.when(step + 2 < n_steps)
            def _():
                idx_copy(step + 2, 1 - sl).start()

            issue_rows(sl)

    x = x_ref[...]
    xb = x.astype(BF16)
    gate = jnp.dot(xb, sg_ref[...], preferred_element_type=F32)
    up = jnp.dot(xb, su_ref[...], preferred_element_type=F32)
    h = (gate * jax.nn.sigmoid(gate) * up).astype(BF16)
    f = jnp.dot(h, sd_ref[...], preferred_element_type=F32)

    def drain(t, c):
        for _ in range(TOP_K):
            row_copy(slot, 0, 0, 0).wait()
        return c

    lax.fori_loop(0, tm, drain, 0)

    w = w_ref[...]
    for k in range(TOP_K):
        f = f + w[:, k:k + 1] * _unpack_token_tiles(rows_ref, tm, npk, lead=(slot, k))
    o_ref[...] = _layer_norm(DN_ALPHA * x + f, g_ref[...], b_ref[...])


def _combine(x, ys, dest_tiles, w_tok, ws_gate, ws_up, ws_down, g, b, tm=128):
    t, d = x.shape
    nc = d // TILE_WORDS
    f = ws_gate.shape[-1]
    row = pl.BlockSpec((tm, d), lambda i: (i, 0))
    vec = pl.BlockSpec((1, d), lambda i: (0, 0))
    return pl.pallas_call(
        functools.partial(_combine_kernel, tm=tm),
        grid=(t // tm,),
        in_specs=[
            pl.BlockSpec(memory_space=pl.ANY),
            pl.BlockSpec(memory_space=pl.ANY),
            row,
            pl.BlockSpec((tm, TOP_K), lambda i: (i, 0)),
            pl.BlockSpec((d, f), lambda i: (0, 0)),
            pl.BlockSpec((d, f), lambda i: (0, 0)),
            pl.BlockSpec((f, d), lambda i: (0, 0)),
            vec,
            vec,
        ],
        out_specs=row,
        out_shape=jax.ShapeDtypeStruct((t, d), F32),
        scratch_shapes=[
            pltpu.SMEM((TOP_K * tm,), jnp.int32),
            pltpu.SMEM((TOP_K * tm,), jnp.int32),
            pltpu.VMEM((2, TOP_K, tm * nc, LANES), U32),
            pltpu.SemaphoreType.DMA((2,)),
            pltpu.SemaphoreType.DMA((2,)),
        ],
        compiler_params=_cparams(1),
        name="combine",
    )(dest_tiles, ys, x, w_tok, ws_gate, ws_up, ws_down, g.reshape(1, d), b.reshape(1, d))


def _tile_major(a, tm):
    k, t = a.shape
    return a.reshape(k, t // tm, tm).transpose(1, 0, 2).reshape(t // tm, k * tm)


def _dest_kernel(start_ref, e_ref, pos_ref, dest_ref, *, tm):
    ne = N_EXPERTS
    start = jnp.concatenate([start_ref[...]] * (tm // LANES), axis=1)
    rows = lax.broadcasted_iota(jnp.int32, (ne, tm), 0)
    for k in range(TOP_K):
        hit = rows == e_ref[pl.ds(k, 1), :]
        first = jnp.sum(jnp.where(hit, start, 0), axis=0, keepdims=True)
        dest_ref[pl.ds(k, 1), :] = first + pos_ref[pl.ds(k, 1), :]


def _dest(starts, top_e, pos, tm=512):
    t = top_e.shape[1]
    tok = pl.BlockSpec((TOP_K, tm), lambda i: (0, i))
    return pl.pallas_call(
        functools.partial(_dest_kernel, tm=tm),
        grid=(t // tm,),
        in_specs=[pl.BlockSpec((N_EXPERTS, LANES), lambda i: (0, 0)), tok, tok],
        out_specs=tok,
        out_shape=jax.ShapeDtypeStruct((TOP_K, t), jnp.int32),
        compiler_params=_cparams(1),
        name="dest",
    )(jnp.broadcast_to(starts[:, None], (N_EXPERTS, LANES)), top_e, pos)


def _expert_row_block(t):
    mean_rows = t * TOP_K // N_EXPERTS
    return max(LANES, -(-(mean_rows + mean_rows // 4) // LANES) * LANES)


def _moe_ln(x, x_tiles, layer, w_router, router_bias, we_gate, we_up, we_down, ws_gate, ws_up, ws_down,
            g, b):
    t, d = x.shape
    rb = _expert_row_block(t)
    top_e, w, pos, counts = _router(x, w_router, router_bias)

    cnt = counts[:, 0].astype(jnp.int32)
    padded = (cnt + rb - 1) // rb * rb
    pend = jnp.cumsum(padded)
    n_blocks = t * TOP_K // rb + N_EXPERTS
    n_real = (pend[-1] // rb).astype(jnp.int32)
    blk_ids = jnp.minimum(jnp.arange(n_blocks, dtype=jnp.int32), jnp.maximum(n_real - 1, 0))
    blk_e = jnp.sum((pend[None, :] <= (blk_ids * rb)[:, None]).astype(jnp.int32), axis=1)
    blk_e = jnp.minimum(blk_e, N_EXPERTS - 1)

    dest = _dest(pend - padded, top_e, pos)
    xs = _dispatch(x_tiles, _tile_major(dest, DISPATCH_TOKENS), n_blocks * rb, d // TILE_WORDS,
                   tm=DISPATCH_TOKENS)
    ys = _experts(xs, we_gate, we_up, we_down, layer, rb, blk_e, blk_ids, n_real.reshape(1))
    return _combine(
        x, ys, _tile_major(dest, COMBINE_TOKENS), w.T, ws_gate.astype(BF16),
        ws_up.astype(BF16), ws_down.astype(BF16), g, b, tm=COMBINE_TOKENS,
    )


def kernel(x, w_qkv_a, w_qkv_b, w_o, ln_mix_g, ln_mix_b, w_router, router_bias, we_gate, we_up,
           we_down, ws_gate, ws_up, ws_down, ln_ffn_g, ln_ffn_b):
    bsz, s, d = x.shape
    t = bsz * s
    slopes = _alibi_slopes()
    xt = x.reshape(t, d)
    for i in range(DEPTH):
        if i % N_MIXERS == 0:
            qkv = _proj(xt, w_qkv_a[i // N_MIXERS].astype(BF16))
            mix = _moba_attention(qkv.reshape(bsz, s, -1), slopes).reshape(t, d)
        else:
            w_b = w_qkv_b[i // N_MIXERS].astype(BF16)
            outs, lses = [], []
            for g, (window, dil) in enumerate(DIL_GROUPS):
                qt, kk, vt = _proj_residue(xt, w_b[:, g * 3 * d:(g + 1) * 3 * d], bsz, dil)
                o, lse = _dilated_group(qt, kk, vt, slopes, g, window, dil)
                outs.append(o.reshape(t, d))
                lses.append(lse.reshape(t, LANES))
            mix = _merge_groups(outs, lses)
        x1, x1_tiles = _wo_ln(mix, w_o[i].astype(BF16), xt, ln_mix_g[i], ln_mix_b[i])
        xt = _moe_ln(x1, x1_tiles, i, w_router[i], router_bias[i], we_gate, we_up, we_down,
                     ws_gate[i], ws_up[i], ws_down[i], ln_ffn_g[i], ln_ffn_b[i])
    return xt.reshape(bsz, s, d)
```

```python
import functools
import math

import jax
import jax.numpy as jnp
import numpy as np
from jax import lax
from jax.experimental import pallas as pl
from jax.experimental.pallas import tpu as pltpu

N_HEADS = 16
HEAD_DIM = 64
DEPTH = 2
N_MIXERS = 2
MOBA_BLOCK = 256
MOBA_TOPK = 3
MOBA_KEY_BLOCKS_PER_STEP = 4
MOBA_QUERY_BLOCKS_PER_STEP = MOBA_KEY_BLOCKS_PER_STEP
MOBA_ACC_ROWS = 80
MOBA_BIAS_PARTS = 3
DIL_GROUPS = ((128, 1), (512, 4), (2048, 16))
DIL_BLOCK = 128
N_EXPERTS = 256
TOP_K = 8
N_GROUPS = 8
TOPK_GROUPS = 4
ROUTED_SCALE = 2.5
DN_ALPHA = (2 * DEPTH) ** 0.25
LN_EPS = 1e-5
NEG = -1e30
LOG2E = math.log2(math.e)

LANES = 128
HEADS_PER_SLAB = LANES // HEAD_DIM
N_SLABS = N_HEADS // HEADS_PER_SLAB

DISPATCH_TOKENS = 512
COMBINE_TOKENS = 512

BF16 = jnp.bfloat16
F32 = jnp.float32

_ARB = "arbitrary"


def _cparams(n_axes, vmem_mb=48):
    return pltpu.CompilerParams(
        dimension_semantics=(_ARB,) * n_axes, vmem_limit_bytes=vmem_mb * 1024 * 1024
    )


def _alibi_slopes():
    return 2.0 ** (-8.0 * jnp.arange(1, N_HEADS + 1, dtype=F32) / N_HEADS)


def _bf16_parts(x, n):
    parts, rest = [], x
    for _ in range(n):
        top = lax.bitcast_convert_type(lax.bitcast_convert_type(rest, jnp.uint32) & np.uint32(0xFFFF0000), F32)
        parts.append(top.astype(BF16))
        rest = rest - top
    return parts


def _proj_kernel(x_ref, w_ref, o_ref):
    o_ref[...] = jnp.dot(
        x_ref[...].astype(BF16), w_ref[...], preferred_element_type=F32
    ).astype(o_ref.dtype)


def _proj(x, w, tm=1024, tn=1024):
    t, k = x.shape
    n = w.shape[1]
    return pl.pallas_call(
        _proj_kernel,
        grid=(t // tm, n // tn),
        in_specs=[
            pl.BlockSpec((tm, k), lambda i, j: (i, 0)),
            pl.BlockSpec((k, tn), lambda i, j: (0, j)),
        ],
        out_specs=pl.BlockSpec((tm, tn), lambda i, j: (i, j)),
        out_shape=jax.ShapeDtypeStruct((t, n), BF16),
        compiler_params=_cparams(2),
        name="proj",
    )(x, w)


def _moba_kernel(slopes_ref, q_ref, qn_ref, k_ref, v_ref, wb_ref, o_ref, ka_ref, vt_ref, km_ref, sel_ref,
                 s_ref, *, n_blocks):
    blk = MOBA_BLOCK
    unroll = MOBA_KEY_BLOCKS_PER_STEP
    span = unroll * blk
    acc_rows = MOBA_ACC_ROWS
    h2 = pl.program_id(1)
    i = pl.program_id(2)
    heads = range(HEADS_PER_SLAB)

    @pl.when(i == 0)
    def _prepare_keys():
        lane = lax.broadcasted_iota(jnp.int32, (blk, LANES), 1)
        row_f = lax.broadcasted_iota(jnp.int32, (blk, LANES), 0).astype(F32)
        tail = lax.broadcasted_iota(jnp.int32, (acc_rows - HEAD_DIM, blk), 0)
        ones_row = jnp.where(tail == 0, 1.0, 0.0)

        def body(n, c):
            r0 = pl.multiple_of(n * blk, blk)
            vt = v_ref[0, pl.ds(r0, blk), :].astype(F32).T
            for hh in heads:
                vt_ref[n, hh] = jnp.concatenate(
                    [vt[hh * HEAD_DIM:(hh + 1) * HEAD_DIM], ones_row], axis=0).astype(BF16)
            kb = k_ref[0, pl.ds(r0, blk), :]
            km_ref[pl.ds(n, 1), :] = jnp.mean(kb.astype(F32), axis=0, keepdims=True)
            sub = jnp.asarray(n % unroll, F32)
            aux = jnp.where(lane < MOBA_BIAS_PARTS, sub, jnp.where(lane < 2 * MOBA_BIAS_PARTS, row_f, 0.0))
            ka_ref[pl.ds(r0, blk), :] = jnp.concatenate([kb, aux.astype(BF16)], axis=1)
            return c

        lax.fori_loop(0, n_blocks, body, 0)

    qb = MOBA_QUERY_BLOCKS_PER_STEP
    wide = qb * blk
    i_first = pl.program_id(2) * qb
    feat = lax.broadcasted_iota(jnp.int32, (LANES, wide), 0)
    kmean = km_ref[...]
    km_lane = lax.broadcasted_iota(jnp.int32, kmean.shape, 1)
    blk_row = lax.broadcasted_iota(jnp.int32, (n_blocks, wide), 0)
    own_blk = i_first + lax.broadcasted_iota(jnp.int32, (n_blocks, wide), 1) // blk

    def per_head_qt(ref):
        qt = ref[0].astype(F32).T
        return [jnp.where((feat >= hh * HEAD_DIM) & (feat < (hh + 1) * HEAD_DIM), qt, 0.0) for hh in heads]

    def score_operand(qt_hs):
        scaled = [(x * (HEAD_DIM ** -0.5 * LOG2E)).astype(BF16) for x in qt_hs]
        return jnp.concatenate([jnp.concatenate(scaled, axis=1), wb_ref[0]], axis=0)

    slopes = [slopes_ref[h2 * HEADS_PER_SLAB + hh] for hh in heads]
    qt_hs = per_head_qt(q_ref)
    for hh in heads:
        qt_h = qt_hs[hh]
        km_h = jnp.where((km_lane >= hh * HEAD_DIM) & (km_lane < (hh + 1) * HEAD_DIM), kmean, 0.0)
        qt_b = qt_h.astype(BF16)
        gate = sum(jnp.dot(part, qt_b, preferred_element_type=F32) for part in _bf16_parts(km_h, 3))
        g = jnp.where(blk_row < own_blk, gate, -jnp.inf)
        sel = jnp.zeros(g.shape, F32)
        for _ in range(MOBA_TOPK):
            gmax = jnp.max(g, axis=0, keepdims=True)
            first = jnp.min(jnp.where(g == gmax, blk_row, n_blocks), axis=0, keepdims=True)
            hit = blk_row == first
            sel = jnp.where(hit & (gmax > -jnp.inf), 1.0, sel)
            g = jnp.where(hit, -jnp.inf, g)
        sel_ref[hh] = sel

    w_all = score_operand(qt_hs)
    w_next = score_operand(per_head_qt(qn_ref))

    groups = [(hh, q) for hh in heads for q in range(qb)]
    group_cols = [slice((hh * qb + q) * blk, (hh * qb + q + 1) * blk) for hh, q in groups]

    def scores(span_idx, cols, w=w_all):
        r0 = pl.multiple_of(span_idx * span, span)
        return jnp.dot(ka_ref[pl.ds(r0, span), :], w[:, cols], preferred_element_type=F32)

    key_row = lax.broadcasted_iota(jnp.int32, (blk, blk), 0)
    qry_col = lax.broadcasted_iota(jnp.int32, (blk, blk), 1)

    def consume(gi, span_idx, m, acc, last):
        hh, q = groups[gi]
        j0 = span_idx * unroll
        c = slopes[hh] * LOG2E * jnp.asarray(blk * (j0 - (i_first + q)), F32)
        for u in range(unroll):
            if last and u > q:
                continue
            j = j0 + u
            s = s_ref[u * blk:(u + 1) * blk, group_cols[gi]]
            if last and u == q:
                s = jnp.where(key_row <= qry_col, s, NEG)
                m_new = jnp.maximum(m, jnp.max(s, axis=0, keepdims=True) + c)
                p = jnp.exp2(s - (m_new - c))
            else:
                ch = sel_ref[hh, pl.ds(j, 1), q * blk:(q + 1) * blk] > 0.5
                m_new = jnp.where(ch, jnp.maximum(m, jnp.max(s, axis=0, keepdims=True) + c), m)
                p = jnp.exp2(s - jnp.where(ch, m_new - c, -NEG))
            acc = jnp.exp2(m - m_new) * acc + jnp.dot(
                vt_ref[j, hh], p.astype(BF16), preferred_element_type=F32)
            m = m_new
        return m, acc

    n_full = i_first // unroll

    @pl.when(i == 0)
    def _first_scores():
        s_ref[...] = scores(0, slice(None))

    def body(it, state):
        out = []
        for gi in range(len(groups)):
            out.append(consume(gi, it, *state[gi], last=False))
            s_ref[:, group_cols[gi]] = scores(it + 1, group_cols[gi])
        return tuple(out)

    init = tuple((jnp.full((1, blk), NEG, F32), jnp.zeros((acc_rows, blk), F32)) for _ in groups)
    state = lax.fori_loop(0, n_full, body, init)
    final = []
    for gi in range(len(groups)):
        final.append(consume(gi, n_full, *state[gi], last=True))
        s_ref[:, group_cols[gi]] = scores(0, group_cols[gi], w_next)
    state = final

    outs = [acc[:HEAD_DIM] / acc[HEAD_DIM:HEAD_DIM + 1] for _, acc in state]
    per_block = [jnp.concatenate([outs[hh * qb + q] for hh in heads], axis=0) for q in range(qb)]
    o_ref[0] = jnp.concatenate(per_block, axis=1).T.astype(o_ref.dtype)


def _moba_attention(qkv, slopes):
    b, s, _ = qkv.shape
    blk = MOBA_BLOCK
    n_blocks = s // blk
    qb = MOBA_QUERY_BLOCKS_PER_STEP
    assert n_blocks % MOBA_KEY_BLOCKS_PER_STEP == 0 and MOBA_KEY_BLOCKS_PER_STEP == qb
    wide = HEADS_PER_SLAB * qb * blk
    grid_spec = pltpu.PrefetchScalarGridSpec(
        num_scalar_prefetch=1,
        grid=(b, N_SLABS, n_blocks // qb),
        in_specs=[
            pl.BlockSpec((1, qb * blk, LANES), lambda bi, h, i, sl: (bi, i, h)),
            pl.BlockSpec((1, qb * blk, LANES), lambda bi, h, i, sl: (bi, jnp.minimum(i + 1, n_blocks // qb - 1), h)),
            pl.BlockSpec((1, s, LANES), lambda bi, h, i, sl: (bi, 0, N_SLABS + h)),
            pl.BlockSpec((1, s, LANES), lambda bi, h, i, sl: (bi, 0, 2 * N_SLABS + h)),
            pl.BlockSpec((1, LANES, wide), lambda bi, h, i, sl: (h, 0, 0)),
        ],
        out_specs=pl.BlockSpec((1, qb * blk, LANES), lambda bi, h, i, sl: (bi, i, h)),
        scratch_shapes=[
            pltpu.VMEM((s, 2 * LANES), BF16),
            pltpu.VMEM((n_blocks, HEADS_PER_SLAB, MOBA_ACC_ROWS, blk), BF16),
            pltpu.VMEM((n_blocks, LANES), F32),
            pltpu.VMEM((HEADS_PER_SLAB, n_blocks, qb * blk), F32),
            pltpu.VMEM((MOBA_KEY_BLOCKS_PER_STEP * blk, wide), F32),
        ],
    )
    return pl.pallas_call(
        functools.partial(_moba_kernel, n_blocks=n_blocks),
        grid_spec=grid_spec,
        out_shape=jax.ShapeDtypeStruct((b, s, N_HEADS * HEAD_DIM), BF16),
        compiler_params=_cparams(3),
        name="moba",
    )(slopes, qkv, qkv, qkv, qkv, _moba_slope_rows(slopes))


def _moba_slope_rows(slopes):
    blk = MOBA_BLOCK
    wide = MOBA_QUERY_BLOCKS_PER_STEP * blk
    parts = _bf16_parts(slopes * LOG2E, MOBA_BIAS_PARTS)
    sp = jnp.stack([p.astype(F32) for p in parts], axis=1)
    rows = jnp.concatenate([sp * blk, sp], axis=1).reshape(N_SLABS, HEADS_PER_SLAB, 2 * MOBA_BIAS_PARTS)
    wb = jnp.zeros((N_SLABS, LANES, HEADS_PER_SLAB * wide), F32)
    for hh in range(HEADS_PER_SLAB):
        wb = wb.at[:, :2 * MOBA_BIAS_PARTS, hh * wide:(hh + 1) * wide].set(rows[:, hh, :, None])
    return wb.astype(BF16)


def _dilated_kernel(bias_ref, qt_ref, kc_ref, kp_ref, vtc_ref, vtp_ref, o_ref, lse_ref, lse_s, s_scr):
    blk = DIL_BLOCK
    lse_s[...] = jnp.zeros(lse_s.shape, F32)
    tail = lax.broadcasted_iota(jnp.int32, (MOBA_ACC_ROWS - HEAD_DIM, 2 * blk), 0)
    ones_row = jnp.where(tail == 0, 1.0, 0.0).astype(BF16)
    feat = lax.broadcasted_iota(jnp.int32, (LANES, blk), 0)
    for h2 in range(N_SLABS):
        cols = slice(h2 * LANES, (h2 + 1) * LANES)
        qt = qt_ref[0, 0, cols, :].astype(F32) * (HEAD_DIM ** -0.5 * LOG2E)
        kband = jnp.concatenate([kp_ref[0, 0, :, cols], kc_ref[0, 0, :, cols]], axis=0)
        qt_both = jnp.concatenate(
            [jnp.where((feat >= hh * HEAD_DIM) & (feat < (hh + 1) * HEAD_DIM), qt, 0.0)
             for hh in range(HEADS_PER_SLAB)], axis=1).astype(BF16)
        s_scr[h2] = jnp.dot(kband, qt_both, preferred_element_type=F32)
    for h2 in range(N_SLABS):
        cols = slice(h2 * LANES, (h2 + 1) * LANES)
        vt = jnp.concatenate([vtp_ref[0, 0, cols, :], vtc_ref[0, 0, cols, :]], axis=1)
        outs = []
        for hh in range(HEADS_PER_SLAB):
            h = h2 * HEADS_PER_SLAB + hh
            s = s_scr[h2, :, hh * blk:(hh + 1) * blk] + bias_ref[0, h]
            m = jnp.max(s, axis=0, keepdims=True)
            p = jnp.exp2(s - m).astype(BF16)
            vt_h = jnp.concatenate([vt[hh * HEAD_DIM:(hh + 1) * HEAD_DIM], ones_row], axis=0)
            acc = jnp.dot(vt_h, p, preferred_element_type=F32)
            l = acc[HEAD_DIM:HEAD_DIM + 1]
            outs.append(acc[:HEAD_DIM] / l)
            lse_s[pl.ds(h, 1), :] = (m + jnp.log2(l)) * (1.0 / LOG2E)
        o_ref[0, 0, :, cols] = jnp.concatenate(outs, axis=0).T.astype(o_ref.dtype)
    lse_ref[0, 0] = lse_s[...].T


def _dilated_bias(slopes, dil):
    blk = DIL_BLOCK
    key = jnp.arange(2 * blk)[:, None]
    steps = jnp.arange(blk)[None, :] - (key - blk)
    steps_max = blk
    valid = (steps >= 0) & (steps <= steps_max)
    bias = -(slopes * LOG2E)[:, None, None] * (steps * dil).astype(F32)[None]
    rest = jnp.where(valid[None], bias, NEG)
    first = jnp.where((key >= blk)[None], rest, NEG)
    return jnp.stack([first, rest]).astype(F32)


def _proj_residue_kernel(x_ref, wqt_ref, wk_ref, wvt_ref, qt_ref, k_ref, vt_ref):
    xb = x_ref[0].astype(BF16)
    nt = (((1,), (1,)), ((), ()))
    qt_ref[0, 0] = lax.dot_general(wqt_ref[...], xb, nt, preferred_element_type=F32).astype(qt_ref.dtype)
    k_ref[0, 0] = jnp.dot(xb, wk_ref[...], preferred_element_type=F32).astype(k_ref.dtype)
    vt_ref[0, 0] = lax.dot_general(wvt_ref[...], xb, nt, preferred_element_type=F32).astype(vt_ref.dtype)


def _proj_residue(x, w, bsz, dil, tm=512):
    t, k = x.shape
    d = w.shape[1] // 3
    l_sub = t // bsz // dil
    tm = min(tm, l_sub)
    view = x.reshape(bsz, l_sub, dil * k)
    const = lambda shape: pl.BlockSpec(shape, lambda b, r, m: (0, 0))
    tr = pl.BlockSpec((1, 1, d, tm), lambda b, r, m: (b, r, 0, m))
    return pl.pallas_call(
        _proj_residue_kernel,
        grid=(bsz, dil, l_sub // tm),
        in_specs=[pl.BlockSpec((1, tm, k), lambda b, r, m: (b, m, r)), const((d, k)), const((k, d)), const((d, k))],
        out_specs=[tr, pl.BlockSpec((1, 1, tm, d), lambda b, r, m: (b, r, m, 0)), tr],
        out_shape=[
            jax.ShapeDtypeStruct((bsz, dil, d, l_sub), BF16),
            jax.ShapeDtypeStruct((bsz, dil, l_sub, d), BF16),
            jax.ShapeDtypeStruct((bsz, dil, d, l_sub), BF16),
        ],
        compiler_params=_cparams(3),
        name=f"proj_residue_d{dil}",
    )(view, w[:, :d].T, w[:, d:2 * d], w[:, 2 * d:].T)


def _dilated_group(qt, k, vt, slopes, g, window, dil):
    b, _, l_sub, d = k.shape
    blk = DIL_BLOCK
    assert window // dil == blk
    nb = l_sub // blk
    bias = _dilated_bias(slopes, dil)
    prev = lambda n: jnp.maximum(n - 1, 0)

    o, lse = pl.pallas_call(
        _dilated_kernel,
        grid=(b, dil, nb),
        in_specs=[
            pl.BlockSpec((1, N_HEADS, 2 * blk, blk), lambda bi, r, n: (jnp.minimum(n, 1), 0, 0, 0)),
            pl.BlockSpec((1, 1, d, blk), lambda bi, r, n: (bi, r, 0, n)),
            pl.BlockSpec((1, 1, blk, d), lambda bi, r, n: (bi, r, n, 0)),
            pl.BlockSpec((1, 1, blk, d), lambda bi, r, n: (bi, r, prev(n), 0)),
            pl.BlockSpec((1, 1, d, blk), lambda bi, r, n: (bi, r, 0, n)),
            pl.BlockSpec((1, 1, d, blk), lambda bi, r, n: (bi, r, 0, prev(n))),
        ],
        out_specs=[
            pl.BlockSpec((1, 1, blk, d), lambda bi, r, n: (bi, r, n, 0)),
            pl.BlockSpec((1, 1, blk, LANES), lambda bi, r, n: (bi, r, n, 0)),
        ],
        out_shape=[
            jax.ShapeDtypeStruct((b, dil, l_sub, d), BF16),
            jax.ShapeDtypeStruct((b, dil, l_sub, LANES), F32),
        ],
        scratch_shapes=[pltpu.VMEM((LANES, blk), F32), pltpu.VMEM((N_SLABS, 2 * blk, 2 * blk), F32)],
        compiler_params=_cparams(3),
        name=f"dilated_g{g}",
    )(bias, qt, k, k, vt, vt)
    return o, lse


def _merge_kernel(e_ref, *refs):
    ng = len(DIL_GROUPS)
    o_refs, l_refs = refs[:ng], refs[ng:2 * ng]
    out_ref, lse_nat, o_nat = refs[2 * ng:]

    def to_natural(dst, src, dil):
        n = src.shape[2]
        for r in range(dil):
            rows = src[0, r].astype(F32)
            for c in range(dst.shape[0]):
                dst[c, pl.ds(r, n, stride=dil), :] = rows[:, c * LANES:(c + 1) * LANES]

    for g, (_, dil) in enumerate(DIL_GROUPS):
        to_natural(lse_nat.at[pl.ds(g, 1)], l_refs[g], dil)
    ls = [lse_nat[g] for g in range(ng)]
    mx = functools.reduce(jnp.maximum, ls)
    es = [jnp.exp(x - mx) for x in ls]
    den = functools.reduce(lambda a, b: a + b, es)
    acc = None
    for g, (_, dil) in enumerate(DIL_GROUPS):
        to_natural(o_nat, o_refs[g], dil)
        w = sum(jnp.dot(part, e_ref[...], preferred_element_type=F32) for part in _bf16_parts(es[g] / den, 2))
        term = w * jnp.concatenate([o_nat[c] for c in range(o_nat.shape[0])], axis=1)
        acc = term if acc is None else acc + term
    out_ref[0] = acc.astype(out_ref.dtype)


def _merge_groups(os, lses, tm=512):
    b, dil0, l0, d = os[0].shape
    s = dil0 * l0
    expand = (jnp.arange(LANES)[:, None] == (jnp.arange(d)[None, :] // HEAD_DIM)).astype(BF16)

    def grp(width, dil):
        return pl.BlockSpec((1, dil, tm // dil, width), lambda bi, i: (bi, 0, i, 0))

    dils = [dil for _, dil in DIL_GROUPS]
    return pl.pallas_call(
        _merge_kernel,
        grid=(b, s // tm),
        in_specs=[pl.BlockSpec((LANES, d), lambda bi, i: (0, 0))] + [grp(d, dil) for dil in dils]
        + [grp(LANES, dil) for dil in dils],
        out_specs=pl.BlockSpec((1, tm, d), lambda bi, i: (bi, i, 0)),
        out_shape=jax.ShapeDtypeStruct((b, s, d), BF16),
        scratch_shapes=[pltpu.VMEM((len(dils), tm, LANES), F32), pltpu.VMEM((d // LANES, tm, LANES), F32)],
        compiler_params=_cparams(2),
        name="merge_groups",
    )(expand, *os, *lses)


def _layer_norm(y, g, b):
    mu = jnp.mean(y, axis=-1, keepdims=True)
    yc = y - mu
    var = jnp.mean(yc * yc, axis=-1, keepdims=True)
    return yc * lax.rsqrt(var + LN_EPS) * g + b


U32 = jnp.uint32
TILE_WORDS = 2 * LANES
HIGH_HALF = np.uint32(0xFFFF0000)


def _pack_token_tiles(ref, y):
    n, d = y.shape
    npk = d // TILE_WORDS
    for c in range(npk):
        lo = y[:, (2 * c) * LANES:(2 * c + 1) * LANES].astype(BF16).astype(F32)
        hi = y[:, (2 * c + 1) * LANES:(2 * c + 2) * LANES].astype(BF16).astype(F32)
        word = (lax.bitcast_convert_type(lo, U32) >> 16) | (lax.bitcast_convert_type(hi, U32) & HIGH_HALF)
        ref[pl.ds(c, n, stride=npk), :] = word


def _unpack_token_tiles(ref, n, npk, lead=()):
    blocks = []
    for c in range(npk):
        word = ref[lead + (pl.ds(c, n, stride=npk), slice(None))]
        blocks.append(lax.bitcast_convert_type(word << 16, F32))
        blocks.append(lax.bitcast_convert_type(word & HIGH_HALF, F32))
    return jnp.concatenate(blocks, axis=1)


def _wo_ln_kernel(mix_ref, wo_ref, x_ref, g_ref, b_ref, o_ref, ot_ref):
    f = jnp.dot(mix_ref[...], wo_ref[...], preferred_element_type=F32)
    y = _layer_norm(DN_ALPHA * x_ref[...] + f, g_ref[...], b_ref[...])
    o_ref[...] = y
    _pack_token_tiles(ot_ref, y)


def _wo_ln(mix, wo, x, g, b, tm=512):
    t, d = x.shape
    npk = d // TILE_WORDS
    row = pl.BlockSpec((tm, d), lambda i: (i, 0))
    vec = pl.BlockSpec((1, d), lambda i: (0, 0))
    return pl.pallas_call(
        _wo_ln_kernel,
        grid=(t // tm,),
        in_specs=[row, pl.BlockSpec((d, d), lambda i: (0, 0)), row, vec, vec],
        out_specs=[row, pl.BlockSpec((tm * npk, LANES), lambda i: (i, 0))],
        out_shape=[jax.ShapeDtypeStruct((t, d), F32), jax.ShapeDtypeStruct((t * npk, LANES), U32)],
        compiler_params=_cparams(1),
        name="wo_ln",
    )(mix, wo, x, g.reshape(1, d), b.reshape(1, d))


def _router_kernel(wr_ref, rb_ref, x_ref, e_ref, w_ref, pos_ref, cnt_ref, run_ref, *, tm):
    step = pl.program_id(0)
    ne = N_EXPERTS
    gs = ne // N_GROUPS

    @pl.when(step == 0)
    def _():
        run_ref[...] = jnp.zeros(run_ref.shape, F32)

    x = x_ref[...]
    x_hi = x.astype(BF16)
    x_lo = (x - x_hi.astype(F32)).astype(BF16)
    nt = (((1,), (1,)), ((), ()))
    logits = (lax.dot_general(wr_ref[0], x_hi, nt, preferred_element_type=F32)
              + lax.dot_general(wr_ref[0], x_lo, nt, preferred_element_type=F32)
              + lax.dot_general(wr_ref[1], x_hi, nt, preferred_element_type=F32))
    scores = jax.nn.sigmoid(logits)
    choice = scores + rb_ref[...][:, :1]

    grp_rows = lax.broadcasted_iota(jnp.int32, (gs, tm), 0)
    blocks, gscore = [], []
    for gi in range(N_GROUPS):
        cb = choice[gi * gs:(gi + 1) * gs]
        m1 = jnp.max(cb, axis=0, keepdims=True)
        i1 = jnp.min(jnp.where(cb == m1, grp_rows, gs), axis=0, keepdims=True)
        m2 = jnp.max(jnp.where(grp_rows == i1, -jnp.inf, cb), axis=0, keepdims=True)
        blocks.append(cb)
        gscore.append(m1 + m2)
    masked = []
    for gi in range(N_GROUPS):
        beaten = jnp.zeros((1, tm), jnp.int32)
        for gj in range(N_GROUPS):
            if gj == gi:
                continue
            wins = (gscore[gj] > gscore[gi]) | ((gscore[gj] == gscore[gi]) & (gj < gi))
            beaten = beaten + wins.astype(jnp.int32)
        masked.append(jnp.where(beaten < TOPK_GROUPS, blocks[gi], NEG))
    cand = jnp.concatenate(masked, axis=0)

    rows = lax.broadcasted_iota(jnp.int32, (ne, tm), 0)
    member = jnp.zeros((ne, tm), F32)
    picks, raw_w = [], []
    for _ in range(TOP_K):
        cmax = jnp.max(cand, axis=0, keepdims=True)
        first = jnp.min(jnp.where(cand == cmax, rows, ne), axis=0, keepdims=True)
        hit = rows == first
        raw_w.append(jnp.sum(jnp.where(hit, scores, 0.0), axis=0, keepdims=True))
        member = jnp.where(hit, 1.0, member)
        cand = jnp.where(hit, -jnp.inf, cand)
        picks.append(first)
    wsum = raw_w[0]
    for r in raw_w[1:]:
        wsum = wsum + r

    tri = (lax.broadcasted_iota(jnp.int32, (tm, tm), 0) < lax.broadcasted_iota(jnp.int32, (tm, tm), 1))
    member_b = member.astype(BF16)
    before = jnp.dot(member_b, tri.astype(BF16), preferred_element_type=F32)
    run = run_ref[...]
    rank = before + jnp.concatenate([run] * (tm // LANES), axis=1)
    for k in range(TOP_K):
        hit = rows == picks[k]
        e_ref[pl.ds(k, 1), :] = picks[k]
        w_ref[pl.ds(k, 1), :] = raw_w[k] / wsum * ROUTED_SCALE
        pos_ref[pl.ds(k, 1), :] = jnp.sum(jnp.where(hit, rank, 0.0), axis=0, keepdims=True).astype(jnp.int32)
    run = run + jnp.dot(member_b, jnp.ones((tm, LANES), BF16), preferred_element_type=F32)
    run_ref[...] = run
    cnt_ref[...] = run


def _router(x, w_router, router_bias, tm=256):
    t, d = x.shape
    ne = N_EXPERTS
    wr_parts = jnp.stack(_bf16_parts(w_router.T.astype(F32), 2))
    rb = jnp.broadcast_to(router_bias.astype(F32)[:, None], (ne, LANES))
    tok = pl.BlockSpec((TOP_K, tm), lambda i: (0, i))
    return pl.pallas_call(
        functools.partial(_router_kernel, tm=tm),
        grid=(t // tm,),
        in_specs=[
            pl.BlockSpec((2, ne, d), lambda i: (0, 0, 0)),
            pl.BlockSpec((ne, LANES), lambda i: (0, 0)),
            pl.BlockSpec((tm, d), lambda i: (i, 0)),
        ],
        out_specs=[tok, tok, tok, pl.BlockSpec((ne, LANES), lambda i: (0, 0))],
        out_shape=[
            jax.ShapeDtypeStruct((TOP_K, t), jnp.int32),
            jax.ShapeDtypeStruct((TOP_K, t), F32),
            jax.ShapeDtypeStruct((TOP_K, t), jnp.int32),
            jax.ShapeDtypeStruct((ne, LANES), F32),
        ],
        scratch_shapes=[pltpu.VMEM((ne, LANES), F32)],
        compiler_params=_cparams(1),
        name="router",
    )(wr_parts, rb, x)


def _tile_rows(ref, i, npk):
    return ref.at[pl.ds(pl.multiple_of(i * npk, npk), npk)]


def _dispatch_kernel(dest_hbm, x_ref, xs_hbm, dest_smem, sem_idx, sem_rows, *, tm):
    step = pl.program_id(0)
    npk = x_ref.shape[0] // tm
    idx_copy = pltpu.make_async_copy(dest_hbm.at[step], dest_smem, sem_idx)
    idx_copy.start()
    idx_copy.wait()

    def row_copy(t, d):
        return pltpu.make_async_copy(_tile_rows(x_ref, t, npk), _tile_rows(xs_hbm, d, npk), sem_rows)

    def issue(t, c):
        for k in range(TOP_K):
            row_copy(t, dest_smem[k * tm + t]).start(priority=k % 2)
        return c

    lax.fori_loop(0, tm, issue, 0)

    def drain(t, c):
        for _ in range(TOP_K):
            row_copy(0, 0).wait()
        return c

    lax.fori_loop(0, tm, drain, 0)


def _dispatch(x_tiles, dest_tiles, n_rows, npk, tm):
    t = x_tiles.shape[0] // npk
    return pl.pallas_call(
        functools.partial(_dispatch_kernel, tm=tm),
        grid=(t // tm,),
        in_specs=[pl.BlockSpec(memory_space=pl.ANY), pl.BlockSpec((tm * npk, LANES), lambda i: (i, 0))],
        out_specs=pl.BlockSpec(memory_space=pl.ANY),
        out_shape=jax.ShapeDtypeStruct((n_rows * npk, LANES), x_tiles.dtype),
        scratch_shapes=[
            pltpu.SMEM((TOP_K * tm,), jnp.int32),
            pltpu.SemaphoreType.DMA,
            pltpu.SemaphoreType.DMA,
        ],
        compiler_params=_cparams(1),
        name="dispatch",
    )(dest_tiles, x_tiles)


def _experts_kernel(blk_e_ref, blk_row_ref, n_real_ref, xs_ref, wg_ref, wu_ref, wd_ref, ys_ref, *, rb):
    @pl.when(pl.program_id(0) < n_real_ref[0])
    def _():
        npk = xs_ref.shape[0] // rb
        xb = _unpack_token_tiles(xs_ref, rb, npk).astype(BF16)
        gate = jnp.dot(xb, wg_ref[0, 0].astype(BF16), preferred_element_type=F32)
        up = jnp.dot(xb, wu_ref[0, 0].astype(BF16), preferred_element_type=F32)
        h = (gate * jax.nn.sigmoid(gate) * up).astype(BF16)
        _pack_token_tiles(ys_ref, jnp.dot(h, wd_ref[0, 0].astype(BF16), preferred_element_type=F32))


def _experts(xs, we_gate, we_up, we_down, layer, rb, blk_e, blk_row, n_real):
    d, f = we_gate.shape[-2:]
    nc = d // TILE_WORDS
    p = xs.shape[0] // nc
    rows = pl.BlockSpec((rb * nc, LANES), lambda i, be, br, nr: (br[i], 0))
    grid_spec = pltpu.PrefetchScalarGridSpec(
        num_scalar_prefetch=3,
        grid=(p // rb,),
        in_specs=[
            rows,
            pl.BlockSpec((1, 1, d, f), lambda i, be, br, nr: (layer, be[i], 0, 0)),
            pl.BlockSpec((1, 1, d, f), lambda i, be, br, nr: (layer, be[i], 0, 0)),
            pl.BlockSpec((1, 1, f, d), lambda i, be, br, nr: (layer, be[i], 0, 0)),
        ],
        out_specs=rows,
    )
    return pl.pallas_call(
        functools.partial(_experts_kernel, rb=rb),
        grid_spec=grid_spec,
        out_shape=jax.ShapeDtypeStruct((p * nc, LANES), U32),
        compiler_params=_cparams(1),
        name="experts",
    )(blk_e, blk_row, n_real, xs, we_gate, we_up, we_down)


def _combine_kernel(dest_hbm, ys_hbm, x_ref, w_ref, sg_ref, su_ref, sd_ref, g_ref, b_ref, o_ref,
                    dest_smem0, dest_smem1, rows_ref, sem_idx, sem_rows, *, tm):
    dest_smems = (dest_smem0, dest_smem1)
    step = pl.program_id(0)
    n_steps = pl.num_programs(0)
    npk = rows_ref.shape[2] // tm
    slot = step % 2

    def row_copy(sl, k, t, d):
        return pltpu.make_async_copy(
            _tile_rows(ys_hbm, d, npk), _tile_rows(rows_ref.at[sl, k], t, npk), sem_rows.at[sl])

    def idx_copy(tile, sl):
        return pltpu.make_async_copy(dest_hbm.at[tile], dest_smems[sl], sem_idx.at[sl])

    def issue_rows(sl):
        dest_smem = dest_smems[sl]

        def issue(t, c):
            for k in range(TOP_K):
                row_copy(sl, k, t, dest_smem[k * tm + t]).start(priority=k % 2)
            return c

        lax.fori_loop(0, tm, issue, 0)

    @pl.when(step == 0)
    def _():
        idx_copy(0, 0).start()
        idx_copy(0, 0).wait()
        issue_rows(0)

        @pl.when(n_steps > 1)
        def _():
            idx_copy(1, 1).start()

    for sl in range(2):
        @pl.when((step + 1 < n_steps) & (slot == 1 - sl))
        def _(sl=sl):
            idx_copy(step + 1, sl).wait()

            @pl.when(step + 2 < n_steps)
            def _():
                idx_copy(step + 2, 1 - sl).start()

            issue_rows(sl)

    x = x_ref[...]
    xb = x.astype(BF16)
    gate = jnp.dot(xb, sg_ref[...], preferred_element_type=F32)
    up = jnp.dot(xb, su_ref[...], preferred_element_type=F32)
    h = (gate * jax.nn.sigmoid(gate) * up).astype(BF16)
    f = jnp.dot(h, sd_ref[...], preferred_element_type=F32)

    def drain(t, c):
        for _ in range(TOP_K):
            row_copy(slot, 0, 0, 0).wait()
        return c

    lax.fori_loop(0, tm, drain, 0)

    w = w_ref[...]
    for k in range(TOP_K):
        f = f + w[:, k:k + 1] * _unpack_token_tiles(rows_ref, tm, npk, lead=(slot, k))
    o_ref[...] = _layer_norm(DN_ALPHA * x + f, g_ref[...], b_ref[...])


def _combine(x, ys, dest_tiles, w_tok, ws_gate, ws_up, ws_down, g, b, tm=128):
    t, d = x.shape
    nc = d // TILE_WORDS
    f = ws_gate.shape[-1]
    row = pl.BlockSpec((tm, d), lambda i: (i, 0))
    vec = pl.BlockSpec((1, d), lambda i: (0, 0))
    return pl.pallas_call(
        functools.partial(_combine_kernel, tm=tm),
        grid=(t // tm,),
        in_specs=[
            pl.BlockSpec(memory_space=pl.ANY),
            pl.BlockSpec(memory_space=pl.ANY),
            row,
            pl.BlockSpec((tm, TOP_K), lambda i: (i, 0)),
            pl.BlockSpec((d, f), lambda i: (0, 0)),
            pl.BlockSpec((d, f), lambda i: (0, 0)),
            pl.BlockSpec((f, d), lambda i: (0, 0)),
            vec,
            vec,
        ],
        out_specs=row,
        out_shape=jax.ShapeDtypeStruct((t, d), F32),
        scratch_shapes=[
            pltpu.SMEM((TOP_K * tm,), jnp.int32),
            pltpu.SMEM((TOP_K * tm,), jnp.int32),
            pltpu.VMEM((2, TOP_K, tm * nc, LANES), U32),
            pltpu.SemaphoreType.DMA((2,)),
            pltpu.SemaphoreType.DMA((2,)),
        ],
        compiler_params=_cparams(1),
        name="combine",
    )(dest_tiles, ys, x, w_tok, ws_gate, ws_up, ws_down, g.reshape(1, d), b.reshape(1, d))


def _tile_major(a, tm):
    k, t = a.shape
    return a.reshape(k, t // tm, tm).transpose(1, 0, 2).reshape(t // tm, k * tm)


def _dest_kernel(start_ref, e_ref, pos_ref, dest_ref, *, tm):
    ne = N_EXPERTS
    start = jnp.concatenate([start_ref[...]] * (tm // LANES), axis=1)
    rows = lax.broadcasted_iota(jnp.int32, (ne, tm), 0)
    for k in range(TOP_K):
        hit = rows == e_ref[pl.ds(k, 1), :]
        first = jnp.sum(jnp.where(hit, start, 0), axis=0, keepdims=True)
        dest_ref[pl.ds(k, 1), :] = first + pos_ref[pl.ds(k, 1), :]


def _dest(starts, top_e, pos, tm=512):
    t = top_e.shape[1]
    tok = pl.BlockSpec((TOP_K, tm), lambda i: (0, i))
    return pl.pallas_call(
        functools.partial(_dest_kernel, tm=tm),
        grid=(t // tm,),
        in_specs=[pl.BlockSpec((N_EXPERTS, LANES), lambda i: (0, 0)), tok, tok],
        out_specs=tok,
        out_shape=jax.ShapeDtypeStruct((TOP_K, t), jnp.int32),
        compiler_params=_cparams(1),
        name="dest",
    )(jnp.broadcast_to(starts[:, None], (N_EXPERTS, LANES)), top_e, pos)


def _expert_row_block(t):
    mean_rows = t * TOP_K // N_EXPERTS
    return max(LANES, -(-(mean_rows + mean_rows // 4) // LANES) * LANES)


def _moe_ln(x, x_tiles, layer, w_router, router_bias, we_gate, we_up, we_down, ws_gate, ws_up, ws_down,
            g, b):
    t, d = x.shape
    rb = _expert_row_block(t)
    top_e, w, pos, counts = _router(x, w_router, router_bias)

    cnt = counts[:, 0].astype(jnp.int32)
    padded = (cnt + rb - 1) // rb * rb
    pend = jnp.cumsum(padded)
    n_blocks = t * TOP_K // rb + N_EXPERTS
    n_real = (pend[-1] // rb).astype(jnp.int32)
    blk_ids = jnp.minimum(jnp.arange(n_blocks, dtype=jnp.int32), jnp.maximum(n_real - 1, 0))
    blk_e = jnp.sum((pend[None, :] <= (blk_ids * rb)[:, None]).astype(jnp.int32), axis=1)
    blk_e = jnp.minimum(blk_e, N_EXPERTS - 1)

    dest = _dest(pend - padded, top_e, pos)
    xs = _dispatch(x_tiles, _tile_major(dest, DISPATCH_TOKENS), n_blocks * rb, d // TILE_WORDS,
                   tm=DISPATCH_TOKENS)
    ys = _experts(xs, we_gate, we_up, we_down, layer, rb, blk_e, blk_ids, n_real.reshape(1))
    return _combine(
        x, ys, _tile_major(dest, COMBINE_TOKENS), w.T, ws_gate.astype(BF16),
        ws_up.astype(BF16), ws_down.astype(BF16), g, b, tm=COMBINE_TOKENS,
    )


def kernel(x, w_qkv_a, w_qkv_b, w_o, ln_mix_g, ln_mix_b, w_router, router_bias, we_gate, we_up,
           we_down, ws_gate, ws_up, ws_down, ln_ffn_g, ln_ffn_b):
    bsz, s, d = x.shape
    t = bsz * s
    slopes = _alibi_slopes()
    xt = x.reshape(t, d)
    for i in range(DEPTH):
        if i % N_MIXERS == 0:
            qkv = _proj(xt, w_qkv_a[i // N_MIXERS].astype(BF16))
            mix = _moba_attention(qkv.reshape(bsz, s, -1), slopes).reshape(t, d)
        else:
            w_b = w_qkv_b[i // N_MIXERS].astype(BF16)
            outs, lses = [], []
            for g, (window, dil) in enumerate(DIL_GROUPS):
                qt, kk, vt = _proj_residue(xt, w_b[:, g * 3 * d:(g + 1) * 3 * d], bsz, dil)
                o, lse = _dilated_group(qt, kk, vt, slopes, g, window, dil)
                outs.append(o)
                lses.append(lse)
            mix = _merge_groups(outs, lses).reshape(t, d)
        x1, x1_tiles = _wo_ln(mix, w_o[i].astype(BF16), xt, ln_mix_g[i], ln_mix_b[i])
        xt = _moe_ln(x1, x1_tiles, i, w_router[i], router_bias[i], we_gate, we_up, we_down,
                     ws_gate[i], ws_up[i], ws_down[i], ln_ffn_g[i], ln_ffn_b[i])
    return xt.reshape(bsz, s, d)
```

```python
import functools
import math

import jax
import jax.numpy as jnp
import numpy as np
from jax import lax
from jax.experimental import pallas as pl
from jax.experimental.pallas import tpu as pltpu

N_HEADS = 16
HEAD_DIM = 64
DEPTH = 2
N_MIXERS = 2
MOBA_BLOCK = 256
MOBA_TOPK = 3
MOBA_KEY_BLOCKS_PER_STEP = 4
MOBA_QUERY_BLOCKS_PER_STEP = MOBA_KEY_BLOCKS_PER_STEP
MOBA_BIAS_PARTS = 3
DIL_GROUPS = ((128, 1), (512, 4), (2048, 16))
DIL_BLOCK = 128
N_EXPERTS = 256
TOP_K = 8
N_GROUPS = 8
TOPK_GROUPS = 4
ROUTED_SCALE = 2.5
DN_ALPHA = (2 * DEPTH) ** 0.25
LN_EPS = 1e-5
NEG = -1e30
LOG2E = math.log2(math.e)

LANES = 128
BF16_SUBLANES = 16
HEADS_PER_SLAB = LANES // HEAD_DIM
N_SLABS = N_HEADS // HEADS_PER_SLAB
VMEM_LIMIT_BYTES = 48 * 1024 * 1024
MOBA_ACC_ROWS = HEAD_DIM + BF16_SUBLANES

DISPATCH_TOKENS = 512
COMBINE_TOKENS = 512

BF16 = jnp.bfloat16
F32 = jnp.float32
U32 = jnp.uint32


def _cparams(n_axes):
    return pltpu.CompilerParams(
        dimension_semantics=("arbitrary",) * n_axes, vmem_limit_bytes=VMEM_LIMIT_BYTES
    )


def _alibi_slopes():
    return 2.0 ** (-8.0 * jnp.arange(1, N_HEADS + 1, dtype=F32) / N_HEADS)


def _bf16_parts(x, n):
    parts, rest = [], x
    for _ in range(n):
        top = lax.bitcast_convert_type(lax.bitcast_convert_type(rest, jnp.uint32) & np.uint32(0xFFFF0000), F32)
        parts.append(top.astype(BF16))
        rest = rest - top
    return parts


def _proj_kernel(x_ref, w_ref, o_ref):
    o_ref[...] = jnp.dot(
        x_ref[...].astype(BF16), w_ref[...], preferred_element_type=F32
    ).astype(o_ref.dtype)


def _proj(x, w, tm=1024, tn=1024):
    t, k = x.shape
    n = w.shape[1]
    return pl.pallas_call(
        _proj_kernel,
        grid=(t // tm, n // tn),
        in_specs=[
            pl.BlockSpec((tm, k), lambda i, j: (i, 0)),
            pl.BlockSpec((k, tn), lambda i, j: (0, j)),
        ],
        out_specs=pl.BlockSpec((tm, tn), lambda i, j: (i, j)),
        out_shape=jax.ShapeDtypeStruct((t, n), BF16),
        compiler_params=_cparams(2),
        name="proj",
    )(x, w)


def _moba_kernel(slopes_ref, q_ref, qn_ref, k_ref, v_ref, wb_ref, o_ref, ka_ref, vt_ref, km_ref, sel_ref,
                 s_ref, *, n_blocks):
    blk = MOBA_BLOCK
    unroll = MOBA_KEY_BLOCKS_PER_STEP
    span = unroll * blk
    acc_rows = MOBA_ACC_ROWS
    h2 = pl.program_id(1)
    i = pl.program_id(2)
    heads = range(HEADS_PER_SLAB)

    @pl.when(i == 0)
    def _prepare_keys():
        lane = lax.broadcasted_iota(jnp.int32, (blk, LANES), 1)
        row_f = lax.broadcasted_iota(jnp.int32, (blk, LANES), 0).astype(F32)
        tail = lax.broadcasted_iota(jnp.int32, (acc_rows - HEAD_DIM, blk), 0)
        ones_row = jnp.where(tail == 0, 1.0, 0.0)

        def body(n, c):
            r0 = pl.multiple_of(n * blk, blk)
            vt = v_ref[0, pl.ds(r0, blk), :].astype(F32).T
            for hh in heads:
                vt_ref[n, hh] = jnp.concatenate(
                    [vt[hh * HEAD_DIM:(hh + 1) * HEAD_DIM], ones_row], axis=0).astype(BF16)
            kb = k_ref[0, pl.ds(r0, blk), :]
            km_ref[pl.ds(n, 1), :] = jnp.mean(kb.astype(F32), axis=0, keepdims=True)
            sub = jnp.asarray(n % unroll, F32)
            aux = jnp.where(lane < MOBA_BIAS_PARTS, sub, jnp.where(lane < 2 * MOBA_BIAS_PARTS, row_f, 0.0))
            ka_ref[pl.ds(r0, blk), :] = jnp.concatenate([kb, aux.astype(BF16)], axis=1)
            return c

        lax.fori_loop(0, n_blocks, body, 0)

    qb = MOBA_QUERY_BLOCKS_PER_STEP
    wide = qb * blk
    i_first = pl.program_id(2) * qb
    feat = lax.broadcasted_iota(jnp.int32, (LANES, wide), 0)
    kmean = km_ref[...]
    km_lane = lax.broadcasted_iota(jnp.int32, kmean.shape, 1)
    blk_row = lax.broadcasted_iota(jnp.int32, (n_blocks, wide), 0)
    own_blk = i_first + lax.broadcasted_iota(jnp.int32, (n_blocks, wide), 1) // blk

    def per_head_qt(ref):
        qt = ref[0].astype(F32).T
        return [jnp.where((feat >= hh * HEAD_DIM) & (feat < (hh + 1) * HEAD_DIM), qt, 0.0) for hh in heads]

    def score_operand(qt_hs):
        scaled = [(x * (HEAD_DIM ** -0.5 * LOG2E)).astype(BF16) for x in qt_hs]
        return jnp.concatenate([jnp.concatenate(scaled, axis=1), wb_ref[0]], axis=0)

    slopes = [slopes_ref[h2 * HEADS_PER_SLAB + hh] for hh in heads]
    qt_hs = per_head_qt(q_ref)
    for hh in heads:
        qt_h = qt_hs[hh]
        km_h = jnp.where((km_lane >= hh * HEAD_DIM) & (km_lane < (hh + 1) * HEAD_DIM), kmean, 0.0)
        qt_b = qt_h.astype(BF16)
        gate = sum(jnp.dot(part, qt_b, preferred_element_type=F32) for part in _bf16_parts(km_h, 3))
        g = jnp.where(blk_row < own_blk, gate, -jnp.inf)
        sel = jnp.zeros(g.shape, F32)
        for _ in range(MOBA_TOPK):
            gmax = jnp.max(g, axis=0, keepdims=True)
            first = jnp.min(jnp.where(g == gmax, blk_row, n_blocks), axis=0, keepdims=True)
            hit = blk_row == first
            sel = jnp.where(hit & (gmax > -jnp.inf), 1.0, sel)
            g = jnp.where(hit, -jnp.inf, g)
        sel_ref[hh] = sel

    w_all = score_operand(qt_hs)
    w_next = score_operand(per_head_qt(qn_ref))

    groups = [(hh, q) for hh in heads for q in range(qb)]
    group_cols = [slice((hh * qb + q) * blk, (hh * qb + q + 1) * blk) for hh, q in groups]

    def scores(span_idx, cols, w=w_all):
        r0 = pl.multiple_of(span_idx * span, span)
        return jnp.dot(ka_ref[pl.ds(r0, span), :], w[:, cols], preferred_element_type=F32)

    key_row = lax.broadcasted_iota(jnp.int32, (blk, blk), 0)
    qry_col = lax.broadcasted_iota(jnp.int32, (blk, blk), 1)

    def consume(gi, span_idx, m, acc, last):
        hh, q = groups[gi]
        j0 = span_idx * unroll
        c = slopes[hh] * LOG2E * jnp.asarray(blk * (j0 - (i_first + q)), F32)
        for u in range(unroll):
            if last and u > q:
                continue
            j = j0 + u
            s = s_ref[u * blk:(u + 1) * blk, group_cols[gi]]
            if last and u == q:
                s = jnp.where(key_row <= qry_col, s, NEG)
                m_new = jnp.maximum(m, jnp.max(s, axis=0, keepdims=True) + c)
                p = jnp.exp2(s - (m_new - c))
            else:
                ch = sel_ref[hh, pl.ds(j, 1), q * blk:(q + 1) * blk] > 0.5
                m_new = jnp.where(ch, jnp.maximum(m, jnp.max(s, axis=0, keepdims=True) + c), m)
                p = jnp.exp2(s - jnp.where(ch, m_new - c, -NEG))
            acc = jnp.exp2(m - m_new) * acc + jnp.dot(
                vt_ref[j, hh], p.astype(BF16), preferred_element_type=F32)
            m = m_new
        return m, acc

    n_full = i_first // unroll

    @pl.when(i == 0)
    def _first_scores():
        s_ref[...] = scores(0, slice(None))

    def body(it, state):
        out = []
        for gi in range(len(groups)):
            out.append(consume(gi, it, *state[gi], last=False))
            s_ref[:, group_cols[gi]] = scores(it + 1, group_cols[gi])
        return tuple(out)

    init = tuple((jnp.full((1, blk), NEG, F32), jnp.zeros((acc_rows, blk), F32)) for _ in groups)
    state = lax.fori_loop(0, n_full, body, init)
    final = []
    for gi in range(len(groups)):
        final.append(consume(gi, n_full, *state[gi], last=True))
        s_ref[:, group_cols[gi]] = scores(0, group_cols[gi], w_next)
    state = final

    outs = [acc[:HEAD_DIM] / acc[HEAD_DIM:HEAD_DIM + 1] for _, acc in state]
    per_block = [jnp.concatenate([outs[hh * qb + q] for hh in heads], axis=0) for q in range(qb)]
    o_ref[0] = jnp.concatenate(per_block, axis=1).T.astype(o_ref.dtype)


def _moba_attention(qkv, slopes):
    b, s, _ = qkv.shape
    blk = MOBA_BLOCK
    n_blocks = s // blk
    qb = MOBA_QUERY_BLOCKS_PER_STEP
    assert n_blocks % MOBA_KEY_BLOCKS_PER_STEP == 0 and MOBA_KEY_BLOCKS_PER_STEP == qb
    wide = HEADS_PER_SLAB * qb * blk
    grid_spec = pltpu.PrefetchScalarGridSpec(
        num_scalar_prefetch=1,
        grid=(b, N_SLABS, n_blocks // qb),
        in_specs=[
            pl.BlockSpec((1, qb * blk, LANES), lambda bi, h, i, sl: (bi, i, h)),
            pl.BlockSpec((1, qb * blk, LANES), lambda bi, h, i, sl: (bi, jnp.minimum(i + 1, n_blocks // qb - 1), h)),
            pl.BlockSpec((1, s, LANES), lambda bi, h, i, sl: (bi, 0, N_SLABS + h)),
            pl.BlockSpec((1, s, LANES), lambda bi, h, i, sl: (bi, 0, 2 * N_SLABS + h)),
            pl.BlockSpec((1, LANES, wide), lambda bi, h, i, sl: (h, 0, 0)),
        ],
        out_specs=pl.BlockSpec((1, qb * blk, LANES), lambda bi, h, i, sl: (bi, i, h)),
        scratch_shapes=[
            pltpu.VMEM((s, 2 * LANES), BF16),
            pltpu.VMEM((n_blocks, HEADS_PER_SLAB, MOBA_ACC_ROWS, blk), BF16),
            pltpu.VMEM((n_blocks, LANES), F32),
            pltpu.VMEM((HEADS_PER_SLAB, n_blocks, qb * blk), F32),
            pltpu.VMEM((MOBA_KEY_BLOCKS_PER_STEP * blk, wide), F32),
        ],
    )
    return pl.pallas_call(
        functools.partial(_moba_kernel, n_blocks=n_blocks),
        grid_spec=grid_spec,
        out_shape=jax.ShapeDtypeStruct((b, s, N_HEADS * HEAD_DIM), BF16),
        compiler_params=_cparams(3),
        name="moba",
    )(slopes, qkv, qkv, qkv, qkv, _moba_slope_rows(slopes))


def _moba_slope_rows(slopes):
    blk = MOBA_BLOCK
    wide = MOBA_QUERY_BLOCKS_PER_STEP * blk
    parts = _bf16_parts(slopes * LOG2E, MOBA_BIAS_PARTS)
    sp = jnp.stack([p.astype(F32) for p in parts], axis=1)
    rows = jnp.concatenate([sp * blk, sp], axis=1).reshape(N_SLABS, HEADS_PER_SLAB, 2 * MOBA_BIAS_PARTS)
    wb = jnp.zeros((N_SLABS, LANES, HEADS_PER_SLAB * wide), F32)
    for hh in range(HEADS_PER_SLAB):
        wb = wb.at[:, :2 * MOBA_BIAS_PARTS, hh * wide:(hh + 1) * wide].set(rows[:, hh, :, None])
    return wb.astype(BF16)


def _dilated_kernel(bias_ref, qt_ref, kc_ref, kp_ref, vtc_ref, vtp_ref, o_ref, lse_ref, lse_s, s_scr):
    blk = DIL_BLOCK
    lse_s[...] = jnp.zeros(lse_s.shape, F32)
    tail = lax.broadcasted_iota(jnp.int32, (MOBA_ACC_ROWS - HEAD_DIM, 2 * blk), 0)
    ones_row = jnp.where(tail == 0, 1.0, 0.0).astype(BF16)
    feat = lax.broadcasted_iota(jnp.int32, (LANES, blk), 0)
    for h2 in range(N_SLABS):
        cols = slice(h2 * LANES, (h2 + 1) * LANES)
        qt = qt_ref[0, 0, cols, :].astype(F32) * (HEAD_DIM ** -0.5 * LOG2E)
        kband = jnp.concatenate([kp_ref[0, 0, :, cols], kc_ref[0, 0, :, cols]], axis=0)
        qt_both = jnp.concatenate(
            [jnp.where((feat >= hh * HEAD_DIM) & (feat < (hh + 1) * HEAD_DIM), qt, 0.0)
             for hh in range(HEADS_PER_SLAB)], axis=1).astype(BF16)
        s_scr[h2] = jnp.dot(kband, qt_both, preferred_element_type=F32)
    for h2 in range(N_SLABS):
        cols = slice(h2 * LANES, (h2 + 1) * LANES)
        vt = jnp.concatenate([vtp_ref[0, 0, cols, :], vtc_ref[0, 0, cols, :]], axis=1)
        outs = []
        for hh in range(HEADS_PER_SLAB):
            h = h2 * HEADS_PER_SLAB + hh
            s = s_scr[h2, :, hh * blk:(hh + 1) * blk] + bias_ref[0, h]
            m = jnp.max(s, axis=0, keepdims=True)
            p = jnp.exp2(s - m).astype(BF16)
            vt_h = jnp.concatenate([vt[hh * HEAD_DIM:(hh + 1) * HEAD_DIM], ones_row], axis=0)
            acc = jnp.dot(vt_h, p, preferred_element_type=F32)
            l = acc[HEAD_DIM:HEAD_DIM + 1]
            outs.append(acc[:HEAD_DIM] / l)
            lse_s[pl.ds(h, 1), :] = (m + jnp.log2(l)) * (1.0 / LOG2E)
        o_ref[0, 0, :, cols] = jnp.concatenate(outs, axis=0).T.astype(o_ref.dtype)
    lse_ref[0, 0] = lse_s[...].T


def _dilated_bias(slopes, dil):
    blk = DIL_BLOCK
    key = jnp.arange(2 * blk)[:, None]
    steps = jnp.arange(blk)[None, :] - (key - blk)
    steps_max = blk
    valid = (steps >= 0) & (steps <= steps_max)
    bias = -(slopes * LOG2E)[:, None, None] * (steps * dil).astype(F32)[None]
    rest = jnp.where(valid[None], bias, NEG)
    first = jnp.where((key >= blk)[None], rest, NEG)
    return jnp.stack([first, rest]).astype(F32)


def _proj_residue_kernel(x_ref, wqt_ref, wk_ref, wvt_ref, qt_ref, k_ref, vt_ref):
    xb = x_ref[0, 0].astype(BF16)
    nt = (((1,), (1,)), ((), ()))
    qt_ref[0, 0] = lax.dot_general(wqt_ref[...], xb, nt, preferred_element_type=F32).astype(qt_ref.dtype)
    k_ref[0, 0] = jnp.dot(xb, wk_ref[...], preferred_element_type=F32).astype(k_ref.dtype)
    vt_ref[0, 0] = lax.dot_general(wvt_ref[...], xb, nt, preferred_element_type=F32).astype(vt_ref.dtype)


def _residue_major(x, bsz, dil):
    t, k = x.shape
    return x.reshape(bsz, t // bsz // dil, dil, k).transpose(0, 2, 1, 3)


def _proj_residue(x_res, w, tm=512):
    bsz, dil, l_sub, k = x_res.shape
    d = w.shape[1] // 3
    tm = min(tm, l_sub)
    const = lambda shape: pl.BlockSpec(shape, lambda b, r, m: (0, 0))
    tr = pl.BlockSpec((1, 1, d, tm), lambda b, r, m: (b, r, 0, m))
    return pl.pallas_call(
        _proj_residue_kernel,
        grid=(bsz, dil, l_sub // tm),
        in_specs=[pl.BlockSpec((1, 1, tm, k), lambda b, r, m: (b, r, m, 0)), const((d, k)), const((k, d)),
                  const((d, k))],
        out_specs=[tr, pl.BlockSpec((1, 1, tm, d), lambda b, r, m: (b, r, m, 0)), tr],
        out_shape=[
            jax.ShapeDtypeStruct((bsz, dil, d, l_sub), BF16),
            jax.ShapeDtypeStruct((bsz, dil, l_sub, d), BF16),
            jax.ShapeDtypeStruct((bsz, dil, d, l_sub), BF16),
        ],
        compiler_params=_cparams(3),
        name=f"proj_residue_d{dil}",
    )(x_res, w[:, :d].T, w[:, d:2 * d], w[:, 2 * d:].T)


def _dilated_group(qt, k, vt, slopes, g, window, dil):
    b, _, l_sub, d = k.shape
    blk = DIL_BLOCK
    assert window // dil == blk
    nb = l_sub // blk
    bias = _dilated_bias(slopes, dil)
    prev = lambda n: jnp.maximum(n - 1, 0)

    o, lse = pl.pallas_call(
        _dilated_kernel,
        grid=(b, dil, nb),
        in_specs=[
            pl.BlockSpec((1, N_HEADS, 2 * blk, blk), lambda bi, r, n: (jnp.minimum(n, 1), 0, 0, 0)),
            pl.BlockSpec((1, 1, d, blk), lambda bi, r, n: (bi, r, 0, n)),
            pl.BlockSpec((1, 1, blk, d), lambda bi, r, n: (bi, r, n, 0)),
            pl.BlockSpec((1, 1, blk, d), lambda bi, r, n: (bi, r, prev(n), 0)),
            pl.BlockSpec((1, 1, d, blk), lambda bi, r, n: (bi, r, 0, n)),
            pl.BlockSpec((1, 1, d, blk), lambda bi, r, n: (bi, r, 0, prev(n))),
        ],
        out_specs=[
            pl.BlockSpec((1, 1, blk, d), lambda bi, r, n: (bi, r, n, 0)),
            pl.BlockSpec((1, 1, blk, LANES), lambda bi, r, n: (bi, r, n, 0)),
        ],
        out_shape=[
            jax.ShapeDtypeStruct((b, dil, l_sub, d), BF16),
            jax.ShapeDtypeStruct((b, dil, l_sub, LANES), F32),
        ],
        scratch_shapes=[pltpu.VMEM((LANES, blk), F32), pltpu.VMEM((N_SLABS, 2 * blk, 2 * blk), F32)],
        compiler_params=_cparams(3),
        name=f"dilated_g{g}",
    )(bias, qt, k, k, vt, vt)
    return o, lse


def _merge_kernel(e_ref, *refs):
    ng = len(DIL_GROUPS)
    o_refs, l_refs = refs[:ng], refs[ng:2 * ng]
    out_ref, lse_nat, o_nat = refs[2 * ng:]

    def to_natural(dst, src, dil):
        n = src.shape[2]
        for r in range(dil):
            rows = src[0, r].astype(F32)
            for c in range(dst.shape[0]):
                dst[c, pl.ds(r, n, stride=dil), :] = rows[:, c * LANES:(c + 1) * LANES]

    for g, (_, dil) in enumerate(DIL_GROUPS):
        to_natural(lse_nat.at[pl.ds(g, 1)], l_refs[g], dil)
    ls = [lse_nat[g] for g in range(ng)]
    mx = functools.reduce(jnp.maximum, ls)
    es = [jnp.exp(x - mx) for x in ls]
    den = functools.reduce(lambda a, b: a + b, es)
    acc = None
    for g, (_, dil) in enumerate(DIL_GROUPS):
        to_natural(o_nat, o_refs[g], dil)
        w = sum(jnp.dot(part, e_ref[...], preferred_element_type=F32) for part in _bf16_parts(es[g] / den, 2))
        term = w * jnp.concatenate([o_nat[c] for c in range(o_nat.shape[0])], axis=1)
        acc = term if acc is None else acc + term
    out_ref[0] = acc.astype(out_ref.dtype)


def _merge_groups(os, lses, tm=512):
    b, dil0, l0, d = os[0].shape
    s = dil0 * l0
    expand = (jnp.arange(LANES)[:, None] == (jnp.arange(d)[None, :] // HEAD_DIM)).astype(BF16)

    def grp(width, dil):
        return pl.BlockSpec((1, dil, tm // dil, width), lambda bi, i: (bi, 0, i, 0))

    dils = [dil for _, dil in DIL_GROUPS]
    return pl.pallas_call(
        _merge_kernel,
        grid=(b, s // tm),
        in_specs=[pl.BlockSpec((LANES, d), lambda bi, i: (0, 0))] + [grp(d, dil) for dil in dils]
        + [grp(LANES, dil) for dil in dils],
        out_specs=pl.BlockSpec((1, tm, d), lambda bi, i: (bi, i, 0)),
        out_shape=jax.ShapeDtypeStruct((b, s, d), BF16),
        scratch_shapes=[pltpu.VMEM((len(dils), tm, LANES), F32), pltpu.VMEM((d // LANES, tm, LANES), F32)],
        compiler_params=_cparams(2),
        name="merge_groups",
    )(expand, *os, *lses)


def _layer_norm(y, g, b):
    mu = jnp.mean(y, axis=-1, keepdims=True)
    yc = y - mu
    var = jnp.mean(yc * yc, axis=-1, keepdims=True)
    return yc * lax.rsqrt(var + LN_EPS) * g + b


TILE_WORDS = 2 * LANES
HIGH_HALF = np.uint32(0xFFFF0000)


def _pack_token_tiles(ref, y):
    n, d = y.shape
    npk = d // TILE_WORDS
    for c in range(npk):
        lo = y[:, (2 * c) * LANES:(2 * c + 1) * LANES].astype(BF16).astype(F32)
        hi = y[:, (2 * c + 1) * LANES:(2 * c + 2) * LANES].astype(BF16).astype(F32)
        word = (lax.bitcast_convert_type(lo, U32) >> 16) | (lax.bitcast_convert_type(hi, U32) & HIGH_HALF)
        ref[pl.ds(c, n, stride=npk), :] = word


def _unpack_token_tiles(ref, n, npk, lead=()):
    blocks = []
    for c in range(npk):
        word = ref[lead + (pl.ds(c, n, stride=npk), slice(None))]
        blocks.append(lax.bitcast_convert_type(word << 16, F32))
        blocks.append(lax.bitcast_convert_type(word & HIGH_HALF, F32))
    return jnp.concatenate(blocks, axis=1)


def _wo_ln_kernel(mix_ref, wo_ref, x_ref, g_ref, b_ref, o_ref, ot_ref):
    f = jnp.dot(mix_ref[...], wo_ref[...], preferred_element_type=F32)
    y = _layer_norm(DN_ALPHA * x_ref[...] + f, g_ref[...], b_ref[...])
    o_ref[...] = y
    _pack_token_tiles(ot_ref, y)


def _wo_ln(mix, wo, x, g, b, tm=512):
    t, d = x.shape
    npk = d // TILE_WORDS
    row = pl.BlockSpec((tm, d), lambda i: (i, 0))
    vec = pl.BlockSpec((1, d), lambda i: (0, 0))
    return pl.pallas_call(
        _wo_ln_kernel,
        grid=(t // tm,),
        in_specs=[row, pl.BlockSpec((d, d), lambda i: (0, 0)), row, vec, vec],
        out_specs=[row, pl.BlockSpec((tm * npk, LANES), lambda i: (i, 0))],
        out_shape=[jax.ShapeDtypeStruct((t, d), F32), jax.ShapeDtypeStruct((t * npk, LANES), U32)],
        compiler_params=_cparams(1),
        name="wo_ln",
    )(mix, wo, x, g.reshape(1, d), b.reshape(1, d))


def _router_kernel(wr_ref, rb_ref, x_ref, e_ref, w_ref, pos_ref, cnt_ref, run_ref, *, tm):
    step = pl.program_id(0)
    ne = N_EXPERTS
    gs = ne // N_GROUPS

    @pl.when(step == 0)
    def _():
        run_ref[...] = jnp.zeros(run_ref.shape, F32)

    x = x_ref[...]
    x_hi = x.astype(BF16)
    x_lo = (x - x_hi.astype(F32)).astype(BF16)
    nt = (((1,), (1,)), ((), ()))
    logits = (lax.dot_general(wr_ref[0], x_hi, nt, preferred_element_type=F32)
              + lax.dot_general(wr_ref[0], x_lo, nt, preferred_element_type=F32)
              + lax.dot_general(wr_ref[1], x_hi, nt, preferred_element_type=F32))
    scores = jax.nn.sigmoid(logits)
    choice = scores + rb_ref[...][:, :1]

    grp_rows = lax.broadcasted_iota(jnp.int32, (gs, tm), 0)
    blocks, gscore = [], []
    for gi in range(N_GROUPS):
        cb = choice[gi * gs:(gi + 1) * gs]
        m1 = jnp.max(cb, axis=0, keepdims=True)
        i1 = jnp.min(jnp.where(cb == m1, grp_rows, gs), axis=0, keepdims=True)
        m2 = jnp.max(jnp.where(grp_rows == i1, -jnp.inf, cb), axis=0, keepdims=True)
        blocks.append(cb)
        gscore.append(m1 + m2)
    masked = []
    for gi in range(N_GROUPS):
        beaten = jnp.zeros((1, tm), jnp.int32)
        for gj in range(N_GROUPS):
            if gj == gi:
                continue
            wins = (gscore[gj] > gscore[gi]) | ((gscore[gj] == gscore[gi]) & (gj < gi))
            beaten = beaten + wins.astype(jnp.int32)
        masked.append(jnp.where(beaten < TOPK_GROUPS, blocks[gi], NEG))
    cand = jnp.concatenate(masked, axis=0)

    rows = lax.broadcasted_iota(jnp.int32, (ne, tm), 0)
    member = jnp.zeros((ne, tm), F32)
    picks, raw_w = [], []
    for _ in range(TOP_K):
        cmax = jnp.max(cand, axis=0, keepdims=True)
        first = jnp.min(jnp.where(cand == cmax, rows, ne), axis=0, keepdims=True)
        hit = rows == first
        raw_w.append(jnp.sum(jnp.where(hit, scores, 0.0), axis=0, keepdims=True))
        member = jnp.where(hit, 1.0, member)
        cand = jnp.where(hit, -jnp.inf, cand)
        picks.append(first)
    wsum = raw_w[0]
    for r in raw_w[1:]:
        wsum = wsum + r

    tri = (lax.broadcasted_iota(jnp.int32, (tm, tm), 0) < lax.broadcasted_iota(jnp.int32, (tm, tm), 1))
    member_b = member.astype(BF16)
    before = jnp.dot(member_b, tri.astype(BF16), preferred_element_type=F32)
    run = run_ref[...]
    rank = before + jnp.concatenate([run] * (tm // LANES), axis=1)
    for k in range(TOP_K):
        hit = rows == picks[k]
        e_ref[pl.ds(k, 1), :] = picks[k]
        w_ref[pl.ds(k, 1), :] = raw_w[k] / wsum * ROUTED_SCALE
        pos_ref[pl.ds(k, 1), :] = jnp.sum(jnp.where(hit, rank, 0.0), axis=0, keepdims=True).astype(jnp.int32)
    run = run + jnp.dot(member_b, jnp.ones((tm, LANES), BF16), preferred_element_type=F32)
    run_ref[...] = run
    cnt_ref[...] = run


def _router(x, w_router, router_bias, tm=256):
    t, d = x.shape
    ne = N_EXPERTS
    wr_parts = jnp.stack(_bf16_parts(w_router.T.astype(F32), 2))
    rb = jnp.broadcast_to(router_bias.astype(F32)[:, None], (ne, LANES))
    tok = pl.BlockSpec((TOP_K, tm), lambda i: (0, i))
    return pl.pallas_call(
        functools.partial(_router_kernel, tm=tm),
        grid=(t // tm,),
        in_specs=[
            pl.BlockSpec((2, ne, d), lambda i: (0, 0, 0)),
            pl.BlockSpec((ne, LANES), lambda i: (0, 0)),
            pl.BlockSpec((tm, d), lambda i: (i, 0)),
        ],
        out_specs=[tok, tok, tok, pl.BlockSpec((ne, LANES), lambda i: (0, 0))],
        out_shape=[
            jax.ShapeDtypeStruct((TOP_K, t), jnp.int32),
            jax.ShapeDtypeStruct((TOP_K, t), F32),
            jax.ShapeDtypeStruct((TOP_K, t), jnp.int32),
            jax.ShapeDtypeStruct((ne, LANES), F32),
        ],
        scratch_shapes=[pltpu.VMEM((ne, LANES), F32)],
        compiler_params=_cparams(1),
        name="router",
    )(wr_parts, rb, x)


def _tile_rows(ref, i, npk):
    return ref.at[pl.ds(pl.multiple_of(i * npk, npk), npk)]


def _dispatch_kernel(dest_hbm, x_ref, xs_hbm, dest_smem, sem_idx, sem_rows, *, tm):
    step = pl.program_id(0)
    npk = x_ref.shape[0] // tm
    idx_copy = pltpu.make_async_copy(dest_hbm.at[step], dest_smem, sem_idx)
    idx_copy.start()
    idx_copy.wait()

    def row_copy(t, d):
        return pltpu.make_async_copy(_tile_rows(x_ref, t, npk), _tile_rows(xs_hbm, d, npk), sem_rows)

    def issue(t, c):
        for k in range(TOP_K):
            row_copy(t, dest_smem[k * tm + t]).start(priority=k % 2)
        return c

    lax.fori_loop(0, tm, issue, 0)

    def drain(t, c):
        for _ in range(TOP_K):
            row_copy(0, 0).wait()
        return c

    lax.fori_loop(0, tm, drain, 0)


def _dispatch(x_tiles, dest_tiles, n_rows, npk, tm):
    t = x_tiles.shape[0] // npk
    return pl.pallas_call(
        functools.partial(_dispatch_kernel, tm=tm),
        grid=(t // tm,),
        in_specs=[pl.BlockSpec(memory_space=pl.ANY), pl.BlockSpec((tm * npk, LANES), lambda i: (i, 0))],
        out_specs=pl.BlockSpec(memory_space=pl.ANY),
        out_shape=jax.ShapeDtypeStruct((n_rows * npk, LANES), x_tiles.dtype),
        scratch_shapes=[
            pltpu.SMEM((TOP_K * tm,), jnp.int32),
            pltpu.SemaphoreType.DMA,
            pltpu.SemaphoreType.DMA,
        ],
        compiler_params=_cparams(1),
        name="dispatch",
    )(dest_tiles, x_tiles)


def _experts_kernel(blk_e_ref, blk_row_ref, n_real_ref, xs_ref, wg_ref, wu_ref, wd_ref, ys_ref, *, rb):
    @pl.when(pl.program_id(0) < n_real_ref[0])
    def _():
        npk = xs_ref.shape[0] // rb
        xb = _unpack_token_tiles(xs_ref, rb, npk).astype(BF16)
        gate = jnp.dot(xb, wg_ref[0, 0].astype(BF16), preferred_element_type=F32)
        up = jnp.dot(xb, wu_ref[0, 0].astype(BF16), preferred_element_type=F32)
        h = (gate * jax.nn.sigmoid(gate) * up).astype(BF16)
        _pack_token_tiles(ys_ref, jnp.dot(h, wd_ref[0, 0].astype(BF16), preferred_element_type=F32))


def _experts(xs, we_gate, we_up, we_down, layer, rb, blk_e, blk_row, n_real):
    d, f = we_gate.shape[-2:]
    nc = d // TILE_WORDS
    p = xs.shape[0] // nc
    rows = pl.BlockSpec((rb * nc, LANES), lambda i, be, br, nr: (br[i], 0))
    grid_spec = pltpu.PrefetchScalarGridSpec(
        num_scalar_prefetch=3,
        grid=(p // rb,),
        in_specs=[
            rows,
            pl.BlockSpec((1, 1, d, f), lambda i, be, br, nr: (layer, be[i], 0, 0)),
            pl.BlockSpec((1, 1, d, f), lambda i, be, br, nr: (layer, be[i], 0, 0)),
            pl.BlockSpec((1, 1, f, d), lambda i, be, br, nr: (layer, be[i], 0, 0)),
        ],
        out_specs=rows,
    )
    return pl.pallas_call(
        functools.partial(_experts_kernel, rb=rb),
        grid_spec=grid_spec,
        out_shape=jax.ShapeDtypeStruct((p * nc, LANES), U32),
        compiler_params=_cparams(1),
        name="experts",
    )(blk_e, blk_row, n_real, xs, we_gate, we_up, we_down)


def _combine_kernel(dest_hbm, ys_hbm, x_ref, w_ref, sg_ref, su_ref, sd_ref, g_ref, b_ref, o_ref, *refs,
                    tm, residue_dils):
    res_refs = refs[:len(residue_dils)]
    dest_smem0, dest_smem1, rows_ref, sem_idx, sem_rows = refs[len(residue_dils):len(residue_dils) + 5]
    dest_smems = (dest_smem0, dest_smem1)
    step = pl.program_id(0)
    n_steps = pl.num_programs(0)
    npk = rows_ref.shape[2] // tm
    slot = step % 2

    def row_copy(sl, k, t, d):
        return pltpu.make_async_copy(
            _tile_rows(ys_hbm, d, npk), _tile_rows(rows_ref.at[sl, k], t, npk), sem_rows.at[sl])

    def idx_copy(tile, sl):
        return pltpu.make_async_copy(dest_hbm.at[tile], dest_smems[sl], sem_idx.at[sl])

    def issue_rows(sl):
        dest_smem = dest_smems[sl]

        def issue(t, c):
            for k in range(TOP_K):
                row_copy(sl, k, t, dest_smem[k * tm + t]).start(priority=k % 2)
            return c

        lax.fori_loop(0, tm, issue, 0)

    @pl.when(step == 0)
    def _():
        idx_copy(0, 0).start()
        idx_copy(0, 0).wait()
        issue_rows(0)

        @pl.when(n_steps > 1)
        def _():
            idx_copy(1, 1).start()

    for sl in range(2):
        @pl.when((step + 1 < n_steps) & (slot == 1 - sl))
        def _(sl=sl):
            idx_copy(step + 1, sl).wait()

            @pl.when(step + 2 < n_steps)
            def _():
                idx_copy(step + 2, 1 - sl).start()

            issue_rows(sl)

    x = x_ref[...]
    xb = x.astype(BF16)
    gate = jnp.dot(xb, sg_ref[...], preferred_element_type=F32)
    up = jnp.dot(xb, su_ref[...], preferred_element_type=F32)
    h = (gate * jax.nn.sigmoid(gate) * up).astype(BF16)
    f = jnp.dot(h, sd_ref[...], preferred_element_type=F32)

    def drain(t, c):
        for _ in range(TOP_K):
            row_copy(slot, 0, 0, 0).wait()
        return c

    lax.fori_loop(0, tm, drain, 0)

    w = w_ref[...]
    for k in range(TOP_K):
        f = f + w[:, k:k + 1] * _unpack_token_tiles(rows_ref, tm, npk, lead=(slot, k))
    y = _layer_norm(DN_ALPHA * x + f, g_ref[...], b_ref[...])
    o_ref[...] = y
    if residue_dils:
        slabs = refs[-1]
        n_slabs = slabs.shape[0]
        for c in range(n_slabs):
            slabs[c] = y[:, c * LANES:(c + 1) * LANES]
        for res_ref, dil in zip(res_refs, residue_dils):
            for r in range(dil):
                res_ref[0, r] = jnp.concatenate(
                    [slabs[c, pl.ds(r, tm // dil, stride=dil), :] for c in range(n_slabs)], axis=1)


def _combine(x, ys, dest_tiles, w_tok, ws_gate, ws_up, ws_down, g, b, tm, seq, residue_dils=()):
    t, d = x.shape
    nc = d // TILE_WORDS
    f = ws_gate.shape[-1]
    tiles_per_seq = seq // tm
    row = pl.BlockSpec((tm, d), lambda i: (i, 0))
    vec = pl.BlockSpec((1, d), lambda i: (0, 0))
    res_specs = [pl.BlockSpec((1, dil, tm // dil, d), lambda i: (i // tiles_per_seq, 0, i % tiles_per_seq, 0))
                 for dil in residue_dils]
    res_shapes = [jax.ShapeDtypeStruct((t // seq, dil, seq // dil, d), F32) for dil in residue_dils]
    slab_scratch = [pltpu.VMEM((d // LANES, tm, LANES), F32)] if residue_dils else []
    return pl.pallas_call(
        functools.partial(_combine_kernel, tm=tm, residue_dils=tuple(residue_dils)),
        grid=(t // tm,),
        in_specs=[
            pl.BlockSpec(memory_space=pl.ANY),
            pl.BlockSpec(memory_space=pl.ANY),
            row,
            pl.BlockSpec((tm, TOP_K), lambda i: (i, 0)),
            pl.BlockSpec((d, f), lambda i: (0, 0)),
            pl.BlockSpec((d, f), lambda i: (0, 0)),
            pl.BlockSpec((f, d), lambda i: (0, 0)),
            vec,
            vec,
        ],
        out_specs=[row] + res_specs,
        out_shape=[jax.ShapeDtypeStruct((t, d), F32)] + res_shapes,
        scratch_shapes=[
            pltpu.SMEM((TOP_K * tm,), jnp.int32),
            pltpu.SMEM((TOP_K * tm,), jnp.int32),
            pltpu.VMEM((2, TOP_K, tm * nc, LANES), U32),
            pltpu.SemaphoreType.DMA((2,)),
            pltpu.SemaphoreType.DMA((2,)),
        ] + slab_scratch,
        compiler_params=_cparams(1),
        name="combine",
    )(dest_tiles, ys, x, w_tok, ws_gate, ws_up, ws_down, g.reshape(1, d), b.reshape(1, d))


def _tile_major(a, tm):
    k, t = a.shape
    return a.reshape(k, t // tm, tm).transpose(1, 0, 2).reshape(t // tm, k * tm)


def _dest_kernel(start_ref, e_ref, pos_ref, dest_ref, *, tm):
    ne = N_EXPERTS
    start = jnp.concatenate([start_ref[...]] * (tm // LANES), axis=1)
    rows = lax.broadcasted_iota(jnp.int32, (ne, tm), 0)
    for k in range(TOP_K):
        hit = rows == e_ref[pl.ds(k, 1), :]
        first = jnp.sum(jnp.where(hit, start, 0), axis=0, keepdims=True)
        dest_ref[pl.ds(k, 1), :] = first + pos_ref[pl.ds(k, 1), :]


def _dest(starts, top_e, pos, tm=512):
    t = top_e.shape[1]
    tok = pl.BlockSpec((TOP_K, tm), lambda i: (0, i))
    return pl.pallas_call(
        functools.partial(_dest_kernel, tm=tm),
        grid=(t // tm,),
        in_specs=[pl.BlockSpec((N_EXPERTS, LANES), lambda i: (0, 0)), tok, tok],
        out_specs=tok,
        out_shape=jax.ShapeDtypeStruct((TOP_K, t), jnp.int32),
        compiler_params=_cparams(1),
        name="dest",
    )(jnp.broadcast_to(starts[:, None], (N_EXPERTS, LANES)), top_e, pos)


def _expert_row_block(t):
    mean_rows = t * TOP_K // N_EXPERTS
    return max(LANES, -(-(mean_rows + mean_rows // 4) // LANES) * LANES)


def _moe_ln(x, x_tiles, layer, seq, w_router, router_bias, we_gate, we_up, we_down, ws_gate, ws_up, ws_down,
            g, b, residue_dils=()):
    t, d = x.shape
    rb = _expert_row_block(t)
    top_e, w, pos, counts = _router(x, w_router, router_bias)

    cnt = counts[:, 0].astype(jnp.int32)
    padded = (cnt + rb - 1) // rb * rb
    pend = jnp.cumsum(padded)
    n_blocks = t * TOP_K // rb + N_EXPERTS
    n_real = (pend[-1] // rb).astype(jnp.int32)
    blk_ids = jnp.minimum(jnp.arange(n_blocks, dtype=jnp.int32), jnp.maximum(n_real - 1, 0))
    blk_e = jnp.sum((pend[None, :] <= (blk_ids * rb)[:, None]).astype(jnp.int32), axis=1)
    blk_e = jnp.minimum(blk_e, N_EXPERTS - 1)

    dest = _dest(pend - padded, top_e, pos)
    xs = _dispatch(x_tiles, _tile_major(dest, DISPATCH_TOKENS), n_blocks * rb, d // TILE_WORDS,
                   tm=DISPATCH_TOKENS)
    ys = _experts(xs, we_gate, we_up, we_down, layer, rb, blk_e, blk_ids, n_real.reshape(1))
    return _combine(
        x, ys, _tile_major(dest, COMBINE_TOKENS), w.T, ws_gate.astype(BF16),
        ws_up.astype(BF16), ws_down.astype(BF16), g, b, tm=COMBINE_TOKENS, seq=seq, residue_dils=residue_dils,
    )


def kernel(x, w_qkv_a, w_qkv_b, w_o, ln_mix_g, ln_mix_b, w_router, router_bias, we_gate, we_up,
           we_down, ws_gate, ws_up, ws_down, ln_ffn_g, ln_ffn_b):
    bsz, s, d = x.shape
    t = bsz * s
    slopes = _alibi_slopes()
    xt = x.reshape(t, d)
    is_dilated = lambda layer: layer % N_MIXERS == 1
    strided_dils = tuple(dil for _, dil in DIL_GROUPS if dil > 1)
    x_res = {}
    for i in range(DEPTH):
        if not is_dilated(i):
            qkv = _proj(xt, w_qkv_a[i // N_MIXERS].astype(BF16))
            mix = _moba_attention(qkv.reshape(bsz, s, -1), slopes).reshape(t, d)
        else:
            w_b = w_qkv_b[i // N_MIXERS].astype(BF16)
            outs, lses = [], []
            for g, (window, dil) in enumerate(DIL_GROUPS):
                xg = x_res[dil] if dil in x_res else _residue_major(xt, bsz, dil)
                qt, kk, vt = _proj_residue(xg, w_b[:, g * 3 * d:(g + 1) * 3 * d])
                o, lse = _dilated_group(qt, kk, vt, slopes, g, window, dil)
                outs.append(o)
                lses.append(lse)
            mix = _merge_groups(outs, lses).reshape(t, d)
        x1, x1_tiles = _wo_ln(mix, w_o[i].astype(BF16), xt, ln_mix_g[i], ln_mix_b[i])
        dils = strided_dils if i + 1 < DEPTH and is_dilated(i + 1) else ()
        xt, *extra = _moe_ln(x1, x1_tiles, i, s, w_router[i], router_bias[i], we_gate, we_up, we_down,
                             ws_gate[i], ws_up[i], ws_down[i], ln_ffn_g[i], ln_ffn_b[i], residue_dils=dils)
        x_res = dict(zip(dils, extra))
    return xt.reshape(bsz, s, d)
```

```python
import functools
import math

import jax
import jax.numpy as jnp
import numpy as np
from jax import lax
from jax.experimental import pallas as pl
from jax.experimental.pallas import tpu as pltpu

N_HEADS = 16
HEAD_DIM = 64
DEPTH = 2
N_MIXERS = 2
MOBA_BLOCK = 256
MOBA_TOPK = 3
MOBA_KEY_BLOCKS_PER_STEP = 4
MOBA_QUERY_BLOCKS_PER_STEP = MOBA_KEY_BLOCKS_PER_STEP
MOBA_BIAS_PARTS = 3
DIL_GROUPS = ((128, 1), (512, 4), (2048, 16))
DIL_BLOCK = 128
N_EXPERTS = 256
TOP_K = 8
N_GROUPS = 8
TOPK_GROUPS = 4
ROUTED_SCALE = 2.5
DN_ALPHA = (2 * DEPTH) ** 0.25
LN_EPS = 1e-5
NEG = -1e30
LOG2E = math.log2(math.e)

LANES = 128
BF16_SUBLANES = 16
HEADS_PER_SLAB = LANES // HEAD_DIM
N_SLABS = N_HEADS // HEADS_PER_SLAB
VMEM_LIMIT_BYTES = 48 * 1024 * 1024
MOBA_ACC_ROWS = HEAD_DIM + BF16_SUBLANES

DISPATCH_TOKENS = 1024
COMBINE_TOKENS = 512

BF16 = jnp.bfloat16
F32 = jnp.float32
U32 = jnp.uint32


def _cparams(n_axes):
    return pltpu.CompilerParams(
        dimension_semantics=("arbitrary",) * n_axes, vmem_limit_bytes=VMEM_LIMIT_BYTES
    )


def _alibi_slopes():
    return 2.0 ** (-8.0 * jnp.arange(1, N_HEADS + 1, dtype=F32) / N_HEADS)


def _bf16_parts(x, n):
    parts, rest = [], x
    for _ in range(n):
        top = lax.bitcast_convert_type(lax.bitcast_convert_type(rest, jnp.uint32) & np.uint32(0xFFFF0000), F32)
        parts.append(top.astype(BF16))
        rest = rest - top
    return parts


def _proj_kernel(x_ref, w_ref, o_ref):
    o_ref[...] = jnp.dot(
        x_ref[...].astype(BF16), w_ref[...], preferred_element_type=F32
    ).astype(o_ref.dtype)


def _proj(x, w, tm=1024, tn=1024):
    t, k = x.shape
    n = w.shape[1]
    return pl.pallas_call(
        _proj_kernel,
        grid=(t // tm, n // tn),
        in_specs=[
            pl.BlockSpec((tm, k), lambda i, j: (i, 0)),
            pl.BlockSpec((k, tn), lambda i, j: (0, j)),
        ],
        out_specs=pl.BlockSpec((tm, tn), lambda i, j: (i, j)),
        out_shape=jax.ShapeDtypeStruct((t, n), BF16),
        compiler_params=_cparams(2),
        name="proj",
    )(x, w)


def _moba_kernel(slopes_ref, q_ref, qn_ref, k_ref, v_ref, wb_ref, o_ref, ka_ref, vt_ref, km_ref, sel_ref,
                 s_ref, *, n_blocks):
    blk = MOBA_BLOCK
    unroll = MOBA_KEY_BLOCKS_PER_STEP
    span = unroll * blk
    acc_rows = MOBA_ACC_ROWS
    h2 = pl.program_id(1)
    i = pl.program_id(2)
    heads = range(HEADS_PER_SLAB)

    @pl.when(i == 0)
    def _prepare_keys():
        lane = lax.broadcasted_iota(jnp.int32, (blk, LANES), 1)
        row_f = lax.broadcasted_iota(jnp.int32, (blk, LANES), 0).astype(F32)
        tail = lax.broadcasted_iota(jnp.int32, (acc_rows - HEAD_DIM, blk), 0)
        ones_row = jnp.where(tail == 0, 1.0, 0.0)

        def body(n, c):
            r0 = pl.multiple_of(n * blk, blk)
            vt = v_ref[0, pl.ds(r0, blk), :].astype(F32).T
            for hh in heads:
                vt_ref[n, hh] = jnp.concatenate(
                    [vt[hh * HEAD_DIM:(hh + 1) * HEAD_DIM], ones_row], axis=0).astype(BF16)
            kb = k_ref[0, pl.ds(r0, blk), :]
            km_ref[pl.ds(n, 1), :] = jnp.mean(kb.astype(F32), axis=0, keepdims=True)
            sub = jnp.asarray(n % unroll, F32)
            aux = jnp.where(lane < MOBA_BIAS_PARTS, sub, jnp.where(lane < 2 * MOBA_BIAS_PARTS, row_f, 0.0))
            ka_ref[pl.ds(r0, blk), :] = jnp.concatenate([kb, aux.astype(BF16)], axis=1)
            return c

        lax.fori_loop(0, n_blocks, body, 0)

    qb = MOBA_QUERY_BLOCKS_PER_STEP
    wide = qb * blk
    i_first = pl.program_id(2) * qb
    feat = lax.broadcasted_iota(jnp.int32, (LANES, wide), 0)
    kmean = km_ref[...]
    km_lane = lax.broadcasted_iota(jnp.int32, kmean.shape, 1)
    blk_row = lax.broadcasted_iota(jnp.int32, (n_blocks, wide), 0)
    own_blk = i_first + lax.broadcasted_iota(jnp.int32, (n_blocks, wide), 1) // blk

    def per_head_qt(ref):
        qt = ref[0].astype(F32).T
        return [jnp.where((feat >= hh * HEAD_DIM) & (feat < (hh + 1) * HEAD_DIM), qt, 0.0) for hh in heads]

    def score_operand(qt_hs):
        scaled = [(x * (HEAD_DIM ** -0.5 * LOG2E)).astype(BF16) for x in qt_hs]
        return jnp.concatenate([jnp.concatenate(scaled, axis=1), wb_ref[0]], axis=0)

    slopes = [slopes_ref[h2 * HEADS_PER_SLAB + hh] for hh in heads]
    qt_hs = per_head_qt(q_ref)
    for hh in heads:
        qt_h = qt_hs[hh]
        km_h = jnp.where((km_lane >= hh * HEAD_DIM) & (km_lane < (hh + 1) * HEAD_DIM), kmean, 0.0)
        qt_b = qt_h.astype(BF16)
        gate = sum(jnp.dot(part, qt_b, preferred_element_type=F32) for part in _bf16_parts(km_h, 3))
        g = jnp.where(blk_row < own_blk, gate, -jnp.inf)
        sel = jnp.zeros(g.shape, F32)
        for _ in range(MOBA_TOPK):
            gmax = jnp.max(g, axis=0, keepdims=True)
            first = jnp.min(jnp.where(g == gmax, blk_row, n_blocks), axis=0, keepdims=True)
            hit = blk_row == first
            sel = jnp.where(hit & (gmax > -jnp.inf), 1.0, sel)
            g = jnp.where(hit, -jnp.inf, g)
        sel_ref[hh] = sel

    w_all = score_operand(qt_hs)
    w_next = score_operand(per_head_qt(qn_ref))

    groups = [(hh, q) for hh in heads for q in range(qb)]
    group_cols = [slice((hh * qb + q) * blk, (hh * qb + q + 1) * blk) for hh, q in groups]

    def scores(span_idx, cols, w=w_all):
        r0 = pl.multiple_of(span_idx * span, span)
        return jnp.dot(ka_ref[pl.ds(r0, span), :], w[:, cols], preferred_element_type=F32)

    key_row = lax.broadcasted_iota(jnp.int32, (blk, blk), 0)
    qry_col = lax.broadcasted_iota(jnp.int32, (blk, blk), 1)

    def consume(gi, span_idx, m, acc, last):
        hh, q = groups[gi]
        j0 = span_idx * unroll
        c = slopes[hh] * LOG2E * jnp.asarray(blk * (j0 - (i_first + q)), F32)
        for u in range(unroll):
            if last and u > q:
                continue
            j = j0 + u
            s = s_ref[u * blk:(u + 1) * blk, group_cols[gi]]
            if last and u == q:
                s = jnp.where(key_row <= qry_col, s, NEG)
                m_new = jnp.maximum(m, jnp.max(s, axis=0, keepdims=True) + c)
                p = jnp.exp2(s - (m_new - c))
            else:
                ch = sel_ref[hh, pl.ds(j, 1), q * blk:(q + 1) * blk] > 0.5
                m_new = jnp.where(ch, jnp.maximum(m, jnp.max(s, axis=0, keepdims=True) + c), m)
                p = jnp.exp2(s - jnp.where(ch, m_new - c, -NEG))
            acc = jnp.exp2(m - m_new) * acc + jnp.dot(
                vt_ref[j, hh], p.astype(BF16), preferred_element_type=F32)
            m = m_new
        return m, acc

    n_full = i_first // unroll

    @pl.when(i == 0)
    def _first_scores():
        s_ref[...] = scores(0, slice(None))

    def body(it, state):
        out = []
        for gi in range(len(groups)):
            out.append(consume(gi, it, *state[gi], last=False))
            s_ref[:, group_cols[gi]] = scores(it + 1, group_cols[gi])
        return tuple(out)

    init = tuple((jnp.full((1, blk), NEG, F32), jnp.zeros((acc_rows, blk), F32)) for _ in groups)
    state = lax.fori_loop(0, n_full, body, init)
    final = []
    for gi in range(len(groups)):
        final.append(consume(gi, n_full, *state[gi], last=True))
        s_ref[:, group_cols[gi]] = scores(0, group_cols[gi], w_next)
    state = final

    outs = [acc[:HEAD_DIM] / acc[HEAD_DIM:HEAD_DIM + 1] for _, acc in state]
    per_block = [jnp.concatenate([outs[hh * qb + q] for hh in heads], axis=0) for q in range(qb)]
    o_ref[0] = jnp.concatenate(per_block, axis=1).T.astype(o_ref.dtype)


def _moba_attention(qkv, slopes):
    b, s, _ = qkv.shape
    blk = MOBA_BLOCK
    n_blocks = s // blk
    qb = MOBA_QUERY_BLOCKS_PER_STEP
    assert n_blocks % MOBA_KEY_BLOCKS_PER_STEP == 0 and MOBA_KEY_BLOCKS_PER_STEP == qb
    wide = HEADS_PER_SLAB * qb * blk
    grid_spec = pltpu.PrefetchScalarGridSpec(
        num_scalar_prefetch=1,
        grid=(b, N_SLABS, n_blocks // qb),
        in_specs=[
            pl.BlockSpec((1, qb * blk, LANES), lambda bi, h, i, sl: (bi, i, h)),
            pl.BlockSpec((1, qb * blk, LANES), lambda bi, h, i, sl: (bi, jnp.minimum(i + 1, n_blocks // qb - 1), h)),
            pl.BlockSpec((1, s, LANES), lambda bi, h, i, sl: (bi, 0, N_SLABS + h)),
            pl.BlockSpec((1, s, LANES), lambda bi, h, i, sl: (bi, 0, 2 * N_SLABS + h)),
            pl.BlockSpec((1, LANES, wide), lambda bi, h, i, sl: (h, 0, 0)),
        ],
        out_specs=pl.BlockSpec((1, qb * blk, LANES), lambda bi, h, i, sl: (bi, i, h)),
        scratch_shapes=[
            pltpu.VMEM((s, 2 * LANES), BF16),
            pltpu.VMEM((n_blocks, HEADS_PER_SLAB, MOBA_ACC_ROWS, blk), BF16),
            pltpu.VMEM((n_blocks, LANES), F32),
            pltpu.VMEM((HEADS_PER_SLAB, n_blocks, qb * blk), F32),
            pltpu.VMEM((MOBA_KEY_BLOCKS_PER_STEP * blk, wide), F32),
        ],
    )
    return pl.pallas_call(
        functools.partial(_moba_kernel, n_blocks=n_blocks),
        grid_spec=grid_spec,
        out_shape=jax.ShapeDtypeStruct((b, s, N_HEADS * HEAD_DIM), BF16),
        compiler_params=_cparams(3),
        name="moba",
    )(slopes, qkv, qkv, qkv, qkv, _moba_slope_rows(slopes))


def _moba_slope_rows(slopes):
    blk = MOBA_BLOCK
    wide = MOBA_QUERY_BLOCKS_PER_STEP * blk
    parts = _bf16_parts(slopes * LOG2E, MOBA_BIAS_PARTS)
    sp = jnp.stack([p.astype(F32) for p in parts], axis=1)
    rows = jnp.concatenate([sp * blk, sp], axis=1).reshape(N_SLABS, HEADS_PER_SLAB, 2 * MOBA_BIAS_PARTS)
    wb = jnp.zeros((N_SLABS, LANES, HEADS_PER_SLAB * wide), F32)
    for hh in range(HEADS_PER_SLAB):
        wb = wb.at[:, :2 * MOBA_BIAS_PARTS, hh * wide:(hh + 1) * wide].set(rows[:, hh, :, None])
    return wb.astype(BF16)


def _dilated_kernel(bias_ref, qt_ref, kc_ref, kp_ref, vtc_ref, vtp_ref, o_ref, lse_ref, lse_s, s_scr):
    blk = DIL_BLOCK
    table = jnp.minimum(pl.program_id(2), 1)
    lse_s[...] = jnp.zeros(lse_s.shape, F32)
    tail = lax.broadcasted_iota(jnp.int32, (MOBA_ACC_ROWS - HEAD_DIM, 2 * blk), 0)
    ones_row = jnp.where(tail == 0, 1.0, 0.0).astype(BF16)
    feat = lax.broadcasted_iota(jnp.int32, (LANES, blk), 0)
    for h2 in range(N_SLABS):
        cols = slice(h2 * LANES, (h2 + 1) * LANES)
        qt = qt_ref[0, 0, cols, :].astype(F32) * (HEAD_DIM ** -0.5 * LOG2E)
        kband = jnp.concatenate([kp_ref[0, 0, :, cols], kc_ref[0, 0, :, cols]], axis=0)
        qt_both = jnp.concatenate(
            [jnp.where((feat >= hh * HEAD_DIM) & (feat < (hh + 1) * HEAD_DIM), qt, 0.0)
             for hh in range(HEADS_PER_SLAB)], axis=1).astype(BF16)
        s_scr[h2] = jnp.dot(kband, qt_both, preferred_element_type=F32)
    for h2 in range(N_SLABS):
        cols = slice(h2 * LANES, (h2 + 1) * LANES)
        vt = jnp.concatenate([vtp_ref[0, 0, cols, :], vtc_ref[0, 0, cols, :]], axis=1)
        outs = []
        for hh in range(HEADS_PER_SLAB):
            h = h2 * HEADS_PER_SLAB + hh
            s = s_scr[h2, :, hh * blk:(hh + 1) * blk] + bias_ref[table, h]
            m = jnp.max(s, axis=0, keepdims=True)
            p = jnp.exp2(s - m).astype(BF16)
            vt_h = jnp.concatenate([vt[hh * HEAD_DIM:(hh + 1) * HEAD_DIM], ones_row], axis=0)
            acc = jnp.dot(vt_h, p, preferred_element_type=F32)
            l = acc[HEAD_DIM:HEAD_DIM + 1]
            outs.append(acc[:HEAD_DIM] / l)
            lse_s[pl.ds(h, 1), :] = (m + jnp.log2(l)) * (1.0 / LOG2E)
        o_ref[0, 0, :, cols] = jnp.concatenate(outs, axis=0).T.astype(o_ref.dtype)
    lse_ref[0, 0] = lse_s[...].T


def _dilated_bias(slopes, dil):
    blk = DIL_BLOCK
    key = jnp.arange(2 * blk)[:, None]
    steps = jnp.arange(blk)[None, :] - (key - blk)
    steps_max = blk
    valid = (steps >= 0) & (steps <= steps_max)
    bias = -(slopes * LOG2E)[:, None, None] * (steps * dil).astype(F32)[None]
    rest = jnp.where(valid[None], bias, NEG)
    first = jnp.where((key >= blk)[None], rest, NEG)
    return jnp.stack([first, rest]).astype(F32)


def _proj_residue_kernel(x_ref, wqt_ref, wk_ref, wvt_ref, qt_ref, k_ref, vt_ref):
    xb = x_ref[0, 0].astype(BF16)
    nt = (((1,), (1,)), ((), ()))
    qt_ref[0, 0] = lax.dot_general(wqt_ref[...], xb, nt, preferred_element_type=F32).astype(qt_ref.dtype)
    k_ref[0, 0] = jnp.dot(xb, wk_ref[...], preferred_element_type=F32).astype(k_ref.dtype)
    vt_ref[0, 0] = lax.dot_general(wvt_ref[...], xb, nt, preferred_element_type=F32).astype(vt_ref.dtype)


def _residue_major(x, bsz, dil):
    t, k = x.shape
    return x.reshape(bsz, t // bsz // dil, dil, k).transpose(0, 2, 1, 3)


def _proj_residue(x_res, w, tm=512):
    bsz, dil, l_sub, k = x_res.shape
    d = w.shape[1] // 3
    tm = min(tm, l_sub)
    const = lambda shape: pl.BlockSpec(shape, lambda b, r, m: (0, 0))
    tr = pl.BlockSpec((1, 1, d, tm), lambda b, r, m: (b, r, 0, m))
    return pl.pallas_call(
        _proj_residue_kernel,
        grid=(bsz, dil, l_sub // tm),
        in_specs=[pl.BlockSpec((1, 1, tm, k), lambda b, r, m: (b, r, m, 0)), const((d, k)), const((k, d)),
                  const((d, k))],
        out_specs=[tr, pl.BlockSpec((1, 1, tm, d), lambda b, r, m: (b, r, m, 0)), tr],
        out_shape=[
            jax.ShapeDtypeStruct((bsz, dil, d, l_sub), BF16),
            jax.ShapeDtypeStruct((bsz, dil, l_sub, d), BF16),
            jax.ShapeDtypeStruct((bsz, dil, d, l_sub), BF16),
        ],
        compiler_params=_cparams(3),
        name=f"proj_residue_d{dil}",
    )(x_res, w[:, :d].T, w[:, d:2 * d], w[:, 2 * d:].T)


def _dilated_group(qt, k, vt, slopes, g, window, dil):
    b, _, l_sub, d = k.shape
    blk = DIL_BLOCK
    assert window // dil == blk
    nb = l_sub // blk
    bias = _dilated_bias(slopes, dil)
    prev = lambda n: jnp.maximum(n - 1, 0)

    o, lse = pl.pallas_call(
        _dilated_kernel,
        grid=(b, dil, nb),
        in_specs=[
            pl.BlockSpec((2, N_HEADS, 2 * blk, blk), lambda bi, r, n: (0, 0, 0, 0)),
            pl.BlockSpec((1, 1, d, blk), lambda bi, r, n: (bi, r, 0, n)),
            pl.BlockSpec((1, 1, blk, d), lambda bi, r, n: (bi, r, n, 0)),
            pl.BlockSpec((1, 1, blk, d), lambda bi, r, n: (bi, r, prev(n), 0)),
            pl.BlockSpec((1, 1, d, blk), lambda bi, r, n: (bi, r, 0, n)),
            pl.BlockSpec((1, 1, d, blk), lambda bi, r, n: (bi, r, 0, prev(n))),
        ],
        out_specs=[
            pl.BlockSpec((1, 1, blk, d), lambda bi, r, n: (bi, r, n, 0)),
            pl.BlockSpec((1, 1, blk, LANES), lambda bi, r, n: (bi, r, n, 0)),
        ],
        out_shape=[
            jax.ShapeDtypeStruct((b, dil, l_sub, d), BF16),
            jax.ShapeDtypeStruct((b, dil, l_sub, LANES), F32),
        ],
        scratch_shapes=[pltpu.VMEM((LANES, blk), F32), pltpu.VMEM((N_SLABS, 2 * blk, 2 * blk), F32)],
        compiler_params=_cparams(3),
        name=f"dilated_g{g}",
    )(bias, qt, k, k, vt, vt)
    return o, lse


def _merge_kernel(e_ref, *refs):
    ng = len(DIL_GROUPS)
    o_refs, l_refs = refs[:ng], refs[ng:2 * ng]
    out_ref, lse_nat, o_nat = refs[2 * ng:]

    def to_natural(dst, src, dil):
        n = src.shape[2]
        for r in range(dil):
            rows = src[0, r].astype(F32)
            for c in range(dst.shape[0]):
                dst[c, pl.ds(r, n, stride=dil), :] = rows[:, c * LANES:(c + 1) * LANES]

    for g, (_, dil) in enumerate(DIL_GROUPS):
        to_natural(lse_nat.at[pl.ds(g, 1)], l_refs[g], dil)
    ls = [lse_nat[g] for g in range(ng)]
    mx = functools.reduce(jnp.maximum, ls)
    es = [jnp.exp(x - mx) for x in ls]
    den = functools.reduce(lambda a, b: a + b, es)
    acc = None
    for g, (_, dil) in enumerate(DIL_GROUPS):
        to_natural(o_nat, o_refs[g], dil)
        w = sum(jnp.dot(part, e_ref[...], preferred_element_type=F32) for part in _bf16_parts(es[g] / den, 2))
        term = w * jnp.concatenate([o_nat[c] for c in range(o_nat.shape[0])], axis=1)
        acc = term if acc is None else acc + term
    out_ref[0] = acc.astype(out_ref.dtype)


def _merge_groups(os, lses, tm=512):
    b, dil0, l0, d = os[0].shape
    s = dil0 * l0
    expand = (jnp.arange(LANES)[:, None] == (jnp.arange(d)[None, :] // HEAD_DIM)).astype(BF16)

    def grp(width, dil):
        return pl.BlockSpec((1, dil, tm // dil, width), lambda bi, i: (bi, 0, i, 0))

    dils = [dil for _, dil in DIL_GROUPS]
    return pl.pallas_call(
        _merge_kernel,
        grid=(b, s // tm),
        in_specs=[pl.BlockSpec((LANES, d), lambda bi, i: (0, 0))] + [grp(d, dil) for dil in dils]
        + [grp(LANES, dil) for dil in dils],
        out_specs=pl.BlockSpec((1, tm, d), lambda bi, i: (bi, i, 0)),
        out_shape=jax.ShapeDtypeStruct((b, s, d), BF16),
        scratch_shapes=[pltpu.VMEM((len(dils), tm, LANES), F32), pltpu.VMEM((d // LANES, tm, LANES), F32)],
        compiler_params=_cparams(2),
        name="merge_groups",
    )(expand, *os, *lses)


def _layer_norm(y, g, b):
    mu = jnp.mean(y, axis=-1, keepdims=True)
    yc = y - mu
    var = jnp.mean(yc * yc, axis=-1, keepdims=True)
    return yc * lax.rsqrt(var + LN_EPS) * g + b


TILE_WORDS = 2 * LANES
HIGH_HALF = np.uint32(0xFFFF0000)


def _pack_token_tiles(ref, y):
    n, d = y.shape
    npk = d // TILE_WORDS
    for c in range(npk):
        lo = y[:, (2 * c) * LANES:(2 * c + 1) * LANES].astype(BF16).astype(F32)
        hi = y[:, (2 * c + 1) * LANES:(2 * c + 2) * LANES].astype(BF16).astype(F32)
        word = (lax.bitcast_convert_type(lo, U32) >> 16) | (lax.bitcast_convert_type(hi, U32) & HIGH_HALF)
        ref[pl.ds(c, n, stride=npk), :] = word


def _unpack_token_tiles(ref, n, npk, lead=()):
    blocks = []
    for c in range(npk):
        word = ref[lead + (pl.ds(c, n, stride=npk), slice(None))]
        blocks.append(lax.bitcast_convert_type(word << 16, F32))
        blocks.append(lax.bitcast_convert_type(word & HIGH_HALF, F32))
    return jnp.concatenate(blocks, axis=1)


def _wo_ln_kernel(mix_ref, wo_ref, x_ref, g_ref, b_ref, o_ref, ot_ref):
    f = jnp.dot(mix_ref[...], wo_ref[...], preferred_element_type=F32)
    y = _layer_norm(DN_ALPHA * x_ref[...] + f, g_ref[...], b_ref[...])
    o_ref[...] = y
    _pack_token_tiles(ot_ref, y)


def _wo_ln(mix, wo, x, g, b, tm=512):
    t, d = x.shape
    npk = d // TILE_WORDS
    row = pl.BlockSpec((tm, d), lambda i: (i, 0))
    vec = pl.BlockSpec((1, d), lambda i: (0, 0))
    return pl.pallas_call(
        _wo_ln_kernel,
        grid=(t // tm,),
        in_specs=[row, pl.BlockSpec((d, d), lambda i: (0, 0)), row, vec, vec],
        out_specs=[row, pl.BlockSpec((tm * npk, LANES), lambda i: (i, 0))],
        out_shape=[jax.ShapeDtypeStruct((t, d), F32), jax.ShapeDtypeStruct((t * npk, LANES), U32)],
        compiler_params=_cparams(1),
        name="wo_ln",
    )(mix, wo, x, g.reshape(1, d), b.reshape(1, d))


def _router_kernel(wr_ref, rb_ref, x_ref, e_ref, w_ref, pos_ref, cnt_ref, run_ref, *, tm):
    step = pl.program_id(0)
    ne = N_EXPERTS
    gs = ne // N_GROUPS

    @pl.when(step == 0)
    def _():
        run_ref[...] = jnp.zeros(run_ref.shape, F32)

    x = x_ref[...]
    x_hi = x.astype(BF16)
    x_lo = (x - x_hi.astype(F32)).astype(BF16)
    nt = (((1,), (1,)), ((), ()))
    logits = (lax.dot_general(wr_ref[0], x_hi, nt, preferred_element_type=F32)
              + lax.dot_general(wr_ref[0], x_lo, nt, preferred_element_type=F32)
              + lax.dot_general(wr_ref[1], x_hi, nt, preferred_element_type=F32))
    scores = jax.nn.sigmoid(logits)
    choice = scores + rb_ref[...][:, :1]

    grp_rows = lax.broadcasted_iota(jnp.int32, (gs, tm), 0)
    blocks, gscore = [], []
    for gi in range(N_GROUPS):
        cb = choice[gi * gs:(gi + 1) * gs]
        m1 = jnp.max(cb, axis=0, keepdims=True)
        i1 = jnp.min(jnp.where(cb == m1, grp_rows, gs), axis=0, keepdims=True)
        m2 = jnp.max(jnp.where(grp_rows == i1, -jnp.inf, cb), axis=0, keepdims=True)
        blocks.append(cb)
        gscore.append(m1 + m2)
    masked = []
    for gi in range(N_GROUPS):
        beaten = jnp.zeros((1, tm), jnp.int32)
        for gj in range(N_GROUPS):
            if gj == gi:
                continue
            wins = (gscore[gj] > gscore[gi]) | ((gscore[gj] == gscore[gi]) & (gj < gi))
            beaten = beaten + wins.astype(jnp.int32)
        masked.append(jnp.where(beaten < TOPK_GROUPS, blocks[gi], NEG))
    cand = jnp.concatenate(masked, axis=0)

    rows = lax.broadcasted_iota(jnp.int32, (ne, tm), 0)
    member = jnp.zeros((ne, tm), F32)
    picks, raw_w = [], []
    for _ in range(TOP_K):
        cmax = jnp.max(cand, axis=0, keepdims=True)
        first = jnp.min(jnp.where(cand == cmax, rows, ne), axis=0, keepdims=True)
        hit = rows == first
        raw_w.append(jnp.sum(jnp.where(hit, scores, 0.0), axis=0, keepdims=True))
        member = jnp.where(hit, 1.0, member)
        cand = jnp.where(hit, -jnp.inf, cand)
        picks.append(first)
    wsum = raw_w[0]
    for r in raw_w[1:]:
        wsum = wsum + r

    tri = (lax.broadcasted_iota(jnp.int32, (tm, tm), 0) < lax.broadcasted_iota(jnp.int32, (tm, tm), 1))
    member_b = member.astype(BF16)
    before = jnp.dot(member_b, tri.astype(BF16), preferred_element_type=F32)
    run = run_ref[...]
    rank = before + jnp.concatenate([run] * (tm // LANES), axis=1)
    for k in range(TOP_K):
        hit = rows == picks[k]
        e_ref[pl.ds(k, 1), :] = picks[k]
        w_ref[pl.ds(k, 1), :] = raw_w[k] / wsum * ROUTED_SCALE
        pos_ref[pl.ds(k, 1), :] = jnp.sum(jnp.where(hit, rank, 0.0), axis=0, keepdims=True).astype(jnp.int32)
    run = run + jnp.dot(member_b, jnp.ones((tm, LANES), BF16), preferred_element_type=F32)
    run_ref[...] = run
    cnt_ref[...] = run


def _router(x, w_router, router_bias, tm=256):
    t, d = x.shape
    ne = N_EXPERTS
    wr_parts = jnp.stack(_bf16_parts(w_router.T.astype(F32), 2))
    rb = jnp.broadcast_to(router_bias.astype(F32)[:, None], (ne, LANES))
    tok = pl.BlockSpec((TOP_K, tm), lambda i: (0, i))
    return pl.pallas_call(
        functools.partial(_router_kernel, tm=tm),
        grid=(t // tm,),
        in_specs=[
            pl.BlockSpec((2, ne, d), lambda i: (0, 0, 0)),
            pl.BlockSpec((ne, LANES), lambda i: (0, 0)),
            pl.BlockSpec((tm, d), lambda i: (i, 0)),
        ],
        out_specs=[tok, tok, tok, pl.BlockSpec((ne, LANES), lambda i: (0, 0))],
        out_shape=[
            jax.ShapeDtypeStruct((TOP_K, t), jnp.int32),
            jax.ShapeDtypeStruct((TOP_K, t), F32),
            jax.ShapeDtypeStruct((TOP_K, t), jnp.int32),
            jax.ShapeDtypeStruct((ne, LANES), F32),
        ],
        scratch_shapes=[pltpu.VMEM((ne, LANES), F32)],
        compiler_params=_cparams(1),
        name="router",
    )(wr_parts, rb, x)


def _tile_rows(ref, i, npk):
    return ref.at[pl.ds(pl.multiple_of(i * npk, npk), npk)]


def _dispatch_kernel(dest_hbm, x_ref, xs_hbm, dest_smem, sem_idx, sem_rows, *, tm):
    step = pl.program_id(0)
    npk = x_ref.shape[0] // tm
    idx_copy = pltpu.make_async_copy(dest_hbm.at[step], dest_smem, sem_idx)
    idx_copy.start()
    idx_copy.wait()

    def row_copy(t, d):
        return pltpu.make_async_copy(_tile_rows(x_ref, t, npk), _tile_rows(xs_hbm, d, npk), sem_rows)

    def issue(t, c):
        for k in range(TOP_K):
            row_copy(t, dest_smem[k * tm + t]).start(priority=k % 2)
        return c

    lax.fori_loop(0, tm, issue, 0)

    def drain(t, c):
        for _ in range(TOP_K):
            row_copy(0, 0).wait()
        return c

    lax.fori_loop(0, tm, drain, 0)


def _dispatch(x_tiles, dest_tiles, n_rows, npk, tm):
    t = x_tiles.shape[0] // npk
    return pl.pallas_call(
        functools.partial(_dispatch_kernel, tm=tm),
        grid=(t // tm,),
        in_specs=[pl.BlockSpec(memory_space=pl.ANY), pl.BlockSpec((tm * npk, LANES), lambda i: (i, 0))],
        out_specs=pl.BlockSpec(memory_space=pl.ANY),
        out_shape=jax.ShapeDtypeStruct((n_rows * npk, LANES), x_tiles.dtype),
        scratch_shapes=[
            pltpu.SMEM((TOP_K * tm,), jnp.int32),
            pltpu.SemaphoreType.DMA,
            pltpu.SemaphoreType.DMA,
        ],
        compiler_params=_cparams(1),
        name="dispatch",
    )(dest_tiles, x_tiles)


def _experts_kernel(blk_e_ref, blk_row_ref, n_real_ref, xs_ref, wg_ref, wu_ref, wd_ref, ys_ref, *, rb):
    @pl.when(pl.program_id(0) < n_real_ref[0])
    def _():
        npk = xs_ref.shape[0] // rb
        xb = _unpack_token_tiles(xs_ref, rb, npk).astype(BF16)
        gate = jnp.dot(xb, wg_ref[0, 0].astype(BF16), preferred_element_type=F32)
        up = jnp.dot(xb, wu_ref[0, 0].astype(BF16), preferred_element_type=F32)
        h = (gate * jax.nn.sigmoid(gate) * up).astype(BF16)
        _pack_token_tiles(ys_ref, jnp.dot(h, wd_ref[0, 0].astype(BF16), preferred_element_type=F32))


def _experts(xs, we_gate, we_up, we_down, layer, rb, blk_e, blk_row, n_real):
    d, f = we_gate.shape[-2:]
    nc = d // TILE_WORDS
    p = xs.shape[0] // nc
    rows = pl.BlockSpec((rb * nc, LANES), lambda i, be, br, nr: (br[i], 0))
    grid_spec = pltpu.PrefetchScalarGridSpec(
        num_scalar_prefetch=3,
        grid=(p // rb,),
        in_specs=[
            rows,
            pl.BlockSpec((1, 1, d, f), lambda i, be, br, nr: (layer, be[i], 0, 0)),
            pl.BlockSpec((1, 1, d, f), lambda i, be, br, nr: (layer, be[i], 0, 0)),
            pl.BlockSpec((1, 1, f, d), lambda i, be, br, nr: (layer, be[i], 0, 0)),
        ],
        out_specs=rows,
    )
    return pl.pallas_call(
        functools.partial(_experts_kernel, rb=rb),
        grid_spec=grid_spec,
        out_shape=jax.ShapeDtypeStruct((p * nc, LANES), U32),
        compiler_params=_cparams(1),
        name="experts",
    )(blk_e, blk_row, n_real, xs, we_gate, we_up, we_down)


def _combine_kernel(dest_hbm, ys_hbm, x_ref, w_ref, sg_ref, su_ref, sd_ref, g_ref, b_ref, o_ref, *refs,
                    tm, residue_dils):
    res_refs = refs[:len(residue_dils)]
    dest_smem0, dest_smem1, rows_ref, sem_idx, sem_rows = refs[len(residue_dils):len(residue_dils) + 5]
    dest_smems = (dest_smem0, dest_smem1)
    step = pl.program_id(0)
    n_steps = pl.num_programs(0)
    npk = rows_ref.shape[2] // tm
    slot = step % 2

    def row_copy(sl, k, t, d):
        return pltpu.make_async_copy(
            _tile_rows(ys_hbm, d, npk), _tile_rows(rows_ref.at[sl, k], t, npk), sem_rows.at[sl])

    def idx_copy(tile, sl):
        return pltpu.make_async_copy(dest_hbm.at[tile], dest_smems[sl], sem_idx.at[sl])

    def issue_rows(sl):
        dest_smem = dest_smems[sl]

        def issue(t, c):
            for k in range(TOP_K):
                row_copy(sl, k, t, dest_smem[k * tm + t]).start(priority=k % 2)
            return c

        lax.fori_loop(0, tm, issue, 0)

    @pl.when(step == 0)
    def _():
        idx_copy(0, 0).start()
        idx_copy(0, 0).wait()
        issue_rows(0)

        @pl.when(n_steps > 1)
        def _():
            idx_copy(1, 1).start()

    for sl in range(2):
        @pl.when((step + 1 < n_steps) & (slot == 1 - sl))
        def _(sl=sl):
            idx_copy(step + 1, sl).wait()

            @pl.when(step + 2 < n_steps)
            def _():
                idx_copy(step + 2, 1 - sl).start()

            issue_rows(sl)

    x = x_ref[...]
    xb = x.astype(BF16)
    gate = jnp.dot(xb, sg_ref[...], preferred_element_type=F32)
    up = jnp.dot(xb, su_ref[...], preferred_element_type=F32)
    h = (gate * jax.nn.sigmoid(gate) * up).astype(BF16)
    f = jnp.dot(h, sd_ref[...], preferred_element_type=F32)

    def drain(t, c):
        for _ in range(TOP_K):
            row_copy(slot, 0, 0, 0).wait()
        return c

    lax.fori_loop(0, tm, drain, 0)

    w = w_ref[...]
    for k in range(TOP_K):
        f = f + w[:, k:k + 1] * _unpack_token_tiles(rows_ref, tm, npk, lead=(slot, k))
    y = _layer_norm(DN_ALPHA * x + f, g_ref[...], b_ref[...])
    o_ref[...] = y
    if residue_dils:
        slabs = refs[-1]
        n_slabs = slabs.shape[0]
        for c in range(n_slabs):
            slabs[c] = y[:, c * LANES:(c + 1) * LANES]
        for res_ref, dil in zip(res_refs, residue_dils):
            for r in range(dil):
                res_ref[0, r] = jnp.concatenate(
                    [slabs[c, pl.ds(r, tm // dil, stride=dil), :] for c in range(n_slabs)], axis=1)


def _combine(x, ys, dest_tiles, w_tok, ws_gate, ws_up, ws_down, g, b, tm, seq, residue_dils=()):
    t, d = x.shape
    nc = d // TILE_WORDS
    f = ws_gate.shape[-1]
    tiles_per_seq = seq // tm
    row = pl.BlockSpec((tm, d), lambda i: (i, 0))
    vec = pl.BlockSpec((1, d), lambda i: (0, 0))
    res_specs = [pl.BlockSpec((1, dil, tm // dil, d), lambda i: (i // tiles_per_seq, 0, i % tiles_per_seq, 0))
                 for dil in residue_dils]
    res_shapes = [jax.ShapeDtypeStruct((t // seq, dil, seq // dil, d), F32) for dil in residue_dils]
    slab_scratch = [pltpu.VMEM((d // LANES, tm, LANES), F32)] if residue_dils else []
    return pl.pallas_call(
        functools.partial(_combine_kernel, tm=tm, residue_dils=tuple(residue_dils)),
        grid=(t // tm,),
        in_specs=[
            pl.BlockSpec(memory_space=pl.ANY),
            pl.BlockSpec(memory_space=pl.ANY),
            row,
            pl.BlockSpec((tm, TOP_K), lambda i: (i, 0)),
            pl.BlockSpec((d, f), lambda i: (0, 0)),
            pl.BlockSpec((d, f), lambda i: (0, 0)),
            pl.BlockSpec((f, d), lambda i: (0, 0)),
            vec,
            vec,
        ],
        out_specs=[row] + res_specs,
        out_shape=[jax.ShapeDtypeStruct((t, d), F32)] + res_shapes,
        scratch_shapes=[
            pltpu.SMEM((TOP_K * tm,), jnp.int32),
            pltpu.SMEM((TOP_K * tm,), jnp.int32),
            pltpu.VMEM((2, TOP_K, tm * nc, LANES), U32),
            pltpu.SemaphoreType.DMA((2,)),
            pltpu.SemaphoreType.DMA((2,)),
        ] + slab_scratch,
        compiler_params=_cparams(1),
        name="combine",
    )(dest_tiles, ys, x, w_tok, ws_gate, ws_up, ws_down, g.reshape(1, d), b.reshape(1, d))


def _tile_major(a, tm):
    k, t = a.shape
    return a.reshape(k, t // tm, tm).transpose(1, 0, 2).reshape(t // tm, k * tm)


def _dest_kernel(start_ref, e_ref, pos_ref, dest_ref, *, tm):
    ne = N_EXPERTS
    start = jnp.concatenate([start_ref[...]] * (tm // LANES), axis=1)
    rows = lax.broadcasted_iota(jnp.int32, (ne, tm), 0)
    for k in range(TOP_K):
        hit = rows == e_ref[pl.ds(k, 1), :]
        first = jnp.sum(jnp.where(hit, start, 0), axis=0, keepdims=True)
        dest_ref[pl.ds(k, 1), :] = first + pos_ref[pl.ds(k, 1), :]


def _dest(starts, top_e, pos, tm=512):
    t = top_e.shape[1]
    tok = pl.BlockSpec((TOP_K, tm), lambda i: (0, i))
    return pl.pallas_call(
        functools.partial(_dest_kernel, tm=tm),
        grid=(t // tm,),
        in_specs=[pl.BlockSpec((N_EXPERTS, LANES), lambda i: (0, 0)), tok, tok],
        out_specs=tok,
        out_shape=jax.ShapeDtypeStruct((TOP_K, t), jnp.int32),
        compiler_params=_cparams(1),
        name="dest",
    )(jnp.broadcast_to(starts[:, None], (N_EXPERTS, LANES)), top_e, pos)


def _expert_row_block(t):
    mean_rows = t * TOP_K // N_EXPERTS
    return max(LANES, -(-(mean_rows + mean_rows // 4) // LANES) * LANES)


def _moe_ln(x, x_tiles, layer, seq, w_router, router_bias, we_gate, we_up, we_down, ws_gate, ws_up, ws_down,
            g, b, residue_dils=()):
    t, d = x.shape
    rb = _expert_row_block(t)
    top_e, w, pos, counts = _router(x, w_router, router_bias)

    cnt = counts[:, 0].astype(jnp.int32)
    padded = (cnt + rb - 1) // rb * rb
    pend = jnp.cumsum(padded)
    n_blocks = t * TOP_K // rb + N_EXPERTS
    n_real = (pend[-1] // rb).astype(jnp.int32)
    blk_ids = jnp.minimum(jnp.arange(n_blocks, dtype=jnp.int32), jnp.maximum(n_real - 1, 0))
    blk_e = jnp.sum((pend[None, :] <= (blk_ids * rb)[:, None]).astype(jnp.int32), axis=1)
    blk_e = jnp.minimum(blk_e, N_EXPERTS - 1)

    dest = _dest(pend - padded, top_e, pos)
    xs = _dispatch(x_tiles, _tile_major(dest, DISPATCH_TOKENS), n_blocks * rb, d // TILE_WORDS,
                   tm=DISPATCH_TOKENS)
    ys = _experts(xs, we_gate, we_up, we_down, layer, rb, blk_e, blk_ids, n_real.reshape(1))
    return _combine(
        x, ys, _tile_major(dest, COMBINE_TOKENS), w.T, ws_gate.astype(BF16),
        ws_up.astype(BF16), ws_down.astype(BF16), g, b, tm=COMBINE_TOKENS, seq=seq, residue_dils=residue_dils,
    )


def kernel(x, w_qkv_a, w_qkv_b, w_o, ln_mix_g, ln_mix_b, w_router, router_bias, we_gate, we_up,
           we_down, ws_gate, ws_up, ws_down, ln_ffn_g, ln_ffn_b):
    bsz, s, d = x.shape
    t = bsz * s
    slopes = _alibi_slopes()
    xt = x.reshape(t, d)
    is_dilated = lambda layer: layer % N_MIXERS == 1
    strided_dils = tuple(dil for _, dil in DIL_GROUPS if dil > 1)
    x_res = {}
    for i in range(DEPTH):
        if not is_dilated(i):
            qkv = _proj(xt, w_qkv_a[i // N_MIXERS].astype(BF16))
            mix = _moba_attention(qkv.reshape(bsz, s, -1), slopes).reshape(t, d)
        else:
            w_b = w_qkv_b[i // N_MIXERS].astype(BF16)
            outs, lses = [], []
            for g, (window, dil) in enumerate(DIL_GROUPS):
                xg = x_res[dil] if dil in x_res else _residue_major(xt, bsz, dil)
                qt, kk, vt = _proj_residue(xg, w_b[:, g * 3 * d:(g + 1) * 3 * d])
                o, lse = _dilated_group(qt, kk, vt, slopes, g, window, dil)
                outs.append(o)
                lses.append(lse)
            mix = _merge_groups(outs, lses).reshape(t, d)
        x1, x1_tiles = _wo_ln(mix, w_o[i].astype(BF16), xt, ln_mix_g[i], ln_mix_b[i])
        dils = strided_dils if i + 1 < DEPTH and is_dilated(i + 1) else ()
        xt, *extra = _moe_ln(x1, x1_tiles, i, s, w_router[i], router_bias[i], we_gate, we_up, we_down,
                             ws_gate[i], ws_up[i], ws_down[i], ln_ffn_g[i], ln_ffn_b[i], residue_dils=dils)
        x_res = dict(zip(dils, extra))
    return xt.reshape(bsz, s, d)
```

```python
import functools
import math

import jax
import jax.numpy as jnp
import numpy as np
from jax import lax
from jax.experimental import pallas as pl
from jax.experimental.pallas import tpu as pltpu

N_HEADS = 16
HEAD_DIM = 64
DEPTH = 2
N_MIXERS = 2
MOBA_BLOCK = 256
MOBA_TOPK = 3
MOBA_KEY_BLOCKS_PER_STEP = 4
MOBA_QUERY_BLOCKS_PER_STEP = MOBA_KEY_BLOCKS_PER_STEP
MOBA_BIAS_PARTS = 3
DIL_GROUPS = ((128, 1), (512, 4), (2048, 16))
DIL_BLOCK = 128
N_EXPERTS = 256
TOP_K = 8
N_GROUPS = 8
TOPK_GROUPS = 4
ROUTED_SCALE = 2.5
DN_ALPHA = (2 * DEPTH) ** 0.25
LN_EPS = 1e-5
NEG = -1e30
LOG2E = math.log2(math.e)

LANES = 128
BF16_SUBLANES = 16
HEADS_PER_SLAB = LANES // HEAD_DIM
N_SLABS = N_HEADS // HEADS_PER_SLAB
VMEM_LIMIT_BYTES = 48 * 1024 * 1024
MOBA_ACC_ROWS = HEAD_DIM + BF16_SUBLANES

DISPATCH_TOKENS = 512
COMBINE_TOKENS = 512

BF16 = jnp.bfloat16
F32 = jnp.float32
U32 = jnp.uint32


def _cparams(n_axes):
    return pltpu.CompilerParams(
        dimension_semantics=("arbitrary",) * n_axes, vmem_limit_bytes=VMEM_LIMIT_BYTES
    )


def _alibi_slopes():
    return 2.0 ** (-8.0 * jnp.arange(1, N_HEADS + 1, dtype=F32) / N_HEADS)


def _bf16_parts(x, n):
    parts, rest = [], x
    for _ in range(n):
        top = lax.bitcast_convert_type(lax.bitcast_convert_type(rest, jnp.uint32) & np.uint32(0xFFFF0000), F32)
        parts.append(top.astype(BF16))
        rest = rest - top
    return parts


def _proj_kernel(x_ref, w_ref, o_ref):
    o_ref[...] = jnp.dot(
        x_ref[...].astype(BF16), w_ref[...], preferred_element_type=F32
    ).astype(o_ref.dtype)


def _proj(x, w, tm=1024, tn=1024):
    t, k = x.shape
    n = w.shape[1]
    return pl.pallas_call(
        _proj_kernel,
        grid=(t // tm, n // tn),
        in_specs=[
            pl.BlockSpec((tm, k), lambda i, j: (i, 0)),
            pl.BlockSpec((k, tn), lambda i, j: (0, j)),
        ],
        out_specs=pl.BlockSpec((tm, tn), lambda i, j: (i, j)),
        out_shape=jax.ShapeDtypeStruct((t, n), BF16),
        compiler_params=_cparams(2),
        name="proj",
    )(x, w)


def _moba_kernel(slopes_ref, q_ref, qn_ref, k_ref, v_ref, wb_ref, o_ref, ka_ref, vt_ref, km_ref, sel_ref,
                 s_ref, *, n_blocks):
    blk = MOBA_BLOCK
    unroll = MOBA_KEY_BLOCKS_PER_STEP
    span = unroll * blk
    acc_rows = MOBA_ACC_ROWS
    h2 = pl.program_id(1)
    i = pl.program_id(2)
    heads = range(HEADS_PER_SLAB)

    @pl.when(i == 0)
    def _prepare_keys():
        lane = lax.broadcasted_iota(jnp.int32, (blk, LANES), 1)
        row_f = lax.broadcasted_iota(jnp.int32, (blk, LANES), 0).astype(F32)
        tail = lax.broadcasted_iota(jnp.int32, (acc_rows - HEAD_DIM, blk), 0)
        ones_row = jnp.where(tail == 0, 1.0, 0.0)

        def body(n, c):
            r0 = pl.multiple_of(n * blk, blk)
            vt = v_ref[0, pl.ds(r0, blk), :].astype(F32).T
            for hh in heads:
                vt_ref[n, hh] = jnp.concatenate(
                    [vt[hh * HEAD_DIM:(hh + 1) * HEAD_DIM], ones_row], axis=0).astype(BF16)
            kb = k_ref[0, pl.ds(r0, blk), :]
            km_ref[pl.ds(n, 1), :] = jnp.mean(kb.astype(F32), axis=0, keepdims=True)
            sub = jnp.asarray(n % unroll, F32)
            aux = jnp.where(lane < MOBA_BIAS_PARTS, sub, jnp.where(lane < 2 * MOBA_BIAS_PARTS, row_f, 0.0))
            ka_ref[pl.ds(r0, blk), :] = jnp.concatenate([kb, aux.astype(BF16)], axis=1)
            return c

        lax.fori_loop(0, n_blocks, body, 0)

    qb = MOBA_QUERY_BLOCKS_PER_STEP
    wide = qb * blk
    i_first = pl.program_id(2) * qb
    feat = lax.broadcasted_iota(jnp.int32, (LANES, wide), 0)
    kmean = km_ref[...]
    km_lane = lax.broadcasted_iota(jnp.int32, kmean.shape, 1)
    blk_row = lax.broadcasted_iota(jnp.int32, (n_blocks, wide), 0)
    own_blk = i_first + lax.broadcasted_iota(jnp.int32, (n_blocks, wide), 1) // blk

    def per_head_qt(ref):
        qt = ref[0].astype(F32).T
        return [jnp.where((feat >= hh * HEAD_DIM) & (feat < (hh + 1) * HEAD_DIM), qt, 0.0) for hh in heads]

    def score_operand(qt_hs):
        scaled = [(x * (HEAD_DIM ** -0.5 * LOG2E)).astype(BF16) for x in qt_hs]
        return jnp.concatenate([jnp.concatenate(scaled, axis=1), wb_ref[0]], axis=0)

    slopes = [slopes_ref[h2 * HEADS_PER_SLAB + hh] for hh in heads]
    qt_hs = per_head_qt(q_ref)
    for hh in heads:
        qt_h = qt_hs[hh]
        km_h = jnp.where((km_lane >= hh * HEAD_DIM) & (km_lane < (hh + 1) * HEAD_DIM), kmean, 0.0)
        qt_b = qt_h.astype(BF16)
        gate = sum(jnp.dot(part, qt_b, preferred_element_type=F32) for part in _bf16_parts(km_h, 3))
        g = jnp.where(blk_row < own_blk, gate, -jnp.inf)
        sel = jnp.zeros(g.shape, F32)
        for _ in range(MOBA_TOPK):
            gmax = jnp.max(g, axis=0, keepdims=True)
            first = jnp.min(jnp.where(g == gmax, blk_row, n_blocks), axis=0, keepdims=True)
            hit = blk_row == first
            sel = jnp.where(hit & (gmax > -jnp.inf), 1.0, sel)
            g = jnp.where(hit, -jnp.inf, g)
        sel_ref[hh] = sel

    w_all = score_operand(qt_hs)
    w_next = score_operand(per_head_qt(qn_ref))

    groups = [(hh, q) for hh in heads for q in range(qb)]
    group_cols = [slice((hh * qb + q) * blk, (hh * qb + q + 1) * blk) for hh, q in groups]

    def scores(span_idx, cols, w=w_all):
        r0 = pl.multiple_of(span_idx * span, span)
        return jnp.dot(ka_ref[pl.ds(r0, span), :], w[:, cols], preferred_element_type=F32)

    key_row = lax.broadcasted_iota(jnp.int32, (blk, blk), 0)
    qry_col = lax.broadcasted_iota(jnp.int32, (blk, blk), 1)

    def consume(gi, span_idx, m, acc, last):
        hh, q = groups[gi]
        j0 = span_idx * unroll
        c = slopes[hh] * LOG2E * jnp.asarray(blk * (j0 - (i_first + q)), F32)
        for u in range(unroll):
            if last and u > q:
                continue
            j = j0 + u
            s = s_ref[u * blk:(u + 1) * blk, group_cols[gi]]
            if last and u == q:
                s = jnp.where(key_row <= qry_col, s, NEG)
                m_new = jnp.maximum(m, jnp.max(s, axis=0, keepdims=True) + c)
                p = jnp.exp2(s - (m_new - c))
            else:
                ch = sel_ref[hh, pl.ds(j, 1), q * blk:(q + 1) * blk] > 0.5
                m_new = jnp.where(ch, jnp.maximum(m, jnp.max(s, axis=0, keepdims=True) + c), m)
                p = jnp.exp2(s - jnp.where(ch, m_new - c, -NEG))
            acc = jnp.exp2(m - m_new) * acc + jnp.dot(
                vt_ref[j, hh], p.astype(BF16), preferred_element_type=F32)
            m = m_new
        return m, acc

    n_full = i_first // unroll

    @pl.when(i == 0)
    def _first_scores():
        s_ref[...] = scores(0, slice(None))

    def body(it, state):
        out = []
        for gi in range(len(groups)):
            out.append(consume(gi, it, *state[gi], last=False))
            s_ref[:, group_cols[gi]] = scores(it + 1, group_cols[gi])
        return tuple(out)

    init = tuple((jnp.full((1, blk), NEG, F32), jnp.zeros((acc_rows, blk), F32)) for _ in groups)
    state = lax.fori_loop(0, n_full, body, init)
    final = []
    for gi in range(len(groups)):
        final.append(consume(gi, n_full, *state[gi], last=True))
        s_ref[:, group_cols[gi]] = scores(0, group_cols[gi], w_next)
    state = final

    outs = [acc[:HEAD_DIM] / acc[HEAD_DIM:HEAD_DIM + 1] for _, acc in state]
    per_block = [jnp.concatenate([outs[hh * qb + q] for hh in heads], axis=0) for q in range(qb)]
    o_ref[0] = jnp.concatenate(per_block, axis=1).T.astype(o_ref.dtype)


def _moba_attention(qkv, slopes):
    b, s, _ = qkv.shape
    blk = MOBA_BLOCK
    n_blocks = s // blk
    qb = MOBA_QUERY_BLOCKS_PER_STEP
    assert n_blocks % MOBA_KEY_BLOCKS_PER_STEP == 0 and MOBA_KEY_BLOCKS_PER_STEP == qb
    wide = HEADS_PER_SLAB * qb * blk
    grid_spec = pltpu.PrefetchScalarGridSpec(
        num_scalar_prefetch=1,
        grid=(b, N_SLABS, n_blocks // qb),
        in_specs=[
            pl.BlockSpec((1, qb * blk, LANES), lambda bi, h, i, sl: (bi, i, h)),
            pl.BlockSpec((1, qb * blk, LANES), lambda bi, h, i, sl: (bi, jnp.minimum(i + 1, n_blocks // qb - 1), h)),
            pl.BlockSpec((1, s, LANES), lambda bi, h, i, sl: (bi, 0, N_SLABS + h)),
            pl.BlockSpec((1, s, LANES), lambda bi, h, i, sl: (bi, 0, 2 * N_SLABS + h)),
            pl.BlockSpec((1, LANES, wide), lambda bi, h, i, sl: (h, 0, 0)),
        ],
        out_specs=pl.BlockSpec((1, qb * blk, LANES), lambda bi, h, i, sl: (bi, i, h)),
        scratch_shapes=[
            pltpu.VMEM((s, 2 * LANES), BF16),
            pltpu.VMEM((n_blocks, HEADS_PER_SLAB, MOBA_ACC_ROWS, blk), BF16),
            pltpu.VMEM((n_blocks, LANES), F32),
            pltpu.VMEM((HEADS_PER_SLAB, n_blocks, qb * blk), F32),
            pltpu.VMEM((MOBA_KEY_BLOCKS_PER_STEP * blk, wide), F32),
        ],
    )
    return pl.pallas_call(
        functools.partial(_moba_kernel, n_blocks=n_blocks),
        grid_spec=grid_spec,
        out_shape=jax.ShapeDtypeStruct((b, s, N_HEADS * HEAD_DIM), BF16),
        compiler_params=_cparams(3),
        name="moba",
    )(slopes, qkv, qkv, qkv, qkv, _moba_slope_rows(slopes))


def _moba_slope_rows(slopes):
    blk = MOBA_BLOCK
    wide = MOBA_QUERY_BLOCKS_PER_STEP * blk
    parts = _bf16_parts(slopes * LOG2E, MOBA_BIAS_PARTS)
    sp = jnp.stack([p.astype(F32) for p in parts], axis=1)
    rows = jnp.concatenate([sp * blk, sp], axis=1).reshape(N_SLABS, HEADS_PER_SLAB, 2 * MOBA_BIAS_PARTS)
    wb = jnp.zeros((N_SLABS, LANES, HEADS_PER_SLAB * wide), F32)
    for hh in range(HEADS_PER_SLAB):
        wb = wb.at[:, :2 * MOBA_BIAS_PARTS, hh * wide:(hh + 1) * wide].set(rows[:, hh, :, None])
    return wb.astype(BF16)


def _dilated_kernel(bias_ref, qt_ref, kc_ref, kp_ref, vtc_ref, vtp_ref, o_ref, lse_ref, lse_s, s_scr):
    blk = DIL_BLOCK
    table = jnp.minimum(pl.program_id(2), 1)
    lse_s[...] = jnp.zeros(lse_s.shape, F32)
    tail = lax.broadcasted_iota(jnp.int32, (MOBA_ACC_ROWS - HEAD_DIM, 2 * blk), 0)
    ones_row = jnp.where(tail == 0, 1.0, 0.0).astype(BF16)
    feat = lax.broadcasted_iota(jnp.int32, (LANES, blk), 0)
    for h2 in range(N_SLABS):
        cols = slice(h2 * LANES, (h2 + 1) * LANES)
        qt = qt_ref[0, 0, cols, :].astype(F32) * (HEAD_DIM ** -0.5 * LOG2E)
        kband = jnp.concatenate([kp_ref[0, 0, :, cols], kc_ref[0, 0, :, cols]], axis=0)
        qt_both = jnp.concatenate(
            [jnp.where((feat >= hh * HEAD_DIM) & (feat < (hh + 1) * HEAD_DIM), qt, 0.0)
             for hh in range(HEADS_PER_SLAB)], axis=1).astype(BF16)
        s_scr[h2] = jnp.dot(kband, qt_both, preferred_element_type=F32)
    for h2 in range(N_SLABS):
        cols = slice(h2 * LANES, (h2 + 1) * LANES)
        vt = jnp.concatenate([vtp_ref[0, 0, cols, :], vtc_ref[0, 0, cols, :]], axis=1)
        outs = []
        for hh in range(HEADS_PER_SLAB):
            h = h2 * HEADS_PER_SLAB + hh
            s = s_scr[h2, :, hh * blk:(hh + 1) * blk] + bias_ref[table, h]
            m = jnp.max(s, axis=0, keepdims=True)
            p = jnp.exp2(s - m).astype(BF16)
            vt_h = jnp.concatenate([vt[hh * HEAD_DIM:(hh + 1) * HEAD_DIM], ones_row], axis=0)
            acc = jnp.dot(vt_h, p, preferred_element_type=F32)
            l = acc[HEAD_DIM:HEAD_DIM + 1]
            outs.append(acc[:HEAD_DIM] / l)
            lse_s[pl.ds(h, 1), :] = (m + jnp.log2(l)) * (1.0 / LOG2E)
        o_ref[0, 0, :, cols] = jnp.concatenate(outs, axis=0).T.astype(o_ref.dtype)
    lse_ref[0, 0] = lse_s[...].T


def _dilated_bias(slopes, dil):
    blk = DIL_BLOCK
    key = jnp.arange(2 * blk)[:, None]
    steps = jnp.arange(blk)[None, :] - (key - blk)
    steps_max = blk
    valid = (steps >= 0) & (steps <= steps_max)
    bias = -(slopes * LOG2E)[:, None, None] * (steps * dil).astype(F32)[None]
    rest = jnp.where(valid[None], bias, NEG)
    first = jnp.where((key >= blk)[None], rest, NEG)
    return jnp.stack([first, rest]).astype(F32)


def _proj_residue_kernel(x_ref, wqt_ref, wk_ref, wvt_ref, qt_ref, k_ref, vt_ref):
    xb = x_ref[0, 0].astype(BF16)
    nt = (((1,), (1,)), ((), ()))
    qt_ref[0, 0] = lax.dot_general(wqt_ref[...], xb, nt, preferred_element_type=F32).astype(qt_ref.dtype)
    k_ref[0, 0] = jnp.dot(xb, wk_ref[...], preferred_element_type=F32).astype(k_ref.dtype)
    vt_ref[0, 0] = lax.dot_general(wvt_ref[...], xb, nt, preferred_element_type=F32).astype(vt_ref.dtype)


def _residue_major(x, bsz, dil):
    t, k = x.shape
    return x.reshape(bsz, t // bsz // dil, dil, k).transpose(0, 2, 1, 3)


def _proj_residue(x_res, w, tm=512):
    bsz, dil, l_sub, k = x_res.shape
    d = w.shape[1] // 3
    tm = min(tm, l_sub)
    const = lambda shape: pl.BlockSpec(shape, lambda b, r, m: (0, 0))
    tr = pl.BlockSpec((1, 1, d, tm), lambda b, r, m: (b, r, 0, m))
    return pl.pallas_call(
        _proj_residue_kernel,
        grid=(bsz, dil, l_sub // tm),
        in_specs=[pl.BlockSpec((1, 1, tm, k), lambda b, r, m: (b, r, m, 0)), const((d, k)), const((k, d)),
                  const((d, k))],
        out_specs=[tr, pl.BlockSpec((1, 1, tm, d), lambda b, r, m: (b, r, m, 0)), tr],
        out_shape=[
            jax.ShapeDtypeStruct((bsz, dil, d, l_sub), BF16),
            jax.ShapeDtypeStruct((bsz, dil, l_sub, d), BF16),
            jax.ShapeDtypeStruct((bsz, dil, d, l_sub), BF16),
        ],
        compiler_params=_cparams(3),
        name=f"proj_residue_d{dil}",
    )(x_res, w[:, :d].T, w[:, d:2 * d], w[:, 2 * d:].T)


def _dilated_group(qt, k, vt, slopes, g, window, dil):
    b, _, l_sub, d = k.shape
    blk = DIL_BLOCK
    assert window // dil == blk
    nb = l_sub // blk
    bias = _dilated_bias(slopes, dil)
    prev = lambda n: jnp.maximum(n - 1, 0)

    o, lse = pl.pallas_call(
        _dilated_kernel,
        grid=(b, dil, nb),
        in_specs=[
            pl.BlockSpec((2, N_HEADS, 2 * blk, blk), lambda bi, r, n: (0, 0, 0, 0)),
            pl.BlockSpec((1, 1, d, blk), lambda bi, r, n: (bi, r, 0, n)),
            pl.BlockSpec((1, 1, blk, d), lambda bi, r, n: (bi, r, n, 0)),
            pl.BlockSpec((1, 1, blk, d), lambda bi, r, n: (bi, r, prev(n), 0)),
            pl.BlockSpec((1, 1, d, blk), lambda bi, r, n: (bi, r, 0, n)),
            pl.BlockSpec((1, 1, d, blk), lambda bi, r, n: (bi, r, 0, prev(n))),
        ],
        out_specs=[
            pl.BlockSpec((1, 1, blk, d), lambda bi, r, n: (bi, r, n, 0)),
            pl.BlockSpec((1, 1, blk, LANES), lambda bi, r, n: (bi, r, n, 0)),
        ],
        out_shape=[
            jax.ShapeDtypeStruct((b, dil, l_sub, d), BF16),
            jax.ShapeDtypeStruct((b, dil, l_sub, LANES), F32),
        ],
        scratch_shapes=[pltpu.VMEM((LANES, blk), F32), pltpu.VMEM((N_SLABS, 2 * blk, 2 * blk), F32)],
        compiler_params=_cparams(3),
        name=f"dilated_g{g}",
    )(bias, qt, k, k, vt, vt)
    return o, lse


def _merge_kernel(e_ref, *refs):
    ng = len(DIL_GROUPS)
    o_refs, l_refs = refs[:ng], refs[ng:2 * ng]
    out_ref, lse_nat, o_nat = refs[2 * ng:]

    def to_natural(dst, src, dil):
        n = src.shape[2]
        for r in range(dil):
            rows = src[0, r].astype(F32)
            for c in range(dst.shape[0]):
                dst[c, pl.ds(r, n, stride=dil), :] = rows[:, c * LANES:(c + 1) * LANES]

    for g, (_, dil) in enumerate(DIL_GROUPS):
        to_natural(lse_nat.at[pl.ds(g, 1)], l_refs[g], dil)
    ls = [lse_nat[g] for g in range(ng)]
    mx = functools.reduce(jnp.maximum, ls)
    es = [jnp.exp(x - mx) for x in ls]
    den = functools.reduce(lambda a, b: a + b, es)
    acc = None
    for g, (_, dil) in enumerate(DIL_GROUPS):
        to_natural(o_nat, o_refs[g], dil)
        w = sum(jnp.dot(part, e_ref[...], preferred_element_type=F32) for part in _bf16_parts(es[g] / den, 2))
        term = w * jnp.concatenate([o_nat[c] for c in range(o_nat.shape[0])], axis=1)
        acc = term if acc is None else acc + term
    out_ref[0] = acc.astype(out_ref.dtype)


def _merge_groups(os, lses, tm=512):
    b, dil0, l0, d = os[0].shape
    s = dil0 * l0
    expand = (jnp.arange(LANES)[:, None] == (jnp.arange(d)[None, :] // HEAD_DIM)).astype(BF16)

    def grp(width, dil):
        return pl.BlockSpec((1, dil, tm // dil, width), lambda bi, i: (bi, 0, i, 0))

    dils = [dil for _, dil in DIL_GROUPS]
    return pl.pallas_call(
        _merge_kernel,
        grid=(b, s // tm),
        in_specs=[pl.BlockSpec((LANES, d), lambda bi, i: (0, 0))] + [grp(d, dil) for dil in dils]
        + [grp(LANES, dil) for dil in dils],
        out_specs=pl.BlockSpec((1, tm, d), lambda bi, i: (bi, i, 0)),
        out_shape=jax.ShapeDtypeStruct((b, s, d), BF16),
        scratch_shapes=[pltpu.VMEM((len(dils), tm, LANES), F32), pltpu.VMEM((d // LANES, tm, LANES), F32)],
        compiler_params=_cparams(2),
        name="merge_groups",
    )(expand, *os, *lses)


def _layer_norm(y, g, b):
    mu = jnp.mean(y, axis=-1, keepdims=True)
    yc = y - mu
    var = jnp.mean(yc * yc, axis=-1, keepdims=True)
    return yc * lax.rsqrt(var + LN_EPS) * g + b


TILE_WORDS = 2 * LANES
HIGH_HALF = np.uint32(0xFFFF0000)


def _pack_token_tiles(ref, y):
    n, d = y.shape
    npk = d // TILE_WORDS
    for c in range(npk):
        lo = y[:, (2 * c) * LANES:(2 * c + 1) * LANES].astype(BF16).astype(F32)
        hi = y[:, (2 * c + 1) * LANES:(2 * c + 2) * LANES].astype(BF16).astype(F32)
        word = (lax.bitcast_convert_type(lo, U32) >> 16) | (lax.bitcast_convert_type(hi, U32) & HIGH_HALF)
        ref[pl.ds(c, n, stride=npk), :] = word


def _unpack_token_tiles(ref, n, npk, lead=()):
    blocks = []
    for c in range(npk):
        word = ref[lead + (pl.ds(c, n, stride=npk), slice(None))]
        blocks.append(lax.bitcast_convert_type(word << 16, F32))
        blocks.append(lax.bitcast_convert_type(word & HIGH_HALF, F32))
    return jnp.concatenate(blocks, axis=1)


def _wo_ln_kernel(mix_ref, wo_ref, x_ref, g_ref, b_ref, o_ref, ot_ref):
    f = jnp.dot(mix_ref[...], wo_ref[...], preferred_element_type=F32)
    y = _layer_norm(DN_ALPHA * x_ref[...] + f, g_ref[...], b_ref[...])
    o_ref[...] = y
    _pack_token_tiles(ot_ref, y)


def _wo_ln(mix, wo, x, g, b, tm=512):
    t, d = x.shape
    npk = d // TILE_WORDS
    row = pl.BlockSpec((tm, d), lambda i: (i, 0))
    vec = pl.BlockSpec((1, d), lambda i: (0, 0))
    return pl.pallas_call(
        _wo_ln_kernel,
        grid=(t // tm,),
        in_specs=[row, pl.BlockSpec((d, d), lambda i: (0, 0)), row, vec, vec],
        out_specs=[row, pl.BlockSpec((tm * npk, LANES), lambda i: (i, 0))],
        out_shape=[jax.ShapeDtypeStruct((t, d), F32), jax.ShapeDtypeStruct((t * npk, LANES), U32)],
        compiler_params=_cparams(1),
        name="wo_ln",
    )(mix, wo, x, g.reshape(1, d), b.reshape(1, d))


def _router_kernel(wr_ref, rb_ref, x_ref, e_ref, w_ref, pos_ref, cnt_ref, run_ref, *, tm):
    step = pl.program_id(0)
    ne = N_EXPERTS
    gs = ne // N_GROUPS

    @pl.when(step == 0)
    def _():
        run_ref[...] = jnp.zeros(run_ref.shape, F32)

    x = x_ref[...]
    x_hi = x.astype(BF16)
    x_lo = (x - x_hi.astype(F32)).astype(BF16)
    nt = (((1,), (1,)), ((), ()))
    logits = (lax.dot_general(wr_ref[0], x_hi, nt, preferred_element_type=F32)
              + lax.dot_general(wr_ref[0], x_lo, nt, preferred_element_type=F32)
              + lax.dot_general(wr_ref[1], x_hi, nt, preferred_element_type=F32))
    scores = jax.nn.sigmoid(logits)
    choice = scores + rb_ref[...][:, :1]

    grp_rows = lax.broadcasted_iota(jnp.int32, (gs, tm), 0)
    blocks, gscore = [], []
    for gi in range(N_GROUPS):
        cb = choice[gi * gs:(gi + 1) * gs]
        m1 = jnp.max(cb, axis=0, keepdims=True)
        i1 = jnp.min(jnp.where(cb == m1, grp_rows, gs), axis=0, keepdims=True)
        m2 = jnp.max(jnp.where(grp_rows == i1, -jnp.inf, cb), axis=0, keepdims=True)
        blocks.append(cb)
        gscore.append(m1 + m2)
    masked = []
    for gi in range(N_GROUPS):
        beaten = jnp.zeros((1, tm), jnp.int32)
        for gj in range(N_GROUPS):
            if gj == gi:
                continue
            wins = (gscore[gj] > gscore[gi]) | ((gscore[gj] == gscore[gi]) & (gj < gi))
            beaten = beaten + wins.astype(jnp.int32)
        masked.append(jnp.where(beaten < TOPK_GROUPS, blocks[gi], NEG))
    cand = jnp.concatenate(masked, axis=0)

    rows = lax.broadcasted_iota(jnp.int32, (ne, tm), 0)
    member = jnp.zeros((ne, tm), F32)
    picks, raw_w = [], []
    for _ in range(TOP_K):
        cmax = jnp.max(cand, axis=0, keepdims=True)
        first = jnp.min(jnp.where(cand == cmax, rows, ne), axis=0, keepdims=True)
        hit = rows == first
        raw_w.append(jnp.sum(jnp.where(hit, scores, 0.0), axis=0, keepdims=True))
        member = jnp.where(hit, 1.0, member)
        cand = jnp.where(hit, -jnp.inf, cand)
        picks.append(first)
    wsum = raw_w[0]
    for r in raw_w[1:]:
        wsum = wsum + r

    tri = (lax.broadcasted_iota(jnp.int32, (tm, tm), 0) < lax.broadcasted_iota(jnp.int32, (tm, tm), 1))
    member_b = member.astype(BF16)
    before = jnp.dot(member_b, tri.astype(BF16), preferred_element_type=F32)
    run = run_ref[...]
    rank = before + jnp.concatenate([run] * (tm // LANES), axis=1)
    for k in range(TOP_K):
        hit = rows == picks[k]
        e_ref[pl.ds(k, 1), :] = picks[k]
        w_ref[pl.ds(k, 1), :] = raw_w[k] / wsum * ROUTED_SCALE
        pos_ref[pl.ds(k, 1), :] = jnp.sum(jnp.where(hit, rank, 0.0), axis=0, keepdims=True).astype(jnp.int32)
    run = run + jnp.dot(member_b, jnp.ones((tm, LANES), BF16), preferred_element_type=F32)
    run_ref[...] = run
    cnt_ref[...] = run


def _router(x, w_router, router_bias, tm=256):
    t, d = x.shape
    ne = N_EXPERTS
    wr_parts = jnp.stack(_bf16_parts(w_router.T.astype(F32), 2))
    rb = jnp.broadcast_to(router_bias.astype(F32)[:, None], (ne, LANES))
    tok = pl.BlockSpec((TOP_K, tm), lambda i: (0, i))
    return pl.pallas_call(
        functools.partial(_router_kernel, tm=tm),
        grid=(t // tm,),
        in_specs=[
            pl.BlockSpec((2, ne, d), lambda i: (0, 0, 0)),
            pl.BlockSpec((ne, LANES), lambda i: (0, 0)),
            pl.BlockSpec((tm, d), lambda i: (i, 0)),
        ],
        out_specs=[tok, tok, tok, pl.BlockSpec((ne, LANES), lambda i: (0, 0))],
        out_shape=[
            jax.ShapeDtypeStruct((TOP_K, t), jnp.int32),
            jax.ShapeDtypeStruct((TOP_K, t), F32),
            jax.ShapeDtypeStruct((TOP_K, t), jnp.int32),
            jax.ShapeDtypeStruct((ne, LANES), F32),
        ],
        scratch_shapes=[pltpu.VMEM((ne, LANES), F32)],
        compiler_params=_cparams(1),
        name="router",
    )(wr_parts, rb, x)


def _tile_rows(ref, i, npk):
    return ref.at[pl.ds(pl.multiple_of(i * npk, npk), npk)]


def _dispatch_kernel(dest_hbm, x_ref, xs_hbm, dest_smem, sem_idx, sem_rows, *, tm):
    step = pl.program_id(0)
    npk = x_ref.shape[0] // tm
    idx_copy = pltpu.make_async_copy(dest_hbm.at[step], dest_smem, sem_idx)
    idx_copy.start()
    idx_copy.wait()

    def row_copy(t, d):
        return pltpu.make_async_copy(_tile_rows(x_ref, t, npk), _tile_rows(xs_hbm, d, npk), sem_rows)

    def issue(t, c):
        for k in range(TOP_K):
            row_copy(t, dest_smem[k * tm + t]).start(priority=k % 2)
        return c

    lax.fori_loop(0, tm, issue, 0)

    def drain(t, c):
        for _ in range(TOP_K):
            row_copy(0, 0).wait()
        return c

    lax.fori_loop(0, tm, drain, 0)


def _dispatch(x_tiles, dest_tiles, n_rows, npk, tm):
    t = x_tiles.shape[0] // npk
    return pl.pallas_call(
        functools.partial(_dispatch_kernel, tm=tm),
        grid=(t // tm,),
        in_specs=[pl.BlockSpec(memory_space=pl.ANY), pl.BlockSpec((tm * npk, LANES), lambda i: (i, 0))],
        out_specs=pl.BlockSpec(memory_space=pl.ANY),
        out_shape=jax.ShapeDtypeStruct((n_rows * npk, LANES), x_tiles.dtype),
        scratch_shapes=[
            pltpu.SMEM((TOP_K * tm,), jnp.int32),
            pltpu.SemaphoreType.DMA,
            pltpu.SemaphoreType.DMA,
        ],
        compiler_params=_cparams(1),
        name="dispatch",
    )(dest_tiles, x_tiles)


def _experts_kernel(blk_e_ref, blk_row_ref, n_real_ref, xs_ref, wg_ref, wu_ref, wd_ref, ys_ref, *, rb):
    @pl.when(pl.program_id(0) < n_real_ref[0])
    def _():
        npk = xs_ref.shape[0] // rb
        xb = _unpack_token_tiles(xs_ref, rb, npk).astype(BF16)
        gate = jnp.dot(xb, wg_ref[0, 0].astype(BF16), preferred_element_type=F32)
        up = jnp.dot(xb, wu_ref[0, 0].astype(BF16), preferred_element_type=F32)
        h = (gate * jax.nn.sigmoid(gate) * up).astype(BF16)
        _pack_token_tiles(ys_ref, jnp.dot(h, wd_ref[0, 0].astype(BF16), preferred_element_type=F32))


def _experts(xs, we_gate, we_up, we_down, layer, rb, blk_e, blk_row, n_real):
    d, f = we_gate.shape[-2:]
    nc = d // TILE_WORDS
    p = xs.shape[0] // nc
    rows = pl.BlockSpec((rb * nc, LANES), lambda i, be, br, nr: (br[i], 0))
    grid_spec = pltpu.PrefetchScalarGridSpec(
        num_scalar_prefetch=3,
        grid=(p // rb,),
        in_specs=[
            rows,
            pl.BlockSpec((1, 1, d, f), lambda i, be, br, nr: (layer, be[i], 0, 0)),
            pl.BlockSpec((1, 1, d, f), lambda i, be, br, nr: (layer, be[i], 0, 0)),
            pl.BlockSpec((1, 1, f, d), lambda i, be, br, nr: (layer, be[i], 0, 0)),
        ],
        out_specs=rows,
    )
    return pl.pallas_call(
        functools.partial(_experts_kernel, rb=rb),
        grid_spec=grid_spec,
        out_shape=jax.ShapeDtypeStruct((p * nc, LANES), U32),
        compiler_params=_cparams(1),
        name="experts",
    )(blk_e, blk_row, n_real, xs, we_gate, we_up, we_down)


def _combine_kernel(dest_hbm, ys_hbm, x_ref, w_ref, sg_ref, su_ref, sd_ref, g_ref, b_ref, o_ref, *refs,
                    tm, residue_dils):
    res_refs = refs[:len(residue_dils)]
    dest_smem0, dest_smem1, rows_ref, sem_idx, sem_rows = refs[len(residue_dils):len(residue_dils) + 5]
    dest_smems = (dest_smem0, dest_smem1)
    step = pl.program_id(0)
    n_steps = pl.num_programs(0)
    npk = rows_ref.shape[2] // tm
    slot = step % 2

    def row_copy(sl, k, t, d):
        return pltpu.make_async_copy(
            _tile_rows(ys_hbm, d, npk), _tile_rows(rows_ref.at[sl, k], t, npk), sem_rows.at[sl])

    def idx_copy(tile, sl):
        return pltpu.make_async_copy(dest_hbm.at[tile], dest_smems[sl], sem_idx.at[sl])

    def issue_rows(sl):
        dest_smem = dest_smems[sl]

        def issue(t, c):
            for k in range(TOP_K):
                row_copy(sl, k, t, dest_smem[k * tm + t]).start(priority=k % 2)
            return c

        lax.fori_loop(0, tm, issue, 0)

    @pl.when(step == 0)
    def _():
        idx_copy(0, 0).start()
        idx_copy(0, 0).wait()
        issue_rows(0)

        @pl.when(n_steps > 1)
        def _():
            idx_copy(1, 1).start()

    for sl in range(2):
        @pl.when((step + 1 < n_steps) & (slot == 1 - sl))
        def _(sl=sl):
            idx_copy(step + 1, sl).wait()

            @pl.when(step + 2 < n_steps)
            def _():
                idx_copy(step + 2, 1 - sl).start()

            issue_rows(sl)

    x = x_ref[...]
    xb = x.astype(BF16)
    gate = jnp.dot(xb, sg_ref[...], preferred_element_type=F32)
    up = jnp.dot(xb, su_ref[...], preferred_element_type=F32)
    h = (gate * jax.nn.sigmoid(gate) * up).astype(BF16)
    f = jnp.dot(h, sd_ref[...], preferred_element_type=F32)

    def drain(t, c):
        for _ in range(TOP_K):
            row_copy(slot, 0, 0, 0).wait()
        return c

    lax.fori_loop(0, tm, drain, 0)

    w = w_ref[...]
    for k in range(TOP_K):
        f = f + w[:, k:k + 1] * _unpack_token_tiles(rows_ref, tm, npk, lead=(slot, k))
    y = _layer_norm(DN_ALPHA * x + f, g_ref[...], b_ref[...])
    o_ref[...] = y
    if residue_dils:
        slabs = refs[-1]
        n_slabs = slabs.shape[0]
        for c in range(n_slabs):
            slabs[c] = y[:, c * LANES:(c + 1) * LANES]
        for res_ref, dil in zip(res_refs, residue_dils):
            for r in range(dil):
                res_ref[0, r] = jnp.concatenate(
                    [slabs[c, pl.ds(r, tm // dil, stride=dil), :] for c in range(n_slabs)], axis=1)


def _combine(x, ys, dest_tiles, w_tok, ws_gate, ws_up, ws_down, g, b, tm, seq, residue_dils=()):
    t, d = x.shape
    nc = d // TILE_WORDS
    f = ws_gate.shape[-1]
    tiles_per_seq = seq // tm
    row = pl.BlockSpec((tm, d), lambda i: (i, 0))
    vec = pl.BlockSpec((1, d), lambda i: (0, 0))
    res_specs = [pl.BlockSpec((1, dil, tm // dil, d), lambda i: (i // tiles_per_seq, 0, i % tiles_per_seq, 0))
                 for dil in residue_dils]
    res_shapes = [jax.ShapeDtypeStruct((t // seq, dil, seq // dil, d), F32) for dil in residue_dils]
    slab_scratch = [pltpu.VMEM((d // LANES, tm, LANES), F32)] if residue_dils else []
    return pl.pallas_call(
        functools.partial(_combine_kernel, tm=tm, residue_dils=tuple(residue_dils)),
        grid=(t // tm,),
        in_specs=[
            pl.BlockSpec(memory_space=pl.ANY),
            pl.BlockSpec(memory_space=pl.ANY),
            row,
            pl.BlockSpec((tm, TOP_K), lambda i: (i, 0)),
            pl.BlockSpec((d, f), lambda i: (0, 0)),
            pl.BlockSpec((d, f), lambda i: (0, 0)),
            pl.BlockSpec((f, d), lambda i: (0, 0)),
            vec,
            vec,
        ],
        out_specs=[row] + res_specs,
        out_shape=[jax.ShapeDtypeStruct((t, d), F32)] + res_shapes,
        scratch_shapes=[
            pltpu.SMEM((TOP_K * tm,), jnp.int32),
            pltpu.SMEM((TOP_K * tm,), jnp.int32),
            pltpu.VMEM((2, TOP_K, tm * nc, LANES), U32),
            pltpu.SemaphoreType.DMA((2,)),
            pltpu.SemaphoreType.DMA((2,)),
        ] + slab_scratch,
        compiler_params=_cparams(1),
        name="combine",
    )(dest_tiles, ys, x, w_tok, ws_gate, ws_up, ws_down, g.reshape(1, d), b.reshape(1, d))


def _tile_major(a, tm):
    k, t = a.shape
    return a.reshape(k, t // tm, tm).transpose(1, 0, 2).reshape(t // tm, k * tm)


def _dest_kernel(start_ref, e_ref, pos_ref, dest_ref, *, tm):
    ne = N_EXPERTS
    start = jnp.concatenate([start_ref[...]] * (tm // LANES), axis=1)
    rows = lax.broadcasted_iota(jnp.int32, (ne, tm), 0)
    for k in range(TOP_K):
        hit = rows == e_ref[pl.ds(k, 1), :]
        first = jnp.sum(jnp.where(hit, start, 0), axis=0, keepdims=True)
        dest_ref[pl.ds(k, 1), :] = first + pos_ref[pl.ds(k, 1), :]


def _dest(starts, top_e, pos, tm=512):
    t = top_e.shape[1]
    tok = pl.BlockSpec((TOP_K, tm), lambda i: (0, i))
    return pl.pallas_call(
        functools.partial(_dest_kernel, tm=tm),
        grid=(t // tm,),
        in_specs=[pl.BlockSpec((N_EXPERTS, LANES), lambda i: (0, 0)), tok, tok],
        out_specs=tok,
        out_shape=jax.ShapeDtypeStruct((TOP_K, t), jnp.int32),
        compiler_params=_cparams(1),
        name="dest",
    )(jnp.broadcast_to(starts[:, None], (N_EXPERTS, LANES)), top_e, pos)


def _expert_row_block(t):
    mean_rows = t * TOP_K // N_EXPERTS
    return max(LANES, -(-(mean_rows + mean_rows // 4) // LANES) * LANES)


def _moe_ln(x, x_tiles, layer, seq, w_router, router_bias, we_gate, we_up, we_down, ws_gate, ws_up, ws_down,
            g, b, residue_dils=()):
    t, d = x.shape
    rb = _expert_row_block(t)
    top_e, w, pos, counts = _router(x, w_router, router_bias)

    cnt = counts[:, 0].astype(jnp.int32)
    padded = (cnt + rb - 1) // rb * rb
    pend = jnp.cumsum(padded)
    n_blocks = t * TOP_K // rb + N_EXPERTS
    n_real = (pend[-1] // rb).astype(jnp.int32)
    blk_ids = jnp.minimum(jnp.arange(n_blocks, dtype=jnp.int32), jnp.maximum(n_real - 1, 0))
    blk_e = jnp.sum((pend[None, :] <= (blk_ids * rb)[:, None]).astype(jnp.int32), axis=1)
    blk_e = jnp.minimum(blk_e, N_EXPERTS - 1)

    dest = _dest(pend - padded, top_e, pos)
    xs = _dispatch(x_tiles, _tile_major(dest, DISPATCH_TOKENS), n_blocks * rb, d // TILE_WORDS,
                   tm=DISPATCH_TOKENS)
    ys = _experts(xs, we_gate, we_up, we_down, layer, rb, blk_e, blk_ids, n_real.reshape(1))
    return _combine(
        x, ys, _tile_major(dest, COMBINE_TOKENS), w.T, ws_gate.astype(BF16),
        ws_up.astype(BF16), ws_down.astype(BF16), g, b, tm=COMBINE_TOKENS, seq=seq, residue_dils=residue_dils,
    )


def kernel(x, w_qkv_a, w_qkv_b, w_o, ln_mix_g, ln_mix_b, w_router, router_bias, we_gate, we_up,
           we_down, ws_gate, ws_up, ws_down, ln_ffn_g, ln_ffn_b):
    bsz, s, d = x.shape
    t = bsz * s
    slopes = _alibi_slopes()
    xt = x.reshape(t, d)
    is_dilated = lambda layer: layer % N_MIXERS == 1
    strided_dils = tuple(dil for _, dil in DIL_GROUPS if dil > 1)
    x_res = {}
    for i in range(DEPTH):
        if not is_dilated(i):
            qkv = _proj(xt, w_qkv_a[i // N_MIXERS].astype(BF16))
            mix = _moba_attention(qkv.reshape(bsz, s, -1), slopes).reshape(t, d)
        else:
            w_b = w_qkv_b[i // N_MIXERS].astype(BF16)
            outs, lses = [], []
            for g, (window, dil) in enumerate(DIL_GROUPS):
                xg = x_res[dil] if dil in x_res else _residue_major(xt, bsz, dil)
                qt, kk, vt = _proj_residue(xg, w_b[:, g * 3 * d:(g + 1) * 3 * d])
                o, lse = _dilated_group(qt, kk, vt, slopes, g, window, dil)
                outs.append(o)
                lses.append(lse)
            mix = _merge_groups(outs, lses).reshape(t, d)
        x1, x1_tiles = _wo_ln(mix, w_o[i].astype(BF16), xt, ln_mix_g[i], ln_mix_b[i])
        dils = strided_dils if i + 1 < DEPTH and is_dilated(i + 1) else ()
        xt, *extra = _moe_ln(x1, x1_tiles, i, s, w_router[i], router_bias[i], we_gate, we_up, we_down,
                             ws_gate[i], ws_up[i], ws_down[i], ln_ffn_g[i], ln_ffn_b[i], residue_dils=dils)
        x_res = dict(zip(dils, extra))
    return xt.reshape(bsz, s, d)
```

```python
import functools
import math

import jax
import jax.numpy as jnp
import numpy as np
from jax import lax
from jax.experimental import pallas as pl
from jax.experimental.pallas import tpu as pltpu

N_HEADS = 16
HEAD_DIM = 64
DEPTH = 2
N_MIXERS = 2
MOBA_BLOCK = 256
MOBA_TOPK = 3
MOBA_KEY_BLOCKS_PER_STEP = 4
MOBA_QUERY_BLOCKS_PER_STEP = MOBA_KEY_BLOCKS_PER_STEP
MOBA_BIAS_PARTS = 3
DIL_GROUPS = ((128, 1), (512, 4), (2048, 16))
DIL_BLOCK = 128
N_EXPERTS = 256
TOP_K = 8
N_GROUPS = 8
TOPK_GROUPS = 4
ROUTED_SCALE = 2.5
DN_ALPHA = (2 * DEPTH) ** 0.25
LN_EPS = 1e-5
NEG = -1e30
LOG2E = math.log2(math.e)

LANES = 128
BF16_SUBLANES = 16
HEADS_PER_SLAB = LANES // HEAD_DIM
N_SLABS = N_HEADS // HEADS_PER_SLAB
VMEM_LIMIT_BYTES = 48 * 1024 * 1024
MOBA_ACC_ROWS = HEAD_DIM + BF16_SUBLANES

DISPATCH_TOKENS = 512
COMBINE_TOKENS = 512

BF16 = jnp.bfloat16
F32 = jnp.float32
U32 = jnp.uint32


def _cparams(n_axes):
    return pltpu.CompilerParams(
        dimension_semantics=("arbitrary",) * n_axes, vmem_limit_bytes=VMEM_LIMIT_BYTES
    )


def _alibi_slopes():
    return 2.0 ** (-8.0 * jnp.arange(1, N_HEADS + 1, dtype=F32) / N_HEADS)


def _bf16_parts(x, n):
    parts, rest = [], x
    for _ in range(n):
        top = lax.bitcast_convert_type(lax.bitcast_convert_type(rest, jnp.uint32) & np.uint32(0xFFFF0000), F32)
        parts.append(top.astype(BF16))
        rest = rest - top
    return parts


def _proj_kernel(x_ref, w_ref, o_ref):
    o_ref[...] = jnp.dot(
        x_ref[...].astype(BF16), w_ref[...], preferred_element_type=F32
    ).astype(o_ref.dtype)


def _proj(x, w, tm=1024, tn=1024):
    t, k = x.shape
    n = w.shape[1]
    return pl.pallas_call(
        _proj_kernel,
        grid=(t // tm, n // tn),
        in_specs=[
            pl.BlockSpec((tm, k), lambda i, j: (i, 0)),
            pl.BlockSpec((k, tn), lambda i, j: (0, j)),
        ],
        out_specs=pl.BlockSpec((tm, tn), lambda i, j: (i, j)),
        out_shape=jax.ShapeDtypeStruct((t, n), BF16),
        compiler_params=_cparams(2),
        name="proj",
    )(x, w)


def _moba_kernel(slopes_ref, q_ref, qn_ref, k_ref, v_ref, wb_ref, o_ref, ka_ref, vt_ref, km_ref, sel_ref,
                 s_ref, *, n_blocks):
    blk = MOBA_BLOCK
    unroll = MOBA_KEY_BLOCKS_PER_STEP
    span = unroll * blk
    acc_rows = MOBA_ACC_ROWS
    h2 = pl.program_id(1)
    i = pl.program_id(2)
    heads = range(HEADS_PER_SLAB)

    @pl.when(i == 0)
    def _prepare_keys():
        lane = lax.broadcasted_iota(jnp.int32, (blk, LANES), 1)
        row_f = lax.broadcasted_iota(jnp.int32, (blk, LANES), 0).astype(F32)
        tail = lax.broadcasted_iota(jnp.int32, (acc_rows - HEAD_DIM, blk), 0)
        ones_row = jnp.where(tail == 0, 1.0, 0.0)

        def body(n, c):
            r0 = pl.multiple_of(n * blk, blk)
            vt = v_ref[0, pl.ds(r0, blk), :].astype(F32).T
            for hh in heads:
                vt_ref[n, hh] = jnp.concatenate(
                    [vt[hh * HEAD_DIM:(hh + 1) * HEAD_DIM], ones_row], axis=0).astype(BF16)
            kb = k_ref[0, pl.ds(r0, blk), :]
            km_ref[pl.ds(n, 1), :] = jnp.mean(kb.astype(F32), axis=0, keepdims=True)
            sub = jnp.asarray(n % unroll, F32)
            aux = jnp.where(lane < MOBA_BIAS_PARTS, sub, jnp.where(lane < 2 * MOBA_BIAS_PARTS, row_f, 0.0))
            ka_ref[pl.ds(r0, blk), :] = jnp.concatenate([kb, aux.astype(BF16)], axis=1)
            return c

        lax.fori_loop(0, n_blocks, body, 0)

    qb = MOBA_QUERY_BLOCKS_PER_STEP
    wide = qb * blk
    i_first = pl.program_id(2) * qb
    feat = lax.broadcasted_iota(jnp.int32, (LANES, wide), 0)
    kmean = km_ref[...]
    km_lane = lax.broadcasted_iota(jnp.int32, kmean.shape, 1)
    blk_row = lax.broadcasted_iota(jnp.int32, (n_blocks, wide), 0)
    own_blk = i_first + lax.broadcasted_iota(jnp.int32, (n_blocks, wide), 1) // blk

    def per_head_qt(ref):
        qt = ref[0].astype(F32).T
        return [jnp.where((feat >= hh * HEAD_DIM) & (feat < (hh + 1) * HEAD_DIM), qt, 0.0) for hh in heads]

    def score_operand(qt_hs):
        scaled = [(x * (HEAD_DIM ** -0.5 * LOG2E)).astype(BF16) for x in qt_hs]
        return jnp.concatenate([jnp.concatenate(scaled, axis=1), wb_ref[0]], axis=0)

    slopes = [slopes_ref[h2 * HEADS_PER_SLAB + hh] for hh in heads]
    qt_hs = per_head_qt(q_ref)
    for hh in heads:
        qt_h = qt_hs[hh]
        km_h = jnp.where((km_lane >= hh * HEAD_DIM) & (km_lane < (hh + 1) * HEAD_DIM), kmean, 0.0)
        qt_b = qt_h.astype(BF16)
        gate = sum(jnp.dot(part, qt_b, preferred_element_type=F32) for part in _bf16_parts(km_h, 3))
        g = jnp.where(blk_row < own_blk, gate, -jnp.inf)
        sel = jnp.zeros(g.shape, F32)
        for _ in range(MOBA_TOPK):
            gmax = jnp.max(g, axis=0, keepdims=True)
            first = jnp.min(jnp.where(g == gmax, blk_row, n_blocks), axis=0, keepdims=True)
            hit = blk_row == first
            sel = jnp.where(hit & (gmax > -jnp.inf), 1.0, sel)
            g = jnp.where(hit, -jnp.inf, g)
        sel_ref[hh] = sel

    w_all = score_operand(qt_hs)
    w_next = score_operand(per_head_qt(qn_ref))

    groups = [(hh, q) for hh in heads for q in range(qb)]
    group_cols = [slice((hh * qb + q) * blk, (hh * qb + q + 1) * blk) for hh, q in groups]

    def scores(span_idx, cols, w=w_all):
        r0 = pl.multiple_of(span_idx * span, span)
        return jnp.dot(ka_ref[pl.ds(r0, span), :], w[:, cols], preferred_element_type=F32)

    key_row = lax.broadcasted_iota(jnp.int32, (blk, blk), 0)
    qry_col = lax.broadcasted_iota(jnp.int32, (blk, blk), 1)

    def consume(gi, span_idx, m, acc, last):
        hh, q = groups[gi]
        j0 = span_idx * unroll
        c = slopes[hh] * LOG2E * jnp.asarray(blk * (j0 - (i_first + q)), F32)
        for u in range(unroll):
            if last and u > q:
                continue
            j = j0 + u
            s = s_ref[u * blk:(u + 1) * blk, group_cols[gi]]
            if last and u == q:
                s = jnp.where(key_row <= qry_col, s, NEG)
                m_new = jnp.maximum(m, jnp.max(s, axis=0, keepdims=True) + c)
                p = jnp.exp2(s - (m_new - c))
            else:
                ch = sel_ref[hh, pl.ds(j, 1), q * blk:(q + 1) * blk] > 0.5
                m_new = jnp.where(ch, jnp.maximum(m, jnp.max(s, axis=0, keepdims=True) + c), m)
                p = jnp.exp2(s - jnp.where(ch, m_new - c, -NEG))
            acc = jnp.exp2(m - m_new) * acc + jnp.dot(
                vt_ref[j, hh], p.astype(BF16), preferred_element_type=F32)
            m = m_new
        return m, acc

    n_full = i_first // unroll

    @pl.when(i == 0)
    def _first_scores():
        s_ref[...] = scores(0, slice(None))

    def body(it, state):
        out = []
        for gi in range(len(groups)):
            out.append(consume(gi, it, *state[gi], last=False))
            s_ref[:, group_cols[gi]] = scores(it + 1, group_cols[gi])
        return tuple(out)

    init = tuple((jnp.full((1, blk), NEG, F32), jnp.zeros((acc_rows, blk), F32)) for _ in groups)
    state = lax.fori_loop(0, n_full, body, init)
    final = []
    for gi in range(len(groups)):
        final.append(consume(gi, n_full, *state[gi], last=True))
        s_ref[:, group_cols[gi]] = scores(0, group_cols[gi], w_next)
    state = final

    outs = [acc[:HEAD_DIM] / acc[HEAD_DIM:HEAD_DIM + 1] for _, acc in state]
    per_block = [jnp.concatenate([outs[hh * qb + q] for hh in heads], axis=0) for q in range(qb)]
    o_ref[0] = jnp.concatenate(per_block, axis=1).T.astype(o_ref.dtype)


def _moba_attention(qkv, slopes):
    b, s, _ = qkv.shape
    blk = MOBA_BLOCK
    n_blocks = s // blk
    qb = MOBA_QUERY_BLOCKS_PER_STEP
    assert n_blocks % MOBA_KEY_BLOCKS_PER_STEP == 0 and MOBA_KEY_BLOCKS_PER_STEP == qb
    wide = HEADS_PER_SLAB * qb * blk
    grid_spec = pltpu.PrefetchScalarGridSpec(
        num_scalar_prefetch=1,
        grid=(b, N_SLABS, n_blocks // qb),
        in_specs=[
            pl.BlockSpec((1, qb * blk, LANES), lambda bi, h, i, sl: (bi, i, h)),
            pl.BlockSpec((1, qb * blk, LANES), lambda bi, h, i, sl: (bi, jnp.minimum(i + 1, n_blocks // qb - 1), h)),
            pl.BlockSpec((1, s, LANES), lambda bi, h, i, sl: (bi, 0, N_SLABS + h)),
            pl.BlockSpec((1, s, LANES), lambda bi, h, i, sl: (bi, 0, 2 * N_SLABS + h)),
            pl.BlockSpec((1, LANES, wide), lambda bi, h, i, sl: (h, 0, 0)),
        ],
        out_specs=pl.BlockSpec((1, qb * blk, LANES), lambda bi, h, i, sl: (bi, i, h)),
        scratch_shapes=[
            pltpu.VMEM((s, 2 * LANES), BF16),
            pltpu.VMEM((n_blocks, HEADS_PER_SLAB, MOBA_ACC_ROWS, blk), BF16),
            pltpu.VMEM((n_blocks, LANES), F32),
            pltpu.VMEM((HEADS_PER_SLAB, n_blocks, qb * blk), F32),
            pltpu.VMEM((MOBA_KEY_BLOCKS_PER_STEP * blk, wide), F32),
        ],
    )
    return pl.pallas_call(
        functools.partial(_moba_kernel, n_blocks=n_blocks),
        grid_spec=grid_spec,
        out_shape=jax.ShapeDtypeStruct((b, s, N_HEADS * HEAD_DIM), BF16),
        compiler_params=_cparams(3),
        name="moba",
    )(slopes, qkv, qkv, qkv, qkv, _moba_slope_rows(slopes))


def _moba_slope_rows(slopes):
    blk = MOBA_BLOCK
    wide = MOBA_QUERY_BLOCKS_PER_STEP * blk
    parts = _bf16_parts(slopes * LOG2E, MOBA_BIAS_PARTS)
    sp = jnp.stack([p.astype(F32) for p in parts], axis=1)
    rows = jnp.concatenate([sp * blk, sp], axis=1).reshape(N_SLABS, HEADS_PER_SLAB, 2 * MOBA_BIAS_PARTS)
    wb = jnp.zeros((N_SLABS, LANES, HEADS_PER_SLAB * wide), F32)
    for hh in range(HEADS_PER_SLAB):
        wb = wb.at[:, :2 * MOBA_BIAS_PARTS, hh * wide:(hh + 1) * wide].set(rows[:, hh, :, None])
    return wb.astype(BF16)


def _dilated_kernel(bias_ref, qt_ref, kc_ref, kp_ref, vtc_ref, vtp_ref, o_ref, lse_ref, lse_s, s_scr):
    blk = DIL_BLOCK
    lse_s[...] = jnp.zeros(lse_s.shape, F32)
    tail = lax.broadcasted_iota(jnp.int32, (MOBA_ACC_ROWS - HEAD_DIM, 2 * blk), 0)
    ones_row = jnp.where(tail == 0, 1.0, 0.0).astype(BF16)
    feat = lax.broadcasted_iota(jnp.int32, (LANES, blk), 0)
    for h2 in range(N_SLABS):
        cols = slice(h2 * LANES, (h2 + 1) * LANES)
        qt = qt_ref[0, 0, cols, :].astype(F32) * (HEAD_DIM ** -0.5 * LOG2E)
        kband = jnp.concatenate([kp_ref[0, 0, :, cols], kc_ref[0, 0, :, cols]], axis=0)
        qt_both = jnp.concatenate(
            [jnp.where((feat >= hh * HEAD_DIM) & (feat < (hh + 1) * HEAD_DIM), qt, 0.0)
             for hh in range(HEADS_PER_SLAB)], axis=1).astype(BF16)
        s_scr[h2] = jnp.dot(kband, qt_both, preferred_element_type=F32)
    for h2 in range(N_SLABS):
        cols = slice(h2 * LANES, (h2 + 1) * LANES)
        vt = jnp.concatenate([vtp_ref[0, 0, cols, :], vtc_ref[0, 0, cols, :]], axis=1)
        outs = []
        for hh in range(HEADS_PER_SLAB):
            h = h2 * HEADS_PER_SLAB + hh
            s = s_scr[h2, :, hh * blk:(hh + 1) * blk] + bias_ref[0, h]
            m = jnp.max(s, axis=0, keepdims=True)
            p = jnp.exp2(s - m).astype(BF16)
            vt_h = jnp.concatenate([vt[hh * HEAD_DIM:(hh + 1) * HEAD_DIM], ones_row], axis=0)
            acc = jnp.dot(vt_h, p, preferred_element_type=F32)
            l = acc[HEAD_DIM:HEAD_DIM + 1]
            outs.append(acc[:HEAD_DIM] / l)
            lse_s[pl.ds(h, 1), :] = (m + jnp.log2(l)) * (1.0 / LOG2E)
        o_ref[0, 0, :, cols] = jnp.concatenate(outs, axis=0).T.astype(o_ref.dtype)
    lse_ref[0, 0] = lse_s[...].T


def _dilated_bias(slopes, dil):
    blk = DIL_BLOCK
    key = jnp.arange(2 * blk)[:, None]
    steps = jnp.arange(blk)[None, :] - (key - blk)
    steps_max = blk
    valid = (steps >= 0) & (steps <= steps_max)
    bias = -(slopes * LOG2E)[:, None, None] * (steps * dil).astype(F32)[None]
    rest = jnp.where(valid[None], bias, NEG)
    first = jnp.where((key >= blk)[None], rest, NEG)
    return jnp.stack([first, rest]).astype(F32)


def _proj_residue_kernel(x_ref, wqt_ref, wk_ref, wvt_ref, qt_ref, k_ref, vt_ref):
    xb = x_ref[0, 0].astype(BF16)
    nt = (((1,), (1,)), ((), ()))
    qt_ref[0, 0] = lax.dot_general(wqt_ref[...], xb, nt, preferred_element_type=F32).astype(qt_ref.dtype)
    k_ref[0, 0] = jnp.dot(xb, wk_ref[...], preferred_element_type=F32).astype(k_ref.dtype)
    vt_ref[0, 0] = lax.dot_general(wvt_ref[...], xb, nt, preferred_element_type=F32).astype(vt_ref.dtype)


def _residue_major(x, bsz, dil):
    t, k = x.shape
    return x.reshape(bsz, t // bsz // dil, dil, k).transpose(0, 2, 1, 3)


def _proj_residue(x_res, w, tm=512):
    bsz, dil, l_sub, k = x_res.shape
    d = w.shape[1] // 3
    tm = min(tm, l_sub)
    const = lambda shape: pl.BlockSpec(shape, lambda b, r, m: (0, 0))
    tr = pl.BlockSpec((1, 1, d, tm), lambda b, r, m: (b, r, 0, m))
    return pl.pallas_call(
        _proj_residue_kernel,
        grid=(bsz, dil, l_sub // tm),
        in_specs=[pl.BlockSpec((1, 1, tm, k), lambda b, r, m: (b, r, m, 0)), const((d, k)), const((k, d)),
                  const((d, k))],
        out_specs=[tr, pl.BlockSpec((1, 1, tm, d), lambda b, r, m: (b, r, m, 0)), tr],
        out_shape=[
            jax.ShapeDtypeStruct((bsz, dil, d, l_sub), BF16),
            jax.ShapeDtypeStruct((bsz, dil, l_sub, d), BF16),
            jax.ShapeDtypeStruct((bsz, dil, d, l_sub), BF16),
        ],
        compiler_params=_cparams(3),
        name=f"proj_residue_d{dil}",
    )(x_res, w[:, :d].T, w[:, d:2 * d], w[:, 2 * d:].T)


def _dilated_group(qt, k, vt, slopes, g, window, dil):
    b, _, l_sub, d = k.shape
    blk = DIL_BLOCK
    assert window // dil == blk
    nb = l_sub // blk
    bias = _dilated_bias(slopes, dil)
    prev = lambda n: jnp.maximum(n - 1, 0)

    o, lse = pl.pallas_call(
        _dilated_kernel,
        grid=(b, dil, nb),
        in_specs=[
            pl.BlockSpec((1, N_HEADS, 2 * blk, blk), lambda bi, r, n: (jnp.minimum(n, 1), 0, 0, 0)),
            pl.BlockSpec((1, 1, d, blk), lambda bi, r, n: (bi, r, 0, n)),
            pl.BlockSpec((1, 1, blk, d), lambda bi, r, n: (bi, r, n, 0)),
            pl.BlockSpec((1, 1, blk, d), lambda bi, r, n: (bi, r, prev(n), 0)),
            pl.BlockSpec((1, 1, d, blk), lambda bi, r, n: (bi, r, 0, n)),
            pl.BlockSpec((1, 1, d, blk), lambda bi, r, n: (bi, r, 0, prev(n))),
        ],
        out_specs=[
            pl.BlockSpec((1, 1, blk, d), lambda bi, r, n: (bi, r, n, 0)),
            pl.BlockSpec((1, 1, blk, LANES), lambda bi, r, n: (bi, r, n, 0)),
        ],
        out_shape=[
            jax.ShapeDtypeStruct((b, dil, l_sub, d), BF16),
            jax.ShapeDtypeStruct((b, dil, l_sub, LANES), F32),
        ],
        scratch_shapes=[pltpu.VMEM((LANES, blk), F32), pltpu.VMEM((N_SLABS, 2 * blk, 2 * blk), F32)],
        compiler_params=_cparams(3),
        name=f"dilated_g{g}",
    )(bias, qt, k, k, vt, vt)
    return o, lse


def _merge_kernel(e_ref, *refs):
    ng = len(DIL_GROUPS)
    o_refs, l_refs = refs[:ng], refs[ng:2 * ng]
    out_ref, lse_nat, o_nat = refs[2 * ng:]

    def to_natural(dst, src, dil):
        n = src.shape[2]
        for r in range(dil):
            rows = src[0, r].astype(F32)
            for c in range(dst.shape[0]):
                dst[c, pl.ds(r, n, stride=dil), :] = rows[:, c * LANES:(c + 1) * LANES]

    for g, (_, dil) in enumerate(DIL_GROUPS):
        to_natural(lse_nat.at[pl.ds(g, 1)], l_refs[g], dil)
    ls = [lse_nat[g] for g in range(ng)]
    mx = functools.reduce(jnp.maximum, ls)
    es = [jnp.exp(x - mx) for x in ls]
    den = functools.reduce(lambda a, b: a + b, es)
    acc = None
    for g, (_, dil) in enumerate(DIL_GROUPS):
        to_natural(o_nat, o_refs[g], dil)
        w = sum(jnp.dot(part, e_ref[...], preferred_element_type=F32) for part in _bf16_parts(es[g] / den, 2))
        term = w * jnp.concatenate([o_nat[c] for c in range(o_nat.shape[0])], axis=1)
        acc = term if acc is None else acc + term
    out_ref[0] = acc.astype(out_ref.dtype)


def _merge_groups(os, lses, tm=512):
    b, dil0, l0, d = os[0].shape
    s = dil0 * l0
    expand = (jnp.arange(LANES)[:, None] == (jnp.arange(d)[None, :] // HEAD_DIM)).astype(BF16)

    def grp(width, dil):
        return pl.BlockSpec((1, dil, tm // dil, width), lambda bi, i: (bi, 0, i, 0))

    dils = [dil for _, dil in DIL_GROUPS]
    return pl.pallas_call(
        _merge_kernel,
        grid=(b, s // tm),
        in_specs=[pl.BlockSpec((LANES, d), lambda bi, i: (0, 0))] + [grp(d, dil) for dil in dils]
        + [grp(LANES, dil) for dil in dils],
        out_specs=pl.BlockSpec((1, tm, d), lambda bi, i: (bi, i, 0)),
        out_shape=jax.ShapeDtypeStruct((b, s, d), BF16),
        scratch_shapes=[pltpu.VMEM((len(dils), tm, LANES), F32), pltpu.VMEM((d // LANES, tm, LANES), F32)],
        compiler_params=_cparams(2),
        name="merge_groups",
    )(expand, *os, *lses)


def _layer_norm(y, g, b):
    mu = jnp.mean(y, axis=-1, keepdims=True)
    yc = y - mu
    var = jnp.mean(yc * yc, axis=-1, keepdims=True)
    return yc * lax.rsqrt(var + LN_EPS) * g + b


TILE_WORDS = 2 * LANES
HIGH_HALF = np.uint32(0xFFFF0000)


def _pack_token_tiles(ref, y):
    n, d = y.shape
    npk = d // TILE_WORDS
    for c in range(npk):
        lo = y[:, (2 * c) * LANES:(2 * c + 1) * LANES].astype(BF16).astype(F32)
        hi = y[:, (2 * c + 1) * LANES:(2 * c + 2) * LANES].astype(BF16).astype(F32)
        word = (lax.bitcast_convert_type(lo, U32) >> 16) | (lax.bitcast_convert_type(hi, U32) & HIGH_HALF)
        ref[pl.ds(c, n, stride=npk), :] = word


def _unpack_token_tiles(ref, n, npk, lead=()):
    blocks = []
    for c in range(npk):
        word = ref[lead + (pl.ds(c, n, stride=npk), slice(None))]
        blocks.append(lax.bitcast_convert_type(word << 16, F32))
        blocks.append(lax.bitcast_convert_type(word & HIGH_HALF, F32))
    return jnp.concatenate(blocks, axis=1)


def _wo_ln_kernel(mix_ref, wo_ref, x_ref, g_ref, b_ref, o_ref, ot_ref):
    f = jnp.dot(mix_ref[...], wo_ref[...], preferred_element_type=F32)
    y = _layer_norm(DN_ALPHA * x_ref[...] + f, g_ref[...], b_ref[...])
    o_ref[...] = y
    _pack_token_tiles(ot_ref, y)


def _wo_ln(mix, wo, x, g, b, tm=512):
    t, d = x.shape
    npk = d // TILE_WORDS
    row = pl.BlockSpec((tm, d), lambda i: (i, 0))
    vec = pl.BlockSpec((1, d), lambda i: (0, 0))
    return pl.pallas_call(
        _wo_ln_kernel,
        grid=(t // tm,),
        in_specs=[row, pl.BlockSpec((d, d), lambda i: (0, 0)), row, vec, vec],
        out_specs=[row, pl.BlockSpec((tm * npk, LANES), lambda i: (i, 0))],
        out_shape=[jax.ShapeDtypeStruct((t, d), F32), jax.ShapeDtypeStruct((t * npk, LANES), U32)],
        compiler_params=_cparams(1),
        name="wo_ln",
    )(mix, wo, x, g.reshape(1, d), b.reshape(1, d))


def _router_kernel(wr_ref, rb_ref, x_ref, e_ref, w_ref, pos_ref, cnt_ref, run_ref, *, tm):
    step = pl.program_id(0)
    ne = N_EXPERTS
    gs = ne // N_GROUPS

    @pl.when(step == 0)
    def _():
        run_ref[...] = jnp.zeros(run_ref.shape, F32)

    x = x_ref[...]
    x_hi = x.astype(BF16)
    x_lo = (x - x_hi.astype(F32)).astype(BF16)
    nt = (((1,), (1,)), ((), ()))
    logits = (lax.dot_general(wr_ref[0], x_hi, nt, preferred_element_type=F32)
              + lax.dot_general(wr_ref[0], x_lo, nt, preferred_element_type=F32)
              + lax.dot_general(wr_ref[1], x_hi, nt, preferred_element_type=F32))
    scores = jax.nn.sigmoid(logits)
    choice = scores + rb_ref[...][:, :1]

    grp_rows = lax.broadcasted_iota(jnp.int32, (gs, tm), 0)
    blocks, gscore = [], []
    for gi in range(N_GROUPS):
        cb = choice[gi * gs:(gi + 1) * gs]
        m1 = jnp.max(cb, axis=0, keepdims=True)
        i1 = jnp.min(jnp.where(cb == m1, grp_rows, gs), axis=0, keepdims=True)
        m2 = jnp.max(jnp.where(grp_rows == i1, -jnp.inf, cb), axis=0, keepdims=True)
        blocks.append(cb)
        gscore.append(m1 + m2)
    masked = []
    for gi in range(N_GROUPS):
        beaten = jnp.zeros((1, tm), jnp.int32)
        for gj in range(N_GROUPS):
            if gj == gi:
                continue
            wins = (gscore[gj] > gscore[gi]) | ((gscore[gj] == gscore[gi]) & (gj < gi))
            beaten = beaten + wins.astype(jnp.int32)
        masked.append(jnp.where(beaten < TOPK_GROUPS, blocks[gi], NEG))
    cand = jnp.concatenate(masked, axis=0)

    rows = lax.broadcasted_iota(jnp.int32, (ne, tm), 0)
    member = jnp.zeros((ne, tm), F32)
    picks, raw_w = [], []
    for _ in range(TOP_K):
        cmax = jnp.max(cand, axis=0, keepdims=True)
        first = jnp.min(jnp.where(cand == cmax, rows, ne), axis=0, keepdims=True)
        hit = rows == first
        raw_w.append(jnp.sum(jnp.where(hit, scores, 0.0), axis=0, keepdims=True))
        member = jnp.where(hit, 1.0, member)
        cand = jnp.where(hit, -jnp.inf, cand)
        picks.append(first)
    wsum = raw_w[0]
    for r in raw_w[1:]:
        wsum = wsum + r

    tri = (lax.broadcasted_iota(jnp.int32, (tm, tm), 0) < lax.broadcasted_iota(jnp.int32, (tm, tm), 1))
    member_b = member.astype(BF16)
    before = jnp.dot(member_b, tri.astype(BF16), preferred_element_type=F32)
    run = run_ref[...]
    rank = before + jnp.concatenate([run] * (tm // LANES), axis=1)
    for k in range(TOP_K):
        hit = rows == picks[k]
        e_ref[pl.ds(k, 1), :] = picks[k]
        w_ref[pl.ds(k, 1), :] = raw_w[k] / wsum * ROUTED_SCALE
        pos_ref[pl.ds(k, 1), :] = jnp.sum(jnp.where(hit, rank, 0.0), axis=0, keepdims=True).astype(jnp.int32)
    run = run + jnp.dot(member_b, jnp.ones((tm, LANES), BF16), preferred_element_type=F32)
    run_ref[...] = run
    cnt_ref[...] = run


def _router(x, w_router, router_bias, tm=256):
    t, d = x.shape
    ne = N_EXPERTS
    wr_parts = jnp.stack(_bf16_parts(w_router.T.astype(F32), 2))
    rb = jnp.broadcast_to(router_bias.astype(F32)[:, None], (ne, LANES))
    tok = pl.BlockSpec((TOP_K, tm), lambda i: (0, i))
    return pl.pallas_call(
        functools.partial(_router_kernel, tm=tm),
        grid=(t // tm,),
        in_specs=[
            pl.BlockSpec((2, ne, d), lambda i: (0, 0, 0)),
            pl.BlockSpec((ne, LANES), lambda i: (0, 0)),
            pl.BlockSpec((tm, d), lambda i: (i, 0)),
        ],
        out_specs=[tok, tok, tok, pl.BlockSpec((ne, LANES), lambda i: (0, 0))],
        out_shape=[
            jax.ShapeDtypeStruct((TOP_K, t), jnp.int32),
            jax.ShapeDtypeStruct((TOP_K, t), F32),
            jax.ShapeDtypeStruct((TOP_K, t), jnp.int32),
            jax.ShapeDtypeStruct((ne, LANES), F32),
        ],
        scratch_shapes=[pltpu.VMEM((ne, LANES), F32)],
        compiler_params=_cparams(1),
        name="router",
    )(wr_parts, rb, x)


def _tile_rows(ref, i, npk):
    return ref.at[pl.ds(pl.multiple_of(i * npk, npk), npk)]


def _dispatch_kernel(dest_hbm, x_ref, xs_hbm, dest_smem0, dest_smem1, sem_idx, sem_rows, *, tm):
    step = pl.program_id(0)
    n_steps = pl.num_programs(0)
    npk = x_ref.shape[0] // tm
    dest_smems = (dest_smem0, dest_smem1)

    def idx_copy(tile, sl):
        return pltpu.make_async_copy(dest_hbm.at[tile], dest_smems[sl], sem_idx.at[sl])

    def row_copy(t, d):
        return pltpu.make_async_copy(_tile_rows(x_ref, t, npk), _tile_rows(xs_hbm, d, npk), sem_rows)

    @pl.when(step == 0)
    def _():
        idx_copy(0, 0).start()

    for sl in range(2):
        @pl.when(step % 2 == sl)
        def _(sl=sl):
            idx_copy(step, sl).wait()

            @pl.when(step + 1 < n_steps)
            def _():
                idx_copy(step + 1, 1 - sl).start()

            def issue(t, c):
                for k in range(TOP_K):
                    row_copy(t, dest_smems[sl][k * tm + t]).start(priority=k % 2)
                return c

            lax.fori_loop(0, tm, issue, 0)

    def drain(t, c):
        for _ in range(TOP_K):
            row_copy(0, 0).wait()
        return c

    lax.fori_loop(0, tm, drain, 0)


def _dispatch(x_tiles, dest_tiles, n_rows, npk, tm):
    t = x_tiles.shape[0] // npk
    return pl.pallas_call(
        functools.partial(_dispatch_kernel, tm=tm),
        grid=(t // tm,),
        in_specs=[pl.BlockSpec(memory_space=pl.ANY), pl.BlockSpec((tm * npk, LANES), lambda i: (i, 0))],
        out_specs=pl.BlockSpec(memory_space=pl.ANY),
        out_shape=jax.ShapeDtypeStruct((n_rows * npk, LANES), x_tiles.dtype),
        scratch_shapes=[
            pltpu.SMEM((TOP_K * tm,), jnp.int32),
            pltpu.SMEM((TOP_K * tm,), jnp.int32),
            pltpu.SemaphoreType.DMA((2,)),
            pltpu.SemaphoreType.DMA,
        ],
        compiler_params=_cparams(1),
        name="dispatch",
    )(dest_tiles, x_tiles)


def _experts_kernel(blk_e_ref, blk_row_ref, n_real_ref, xs_ref, wg_ref, wu_ref, wd_ref, ys_ref, *, rb):
    @pl.when(pl.program_id(0) < n_real_ref[0])
    def _():
        npk = xs_ref.shape[0] // rb
        xb = _unpack_token_tiles(xs_ref, rb, npk).astype(BF16)
        gate = jnp.dot(xb, wg_ref[0, 0].astype(BF16), preferred_element_type=F32)
        up = jnp.dot(xb, wu_ref[0, 0].astype(BF16), preferred_element_type=F32)
        h = (gate * jax.nn.sigmoid(gate) * up).astype(BF16)
        _pack_token_tiles(ys_ref, jnp.dot(h, wd_ref[0, 0].astype(BF16), preferred_element_type=F32))


def _experts(xs, we_gate, we_up, we_down, layer, rb, blk_e, blk_row, n_real):
    d, f = we_gate.shape[-2:]
    nc = d // TILE_WORDS
    p = xs.shape[0] // nc
    rows = pl.BlockSpec((rb * nc, LANES), lambda i, be, br, nr: (br[i], 0))
    grid_spec = pltpu.PrefetchScalarGridSpec(
        num_scalar_prefetch=3,
        grid=(p // rb,),
        in_specs=[
            rows,
            pl.BlockSpec((1, 1, d, f), lambda i, be, br, nr: (layer, be[i], 0, 0)),
            pl.BlockSpec((1, 1, d, f), lambda i, be, br, nr: (layer, be[i], 0, 0)),
            pl.BlockSpec((1, 1, f, d), lambda i, be, br, nr: (layer, be[i], 0, 0)),
        ],
        out_specs=rows,
    )
    return pl.pallas_call(
        functools.partial(_experts_kernel, rb=rb),
        grid_spec=grid_spec,
        out_shape=jax.ShapeDtypeStruct((p * nc, LANES), U32),
        compiler_params=_cparams(1),
        name="experts",
    )(blk_e, blk_row, n_real, xs, we_gate, we_up, we_down)


def _combine_kernel(dest_hbm, ys_hbm, x_ref, w_ref, sg_ref, su_ref, sd_ref, g_ref, b_ref, o_ref, *refs,
                    tm, residue_dils):
    res_refs = refs[:len(residue_dils)]
    dest_smem0, dest_smem1, rows_ref, sem_idx, sem_rows = refs[len(residue_dils):len(residue_dils) + 5]
    dest_smems = (dest_smem0, dest_smem1)
    step = pl.program_id(0)
    n_steps = pl.num_programs(0)
    npk = rows_ref.shape[2] // tm
    slot = step % 2

    def row_copy(sl, k, t, d):
        return pltpu.make_async_copy(
            _tile_rows(ys_hbm, d, npk), _tile_rows(rows_ref.at[sl, k], t, npk), sem_rows.at[sl])

    def idx_copy(tile, sl):
        return pltpu.make_async_copy(dest_hbm.at[tile], dest_smems[sl], sem_idx.at[sl])

    def issue_rows(sl):
        dest_smem = dest_smems[sl]

        def issue(t, c):
            for k in range(TOP_K):
                row_copy(sl, k, t, dest_smem[k * tm + t]).start(priority=k % 2)
            return c

        lax.fori_loop(0, tm, issue, 0)

    @pl.when(step == 0)
    def _():
        idx_copy(0, 0).start()
        idx_copy(0, 0).wait()
        issue_rows(0)

        @pl.when(n_steps > 1)
        def _():
            idx_copy(1, 1).start()

    for sl in range(2):
        @pl.when((step + 1 < n_steps) & (slot == 1 - sl))
        def _(sl=sl):
            idx_copy(step + 1, sl).wait()

            @pl.when(step + 2 < n_steps)
            def _():
                idx_copy(step + 2, 1 - sl).start()

            issue_rows(sl)

    x = x_ref[...]
    xb = x.astype(BF16)
    gate = jnp.dot(xb, sg_ref[...], preferred_element_type=F32)
    up = jnp.dot(xb, su_ref[...], preferred_element_type=F32)
    h = (gate * jax.nn.sigmoid(gate) * up).astype(BF16)
    f = jnp.dot(h, sd_ref[...], preferred_element_type=F32)

    def drain(t, c):
        for _ in range(TOP_K):
            row_copy(slot, 0, 0, 0).wait()
        return c

    lax.fori_loop(0, tm, drain, 0)

    w = w_ref[...]
    for k in range(TOP_K):
        f = f + w[:, k:k + 1] * _unpack_token_tiles(rows_ref, tm, npk, lead=(slot, k))
    y = _layer_norm(DN_ALPHA * x + f, g_ref[...], b_ref[...])
    o_ref[...] = y
    if residue_dils:
        slabs = refs[-1]
        n_slabs = slabs.shape[0]
        for c in range(n_slabs):
            slabs[c] = y[:, c * LANES:(c + 1) * LANES]
        for res_ref, dil in zip(res_refs, residue_dils):
            for r in range(dil):
                res_ref[0, r] = jnp.concatenate(
                    [slabs[c, pl.ds(r, tm // dil, stride=dil), :] for c in range(n_slabs)], axis=1)


def _combine(x, ys, dest_tiles, w_tok, ws_gate, ws_up, ws_down, g, b, tm, seq, residue_dils=()):
    t, d = x.shape
    nc = d // TILE_WORDS
    f = ws_gate.shape[-1]
    tiles_per_seq = seq // tm
    row = pl.BlockSpec((tm, d), lambda i: (i, 0))
    vec = pl.BlockSpec((1, d), lambda i: (0, 0))
    res_specs = [pl.BlockSpec((1, dil, tm // dil, d), lambda i: (i // tiles_per_seq, 0, i % tiles_per_seq, 0))
                 for dil in residue_dils]
    res_shapes = [jax.ShapeDtypeStruct((t // seq, dil, seq // dil, d), F32) for dil in residue_dils]
    slab_scratch = [pltpu.VMEM((d // LANES, tm, LANES), F32)] if residue_dils else []
    return pl.pallas_call(
        functools.partial(_combine_kernel, tm=tm, residue_dils=tuple(residue_dils)),
        grid=(t // tm,),
        in_specs=[
            pl.BlockSpec(memory_space=pl.ANY),
            pl.BlockSpec(memory_space=pl.ANY),
            row,
            pl.BlockSpec((tm, TOP_K), lambda i: (i, 0)),
            pl.BlockSpec((d, f), lambda i: (0, 0)),
            pl.BlockSpec((d, f), lambda i: (0, 0)),
            pl.BlockSpec((f, d), lambda i: (0, 0)),
            vec,
            vec,
        ],
        out_specs=[row] + res_specs,
        out_shape=[jax.ShapeDtypeStruct((t, d), F32)] + res_shapes,
        scratch_shapes=[
            pltpu.SMEM((TOP_K * tm,), jnp.int32),
            pltpu.SMEM((TOP_K * tm,), jnp.int32),
            pltpu.VMEM((2, TOP_K, tm * nc, LANES), U32),
            pltpu.SemaphoreType.DMA((2,)),
            pltpu.SemaphoreType.DMA((2,)),
        ] + slab_scratch,
        compiler_params=_cparams(1),
        name="combine",
    )(dest_tiles, ys, x, w_tok, ws_gate, ws_up, ws_down, g.reshape(1, d), b.reshape(1, d))


def _tile_major(a, tm):
    k, t = a.shape
    return a.reshape(k, t // tm, tm).transpose(1, 0, 2).reshape(t // tm, k * tm)


def _dest_kernel(start_ref, e_ref, pos_ref, dest_ref, *, tm):
    ne = N_EXPERTS
    start = jnp.concatenate([start_ref[...]] * (tm // LANES), axis=1)
    rows = lax.broadcasted_iota(jnp.int32, (ne, tm), 0)
    for k in range(TOP_K):
        hit = rows == e_ref[pl.ds(k, 1), :]
        first = jnp.sum(jnp.where(hit, start, 0), axis=0, keepdims=True)
        dest_ref[pl.ds(k, 1), :] = first + pos_ref[pl.ds(k, 1), :]


def _dest(starts, top_e, pos, tm=512):
    t = top_e.shape[1]
    tok = pl.BlockSpec((TOP_K, tm), lambda i: (0, i))
    return pl.pallas_call(
        functools.partial(_dest_kernel, tm=tm),
        grid=(t // tm,),
        in_specs=[pl.BlockSpec((N_EXPERTS, LANES), lambda i: (0, 0)), tok, tok],
        out_specs=tok,
        out_shape=jax.ShapeDtypeStruct((TOP_K, t), jnp.int32),
        compiler_params=_cparams(1),
        name="dest",
    )(jnp.broadcast_to(starts[:, None], (N_EXPERTS, LANES)), top_e, pos)


def _expert_row_block(t):
    mean_rows = t * TOP_K // N_EXPERTS
    return max(LANES, -(-(mean_rows + mean_rows // 4) // LANES) * LANES)


def _moe_ln(x, x_tiles, layer, seq, w_router, router_bias, we_gate, we_up, we_down, ws_gate, ws_up, ws_down,
            g, b, residue_dils=()):
    t, d = x.shape
    rb = _expert_row_block(t)
    top_e, w, pos, counts = _router(x, w_router, router_bias)

    cnt = counts[:, 0].astype(jnp.int32)
    padded = (cnt + rb - 1) // rb * rb
    pend = jnp.cumsum(padded)
    n_blocks = t * TOP_K // rb + N_EXPERTS
    n_real = (pend[-1] // rb).astype(jnp.int32)
    blk_ids = jnp.minimum(jnp.arange(n_blocks, dtype=jnp.int32), jnp.maximum(n_real - 1, 0))
    blk_e = jnp.sum((pend[None, :] <= (blk_ids * rb)[:, None]).astype(jnp.int32), axis=1)
    blk_e = jnp.minimum(blk_e, N_EXPERTS - 1)

    dest = _dest(pend - padded, top_e, pos)
    xs = _dispatch(x_tiles, _tile_major(dest, DISPATCH_TOKENS), n_blocks * rb, d // TILE_WORDS,
                   tm=DISPATCH_TOKENS)
    ys = _experts(xs, we_gate, we_up, we_down, layer, rb, blk_e, blk_ids, n_real.reshape(1))
    return _combine(
        x, ys, _tile_major(dest, COMBINE_TOKENS), w.T, ws_gate.astype(BF16),
        ws_up.astype(BF16), ws_down.astype(BF16), g, b, tm=COMBINE_TOKENS, seq=seq, residue_dils=residue_dils,
    )


def kernel(x, w_qkv_a, w_qkv_b, w_o, ln_mix_g, ln_mix_b, w_router, router_bias, we_gate, we_up,
           we_down, ws_gate, ws_up, ws_down, ln_ffn_g, ln_ffn_b):
    bsz, s, d = x.shape
    t = bsz * s
    slopes = _alibi_slopes()
    xt = x.reshape(t, d)
    is_dilated = lambda layer: layer % N_MIXERS == 1
    strided_dils = tuple(dil for _, dil in DIL_GROUPS if dil > 1)
    x_res = {}
    for i in range(DEPTH):
        if not is_dilated(i):
            qkv = _proj(xt, w_qkv_a[i // N_MIXERS].astype(BF16))
            mix = _moba_attention(qkv.reshape(bsz, s, -1), slopes).reshape(t, d)
        else:
            w_b = w_qkv_b[i // N_MIXERS].astype(BF16)
            outs, lses = [], []
            for g, (window, dil) in enumerate(DIL_GROUPS):
                xg = x_res[dil] if dil in x_res else _residue_major(xt, bsz, dil)
                qt, kk, vt = _proj_residue(xg, w_b[:, g * 3 * d:(g + 1) * 3 * d])
                o, lse = _dilated_group(qt, kk, vt, slopes, g, window, dil)
                outs.append(o)
                lses.append(lse)
            mix = _merge_groups(outs, lses).reshape(t, d)
        x1, x1_tiles = _wo_ln(mix, w_o[i].astype(BF16), xt, ln_mix_g[i], ln_mix_b[i])
        dils = strided_dils if i + 1 < DEPTH and is_dilated(i + 1) else ()
        xt, *extra = _moe_ln(x1, x1_tiles, i, s, w_router[i], router_bias[i], we_gate, we_up, we_down,
                             ws_gate[i], ws_up[i], ws_down[i], ln_ffn_g[i], ln_ffn_b[i], residue_dils=dils)
        x_res = dict(zip(dils, extra))
    return xt.reshape(bsz, s, d)
```

```python
import functools
import math

import jax
import jax.numpy as jnp
import numpy as np
from jax import lax
from jax.experimental import pallas as pl
from jax.experimental.pallas import tpu as pltpu

N_HEADS = 16
HEAD_DIM = 64
DEPTH = 2
N_MIXERS = 2
MOBA_BLOCK = 256
MOBA_TOPK = 3
MOBA_KEY_BLOCKS_PER_STEP = 4
MOBA_QUERY_BLOCKS_PER_STEP = MOBA_KEY_BLOCKS_PER_STEP
MOBA_BIAS_PARTS = 3
DIL_GROUPS = ((128, 1), (512, 4), (2048, 16))
DIL_BLOCK = 128
N_EXPERTS = 256
TOP_K = 8
N_GROUPS = 8
TOPK_GROUPS = 4
ROUTED_SCALE = 2.5
DN_ALPHA = (2 * DEPTH) ** 0.25
LN_EPS = 1e-5
NEG = -1e30
LOG2E = math.log2(math.e)

LANES = 128
BF16_SUBLANES = 16
HEADS_PER_SLAB = LANES // HEAD_DIM
N_SLABS = N_HEADS // HEADS_PER_SLAB
VMEM_LIMIT_BYTES = 48 * 1024 * 1024
MOBA_ACC_ROWS = HEAD_DIM + BF16_SUBLANES

DISPATCH_TOKENS = 512
COMBINE_TOKENS = 512

BF16 = jnp.bfloat16
F32 = jnp.float32
U32 = jnp.uint32


def _cparams(n_axes):
    return pltpu.CompilerParams(
        dimension_semantics=("arbitrary",) * n_axes, vmem_limit_bytes=VMEM_LIMIT_BYTES
    )


def _alibi_slopes():
    return 2.0 ** (-8.0 * jnp.arange(1, N_HEADS + 1, dtype=F32) / N_HEADS)


def _bf16_parts(x, n):
    parts, rest = [], x
    for _ in range(n):
        top = lax.bitcast_convert_type(lax.bitcast_convert_type(rest, jnp.uint32) & np.uint32(0xFFFF0000), F32)
        parts.append(top.astype(BF16))
        rest = rest - top
    return parts


def _proj_kernel(x_ref, w_ref, o_ref):
    o_ref[...] = jnp.dot(
        x_ref[...].astype(BF16), w_ref[...], preferred_element_type=F32
    ).astype(o_ref.dtype)


def _proj(x, w, tm=1024, tn=1024):
    t, k = x.shape
    n = w.shape[1]
    return pl.pallas_call(
        _proj_kernel,
        grid=(t // tm, n // tn),
        in_specs=[
            pl.BlockSpec((tm, k), lambda i, j: (i, 0)),
            pl.BlockSpec((k, tn), lambda i, j: (0, j)),
        ],
        out_specs=pl.BlockSpec((tm, tn), lambda i, j: (i, j)),
        out_shape=jax.ShapeDtypeStruct((t, n), BF16),
        compiler_params=_cparams(2),
        name="proj",
    )(x, w)


def _moba_kernel(slopes_ref, q_ref, qn_ref, k_ref, v_ref, wb_ref, o_ref, ka_ref, vt_ref, km_ref, sel_ref,
                 s_ref, *, n_blocks):
    blk = MOBA_BLOCK
    unroll = MOBA_KEY_BLOCKS_PER_STEP
    span = unroll * blk
    acc_rows = MOBA_ACC_ROWS
    h2 = pl.program_id(1)
    i = pl.program_id(2)
    heads = range(HEADS_PER_SLAB)

    @pl.when(i == 0)
    def _prepare_keys():
        lane = lax.broadcasted_iota(jnp.int32, (blk, LANES), 1)
        row_f = lax.broadcasted_iota(jnp.int32, (blk, LANES), 0).astype(F32)
        tail = lax.broadcasted_iota(jnp.int32, (acc_rows - HEAD_DIM, blk), 0)
        ones_row = jnp.where(tail == 0, 1.0, 0.0)

        def body(n, c):
            r0 = pl.multiple_of(n * blk, blk)
            vt = v_ref[0, pl.ds(r0, blk), :].astype(F32).T
            for hh in heads:
                vt_ref[n, hh] = jnp.concatenate(
                    [vt[hh * HEAD_DIM:(hh + 1) * HEAD_DIM], ones_row], axis=0).astype(BF16)
            kb = k_ref[0, pl.ds(r0, blk), :]
            km_ref[pl.ds(n, 1), :] = jnp.mean(kb.astype(F32), axis=0, keepdims=True)
            sub = jnp.asarray(n % unroll, F32)
            aux = jnp.where(lane < MOBA_BIAS_PARTS, sub, jnp.where(lane < 2 * MOBA_BIAS_PARTS, row_f, 0.0))
            ka_ref[pl.ds(r0, blk), :] = jnp.concatenate([kb, aux.astype(BF16)], axis=1)
            return c

        lax.fori_loop(0, n_blocks, body, 0)

    qb = MOBA_QUERY_BLOCKS_PER_STEP
    wide = qb * blk
    i_first = pl.program_id(2) * qb
    feat = lax.broadcasted_iota(jnp.int32, (LANES, wide), 0)
    kmean = km_ref[...]
    km_lane = lax.broadcasted_iota(jnp.int32, kmean.shape, 1)
    blk_row = lax.broadcasted_iota(jnp.int32, (n_blocks, wide), 0)
    own_blk = i_first + lax.broadcasted_iota(jnp.int32, (n_blocks, wide), 1) // blk

    def per_head_qt(ref):
        qt = ref[0].astype(F32).T
        return [jnp.where((feat >= hh * HEAD_DIM) & (feat < (hh + 1) * HEAD_DIM), qt, 0.0) for hh in heads]

    def score_operand(qt_hs):
        scaled = [(x * (HEAD_DIM ** -0.5 * LOG2E)).astype(BF16) for x in qt_hs]
        return jnp.concatenate([jnp.concatenate(scaled, axis=1), wb_ref[0]], axis=0)

    slopes = [slopes_ref[h2 * HEADS_PER_SLAB + hh] for hh in heads]
    qt_hs = per_head_qt(q_ref)
    for hh in heads:
        qt_h = qt_hs[hh]
        km_h = jnp.where((km_lane >= hh * HEAD_DIM) & (km_lane < (hh + 1) * HEAD_DIM), kmean, 0.0)
        qt_b = qt_h.astype(BF16)
        gate = sum(jnp.dot(part, qt_b, preferred_element_type=F32) for part in _bf16_parts(km_h, 3))
        g = jnp.where(blk_row < own_blk, gate, -jnp.inf)
        sel = jnp.zeros(g.shape, F32)
        for _ in range(MOBA_TOPK):
            gmax = jnp.max(g, axis=0, keepdims=True)
            first = jnp.min(jnp.where(g == gmax, blk_row, n_blocks), axis=0, keepdims=True)
            hit = blk_row == first
            sel = jnp.where(hit & (gmax > -jnp.inf), 1.0, sel)
            g = jnp.where(hit, -jnp.inf, g)
        sel_ref[hh] = sel

    w_all = score_operand(qt_hs)
    w_next = score_operand(per_head_qt(qn_ref))

    groups = [(hh, q) for hh in heads for q in range(qb)]
    group_cols = [slice((hh * qb + q) * blk, (hh * qb + q + 1) * blk) for hh, q in groups]

    def scores(span_idx, cols, w=w_all):
        r0 = pl.multiple_of(span_idx * span, span)
        return jnp.dot(ka_ref[pl.ds(r0, span), :], w[:, cols], preferred_element_type=F32)

    key_row = lax.broadcasted_iota(jnp.int32, (blk, blk), 0)
    qry_col = lax.broadcasted_iota(jnp.int32, (blk, blk), 1)

    def consume(gi, span_idx, m, acc, last):
        hh, q = groups[gi]
        j0 = span_idx * unroll
        c = slopes[hh] * LOG2E * jnp.asarray(blk * (j0 - (i_first + q)), F32)
        for u in range(unroll):
            if last and u > q:
                continue
            j = j0 + u
            s = s_ref[u * blk:(u + 1) * blk, group_cols[gi]]
            if last and u == q:
                s = jnp.where(key_row <= qry_col, s, NEG)
                m_new = jnp.maximum(m, jnp.max(s, axis=0, keepdims=True) + c)
                p = jnp.exp2(s - (m_new - c))
            else:
                ch = sel_ref[hh, pl.ds(j, 1), q * blk:(q + 1) * blk] > 0.5
                m_new = jnp.where(ch, jnp.maximum(m, jnp.max(s, axis=0, keepdims=True) + c), m)
                p = jnp.exp2(s - jnp.where(ch, m_new - c, -NEG))
            acc = jnp.exp2(m - m_new) * acc + jnp.dot(
                vt_ref[j, hh], p.astype(BF16), preferred_element_type=F32)
            m = m_new
        return m, acc

    n_full = i_first // unroll

    @pl.when(i == 0)
    def _first_scores():
        s_ref[...] = scores(0, slice(None))

    def body(it, state):
        out = []
        for gi in range(len(groups)):
            out.append(consume(gi, it, *state[gi], last=False))
            s_ref[:, group_cols[gi]] = scores(it + 1, group_cols[gi])
        return tuple(out)

    init = tuple((jnp.full((1, blk), NEG, F32), jnp.zeros((acc_rows, blk), F32)) for _ in groups)
    state = lax.fori_loop(0, n_full, body, init)
    final = []
    for gi in range(len(groups)):
        final.append(consume(gi, n_full, *state[gi], last=True))
        s_ref[:, group_cols[gi]] = scores(0, group_cols[gi], w_next)
    state = final

    outs = [acc[:HEAD_DIM] / acc[HEAD_DIM:HEAD_DIM + 1] for _, acc in state]
    per_block = [jnp.concatenate([outs[hh * qb + q] for hh in heads], axis=0) for q in range(qb)]
    o_ref[0] = jnp.concatenate(per_block, axis=1).T.astype(o_ref.dtype)


def _moba_attention(qkv, slopes):
    b, s, _ = qkv.shape
    blk = MOBA_BLOCK
    n_blocks = s // blk
    qb = MOBA_QUERY_BLOCKS_PER_STEP
    assert n_blocks % MOBA_KEY_BLOCKS_PER_STEP == 0 and MOBA_KEY_BLOCKS_PER_STEP == qb
    wide = HEADS_PER_SLAB * qb * blk
    grid_spec = pltpu.PrefetchScalarGridSpec(
        num_scalar_prefetch=1,
        grid=(b, N_SLABS, n_blocks // qb),
        in_specs=[
            pl.BlockSpec((1, qb * blk, LANES), lambda bi, h, i, sl: (bi, i, h)),
            pl.BlockSpec((1, qb * blk, LANES), lambda bi, h, i, sl: (bi, jnp.minimum(i + 1, n_blocks // qb - 1), h)),
            pl.BlockSpec((1, s, LANES), lambda bi, h, i, sl: (bi, 0, N_SLABS + h)),
            pl.BlockSpec((1, s, LANES), lambda bi, h, i, sl: (bi, 0, 2 * N_SLABS + h)),
            pl.BlockSpec((1, LANES, wide), lambda bi, h, i, sl: (h, 0, 0)),
        ],
        out_specs=pl.BlockSpec((1, qb * blk, LANES), lambda bi, h, i, sl: (bi, i, h)),
        scratch_shapes=[
            pltpu.VMEM((s, 2 * LANES), BF16),
            pltpu.VMEM((n_blocks, HEADS_PER_SLAB, MOBA_ACC_ROWS, blk), BF16),
            pltpu.VMEM((n_blocks, LANES), F32),
            pltpu.VMEM((HEADS_PER_SLAB, n_blocks, qb * blk), F32),
            pltpu.VMEM((MOBA_KEY_BLOCKS_PER_STEP * blk, wide), F32),
        ],
    )
    return pl.pallas_call(
        functools.partial(_moba_kernel, n_blocks=n_blocks),
        grid_spec=grid_spec,
        out_shape=jax.ShapeDtypeStruct((b, s, N_HEADS * HEAD_DIM), BF16),
        compiler_params=_cparams(3),
        name="moba",
    )(slopes, qkv, qkv, qkv, qkv, _moba_slope_rows(slopes))


def _moba_slope_rows(slopes):
    blk = MOBA_BLOCK
    wide = MOBA_QUERY_BLOCKS_PER_STEP * blk
    parts = _bf16_parts(slopes * LOG2E, MOBA_BIAS_PARTS)
    sp = jnp.stack([p.astype(F32) for p in parts], axis=1)
    rows = jnp.concatenate([sp * blk, sp], axis=1).reshape(N_SLABS, HEADS_PER_SLAB, 2 * MOBA_BIAS_PARTS)
    wb = jnp.zeros((N_SLABS, LANES, HEADS_PER_SLAB * wide), F32)
    for hh in range(HEADS_PER_SLAB):
        wb = wb.at[:, :2 * MOBA_BIAS_PARTS, hh * wide:(hh + 1) * wide].set(rows[:, hh, :, None])
    return wb.astype(BF16)


def _dilated_kernel(bias_ref, qt_ref, kc_ref, kp_ref, vtc_ref, vtp_ref, o_ref, lse_ref, lse_s, s_scr):
    blk = DIL_BLOCK
    lse_s[...] = jnp.zeros(lse_s.shape, F32)
    tail = lax.broadcasted_iota(jnp.int32, (MOBA_ACC_ROWS - HEAD_DIM, 2 * blk), 0)
    ones_row = jnp.where(tail == 0, 1.0, 0.0).astype(BF16)
    feat = lax.broadcasted_iota(jnp.int32, (LANES, blk), 0)
    for h2 in range(N_SLABS):
        cols = slice(h2 * LANES, (h2 + 1) * LANES)
        qt = qt_ref[0, 0, cols, :].astype(F32) * (HEAD_DIM ** -0.5 * LOG2E)
        kband = jnp.concatenate([kp_ref[0, 0, :, cols], kc_ref[0, 0, :, cols]], axis=0)
        qt_both = jnp.concatenate(
            [jnp.where((feat >= hh * HEAD_DIM) & (feat < (hh + 1) * HEAD_DIM), qt, 0.0)
             for hh in range(HEADS_PER_SLAB)], axis=1).astype(BF16)
        s_scr[h2] = jnp.dot(kband, qt_both, preferred_element_type=F32)
    for h2 in range(N_SLABS):
        cols = slice(h2 * LANES, (h2 + 1) * LANES)
        vt = jnp.concatenate([vtp_ref[0, 0, cols, :], vtc_ref[0, 0, cols, :]], axis=1)
        outs = []
        for hh in range(HEADS_PER_SLAB):
            h = h2 * HEADS_PER_SLAB + hh
            s = s_scr[h2, :, hh * blk:(hh + 1) * blk] + bias_ref[0, h]
            m = jnp.max(s, axis=0, keepdims=True)
            p = jnp.exp2(s - m).astype(BF16)
            vt_h = jnp.concatenate([vt[hh * HEAD_DIM:(hh + 1) * HEAD_DIM], ones_row], axis=0)
            acc = jnp.dot(vt_h, p, preferred_element_type=F32)
            l = acc[HEAD_DIM:HEAD_DIM + 1]
            outs.append(acc[:HEAD_DIM] / l)
            lse_s[pl.ds(h, 1), :] = (m + jnp.log2(l)) * (1.0 / LOG2E)
        o_ref[0, 0, :, cols] = jnp.concatenate(outs, axis=0).T.astype(o_ref.dtype)
    lse_ref[0, 0] = lse_s[...].T


def _dilated_bias(slopes, dil):
    blk = DIL_BLOCK
    key = jnp.arange(2 * blk)[:, None]
    steps = jnp.arange(blk)[None, :] - (key - blk)
    steps_max = blk
    valid = (steps >= 0) & (steps <= steps_max)
    bias = -(slopes * LOG2E)[:, None, None] * (steps * dil).astype(F32)[None]
    rest = jnp.where(valid[None], bias, NEG)
    first = jnp.where((key >= blk)[None], rest, NEG)
    return jnp.stack([first, rest]).astype(F32)


def _proj_residue_kernel(x_ref, wqt_ref, wk_ref, wvt_ref, qt_ref, k_ref, vt_ref):
    xb = x_ref[0, 0].astype(BF16)
    nt = (((1,), (1,)), ((), ()))
    qt_ref[0, 0] = lax.dot_general(wqt_ref[...], xb, nt, preferred_element_type=F32).astype(qt_ref.dtype)
    k_ref[0, 0] = jnp.dot(xb, wk_ref[...], preferred_element_type=F32).astype(k_ref.dtype)
    vt_ref[0, 0] = lax.dot_general(wvt_ref[...], xb, nt, preferred_element_type=F32).astype(vt_ref.dtype)


def _residue_major(x, bsz, dil):
    t, k = x.shape
    return x.reshape(bsz, t // bsz // dil, dil, k).transpose(0, 2, 1, 3)


def _proj_residue(x_res, w, tm=512):
    bsz, dil, l_sub, k = x_res.shape
    d = w.shape[1] // 3
    tm = min(tm, l_sub)
    const = lambda shape: pl.BlockSpec(shape, lambda b, r, m: (0, 0))
    tr = pl.BlockSpec((1, 1, d, tm), lambda b, r, m: (b, r, 0, m))
    return pl.pallas_call(
        _proj_residue_kernel,
        grid=(bsz, dil, l_sub // tm),
        in_specs=[pl.BlockSpec((1, 1, tm, k), lambda b, r, m: (b, r, m, 0)), const((d, k)), const((k, d)),
                  const((d, k))],
        out_specs=[tr, pl.BlockSpec((1, 1, tm, d), lambda b, r, m: (b, r, m, 0)), tr],
        out_shape=[
            jax.ShapeDtypeStruct((bsz, dil, d, l_sub), BF16),
            jax.ShapeDtypeStruct((bsz, dil, l_sub, d), BF16),
            jax.ShapeDtypeStruct((bsz, dil, d, l_sub), BF16),
        ],
        compiler_params=_cparams(3),
        name=f"proj_residue_d{dil}",
    )(x_res, w[:, :d].T, w[:, d:2 * d], w[:, 2 * d:].T)


def _dilated_group(qt, k, vt, slopes, g, window, dil):
    b, _, l_sub, d = k.shape
    blk = DIL_BLOCK
    assert window // dil == blk
    nb = l_sub // blk
    bias = _dilated_bias(slopes, dil)
    prev = lambda n: jnp.maximum(n - 1, 0)

    o, lse = pl.pallas_call(
        _dilated_kernel,
        grid=(b, dil, nb),
        in_specs=[
            pl.BlockSpec((1, N_HEADS, 2 * blk, blk), lambda bi, r, n: (jnp.minimum(n, 1), 0, 0, 0)),
            pl.BlockSpec((1, 1, d, blk), lambda bi, r, n: (bi, r, 0, n)),
            pl.BlockSpec((1, 1, blk, d), lambda bi, r, n: (bi, r, n, 0)),
            pl.BlockSpec((1, 1, blk, d), lambda bi, r, n: (bi, r, prev(n), 0)),
            pl.BlockSpec((1, 1, d, blk), lambda bi, r, n: (bi, r, 0, n)),
            pl.BlockSpec((1, 1, d, blk), lambda bi, r, n: (bi, r, 0, prev(n))),
        ],
        out_specs=[
            pl.BlockSpec((1, 1, blk, d), lambda bi, r, n: (bi, r, n, 0)),
            pl.BlockSpec((1, 1, blk, LANES), lambda bi, r, n: (bi, r, n, 0)),
        ],
        out_shape=[
            jax.ShapeDtypeStruct((b, dil, l_sub, d), BF16),
            jax.ShapeDtypeStruct((b, dil, l_sub, LANES), F32),
        ],
        scratch_shapes=[pltpu.VMEM((LANES, blk), F32), pltpu.VMEM((N_SLABS, 2 * blk, 2 * blk), F32)],
        compiler_params=_cparams(3),
        name=f"dilated_g{g}",
    )(bias, qt, k, k, vt, vt)
    return o, lse


def _merged_mix(e_ref, o_refs, l_refs, lse_nat, o_nat):
    ng = len(DIL_GROUPS)

    def to_natural(dst, src, dil):
        n = src.shape[2]
        for r in range(dil):
            rows = src[0, r].astype(F32)
            for c in range(dst.shape[0]):
                dst[c, pl.ds(r, n, stride=dil), :] = rows[:, c * LANES:(c + 1) * LANES]

    for g, (_, dil) in enumerate(DIL_GROUPS):
        to_natural(lse_nat.at[pl.ds(g, 1)], l_refs[g], dil)
    ls = [lse_nat[g] for g in range(ng)]
    mx = functools.reduce(jnp.maximum, ls)
    es = [jnp.exp(x - mx) for x in ls]
    den = functools.reduce(lambda a, b: a + b, es)
    acc = None
    for g, (_, dil) in enumerate(DIL_GROUPS):
        to_natural(o_nat, o_refs[g], dil)
        w = sum(jnp.dot(part, e_ref[...], preferred_element_type=F32) for part in _bf16_parts(es[g] / den, 2))
        term = w * jnp.concatenate([o_nat[c] for c in range(o_nat.shape[0])], axis=1)
        acc = term if acc is None else acc + term
    return acc


def _merge_kernel(e_ref, *refs):
    ng = len(DIL_GROUPS)
    out_ref, lse_nat, o_nat = refs[2 * ng:]
    out_ref[0] = _merged_mix(e_ref, refs[:ng], refs[ng:2 * ng], lse_nat, o_nat).astype(out_ref.dtype)


def _wo_ln_merged_kernel(e_ref, *refs):
    ng = len(DIL_GROUPS)
    wo_ref, x_ref, g_ref, b_ref, o_ref, ot_ref, lse_nat, o_nat = refs[2 * ng:]
    mix = _merged_mix(e_ref, refs[:ng], refs[ng:2 * ng], lse_nat, o_nat).astype(BF16)
    f = jnp.dot(mix, wo_ref[...], preferred_element_type=F32)
    y = _layer_norm(DN_ALPHA * x_ref[...] + f, g_ref[...], b_ref[...])
    o_ref[...] = y
    _pack_token_tiles(ot_ref, y)


def _wo_ln_merged(os, lses, wo, x, g, b, tm=512):
    bsz, dil0, l0, d = os[0].shape
    s = dil0 * l0
    t = bsz * s
    npk = d // TILE_WORDS
    tiles = s // tm
    expand = (jnp.arange(LANES)[:, None] == (jnp.arange(d)[None, :] // HEAD_DIM)).astype(BF16)

    def grp(width, dil):
        return pl.BlockSpec((1, dil, tm // dil, width), lambda bi, i: (bi, 0, i, 0))

    dils = [dil for _, dil in DIL_GROUPS]
    row = pl.BlockSpec((tm, d), lambda bi, i: (bi * tiles + i, 0))
    const = lambda shape: pl.BlockSpec(shape, lambda bi, i: (0, 0))
    return pl.pallas_call(
        _wo_ln_merged_kernel,
        grid=(bsz, tiles),
        in_specs=[const((LANES, d))] + [grp(d, dil) for dil in dils] + [grp(LANES, dil) for dil in dils]
        + [const((d, d)), row, const((1, d)), const((1, d))],
        out_specs=[row, pl.BlockSpec((tm * npk, LANES), lambda bi, i: (bi * tiles + i, 0))],
        out_shape=[jax.ShapeDtypeStruct((t, d), F32), jax.ShapeDtypeStruct((t * npk, LANES), U32)],
        scratch_shapes=[pltpu.VMEM((len(dils), tm, LANES), F32), pltpu.VMEM((d // LANES, tm, LANES), F32)],
        compiler_params=_cparams(2),
        name="wo_ln_merged",
    )(expand, *os, *lses, wo, x, g.reshape(1, d), b.reshape(1, d))


def _merge_groups(os, lses, tm=512):
    b, dil0, l0, d = os[0].shape
    s = dil0 * l0
    expand = (jnp.arange(LANES)[:, None] == (jnp.arange(d)[None, :] // HEAD_DIM)).astype(BF16)

    def grp(width, dil):
        return pl.BlockSpec((1, dil, tm // dil, width), lambda bi, i: (bi, 0, i, 0))

    dils = [dil for _, dil in DIL_GROUPS]
    return pl.pallas_call(
        _merge_kernel,
        grid=(b, s // tm),
        in_specs=[pl.BlockSpec((LANES, d), lambda bi, i: (0, 0))] + [grp(d, dil) for dil in dils]
        + [grp(LANES, dil) for dil in dils],
        out_specs=pl.BlockSpec((1, tm, d), lambda bi, i: (bi, i, 0)),
        out_shape=jax.ShapeDtypeStruct((b, s, d), BF16),
        scratch_shapes=[pltpu.VMEM((len(dils), tm, LANES), F32), pltpu.VMEM((d // LANES, tm, LANES), F32)],
        compiler_params=_cparams(2),
        name="merge_groups",
    )(expand, *os, *lses)


def _layer_norm(y, g, b):
    mu = jnp.mean(y, axis=-1, keepdims=True)
    yc = y - mu
    var = jnp.mean(yc * yc, axis=-1, keepdims=True)
    return yc * lax.rsqrt(var + LN_EPS) * g + b


TILE_WORDS = 2 * LANES
HIGH_HALF = np.uint32(0xFFFF0000)


def _pack_token_tiles(ref, y):
    n, d = y.shape
    npk = d // TILE_WORDS
    for c in range(npk):
        lo = y[:, (2 * c) * LANES:(2 * c + 1) * LANES].astype(BF16).astype(F32)
        hi = y[:, (2 * c + 1) * LANES:(2 * c + 2) * LANES].astype(BF16).astype(F32)
        word = (lax.bitcast_convert_type(lo, U32) >> 16) | (lax.bitcast_convert_type(hi, U32) & HIGH_HALF)
        ref[pl.ds(c, n, stride=npk), :] = word


def _unpack_token_tiles(ref, n, npk, lead=()):
    blocks = []
    for c in range(npk):
        word = ref[lead + (pl.ds(c, n, stride=npk), slice(None))]
        blocks.append(lax.bitcast_convert_type(word << 16, F32))
        blocks.append(lax.bitcast_convert_type(word & HIGH_HALF, F32))
    return jnp.concatenate(blocks, axis=1)


def _wo_ln_kernel(mix_ref, wo_ref, x_ref, g_ref, b_ref, o_ref, ot_ref):
    f = jnp.dot(mix_ref[...], wo_ref[...], preferred_element_type=F32)
    y = _layer_norm(DN_ALPHA * x_ref[...] + f, g_ref[...], b_ref[...])
    o_ref[...] = y
    _pack_token_tiles(ot_ref, y)


def _wo_ln(mix, wo, x, g, b, tm=512):
    t, d = x.shape
    npk = d // TILE_WORDS
    row = pl.BlockSpec((tm, d), lambda i: (i, 0))
    vec = pl.BlockSpec((1, d), lambda i: (0, 0))
    return pl.pallas_call(
        _wo_ln_kernel,
        grid=(t // tm,),
        in_specs=[row, pl.BlockSpec((d, d), lambda i: (0, 0)), row, vec, vec],
        out_specs=[row, pl.BlockSpec((tm * npk, LANES), lambda i: (i, 0))],
        out_shape=[jax.ShapeDtypeStruct((t, d), F32), jax.ShapeDtypeStruct((t * npk, LANES), U32)],
        compiler_params=_cparams(1),
        name="wo_ln",
    )(mix, wo, x, g.reshape(1, d), b.reshape(1, d))


def _router_kernel(wr_ref, rb_ref, x_ref, e_ref, w_ref, pos_ref, cnt_ref, run_ref, *, tm):
    step = pl.program_id(0)
    ne = N_EXPERTS
    gs = ne // N_GROUPS

    @pl.when(step == 0)
    def _():
        run_ref[...] = jnp.zeros(run_ref.shape, F32)

    x = x_ref[...]
    x_hi = x.astype(BF16)
    x_lo = (x - x_hi.astype(F32)).astype(BF16)
    nt = (((1,), (1,)), ((), ()))
    logits = (lax.dot_general(wr_ref[0], x_hi, nt, preferred_element_type=F32)
              + lax.dot_general(wr_ref[0], x_lo, nt, preferred_element_type=F32)
              + lax.dot_general(wr_ref[1], x_hi, nt, preferred_element_type=F32))
    scores = jax.nn.sigmoid(logits)
    choice = scores + rb_ref[...][:, :1]

    grp_rows = lax.broadcasted_iota(jnp.int32, (gs, tm), 0)
    blocks, gscore = [], []
    for gi in range(N_GROUPS):
        cb = choice[gi * gs:(gi + 1) * gs]
        m1 = jnp.max(cb, axis=0, keepdims=True)
        i1 = jnp.min(jnp.where(cb == m1, grp_rows, gs), axis=0, keepdims=True)
        m2 = jnp.max(jnp.where(grp_rows == i1, -jnp.inf, cb), axis=0, keepdims=True)
        blocks.append(cb)
        gscore.append(m1 + m2)
    masked = []
    for gi in range(N_GROUPS):
        beaten = jnp.zeros((1, tm), jnp.int32)
        for gj in range(N_GROUPS):
            if gj == gi:
                continue
            wins = (gscore[gj] > gscore[gi]) | ((gscore[gj] == gscore[gi]) & (gj < gi))
            beaten = beaten + wins.astype(jnp.int32)
        masked.append(jnp.where(beaten < TOPK_GROUPS, blocks[gi], NEG))
    cand = jnp.concatenate(masked, axis=0)

    rows = lax.broadcasted_iota(jnp.int32, (ne, tm), 0)
    member = jnp.zeros((ne, tm), F32)
    picks, raw_w = [], []
    for _ in range(TOP_K):
        cmax = jnp.max(cand, axis=0, keepdims=True)
        first = jnp.min(jnp.where(cand == cmax, rows, ne), axis=0, keepdims=True)
        hit = rows == first
        raw_w.append(jnp.sum(jnp.where(hit, scores, 0.0), axis=0, keepdims=True))
        member = jnp.where(hit, 1.0, member)
        cand = jnp.where(hit, -jnp.inf, cand)
        picks.append(first)
    wsum = raw_w[0]
    for r in raw_w[1:]:
        wsum = wsum + r

    tri = (lax.broadcasted_iota(jnp.int32, (tm, tm), 0) < lax.broadcasted_iota(jnp.int32, (tm, tm), 1))
    member_b = member.astype(BF16)
    before = jnp.dot(member_b, tri.astype(BF16), preferred_element_type=F32)
    run = run_ref[...]
    rank = before + jnp.concatenate([run] * (tm // LANES), axis=1)
    for k in range(TOP_K):
        hit = rows == picks[k]
        e_ref[pl.ds(k, 1), :] = picks[k]
        w_ref[pl.ds(k, 1), :] = raw_w[k] / wsum * ROUTED_SCALE
        pos_ref[pl.ds(k, 1), :] = jnp.sum(jnp.where(hit, rank, 0.0), axis=0, keepdims=True).astype(jnp.int32)
    run = run + jnp.dot(member_b, jnp.ones((tm, LANES), BF16), preferred_element_type=F32)
    run_ref[...] = run
    cnt_ref[...] = run


def _router(x, w_router, router_bias, tm=256):
    t, d = x.shape
    ne = N_EXPERTS
    wr_parts = jnp.stack(_bf16_parts(w_router.T.astype(F32), 2))
    rb = jnp.broadcast_to(router_bias.astype(F32)[:, None], (ne, LANES))
    tok = pl.BlockSpec((TOP_K, tm), lambda i: (0, i))
    return pl.pallas_call(
        functools.partial(_router_kernel, tm=tm),
        grid=(t // tm,),
        in_specs=[
            pl.BlockSpec((2, ne, d), lambda i: (0, 0, 0)),
            pl.BlockSpec((ne, LANES), lambda i: (0, 0)),
            pl.BlockSpec((tm, d), lambda i: (i, 0)),
        ],
        out_specs=[tok, tok, tok, pl.BlockSpec((ne, LANES), lambda i: (0, 0))],
        out_shape=[
            jax.ShapeDtypeStruct((TOP_K, t), jnp.int32),
            jax.ShapeDtypeStruct((TOP_K, t), F32),
            jax.ShapeDtypeStruct((TOP_K, t), jnp.int32),
            jax.ShapeDtypeStruct((ne, LANES), F32),
        ],
        scratch_shapes=[pltpu.VMEM((ne, LANES), F32)],
        compiler_params=_cparams(1),
        name="router",
    )(wr_parts, rb, x)


def _tile_rows(ref, i, npk):
    return ref.at[pl.ds(pl.multiple_of(i * npk, npk), npk)]


def _dispatch_kernel(dest_hbm, x_ref, xs_hbm, dest_smem0, dest_smem1, sem_idx, sem_rows, *, tm):
    step = pl.program_id(0)
    n_steps = pl.num_programs(0)
    npk = x_ref.shape[0] // tm
    dest_smems = (dest_smem0, dest_smem1)

    def idx_copy(tile, sl):
        return pltpu.make_async_copy(dest_hbm.at[tile], dest_smems[sl], sem_idx.at[sl])

    def row_copy(t, d):
        return pltpu.make_async_copy(_tile_rows(x_ref, t, npk), _tile_rows(xs_hbm, d, npk), sem_rows)

    @pl.when(step == 0)
    def _():
        idx_copy(0, 0).start()

    for sl in range(2):
        @pl.when(step % 2 == sl)
        def _(sl=sl):
            idx_copy(step, sl).wait()

            @pl.when(step + 1 < n_steps)
            def _():
                idx_copy(step + 1, 1 - sl).start()

            def issue(t, c):
                for k in range(TOP_K):
                    row_copy(t, dest_smems[sl][k * tm + t]).start(priority=k % 2)
                return c

            lax.fori_loop(0, tm, issue, 0)

    def drain(t, c):
        for _ in range(TOP_K):
            row_copy(0, 0).wait()
        return c

    lax.fori_loop(0, tm, drain, 0)


def _dispatch(x_tiles, dest_tiles, n_rows, npk, tm):
    t = x_tiles.shape[0] // npk
    return pl.pallas_call(
        functools.partial(_dispatch_kernel, tm=tm),
        grid=(t // tm,),
        in_specs=[pl.BlockSpec(memory_space=pl.ANY), pl.BlockSpec((tm * npk, LANES), lambda i: (i, 0))],
        out_specs=pl.BlockSpec(memory_space=pl.ANY),
        out_shape=jax.ShapeDtypeStruct((n_rows * npk, LANES), x_tiles.dtype),
        scratch_shapes=[
            pltpu.SMEM((TOP_K * tm,), jnp.int32),
            pltpu.SMEM((TOP_K * tm,), jnp.int32),
            pltpu.SemaphoreType.DMA((2,)),
            pltpu.SemaphoreType.DMA,
        ],
        compiler_params=_cparams(1),
        name="dispatch",
    )(dest_tiles, x_tiles)


def _experts_kernel(blk_e_ref, blk_row_ref, n_real_ref, xs_ref, wg_ref, wu_ref, wd_ref, ys_ref, *, rb):
    @pl.when(pl.program_id(0) < n_real_ref[0])
    def _():
        npk = xs_ref.shape[0] // rb
        xb = _unpack_token_tiles(xs_ref, rb, npk).astype(BF16)
        gate = jnp.dot(xb, wg_ref[0, 0].astype(BF16), preferred_element_type=F32)
        up = jnp.dot(xb, wu_ref[0, 0].astype(BF16), preferred_element_type=F32)
        h = (gate * jax.nn.sigmoid(gate) * up).astype(BF16)
        _pack_token_tiles(ys_ref, jnp.dot(h, wd_ref[0, 0].astype(BF16), preferred_element_type=F32))


def _experts(xs, we_gate, we_up, we_down, layer, rb, blk_e, blk_row, n_real):
    d, f = we_gate.shape[-2:]
    nc = d // TILE_WORDS
    p = xs.shape[0] // nc
    rows = pl.BlockSpec((rb * nc, LANES), lambda i, be, br, nr: (br[i], 0))
    grid_spec = pltpu.PrefetchScalarGridSpec(
        num_scalar_prefetch=3,
        grid=(p // rb,),
        in_specs=[
            rows,
            pl.BlockSpec((1, 1, d, f), lambda i, be, br, nr: (layer, be[i], 0, 0)),
            pl.BlockSpec((1, 1, d, f), lambda i, be, br, nr: (layer, be[i], 0, 0)),
            pl.BlockSpec((1, 1, f, d), lambda i, be, br, nr: (layer, be[i], 0, 0)),
        ],
        out_specs=rows,
    )
    return pl.pallas_call(
        functools.partial(_experts_kernel, rb=rb),
        grid_spec=grid_spec,
        out_shape=jax.ShapeDtypeStruct((p * nc, LANES), U32),
        compiler_params=_cparams(1),
        name="experts",
    )(blk_e, blk_row, n_real, xs, we_gate, we_up, we_down)


def _combine_kernel(dest_hbm, ys_hbm, x_ref, w_ref, sg_ref, su_ref, sd_ref, g_ref, b_ref, o_ref, *refs,
                    tm, residue_dils):
    res_refs = refs[:len(residue_dils)]
    dest_smem0, dest_smem1, rows_ref, sem_idx, sem_rows = refs[len(residue_dils):len(residue_dils) + 5]
    dest_smems = (dest_smem0, dest_smem1)
    step = pl.program_id(0)
    n_steps = pl.num_programs(0)
    npk = rows_ref.shape[2] // tm
    slot = step % 2

    def row_copy(sl, k, t, d):
        return pltpu.make_async_copy(
            _tile_rows(ys_hbm, d, npk), _tile_rows(rows_ref.at[sl, k], t, npk), sem_rows.at[sl])

    def idx_copy(tile, sl):
        return pltpu.make_async_copy(dest_hbm.at[tile], dest_smems[sl], sem_idx.at[sl])

    def issue_rows(sl):
        dest_smem = dest_smems[sl]

        def issue(t, c):
            for k in range(TOP_K):
                row_copy(sl, k, t, dest_smem[k * tm + t]).start(priority=k % 2)
            return c

        lax.fori_loop(0, tm, issue, 0)

    @pl.when(step == 0)
    def _():
        idx_copy(0, 0).start()
        idx_copy(0, 0).wait()
        issue_rows(0)

        @pl.when(n_steps > 1)
        def _():
            idx_copy(1, 1).start()

    for sl in range(2):
        @pl.when((step + 1 < n_steps) & (slot == 1 - sl))
        def _(sl=sl):
            idx_copy(step + 1, sl).wait()

            @pl.when(step + 2 < n_steps)
            def _():
                idx_copy(step + 2, 1 - sl).start()

            issue_rows(sl)

    x = x_ref[...]
    xb = x.astype(BF16)
    gate = jnp.dot(xb, sg_ref[...], preferred_element_type=F32)
    up = jnp.dot(xb, su_ref[...], preferred_element_type=F32)
    h = (gate * jax.nn.sigmoid(gate) * up).astype(BF16)
    f = jnp.dot(h, sd_ref[...], preferred_element_type=F32)

    def drain(t, c):
        for _ in range(TOP_K):
            row_copy(slot, 0, 0, 0).wait()
        return c

    lax.fori_loop(0, tm, drain, 0)

    w = w_ref[...]
    for k in range(TOP_K):
        f = f + w[:, k:k + 1] * _unpack_token_tiles(rows_ref, tm, npk, lead=(slot, k))
    y = _layer_norm(DN_ALPHA * x + f, g_ref[...], b_ref[...])
    o_ref[...] = y
    if residue_dils:
        slabs = refs[-1]
        n_slabs = slabs.shape[0]
        for c in range(n_slabs):
            slabs[c] = y[:, c * LANES:(c + 1) * LANES]
        for res_ref, dil in zip(res_refs, residue_dils):
            for r in range(dil):
                res_ref[0, r] = jnp.concatenate(
                    [slabs[c, pl.ds(r, tm // dil, stride=dil), :] for c in range(n_slabs)], axis=1)


def _combine(x, ys, dest_tiles, w_tok, ws_gate, ws_up, ws_down, g, b, tm, seq, residue_dils=()):
    t, d = x.shape
    nc = d // TILE_WORDS
    f = ws_gate.shape[-1]
    tiles_per_seq = seq // tm
    row = pl.BlockSpec((tm, d), lambda i: (i, 0))
    vec = pl.BlockSpec((1, d), lambda i: (0, 0))
    res_specs = [pl.BlockSpec((1, dil, tm // dil, d), lambda i: (i // tiles_per_seq, 0, i % tiles_per_seq, 0))
                 for dil in residue_dils]
    res_shapes = [jax.ShapeDtypeStruct((t // seq, dil, seq // dil, d), F32) for dil in residue_dils]
    slab_scratch = [pltpu.VMEM((d // LANES, tm, LANES), F32)] if residue_dils else []
    return pl.pallas_call(
        functools.partial(_combine_kernel, tm=tm, residue_dils=tuple(residue_dils)),
        grid=(t // tm,),
        in_specs=[
            pl.BlockSpec(memory_space=pl.ANY),
            pl.BlockSpec(memory_space=pl.ANY),
            row,
            pl.BlockSpec((tm, TOP_K), lambda i: (i, 0)),
            pl.BlockSpec((d, f), lambda i: (0, 0)),
            pl.BlockSpec((d, f), lambda i: (0, 0)),
            pl.BlockSpec((f, d), lambda i: (0, 0)),
            vec,
            vec,
        ],
        out_specs=[row] + res_specs,
        out_shape=[jax.ShapeDtypeStruct((t, d), F32)] + res_shapes,
        scratch_shapes=[
            pltpu.SMEM((TOP_K * tm,), jnp.int32),
            pltpu.SMEM((TOP_K * tm,), jnp.int32),
            pltpu.VMEM((2, TOP_K, tm * nc, LANES), U32),
            pltpu.SemaphoreType.DMA((2,)),
            pltpu.SemaphoreType.DMA((2,)),
        ] + slab_scratch,
        compiler_params=_cparams(1),
        name="combine",
    )(dest_tiles, ys, x, w_tok, ws_gate, ws_up, ws_down, g.reshape(1, d), b.reshape(1, d))


def _tile_major(a, tm):
    k, t = a.shape
    return a.reshape(k, t // tm, tm).transpose(1, 0, 2).reshape(t // tm, k * tm)


def _dest_kernel(start_ref, e_ref, pos_ref, dest_ref, *, tm):
    ne = N_EXPERTS
    start = jnp.concatenate([start_ref[...]] * (tm // LANES), axis=1)
    rows = lax.broadcasted_iota(jnp.int32, (ne, tm), 0)
    for k in range(TOP_K):
        hit = rows == e_ref[pl.ds(k, 1), :]
        first = jnp.sum(jnp.where(hit, start, 0), axis=0, keepdims=True)
        dest_ref[pl.ds(k, 1), :] = first + pos_ref[pl.ds(k, 1), :]


def _dest(starts, top_e, pos, tm=512):
    t = top_e.shape[1]
    tok = pl.BlockSpec((TOP_K, tm), lambda i: (0, i))
    return pl.pallas_call(
        functools.partial(_dest_kernel, tm=tm),
        grid=(t // tm,),
        in_specs=[pl.BlockSpec((N_EXPERTS, LANES), lambda i: (0, 0)), tok, tok],
        out_specs=tok,
        out_shape=jax.ShapeDtypeStruct((TOP_K, t), jnp.int32),
        compiler_params=_cparams(1),
        name="dest",
    )(jnp.broadcast_to(starts[:, None], (N_EXPERTS, LANES)), top_e, pos)


def _expert_row_block(t):
    mean_rows = t * TOP_K // N_EXPERTS
    return max(LANES, -(-(mean_rows + mean_rows // 4) // LANES) * LANES)


def _moe_ln(x, x_tiles, layer, seq, w_router, router_bias, we_gate, we_up, we_down, ws_gate, ws_up, ws_down,
            g, b, residue_dils=()):
    t, d = x.shape
    rb = _expert_row_block(t)
    top_e, w, pos, counts = _router(x, w_router, router_bias)

    cnt = counts[:, 0].astype(jnp.int32)
    padded = (cnt + rb - 1) // rb * rb
    pend = jnp.cumsum(padded)
    n_blocks = t * TOP_K // rb + N_EXPERTS
    n_real = (pend[-1] // rb).astype(jnp.int32)
    blk_ids = jnp.minimum(jnp.arange(n_blocks, dtype=jnp.int32), jnp.maximum(n_real - 1, 0))
    blk_e = jnp.sum((pend[None, :] <= (blk_ids * rb)[:, None]).astype(jnp.int32), axis=1)
    blk_e = jnp.minimum(blk_e, N_EXPERTS - 1)

    dest = _dest(pend - padded, top_e, pos)
    xs = _dispatch(x_tiles, _tile_major(dest, DISPATCH_TOKENS), n_blocks * rb, d // TILE_WORDS,
                   tm=DISPATCH_TOKENS)
    ys = _experts(xs, we_gate, we_up, we_down, layer, rb, blk_e, blk_ids, n_real.reshape(1))
    return _combine(
        x, ys, _tile_major(dest, COMBINE_TOKENS), w.T, ws_gate.astype(BF16),
        ws_up.astype(BF16), ws_down.astype(BF16), g, b, tm=COMBINE_TOKENS, seq=seq, residue_dils=residue_dils,
    )


def kernel(x, w_qkv_a, w_qkv_b, w_o, ln_mix_g, ln_mix_b, w_router, router_bias, we_gate, we_up,
           we_down, ws_gate, ws_up, ws_down, ln_ffn_g, ln_ffn_b):
    bsz, s, d = x.shape
    t = bsz * s
    slopes = _alibi_slopes()
    xt = x.reshape(t, d)
    is_dilated = lambda layer: layer % N_MIXERS == 1
    strided_dils = tuple(dil for _, dil in DIL_GROUPS if dil > 1)
    x_res = {}
    for i in range(DEPTH):
        wo = w_o[i].astype(BF16)
        if not is_dilated(i):
            qkv = _proj(xt, w_qkv_a[i // N_MIXERS].astype(BF16))
            mix = _moba_attention(qkv.reshape(bsz, s, -1), slopes).reshape(t, d)
            x1, x1_tiles = _wo_ln(mix, wo, xt, ln_mix_g[i], ln_mix_b[i])
        else:
            w_b = w_qkv_b[i // N_MIXERS].astype(BF16)
            outs, lses = [], []
            for g, (window, dil) in enumerate(DIL_GROUPS):
                xg = x_res[dil] if dil in x_res else _residue_major(xt, bsz, dil)
                qt, kk, vt = _proj_residue(xg, w_b[:, g * 3 * d:(g + 1) * 3 * d])
                o, lse = _dilated_group(qt, kk, vt, slopes, g, window, dil)
                outs.append(o)
                lses.append(lse)
            x1, x1_tiles = _wo_ln_merged(outs, lses, wo, xt, ln_mix_g[i], ln_mix_b[i])
        dils = strided_dils if i + 1 < DEPTH and is_dilated(i + 1) else ()
        xt, *extra = _moe_ln(x1, x1_tiles, i, s, w_router[i], router_bias[i], we_gate, we_up, we_down,
                             ws_gate[i], ws_up[i], ws_down[i], ln_ffn_g[i], ln_ffn_b[i], residue_dils=dils)
        x_res = dict(zip(dils, extra))
    return xt.reshape(bsz, s, d)
```
